```python
import math
import jax
import jax.numpy as jnp
from jax import lax
import numpy as np

D_MODEL = 1024
BATCH = 16
SEQ = 256
DEPTH = 4
DEC_BATCH = 4
DEC_SEQ = 1024
PAST_LEN = 512

GRID_W = 64
N_HEADS = 8
N_KV_HEADS = 2
HEAD_DIM = 64
GQA_GROUP = N_HEADS // N_KV_HEADS
ATTN_W = N_HEADS * HEAD_DIM
KV_W = N_KV_HEADS * HEAD_DIM
ATTN_SCALE = HEAD_DIM ** -0.5
WINDOW = 128
BLOCK = 128
ROPE_BASE = 10000.0
FOURIER_HEADS = 4
FOURIER_HEAD_DIM = 64
FOURIER_W = FOURIER_HEADS * FOURIER_HEAD_DIM
S5_GROUPS = 16
S5_GROUP_CH = 16
S5_STATE = 64
S5_W = S5_GROUPS * S5_GROUP_CH
MIX_W = ATTN_W + FOURIER_W + S5_W
IN_W = ATTN_W + 2 * KV_W + FOURIER_W + S5_W
IN_SPLITS = (ATTN_W, ATTN_W + KV_W, ATTN_W + 2 * KV_W, ATTN_W + 2 * KV_W + FOURIER_W)
N_EXPERTS = 16
EXPERT_FF = 1024
CAPACITY_FACTOR = 2
LN_EPS = 1e-5
NEG_INF = -1e30
DEEPNORM_ALPHA = (2 * DEPTH) ** 0.25
DEEPNORM_BETA = (8 * DEPTH) ** -0.25

kernel_name = 'hybrid_diffusion_trunk_step'


def layer_norm(x, g, b):
    xf = x.astype(jnp.float32)
    mu = jnp.mean(xf, axis=-1, keepdims=True)
    var = jnp.mean(jnp.square(xf - mu), axis=-1, keepdims=True)
    return ((xf - mu) * lax.rsqrt(var + LN_EPS)).astype(x.dtype) * g + b


def adaln(cond, w_ada, b_ada):
    m = jax.nn.silu(cond) @ w_ada + b_ada
    return jnp.split(jnp.expand_dims(m, -2), 6, axis=-1)


def axial_rope(x):
    n = x.shape[1]
    rows = n // GRID_W
    row = jnp.repeat(jnp.arange(rows, dtype=jnp.float32), GRID_W)
    col = jnp.tile(jnp.arange(GRID_W, dtype=jnp.float32), rows)
    n_freq = HEAD_DIM // 4
    freqs = ROPE_BASE ** (-jnp.arange(n_freq, dtype=jnp.float32) / n_freq)
    ang = jnp.concatenate([row[:, None] * freqs, col[:, None] * freqs], axis=-1)
    cos = jnp.cos(ang)[None, :, None, :].astype(x.dtype)
    sin = jnp.sin(ang)[None, :, None, :].astype(x.dtype)
    x1, x2 = x[..., 0::2], x[..., 1::2]
    return jnp.stack([x1 * cos - x2 * sin, x1 * sin + x2 * cos], axis=-1).reshape(x.shape)


def attend(q, k, v, valid, sink):
    s = jnp.einsum('bqkgd,bskd->bkgqs', q.astype(jnp.float32), k.astype(jnp.float32)) * ATTN_SCALE
    if valid is not None:
        s = jnp.where(valid, s, NEG_INF)
    sink_l = jnp.broadcast_to(sink.astype(jnp.float32).reshape(1, N_KV_HEADS, GQA_GROUP, 1, 1), s.shape[:-1] + (1,))
    p = jax.nn.softmax(jnp.concatenate([s, sink_l], axis=-1), axis=-1)[..., :-1]
    o = jnp.einsum('bkgqs,bskd->bqkgd', p, v.astype(jnp.float32))
    return o.reshape(o.shape[0], o.shape[1], ATTN_W).astype(q.dtype)


def context_attention(q, k, v, sink):
    b, n = q.shape[:2]
    nb = n // BLOCK
    qb = jnp.moveaxis(q.reshape(b, nb, BLOCK, N_KV_HEADS, GQA_GROUP, HEAD_DIM), 1, 0)
    out = lax.map(lambda qi: attend(qi, k, v, None, sink), qb)
    return jnp.moveaxis(out, 0, 1).reshape(b, n, ATTN_W)


def latent_attention(q, k, v, k_ctx, v_ctx, sink):
    b, n = q.shape[:2]
    nb = n // BLOCK
    n_ctx = k_ctx.shape[1]
    pad = ((0, 0), (BLOCK, BLOCK), (0, 0), (0, 0))
    k_pad = jnp.pad(k, pad)
    v_pad = jnp.pad(v, pad)
    qb = jnp.moveaxis(q.reshape(b, nb, BLOCK, N_KV_HEADS, GQA_GROUP, HEAD_DIM), 1, 0)
    key_off = jnp.arange(3 * BLOCK) - BLOCK
    rel = key_off[None, :] - jnp.arange(BLOCK)[:, None]
    ctx_valid = jnp.ones((BLOCK, n_ctx), dtype=bool)

    def block(args):
        qi, i = args
        start = i * BLOCK
        kb = lax.dynamic_slice_in_dim(k_pad, start, 3 * BLOCK, axis=1)
        vb = lax.dynamic_slice_in_dim(v_pad, start, 3 * BLOCK, axis=1)
        kpos = start + key_off
        band = (jnp.abs(rel) <= WINDOW) & (kpos >= 0)[None, :] & (kpos < n)[None, :]
        valid = jnp.concatenate([band, ctx_valid], axis=1)
        keys = jnp.concatenate([kb, k_ctx.astype(kb.dtype)], axis=1)
        vals = jnp.concatenate([vb, v_ctx.astype(vb.dtype)], axis=1)
        return attend(qi, keys, vals, valid, sink)

    out = lax.map(block, (qb, jnp.arange(nb)))
    return jnp.moveaxis(out, 0, 1).reshape(b, n, ATTN_W)


def fourier_mixer(u, w_fourier):
    b, n = u.shape[:2]
    uh = u.astype(jnp.float32).reshape(b, n, FOURIER_HEADS, FOURIER_HEAD_DIM)
    f = jnp.fft.fft2(uh, axes=(1, 3), norm='ortho').real
    return f.reshape(b, n, FOURIER_W).astype(u.dtype) @ w_fourier


def _ssm_combine(left, right):
    a_l, b_l = left
    a_r, b_r = right
    return a_r * a_l, a_r * b_l + b_r


def s5_discretise(a_re, a_im, log_dt, b_re, b_im):
    a = lax.complex(a_re.astype(jnp.float32), a_im.astype(jnp.float32))
    dt = jnp.exp(log_dt.astype(jnp.float32))[:, None]
    a_bar = jnp.exp(a * dt)
    b = lax.complex(b_re.astype(jnp.float32), b_im.astype(jnp.float32))
    b_bar = ((a_bar - 1.0) / a)[..., None] * b
    return a_bar, b_bar


def s5_scan(u, a_bar, b_bar, s0, reverse):
    bu = jnp.einsum('gph,bngh->bngp', b_bar, u.astype(jnp.complex64))
    if s0 is not None:
        edge = -1 if reverse else 0
        bu = bu.at[:, edge].add(a_bar * s0)
    a_seq = jnp.broadcast_to(a_bar, bu.shape)
    _, states = lax.associative_scan(_ssm_combine, (a_seq, bu), reverse=reverse, axis=1)
    return states


def s5_mixer(u, p, s0_fwd, s0_bwd):
    b, n = u.shape[:2]
    uf = u.astype(jnp.float32).reshape(b, n, S5_GROUPS, S5_GROUP_CH)
    y = uf * p['s5_d'].astype(jnp.float32).reshape(S5_GROUPS, S5_GROUP_CH)
    finals = []
    for direction, (reverse, s0) in enumerate(((False, s0_fwd), (True, s0_bwd))):
        a_bar, b_bar = s5_discretise(p['s5_a_re'][direction], p['s5_a_im'][direction], p['s5_log_dt'][direction],
                                     p['s5_b_re'][direction], p['s5_b_im'][direction])
        states = s5_scan(uf, a_bar, b_bar, s0, reverse)
        c = lax.complex(p['s5_c_re'][direction].astype(jnp.float32), p['s5_c_im'][direction].astype(jnp.float32))
        y = y + jnp.einsum('ghp,bngp->bngh', c, states).real
        finals.append(states[:, 0] if reverse else states[:, -1])
    g = jax.nn.gelu(y.reshape(b, n, S5_W)).astype(u.dtype)
    return g * jax.nn.sigmoid(g @ p['s5_w_glu']), finals


def expert_choice_moe(h, w_router, w_gate, w_up, w_down):
    b, n, d = h.shape
    cap = CAPACITY_FACTOR * n // N_EXPERTS
    aff = jax.nn.softmax((h @ w_router).astype(jnp.float32), axis=-1)
    gate, idx = lax.top_k(jnp.swapaxes(aff, 1, 2), cap)
    bidx = jnp.arange(b)[:, None, None]
    xs = h[bidx, idx]
    hid = jax.nn.silu(jnp.einsum('becd,edf->becf', xs, w_gate)) * jnp.einsum('becd,edf->becf', xs, w_up)
    ys = jnp.einsum('becf,efd->becd', hid, w_down) * gate[..., None].astype(h.dtype)
    flat = (bidx * n + idx).reshape(-1)
    out = jnp.zeros((b * n, d), h.dtype).at[flat].add(ys.reshape(-1, d))
    return out.reshape(b, n, d)


def trunk_layer(x, p, mod, ctx=None):
    shift1, scale1, gate1, shift2, scale2, gate2 = mod
    b, n = x.shape[:2]
    h = x * (1 + scale1) + shift1
    q, k, v, u_four, u_s5 = jnp.split(h @ p['w_in'], IN_SPLITS, axis=-1)
    q = q.reshape(b, n, N_HEADS, HEAD_DIM)
    k = k.reshape(b, n, N_KV_HEADS, HEAD_DIM)
    v = v.reshape(b, n, N_KV_HEADS, HEAD_DIM)
    if ctx is None:
        attn = context_attention(q.reshape(b, n, N_KV_HEADS, GQA_GROUP, HEAD_DIM), k, v, p['attn_sink'])
        s5_out, finals = s5_mixer(u_s5, p, None, None)
    else:
        k_ctx, v_ctx, s0_fwd, s0_bwd = ctx
        q = axial_rope(q)
        k = axial_rope(k)
        attn = latent_attention(q.reshape(b, n, N_KV_HEADS, GQA_GROUP, HEAD_DIM), k, v, k_ctx, v_ctx, p['attn_sink'])
        s5_out, finals = s5_mixer(u_s5, p, s0_fwd, s0_bwd)
    four = fourier_mixer(u_four, p['w_fourier'])
    mix = jnp.concatenate([attn, four, s5_out], axis=-1) @ p['w_out']
    x = layer_norm(DEEPNORM_ALPHA * x + gate1 * mix, p['ln1_g'], p['ln1_b'])
    h = x * (1 + scale2) + shift2
    moe = expert_choice_moe(h, p['w_router'], p['w_gate'], p['w_up'], p['w_down'])
    x = layer_norm(DEEPNORM_ALPHA * x + gate2 * moe, p['ln2_g'], p['ln2_b'])
    return x, k, v, finals


def setup_inputs(seed: int = 0) -> dict:
    key = jax.random.key(seed)
    ks = jax.random.split(key, 36)
    f32 = jnp.float32
    L, D, E, F = DEPTH, D_MODEL, N_EXPERTS, EXPERT_FF
    G, P, H = S5_GROUPS, S5_STATE, S5_GROUP_CH

    def nrm(k, shape, scale):
        return scale * jax.random.normal(k, shape, f32)

    n_idx = jnp.arange(P, dtype=f32)
    return {
        'x_prompt': nrm(ks[0], (BATCH, SEQ, D), 1.0),
        'x_sample': nrm(ks[1], (DEC_BATCH, DEC_SEQ, D), 1.0),
        'cache_k': nrm(ks[2], (DEC_BATCH, L, PAST_LEN, N_KV_HEADS, HEAD_DIM), 1.0),
        'cache_v': nrm(ks[3], (DEC_BATCH, L, PAST_LEN, N_KV_HEADS, HEAD_DIM), 1.0),
        'state_s5_re': nrm(ks[4], (DEC_BATCH, L, 2, G, P), 0.1),
        'state_s5_im': nrm(ks[5], (DEC_BATCH, L, 2, G, P), 0.1),
        'c': nrm(ks[6], (DEC_BATCH, D), 1.0),
        'c_ctx': nrm(ks[7], (D,), 1.0),
        'ln_in_g': 1.0 + nrm(ks[8], (D,), 0.02),
        'ln_in_b': nrm(ks[9], (D,), 0.02),
        'w_ada': nrm(ks[10], (L, D, 6 * D), 0.5 * D ** -0.5),
        'b_ada': nrm(ks[11], (L, 6 * D), 0.02),
        'w_in': nrm(ks[12], (L, D, IN_W), D ** -0.5),
        'w_fourier': nrm(ks[13], (L, FOURIER_W, FOURIER_W), FOURIER_W ** -0.5),
        'attn_sink': nrm(ks[14], (L, N_HEADS), 0.5),
        's5_a_re': -0.5 + nrm(ks[15], (L, 2, G, P), 0.01),
        's5_a_im': math.pi * n_idx + nrm(ks[16], (L, 2, G, P), 0.01),
        's5_log_dt': jnp.log(jax.random.uniform(ks[17], (L, 2, G), f32, 1e-3, 1e-1)),
        's5_b_re': nrm(ks[18], (L, 2, G, P, H), (2 * H) ** -0.5),
        's5_b_im': nrm(ks[19], (L, 2, G, P, H), (2 * H) ** -0.5),
        's5_c_re': nrm(ks[20], (L, 2, G, H, P), P ** -0.5),
        's5_c_im': nrm(ks[21], (L, 2, G, H, P), P ** -0.5),
        's5_d': nrm(ks[22], (L, S5_W), 1.0),
        's5_w_glu': nrm(ks[23], (L, S5_W, S5_W), S5_W ** -0.5),
        'w_out': nrm(ks[24], (L, MIX_W, D), DEEPNORM_BETA * MIX_W ** -0.5),
        'ln1_g': 1.0 + nrm(ks[25], (L, D), 0.02),
        'ln1_b': nrm(ks[26], (L, D), 0.02),
        'w_router': nrm(ks[27], (L, D, E), D ** -0.5),
        'w_gate': nrm(ks[28], (L, E, D, F), D ** -0.5),
        'w_up': nrm(ks[29], (L, E, D, F), D ** -0.5),
        'w_down': nrm(ks[30], (L, E, F, D), DEEPNORM_BETA * F ** -0.5),
        'ln2_g': 1.0 + nrm(ks[31], (L, D), 0.02),
        'ln2_b': nrm(ks[32], (L, D), 0.02),
    }


def reference(x_prompt, x_sample, cache_k, cache_v, state_s5_re, state_s5_im, c, c_ctx,
              ln_in_g, ln_in_b, w_ada, b_ada, w_in, w_fourier, attn_sink,
              s5_a_re, s5_a_im, s5_log_dt, s5_b_re, s5_b_im, s5_c_re, s5_c_im, s5_d, s5_w_glu,
              w_out, ln1_g, ln1_b, w_router, w_gate, w_up, w_down, ln2_g, ln2_b):
    xp = layer_norm(x_prompt, ln_in_g, ln_in_b)
    xs = layer_norm(x_sample, ln_in_g, ln_in_b)
    new_ks, new_vs, new_ss = [], [], []
    for l in range(DEPTH):
        p = {
            'w_in': w_in[l], 'w_fourier': w_fourier[l], 'attn_sink': attn_sink[l],
            's5_a_re': s5_a_re[l], 's5_a_im': s5_a_im[l], 's5_log_dt': s5_log_dt[l],
            's5_b_re': s5_b_re[l], 's5_b_im': s5_b_im[l], 's5_c_re': s5_c_re[l], 's5_c_im': s5_c_im[l],
            's5_d': s5_d[l], 's5_w_glu': s5_w_glu[l], 'w_out': w_out[l],
            'ln1_g': ln1_g[l], 'ln1_b': ln1_b[l], 'w_router': w_router[l],
            'w_gate': w_gate[l], 'w_up': w_up[l], 'w_down': w_down[l],
            'ln2_g': ln2_g[l], 'ln2_b': ln2_b[l],
        }
        xp, k_l, v_l, finals = trunk_layer(xp, p, adaln(c_ctx, w_ada[l], b_ada[l]))
        new_ks.append(k_l)
        new_vs.append(v_l)
        new_ss.append(jnp.stack(finals, axis=1))
        s0_fwd = lax.complex(state_s5_re[:, l, 0].astype(jnp.float32), state_s5_im[:, l, 0].astype(jnp.float32))
        s0_bwd = lax.complex(state_s5_re[:, l, 1].astype(jnp.float32), state_s5_im[:, l, 1].astype(jnp.float32))
        xs = trunk_layer(xs, p, adaln(c, w_ada[l], b_ada[l]), ctx=(cache_k[:, l], cache_v[:, l], s0_fwd, s0_bwd))[0]
    new_k = jnp.stack(new_ks, axis=1)
    new_v = jnp.stack(new_vs, axis=1)
    new_s = jnp.stack(new_ss, axis=1)
    return (xp, xs, new_k, new_v, new_s.real, new_s.imag)
```

```python
import functools

import jax
import jax.numpy as jnp
import numpy as np
from jax import lax
from jax.experimental import pallas as pl
from jax.experimental.pallas import tpu as pltpu

F32 = jnp.float32
BF16 = jnp.bfloat16

D = 1024
N_CTX, T_CTX = 16, 256
N_LAT, T_LAT = 4, 1024
ROWS_CTX = N_CTX * T_CTX
ROWS = ROWS_CTX + N_LAT * T_LAT
DEPTH = 4
PAST = 512
GRID_W = 64
N_HEADS, N_KV, HD = 8, 2, 64
GQA = N_HEADS // N_KV
ATTN_W, KV_W = N_HEADS * HD, N_KV * HD
SCALE = HD ** -0.5
WINDOW = 128
FOUR_H, FOUR_W = 4, 256
S5_G, S5_H, S5_P, S5_W = 16, 16, 64, 256
S5_T = 16
IN_W = ATTN_W + 2 * KV_W + FOUR_W + S5_W
N_EXP, FF = 16, 1024
CAP_CTX, CAP_LAT = 2 * T_CTX // N_EXP, 2 * T_LAT // N_EXP
LN_EPS = 1e-5
NEG_INF = -1e30
ALPHA = (2 * DEPTH) ** 0.25
ROPE_BASE = 10000.0
RB = 256
N_RB = ROWS // RB
N_RB_CTX = ROWS_CTX // RB
RB_PER_LAT = T_LAT // RB
LANES = 128
MIB = 2 ** 20


def _cparams(sem, vmem_mib):
    return pltpu.CompilerParams(dimension_semantics=sem, vmem_limit_bytes=vmem_mib * MIB)


def _dot(a, b):
    return jnp.dot(a, b, preferred_element_type=F32)


def _dot_nt(a, b):
    return lax.dot_general(a, b, (((1,), (1,)), ((), ())), preferred_element_type=F32)


def _layer_norm(x, g, b):
    mu = jnp.mean(x, axis=-1, keepdims=True)
    xc = x - mu
    var = jnp.mean(xc * xc, axis=-1, keepdims=True)
    return xc * lax.rsqrt(var + LN_EPS) * g + b


def _mod_index(i):
    return jnp.where(i < N_RB_CTX, 0, 1 + (i - N_RB_CTX) // RB_PER_LAT)


def _ln_in_kernel(x_ref, g_ref, b_ref, o_ref):
    o_ref[...] = _layer_norm(x_ref[...], g_ref[...], b_ref[...])


def _ln_in(x, g, b):
    blk = 512
    return pl.pallas_call(
        _ln_in_kernel,
        grid=(ROWS // blk,),
        in_specs=[pl.BlockSpec((blk, D), lambda i: (i, 0)),
                  pl.BlockSpec((1, D), lambda i: (0, 0)),
                  pl.BlockSpec((1, D), lambda i: (0, 0))],
        out_specs=pl.BlockSpec((blk, D), lambda i: (i, 0)),
        out_shape=jax.ShapeDtypeStruct((ROWS, D), F32),
        compiler_params=_cparams(("arbitrary",), 32),
        name="ln_in",
    )(x, g.reshape(1, D), b.reshape(1, D))


def _ada_kernel(c_ref, w_ref, b_ref, o_ref):
    c = c_ref[...]
    s = (c * jax.nn.sigmoid(c)).astype(BF16)
    o_ref[0] = _dot(s, w_ref[0].astype(BF16)) + b_ref[0]


def _ada(cond8, w_ada, b_ada):
    tn = 1536
    return pl.pallas_call(
        _ada_kernel,
        grid=(DEPTH, 6 * D // tn),
        in_specs=[pl.BlockSpec((8, D), lambda l, j: (0, 0)),
                  pl.BlockSpec((1, D, tn), lambda l, j: (l, 0, j)),
                  pl.BlockSpec((1, 1, tn), lambda l, j: (l, 0, j))],
        out_specs=pl.BlockSpec((1, 8, tn), lambda l, j: (l, 0, j)),
        out_shape=jax.ShapeDtypeStruct((DEPTH, 8, 6 * D), F32),
        compiler_params=_cparams(("arbitrary", "arbitrary"), 40),
        name="adaln",
    )(cond8, w_ada, b_ada.reshape(DEPTH, 1, 6 * D))


def _inproj_kernel(x_ref, mod_ref, w_ref, q_ref, k_ref, v_ref, uf_ref, us_ref, wb_ref):
    @pl.when(pl.program_id(0) == 0)
    def _():
        wb_ref[...] = w_ref[0].astype(BF16)

    m = mod_ref[0, 0]
    h = x_ref[...] * (1.0 + m[1:2]) + m[0:1]
    p = _dot(h.astype(BF16), wb_ref[...])
    q_ref[...] = p[:, :ATTN_W]
    k_ref[...] = p[:, ATTN_W:ATTN_W + KV_W]
    v_ref[...] = p[:, ATTN_W + KV_W:ATTN_W + 2 * KV_W]
    uf_ref[...] = p[:, ATTN_W + 2 * KV_W:ATTN_W + 2 * KV_W + FOUR_W]
    us_ref[...] = p[:, ATTN_W + 2 * KV_W + FOUR_W:]


def _inproj(x, mod, w_in, l):
    widths = (ATTN_W, KV_W, KV_W, FOUR_W, S5_W)
    return pl.pallas_call(
        _inproj_kernel,
        grid=(N_RB,),
        in_specs=[pl.BlockSpec((RB, D), lambda i: (i, 0)),
                  pl.BlockSpec((1, 1, 6, D), lambda i: (l, _mod_index(i), 0, 0)),
                  pl.BlockSpec((1, D, IN_W), lambda i: (l, 0, 0))],
        out_specs=[pl.BlockSpec((RB, w), lambda i: (i, 0)) for w in widths],
        out_shape=[jax.ShapeDtypeStruct((ROWS, w), F32) for w in widths],
        scratch_shapes=[pltpu.VMEM((D, IN_W), BF16)],
        compiler_params=_cparams(("arbitrary",), 40),
        name="inproj",
    )(x, mod, w_in)


def _softmax_av(s, sink, v):
    m = jnp.maximum(jnp.max(s, axis=-1, keepdims=True), sink)
    p = jnp.exp(s - m)
    den = jnp.sum(p, axis=-1, keepdims=True) + jnp.exp(sink - m)
    return _dot(p.astype(BF16), v) / den


def _attn_ctx_kernel(q_ref, k_ref, v_ref, sink_ref, o_ref):
    outs = []
    for h in range(N_HEADS):
        kv = h // GQA
        qh = q_ref[:, h * HD:(h + 1) * HD].astype(BF16)
        kh = k_ref[:, kv * HD:(kv + 1) * HD].astype(BF16)
        vh = v_ref[:, kv * HD:(kv + 1) * HD].astype(BF16)
        s = _dot_nt(qh, kh) * SCALE
        outs.append(_softmax_av(s, sink_ref[h:h + 1, 0:1], vh))
    o_ref[...] = jnp.concatenate(outs, axis=1).astype(BF16)


def _attn_ctx(q, k, v, sink_b):
    return pl.pallas_call(
        _attn_ctx_kernel,
        grid=(N_CTX,),
        in_specs=[pl.BlockSpec((T_CTX, ATTN_W), lambda s: (s, 0)),
                  pl.BlockSpec((T_CTX, KV_W), lambda s: (s, 0)),
                  pl.BlockSpec((T_CTX, KV_W), lambda s: (s, 0)),
                  pl.BlockSpec((N_HEADS, LANES), lambda s: (0, 0))],
        out_specs=pl.BlockSpec((T_CTX, ATTN_W), lambda s: (s, 0)),
        out_shape=jax.ShapeDtypeStruct((ROWS_CTX, ATTN_W), BF16),
        compiler_params=_cparams(("arbitrary",), 32),
        name="attn_ctx",
    )(q, k, v, sink_b)


def _rope(x, cos, sin_signed):
    w = x.shape[-1]
    nxt = pltpu.roll(x, w - 1, 1)
    prv = pltpu.roll(x, 1, 1)
    lane = lax.broadcasted_iota(jnp.int32, x.shape, 1)
    swapped = jnp.where((lane & 1) == 0, nxt, prv)
    return x * cos + swapped * sin_signed


def _attn_lat_kernel(q_ref, k_ref, v_ref, ck_ref, cv_ref, cq_ref, sq_ref, ckt_ref, skt_ref, sink_ref, o_ref):
    i = pl.program_id(1)
    n_blk = T_LAT // WINDOW
    q = _rope(q_ref[...], cq_ref[...], sq_ref[...])

    def kv_block(j):
        st = pl.multiple_of(j * WINDOW, WINDOW)
        kk = _rope(k_ref[pl.ds(st, WINDOW), :], ckt_ref[pl.ds(st, WINDOW), :], skt_ref[pl.ds(st, WINDOW), :])
        return kk.astype(BF16), v_ref[pl.ds(st, WINDOW), :].astype(BF16)

    k0, v0 = kv_block(jnp.maximum(i - 1, 0))
    k1, v1 = kv_block(i)
    k2, v2 = kv_block(jnp.minimum(i + 1, n_blk - 1))
    k_all = jnp.concatenate([k0, k1, k2, ck_ref[0, 0].astype(BF16)], axis=0)
    v_all = jnp.concatenate([v0, v1, v2, cv_ref[0, 0].astype(BF16)], axis=0)
    n_keys = 3 * WINDOW + PAST
    rows = GQA * WINDOW
    r = lax.broadcasted_iota(jnp.int32, (rows, n_keys), 0) & (WINDOW - 1)
    col = lax.broadcasted_iota(jnp.int32, (rows, n_keys), 1)
    c = col & (WINDOW - 1)
    far = 2 * WINDOW
    prev_ok = c >= r + jnp.where(i > 0, 0, far)
    next_ok = c + jnp.where(i < n_blk - 1, 0, far) <= r
    valid = ((col >= WINDOW) | prev_ok) & ((col < 2 * WINDOW) | (col >= 3 * WINDOW) | next_ok)
    outs = []
    for kv in range(N_KV):
        qs = jnp.concatenate([q[:, (kv * GQA + g) * HD:(kv * GQA + g + 1) * HD] for g in range(GQA)], axis=0)
        s = _dot_nt(qs.astype(BF16), k_all[:, kv * HD:(kv + 1) * HD]) * SCALE
        s = jnp.where(valid, s, NEG_INF)
        sink = jnp.concatenate(
            [jnp.broadcast_to(sink_ref[kv * GQA + g:kv * GQA + g + 1, 0:1], (WINDOW, 1)) for g in range(GQA)], axis=0)
        o = _softmax_av(s, sink, v_all[:, kv * HD:(kv + 1) * HD])
        outs += [o[g * WINDOW:(g + 1) * WINDOW] for g in range(GQA)]
    o_ref[...] = jnp.concatenate(outs, axis=1).astype(BF16)


def _attn_lat(q, k, v, cache_k, cache_v, tabs, sink_b, l):
    cos_q, sin_q, cos_k, sin_k = tabs
    n_blk = T_LAT // WINDOW
    off = ROWS_CTX // WINDOW
    return pl.pallas_call(
        _attn_lat_kernel,
        grid=(N_LAT, n_blk),
        in_specs=[pl.BlockSpec((WINDOW, ATTN_W), lambda b, i: (off + b * n_blk + i, 0)),
                  pl.BlockSpec((T_LAT, KV_W), lambda b, i: (ROWS_CTX // T_LAT + b, 0)),
                  pl.BlockSpec((T_LAT, KV_W), lambda b, i: (ROWS_CTX // T_LAT + b, 0)),
                  pl.BlockSpec((1, 1, PAST, KV_W), lambda b, i: (b, l, 0, 0)),
                  pl.BlockSpec((1, 1, PAST, KV_W), lambda b, i: (b, l, 0, 0)),
                  pl.BlockSpec((WINDOW, ATTN_W), lambda b, i: (i, 0)),
                  pl.BlockSpec((WINDOW, ATTN_W), lambda b, i: (i, 0)),
                  pl.BlockSpec((T_LAT, KV_W), lambda b, i: (0, 0)),
                  pl.BlockSpec((T_LAT, KV_W), lambda b, i: (0, 0)),
                  pl.BlockSpec((N_HEADS, LANES), lambda b, i: (0, 0))],
        out_specs=pl.BlockSpec((WINDOW, ATTN_W), lambda b, i: (b * n_blk + i, 0)),
        out_shape=jax.ShapeDtypeStruct((N_LAT * T_LAT, ATTN_W), BF16),
        compiler_params=_cparams(("arbitrary", "arbitrary"), 40),
        name="attn_lat",
    )(q, k, v, cache_k, cache_v, cos_q, sin_q, cos_k, sin_k, sink_b)


def _rope_tables():
    rows = T_LAT // GRID_W
    row = jnp.repeat(jnp.arange(rows, dtype=F32), GRID_W)
    col = jnp.tile(jnp.arange(GRID_W, dtype=F32), rows)
    n_freq = HD // 4
    freqs = ROPE_BASE ** (-jnp.arange(n_freq, dtype=F32) / n_freq)
    ang = jnp.concatenate([row[:, None] * freqs, col[:, None] * freqs], axis=-1)
    cos = jnp.repeat(jnp.cos(ang), 2, axis=-1)
    sign = jnp.tile(jnp.array([-1.0, 1.0], F32), HD // 2)
    sin = jnp.repeat(jnp.sin(ang), 2, axis=-1) * sign
    return (jnp.tile(cos, (1, N_HEADS)), jnp.tile(sin, (1, N_HEADS)),
            jnp.tile(cos, (1, N_KV)), jnp.tile(sin, (1, N_KV)))


def _four_kernel(u_ref, cs_ref, c64_ref, s64_ref, wf_ref, o_ref):
    ub = u_ref[...].astype(BF16)
    uc = _dot(ub, c64_ref[...]).astype(BF16)
    us = _dot(ub, s64_ref[...]).astype(BF16)
    f = _dot(cs_ref[...], jnp.concatenate([uc, us], axis=0))
    o_ref[...] = _dot(f.astype(BF16), wf_ref[0].astype(BF16)).astype(BF16)


def _dft_tables(n):
    j = np.arange(n)
    ang = 2.0 * np.pi * ((j[:, None] * j[None, :]) % n) / n
    cs = np.concatenate([np.cos(ang), -np.sin(ang)], axis=1) / np.sqrt(n)
    return jnp.asarray(cs.astype(np.float32)).astype(BF16)


def _dft_channel_tables():
    j = np.arange(HD)
    ang = 2.0 * np.pi * ((j[:, None] * j[None, :]) % HD) / HD
    eye = np.eye(FOUR_H)
    c = np.kron(eye, np.cos(ang)) / np.sqrt(HD)
    s = np.kron(eye, np.sin(ang)) / np.sqrt(HD)
    return jnp.asarray(c.astype(np.float32)).astype(BF16), jnp.asarray(s.astype(np.float32)).astype(BF16)


def _fourier(uf, cs, c64, s64, w_fourier, l, n, n_seq, blk_off):
    return pl.pallas_call(
        _four_kernel,
        grid=(n_seq,),
        in_specs=[pl.BlockSpec((n, FOUR_W), lambda s: (blk_off + s, 0)),
                  pl.BlockSpec((n, 2 * n), lambda s: (0, 0)),
                  pl.BlockSpec((FOUR_W, FOUR_W), lambda s: (0, 0)),
                  pl.BlockSpec((FOUR_W, FOUR_W), lambda s: (0, 0)),
                  pl.BlockSpec((1, FOUR_W, FOUR_W), lambda s: (l, 0, 0))],
        out_specs=pl.BlockSpec((n, FOUR_W), lambda s: (s, 0)),
        out_shape=jax.ShapeDtypeStruct((n_seq * n, FOUR_W), BF16),
        compiler_params=_cparams(("arbitrary",), 40),
        name=f"fourier_{n}",
    )(uf, cs, c64, s64, w_fourier)


S5_NC_CTX, S5_NC_LAT = T_CTX // S5_T, T_LAT // S5_T
S5_LAT_PAD = 8
S5_R_CTX = S5_NC_CTX * N_CTX
S5_R_LAT = S5_NC_LAT * S5_LAT_PAD
S5_R = S5_R_CTX + S5_R_LAT


def _s5_tables(a_re, a_im, log_dt, b_re, b_im, c_re, c_im):
    a = lax.complex(a_re.astype(F32), a_im.astype(F32))
    adt = a * jnp.exp(log_dt.astype(F32))[..., None]
    abar = jnp.exp(adt)
    bbar = ((abar - 1.0) / a)[..., None] * lax.complex(b_re.astype(F32), b_im.astype(F32))
    c = lax.complex(c_re.astype(F32), c_im.astype(F32))
    kk = jnp.arange(S5_T + 1, dtype=F32)
    pw = jnp.exp(adt[:, None] * kk[None, :, None, None])
    kern = jnp.einsum('dgop,dkgp,dgpi->dkgoi', c, pw[:, :S5_T], bbar).real
    t = jnp.arange(S5_T)
    lag = t[None, :] - t[:, None]
    toeps, es, fs = [], [], []
    for d in range(2):
        lg = lag if d == 0 else -lag
        tp = jnp.where((lg >= 0)[:, :, None, None, None], kern[d][jnp.clip(lg, 0, S5_T - 1)], 0.0)
        toeps.append(tp.transpose(2, 0, 4, 1, 3).reshape(S5_G, S5_T * S5_H, S5_T * S5_H))
        e_pow = pw[d][S5_T - 1 - t] if d == 0 else pw[d][t]
        e = e_pow[:, :, :, None] * bbar[d][None]
        e = e.transpose(1, 0, 3, 2)
        es.append(jnp.concatenate([e.real, e.imag], axis=-1).reshape(S5_G, S5_T * S5_H, 2 * S5_P))
        f_pow = pw[d][1 + t] if d == 0 else pw[d][S5_T - t]
        w = c[d][None] * f_pow[:, :, None, :]
        w = jnp.concatenate([w.real, -w.imag], axis=-1)
        fs.append(w.transpose(1, 3, 0, 2).reshape(S5_G, 2 * S5_P, S5_T * S5_H))
    a_t = pw[:, S5_T]
    coef_a = jnp.concatenate([a_t.real, a_t.real], axis=-1)
    coef_b = jnp.concatenate([-a_t.imag, a_t.imag], axis=-1)
    decay = jnp.stack([coef_a, coef_b], axis=2)
    return (jnp.stack(toeps).astype(BF16), jnp.stack(es).astype(BF16), jnp.stack(fs).astype(BF16), decay)


def _s5_kernel(x_ref, t_ref, e_ref, f_ref, a_ref, s0_ref, y_ref, fin_ref, lc_ref, ll_ref, pc_ref, pq_ref):
    x = x_ref[0]
    y = jnp.zeros((S5_R, S5_T * S5_H), F32)
    for d in range(2):
        y = y + _dot(x, t_ref[d, 0])
        loc = _dot(x, e_ref[d, 0])
        lc_ref[...] = loc[:S5_R_CTX].reshape(S5_NC_CTX, N_CTX, 2 * S5_P)
        ll_ref[...] = loc[S5_R_CTX:].reshape(S5_NC_LAT, S5_LAT_PAD, 2 * S5_P)
        ca = a_ref[d, 0, 0:1, :]
        cb = a_ref[d, 0, 1:2, :]

        def scan(loc_ref, prev_ref, n_chunks, init):
            def body(j, s):
                ci = j if d == 0 else n_chunks - 1 - j
                prev_ref[ci] = s
                return s * ca + pltpu.roll(s, S5_P, 1) * cb + loc_ref[ci]
            return lax.fori_loop(0, n_chunks, body, init)

        fin_ref[0, d] = scan(lc_ref, pc_ref, S5_NC_CTX, jnp.zeros((N_CTX, 2 * S5_P), F32))
        scan(ll_ref, pq_ref, S5_NC_LAT, s0_ref[0, d])
        s_prev = jnp.concatenate([pc_ref[...].reshape(S5_R_CTX, 2 * S5_P),
                                  pq_ref[...].reshape(S5_R_LAT, 2 * S5_P)], axis=0)
        y = y + _dot(s_prev.astype(BF16), f_ref[d, 0])
    y_ref[0] = y


def _s5(xg, toep, e, f, decay, s0):
    tw = S5_T * S5_H
    return pl.pallas_call(
        _s5_kernel,
        grid=(S5_G,),
        in_specs=[pl.BlockSpec((1, S5_R, tw), lambda g: (g, 0, 0)),
                  pl.BlockSpec((2, 1, tw, tw), lambda g: (0, g, 0, 0)),
                  pl.BlockSpec((2, 1, tw, 2 * S5_P), lambda g: (0, g, 0, 0)),
                  pl.BlockSpec((2, 1, 2 * S5_P, tw), lambda g: (0, g, 0, 0)),
                  pl.BlockSpec((2, 1, 2, 2 * S5_P), lambda g: (0, g, 0, 0)),
                  pl.BlockSpec((1, 2, S5_LAT_PAD, 2 * S5_P), lambda g: (g, 0, 0, 0))],
        out_specs=[pl.BlockSpec((1, S5_R, tw), lambda g: (g, 0, 0)),
                   pl.BlockSpec((1, 2, N_CTX, 2 * S5_P), lambda g: (g, 0, 0, 0))],
        out_shape=[jax.ShapeDtypeStruct((S5_G, S5_R, tw), F32),
                   jax.ShapeDtypeStruct((S5_G, 2, N_CTX, 2 * S5_P), F32)],
        scratch_shapes=[pltpu.VMEM((S5_NC_CTX, N_CTX, 2 * S5_P), F32),
                        pltpu.VMEM((S5_NC_LAT, S5_LAT_PAD, 2 * S5_P), F32),
                        pltpu.VMEM((S5_NC_CTX, N_CTX, 2 * S5_P), F32),
                        pltpu.VMEM((S5_NC_LAT, S5_LAT_PAD, 2 * S5_P), F32)],
        compiler_params=_cparams(("arbitrary",), 32),
        name="s5",
    )(xg, toep, e, f, decay, s0)


def _s5_to_chunks(us):
    c = us[:ROWS_CTX].reshape(N_CTX, S5_NC_CTX, S5_T, S5_G, S5_H).transpose(3, 1, 0, 2, 4)
    q = us[ROWS_CTX:].reshape(N_LAT, S5_NC_LAT, S5_T, S5_G, S5_H).transpose(3, 1, 0, 2, 4)
    q = jnp.pad(q, ((0, 0), (0, 0), (0, S5_LAT_PAD - N_LAT), (0, 0), (0, 0)))
    tw = S5_T * S5_H
    return jnp.concatenate([c.reshape(S5_G, S5_R_CTX, tw), q.reshape(S5_G, S5_R_LAT, tw)], axis=1).astype(BF16)


def _s5_from_chunks(y):
    c = y[:, :S5_R_CTX].reshape(S5_G, S5_NC_CTX, N_CTX, S5_T, S5_H).transpose(2, 1, 3, 0, 4)
    q = y[:, S5_R_CTX:].reshape(S5_G, S5_NC_LAT, S5_LAT_PAD, S5_T, S5_H)[:, :, :N_LAT].transpose(2, 1, 3, 0, 4)
    return jnp.concatenate([c.reshape(ROWS_CTX, S5_W), q.reshape(ROWS - ROWS_CTX, S5_W)], axis=0)


def _split_bf16(a):
    hi = a.astype(BF16)
    return hi, (a - hi.astype(F32)).astype(BF16)


def _outproj_kernel(at_ref, fo_ref, ys_ref, us_ref, d_ref, wg_ref, wo_ref, x_ref, mod_ref, g_ref, b_ref, wr_ref,
                    x1_ref, h2_ref, lg_ref, wob_ref):
    @pl.when(pl.program_id(0) == 0)
    def _():
        wob_ref[...] = wo_ref[0].astype(BF16)

    g = jax.nn.gelu(ys_ref[...] + us_ref[...] * d_ref[0])
    s5 = g * jax.nn.sigmoid(_dot(g.astype(BF16), wg_ref[0].astype(BF16)))
    mix = (_dot(at_ref[...], wob_ref[0:ATTN_W, :])
           + _dot(fo_ref[...], wob_ref[ATTN_W:ATTN_W + FOUR_W, :])
           + _dot(s5.astype(BF16), wob_ref[ATTN_W + FOUR_W:, :]))
    m = mod_ref[0, 0]
    x1 = _layer_norm(ALPHA * x_ref[...] + m[2:3] * mix, g_ref[0], b_ref[0])
    x1_ref[...] = x1
    h2 = x1 * (1.0 + m[4:5]) + m[3:4]
    h2_ref[...] = h2.astype(BF16)
    hh, hl = _split_bf16(h2)
    wh, wl = _split_bf16(wr_ref[0])
    lg_ref[...] = _dot_nt(wh, hh) + _dot_nt(wh, hl) + _dot_nt(wl, hh)


def _outproj(attn, four, ys5, us, s5_d, w_glu, w_out, x, mod, ln_g, ln_b, w_router_t, l):
    row = lambda w: pl.BlockSpec((RB, w), lambda i: (i, 0))
    lay = lambda *shape: pl.BlockSpec((1,) + shape, lambda i: (l,) + (0,) * len(shape))
    return pl.pallas_call(
        _outproj_kernel,
        grid=(N_RB,),
        in_specs=[row(ATTN_W), row(FOUR_W), row(S5_W), row(S5_W), lay(1, S5_W), lay(S5_W, S5_W), lay(D, D), row(D),
                  pl.BlockSpec((1, 1, 6, D), lambda i: (l, _mod_index(i), 0, 0)),
                  lay(1, D), lay(1, D), lay(N_EXP, D)],
        out_specs=[row(D), row(D), pl.BlockSpec((N_EXP, RB), lambda i: (0, i))],
        out_shape=[jax.ShapeDtypeStruct((ROWS, D), F32), jax.ShapeDtypeStruct((ROWS, D), BF16),
                   jax.ShapeDtypeStruct((N_EXP, ROWS), F32)],
        scratch_shapes=[pltpu.VMEM((D, D), BF16)],
        compiler_params=_cparams(("arbitrary",), 40),
        name="outproj",
    )(attn, four, ys5, us, s5_d, w_glu, w_out, x, mod, ln_g, ln_b, w_router_t)


def _router_kernel(lg_ref, crow_ref, ccol_ref, gcol_ref, *, n, cap):
    lg = lg_ref[...]
    e = jnp.exp(lg - jnp.max(lg, axis=0, keepdims=True))
    aff = e / jnp.sum(e, axis=0, keepdims=True)
    thr_bits = jnp.zeros((N_EXP, 1), jnp.int32)
    for bit in range(30, -1, -1):
        cand = thr_bits | (1 << bit)
        cnt = jnp.sum(jnp.where(aff >= lax.bitcast_convert_type(cand, F32), 1.0, 0.0), axis=1, keepdims=True)
        thr_bits = jnp.where(cnt >= cap, cand, thr_bits)
    thr = lax.bitcast_convert_type(thr_bits, F32)
    above = aff > thr
    tied = aff == thr
    need = cap - jnp.sum(jnp.where(above, 1.0, 0.0), axis=1, keepdims=True)
    r0 = lax.broadcasted_iota(jnp.int32, (n, n), 0)
    r1 = lax.broadcasted_iota(jnp.int32, (n, n), 1)
    before = jnp.where(r0 < r1, 1.0, 0.0).astype(BF16)
    eye = jnp.where(r0 == r1, 1.0, 0.0).astype(BF16)
    tied_rank = _dot(jnp.where(tied, 1.0, 0.0).astype(BF16), before)
    sel = above | (tied & (tied_rank < need))
    slot = _dot(jnp.where(sel, 1.0, 0.0).astype(BF16), before)
    code = jnp.where(sel, slot + 1.0, 0.0)
    crow_ref[...] = code
    pad = jnp.zeros((LANES - N_EXP, n), F32)
    ccol_ref[...] = _dot_nt(eye, jnp.concatenate([code, pad], axis=0).astype(BF16))
    gate = jnp.where(sel, aff, 0.0)
    gcol_ref[...] = _dot_nt(eye, jnp.concatenate([gate, pad], axis=0).astype(BF16))


def _router(lg, n, n_seq, blk_off, cap):
    return pl.pallas_call(
        functools.partial(_router_kernel, n=n, cap=cap),
        grid=(n_seq,),
        in_specs=[pl.BlockSpec((N_EXP, n), lambda s: (0, blk_off + s))],
        out_specs=[pl.BlockSpec((N_EXP, n), lambda s: (0, s)),
                   pl.BlockSpec((n, LANES), lambda s: (s, 0)),
                   pl.BlockSpec((n, LANES), lambda s: (s, 0))],
        out_shape=[jax.ShapeDtypeStruct((N_EXP, n_seq * n), F32),
                   jax.ShapeDtypeStruct((n_seq * n, LANES), F32),
                   jax.ShapeDtypeStruct((n_seq * n, LANES), F32)],
        compiler_params=_cparams(("arbitrary",), 40),
        name=f"router_{n}",
    )(lg)


def _gather_kernel(crow_ref, h_ref, o_ref, *, n, cap):
    code = crow_ref[...]
    slot1 = (lax.broadcasted_iota(jnp.int32, (cap, n), 0) + 1).astype(F32)
    onehot = jnp.concatenate(
        [jnp.where(code[e:e + 1, :] == slot1, 1.0, 0.0).astype(BF16) for e in range(N_EXP)], axis=0)
    xs = _dot(onehot, h_ref[...])
    o_ref[...] = xs.reshape(N_EXP, cap, D).astype(BF16)


def _gather(crow, h2, n, n_seq, blk_off, cap):
    return pl.pallas_call(
        functools.partial(_gather_kernel, n=n, cap=cap),
        grid=(n_seq,),
        in_specs=[pl.BlockSpec((N_EXP, n), lambda s: (0, s)),
                  pl.BlockSpec((n, D), lambda s: (blk_off + s, 0))],
        out_specs=pl.BlockSpec((N_EXP, cap, D), lambda s: (0, s, 0)),
        out_shape=jax.ShapeDtypeStruct((N_EXP, n_seq * cap, D), BF16),
        compiler_params=_cparams(("arbitrary",), 48),
        name=f"gather_{n}",
    )(crow, h2)


FF_BLK = 512
SLOTS = N_CTX * CAP_CTX


def _ffn_kernel(xc_ref, xq_ref, wg_ref, wu_ref, wd_ref, yc_ref, yq_ref, acc_ref):
    f = pl.program_id(1)
    x = jnp.concatenate([xc_ref[0], xq_ref[0]], axis=0)
    g = _dot(x, wg_ref[0, 0].astype(BF16))
    u = _dot(x, wu_ref[0, 0].astype(BF16))
    hid = (g * jax.nn.sigmoid(g) * u).astype(BF16)
    y = _dot(hid, wd_ref[0, 0].astype(BF16))

    @pl.when(f == 0)
    def _():
        acc_ref[...] = y

    @pl.when(f == FF // FF_BLK - 1)
    def _():
        tot = acc_ref[...] + y
        yc_ref[0] = tot[:SLOTS].astype(BF16)
        yq_ref[0] = tot[SLOTS:].astype(BF16)


def _ffn(xs_c, xs_q, w_gate, w_up, w_down, l):
    assert FF // FF_BLK == 2
    xspec = pl.BlockSpec((1, SLOTS, D), lambda e, f: (e, 0, 0))
    return pl.pallas_call(
        _ffn_kernel,
        grid=(N_EXP, FF // FF_BLK),
        in_specs=[xspec, xspec,
                  pl.BlockSpec((1, 1, D, FF_BLK), lambda e, f: (l, e, 0, f)),
                  pl.BlockSpec((1, 1, D, FF_BLK), lambda e, f: (l, e, 0, f)),
                  pl.BlockSpec((1, 1, FF_BLK, D), lambda e, f: (l, e, f, 0))],
        out_specs=[xspec, xspec],
        out_shape=[jax.ShapeDtypeStruct((N_EXP, SLOTS, D), BF16)] * 2,
        scratch_shapes=[pltpu.VMEM((2 * SLOTS, D), F32)],
        compiler_params=_cparams(("arbitrary", "arbitrary"), 48),
        name="ffn",
    )(xs_c, xs_q, w_gate, w_up, w_down)


def _combine_kernel(ccol_ref, gcol_ref, ys_ref, x_ref, mod_ref, g_ref, b_ref, o_ref, *, cap):
    width = N_EXP * cap
    k = lax.broadcasted_iota(jnp.int32, (LANES, width), 0)
    j = lax.broadcasted_iota(jnp.int32, (LANES, width), 1)
    shift = cap.bit_length() - 1
    spread = jnp.where((j >> shift) == k, 1.0, 0.0).astype(BF16)
    code = _dot(ccol_ref[...].astype(BF16), spread)
    gate = _dot(gcol_ref[...].astype(BF16), spread)
    slot1 = ((lax.broadcasted_iota(jnp.int32, (1, width), 1) & (cap - 1)) + 1).astype(F32)
    weights = jnp.where(code == slot1, gate, 0.0).astype(BF16)
    moe = _dot(weights, ys_ref[...].reshape(width, D))
    m = mod_ref[0, 0]
    o_ref[...] = _layer_norm(ALPHA * x_ref[...] + m[5:6] * moe, g_ref[0], b_ref[0])


def _combine(ccol, gcol, ys, x1, mod, ln_g, ln_b, l, cap, n_seq, rb_per_seq, rb_off):
    lay = lambda *shape: pl.BlockSpec((1,) + shape, lambda s, r: (l,) + (0,) * len(shape))
    return pl.pallas_call(
        functools.partial(_combine_kernel, cap=cap),
        grid=(n_seq, rb_per_seq),
        in_specs=[pl.BlockSpec((RB, LANES), lambda s, r: (s * rb_per_seq + r, 0)),
                  pl.BlockSpec((RB, LANES), lambda s, r: (s * rb_per_seq + r, 0)),
                  pl.BlockSpec((N_EXP, cap, D), lambda s, r: (0, s, 0)),
                  pl.BlockSpec((RB, D), lambda s, r: (rb_off + s * rb_per_seq + r, 0)),
                  pl.BlockSpec((1, 1, 6, D), lambda s, r: (l, _mod_index(rb_off + s * rb_per_seq + r), 0, 0)),
                  lay(1, D), lay(1, D)],
        out_specs=pl.BlockSpec((RB, D), lambda s, r: (s * rb_per_seq + r, 0)),
        out_shape=jax.ShapeDtypeStruct((n_seq * rb_per_seq * RB, D), F32),
        compiler_params=_cparams(("arbitrary", "arbitrary"), 40),
        name=f"combine_{cap}",
    )(ccol, gcol, ys, x1, mod, ln_g, ln_b)


def kernel(x_prompt, x_sample, cache_k, cache_v, state_s5_re, state_s5_im, c, c_ctx, ln_in_g, ln_in_b, w_ada, b_ada,
           w_in, w_fourier, attn_sink, s5_a_re, s5_a_im, s5_log_dt, s5_b_re, s5_b_im, s5_c_re, s5_c_im, s5_d,
           s5_w_glu, w_out, ln1_g, ln1_b, w_router, w_gate, w_up, w_down, ln2_g, ln2_b):
    x = jnp.concatenate([x_prompt.reshape(ROWS_CTX, D), x_sample.reshape(ROWS - ROWS_CTX, D)], axis=0)
    x = _ln_in(x, ln_in_g, ln_in_b)
    cond8 = jnp.concatenate([c_ctx[None], c, jnp.zeros((8 - 1 - N_LAT, D), F32)], axis=0)
    mod = _ada(cond8, w_ada, b_ada).reshape(DEPTH, 8, 6, D)

    rope_tabs = _rope_tables()
    cs_ctx, cs_lat = _dft_tables(T_CTX), _dft_tables(T_LAT)
    c64, s64 = _dft_channel_tables()
    cache_k = cache_k.reshape(N_LAT, DEPTH, PAST, KV_W)
    cache_v = cache_v.reshape(N_LAT, DEPTH, PAST, KV_W)
    w_router_t = jnp.swapaxes(w_router, 1, 2)
    sink_b = jnp.broadcast_to(attn_sink[:, :, None], (DEPTH, N_HEADS, LANES))
    s0 = jnp.concatenate([state_s5_re, state_s5_im], axis=-1).transpose(1, 3, 2, 0, 4)
    s0 = jnp.pad(s0, ((0, 0), (0, 0), (0, 0), (0, S5_LAT_PAD - N_LAT), (0, 0)))
    ln1_g, ln1_b, ln2_g, ln2_b, s5_d = (a.reshape(DEPTH, 1, -1) for a in (ln1_g, ln1_b, ln2_g, ln2_b, s5_d))

    new_k, new_v, new_s = [], [], []
    for l in range(DEPTH):
        q, k, v, uf, us = _inproj(x, mod, w_in, l)
        new_k.append(k[:ROWS_CTX].reshape(N_CTX, T_CTX, N_KV, HD))
        new_v.append(v[:ROWS_CTX].reshape(N_CTX, T_CTX, N_KV, HD))
        attn = jnp.concatenate([_attn_ctx(q, k, v, sink_b[l]),
                                _attn_lat(q, k, v, cache_k, cache_v, rope_tabs, sink_b[l], l)], axis=0)
        four = jnp.concatenate([_fourier(uf, cs_ctx, c64, s64, w_fourier, l, T_CTX, N_CTX, 0),
                                _fourier(uf, cs_lat, c64, s64, w_fourier, l, T_LAT, N_LAT, ROWS_CTX // T_LAT)], axis=0)
        toep, e_op, f_op, decay = _s5_tables(s5_a_re[l], s5_a_im[l], s5_log_dt[l], s5_b_re[l], s5_b_im[l],
                                             s5_c_re[l], s5_c_im[l])
        y5, fin = _s5(_s5_to_chunks(us), toep, e_op, f_op, decay, s0[l])
        new_s.append(fin.transpose(2, 1, 0, 3))
        x1, h2, lg = _outproj(attn, four, _s5_from_chunks(y5), us, s5_d, s5_w_glu, w_out, x, mod, ln1_g, ln1_b,
                              w_router_t, l)
        crow_c, ccol_c, gcol_c = _router(lg, T_CTX, N_CTX, 0, CAP_CTX)
        crow_q, ccol_q, gcol_q = _router(lg, T_LAT, N_LAT, ROWS_CTX // T_LAT, CAP_LAT)
        xs_c = _gather(crow_c, h2, T_CTX, N_CTX, 0, CAP_CTX)
        xs_q = _gather(crow_q, h2, T_LAT, N_LAT, ROWS_CTX // T_LAT, CAP_LAT)
        ys_c, ys_q = _ffn(xs_c, xs_q, w_gate, w_up, w_down, l)
        x = jnp.concatenate([
            _combine(ccol_c, gcol_c, ys_c, x1, mod, ln2_g, ln2_b, l, CAP_CTX, N_CTX, 1, 0),
            _combine(ccol_q, gcol_q, ys_q, x1, mod, ln2_g, ln2_b, l, CAP_LAT, N_LAT, RB_PER_LAT, N_RB_CTX)], axis=0)

    new_s = jnp.stack(new_s, axis=1)
    return (x[:ROWS_CTX].reshape(N_CTX, T_CTX, D), x[ROWS_CTX:].reshape(N_LAT, T_LAT, D),
            jnp.stack(new_k, axis=1), jnp.stack(new_v, axis=1), new_s[..., :S5_P], new_s[..., S5_P:])
```

```python
import functools

import jax
import jax.numpy as jnp
import numpy as np
from jax import lax
from jax.experimental import pallas as pl
from jax.experimental.pallas import tpu as pltpu

F32 = jnp.float32
BF16 = jnp.bfloat16

D = 1024
N_CTX, T_CTX = 16, 256
N_LAT, T_LAT = 4, 1024
ROWS_CTX = N_CTX * T_CTX
ROWS = ROWS_CTX + N_LAT * T_LAT
DEPTH = 4
PAST = 512
GRID_W = 64
N_HEADS, N_KV, HD = 8, 2, 64
GQA = N_HEADS // N_KV
ATTN_W, KV_W = N_HEADS * HD, N_KV * HD
SCALE = HD ** -0.5
WINDOW = 128
FOUR_H, FOUR_W = 4, 256
S5_G, S5_H, S5_P, S5_W = 16, 16, 64, 256
S5_T = 16
IN_W = ATTN_W + 2 * KV_W + FOUR_W + S5_W
N_EXP, FF = 16, 1024
CAP_CTX, CAP_LAT = 2 * T_CTX // N_EXP, 2 * T_LAT // N_EXP
LN_EPS = 1e-5
NEG_INF = -1e30
ALPHA = (2 * DEPTH) ** 0.25
ROPE_BASE = 10000.0
RB = 256
N_RB = ROWS // RB
N_RB_CTX = ROWS_CTX // RB
RB_PER_LAT = T_LAT // RB
LANES = 128
MIB = 2 ** 20
HIGHEST = lax.Precision.HIGHEST


def _cparams(sem, vmem_mib):
    return pltpu.CompilerParams(dimension_semantics=sem, vmem_limit_bytes=vmem_mib * MIB)


def _dot(a, b):
    return jnp.dot(a, b, preferred_element_type=F32)


def _dot_nt(a, b):
    return lax.dot_general(a, b, (((1,), (1,)), ((), ())), preferred_element_type=F32)


def _layer_norm(x, g, b):
    mu = jnp.mean(x, axis=-1, keepdims=True)
    xc = x - mu
    var = jnp.mean(xc * xc, axis=-1, keepdims=True)
    return xc * lax.rsqrt(var + LN_EPS) * g + b


def _mod_index(i):
    return jnp.where(i < N_RB_CTX, 0, 1 + (i - N_RB_CTX) // RB_PER_LAT)


def _chained(kernel_fn):
    def wrapped(_, *refs):
        kernel_fn(*refs)
    return wrapped


_ANY = pl.BlockSpec(memory_space=pl.ANY)

_HALVES_SHAPE = jax.ShapeDtypeStruct((2, ROWS, LANES), F32)


def _halves_spec(rows, row_block):
    return pl.BlockSpec((2, rows, LANES), lambda *ids: (0, row_block(*ids), 0))


def _ln_in_kernel(x_ref, g_ref, b_ref, o_ref):
    o_ref[...] = _layer_norm(x_ref[...], g_ref[...], b_ref[...])


def _ln_in(x, g, b):
    blk = 512
    return pl.pallas_call(
        _ln_in_kernel,
        grid=(ROWS // blk,),
        in_specs=[pl.BlockSpec((blk, D), lambda i: (i, 0)),
                  pl.BlockSpec((1, D), lambda i: (0, 0)),
                  pl.BlockSpec((1, D), lambda i: (0, 0))],
        out_specs=pl.BlockSpec((blk, D), lambda i: (i, 0)),
        out_shape=jax.ShapeDtypeStruct((ROWS, D), F32),
        compiler_params=_cparams(("arbitrary",), 32),
        name="ln_in",
    )(x, g.reshape(1, D), b.reshape(1, D))


def _ada_kernel(c_ref, w_ref, b_ref, o_ref):
    c = c_ref[...]
    s = (c * jax.nn.sigmoid(c)).astype(BF16)
    o_ref[0] = _dot(s, w_ref[0].astype(BF16)) + b_ref[0]


def _ada(cond8, w_ada, b_ada):
    tn = 1536
    return pl.pallas_call(
        _ada_kernel,
        grid=(DEPTH, 6 * D // tn),
        in_specs=[pl.BlockSpec((8, D), lambda l, j: (0, 0)),
                  pl.BlockSpec((1, D, tn), lambda l, j: (l, 0, j)),
                  pl.BlockSpec((1, 1, tn), lambda l, j: (l, 0, j))],
        out_specs=pl.BlockSpec((1, 8, tn), lambda l, j: (l, 0, j)),
        out_shape=jax.ShapeDtypeStruct((DEPTH, 8, 6 * D), F32),
        compiler_params=_cparams(("arbitrary", "arbitrary"), 40),
        name="adaln",
    )(cond8, w_ada, b_ada.reshape(DEPTH, 1, 6 * D))


def _inproj_kernel(x_ref, mod_ref, w_ref, q_ref, k_ref, v_ref, uf_ref, us_ref, wb_ref):
    @pl.when(pl.program_id(0) == 0)
    def _():
        wb_ref[...] = w_ref[0].astype(BF16)

    m = mod_ref[0, 0]
    h = x_ref[...] * (1.0 + m[1:2]) + m[0:1]
    p = _dot(h.astype(BF16), wb_ref[...])
    q_ref[...] = p[:, :ATTN_W]
    k_ref[...] = p[:, ATTN_W:ATTN_W + KV_W]
    v_ref[...] = p[:, ATTN_W + KV_W:ATTN_W + 2 * KV_W]
    uf_ref[...] = p[:, ATTN_W + 2 * KV_W:ATTN_W + 2 * KV_W + FOUR_W]
    for half in range(S5_W // LANES):
        lo = ATTN_W + 2 * KV_W + FOUR_W + half * LANES
        us_ref[half] = p[:, lo:lo + LANES]


def _inproj(x, mod, w_in, l):
    widths = (ATTN_W, KV_W, KV_W, FOUR_W)
    return pl.pallas_call(
        _inproj_kernel,
        grid=(N_RB,),
        in_specs=[pl.BlockSpec((RB, D), lambda i: (i, 0)),
                  pl.BlockSpec((1, 1, 6, D), lambda i: (l, _mod_index(i), 0, 0)),
                  pl.BlockSpec((1, D, IN_W), lambda i: (l, 0, 0))],
        out_specs=[pl.BlockSpec((RB, w), lambda i: (i, 0)) for w in widths] + [_halves_spec(RB, lambda i: i)],
        out_shape=[jax.ShapeDtypeStruct((ROWS, w), F32) for w in widths] + [_HALVES_SHAPE],
        scratch_shapes=[pltpu.VMEM((D, IN_W), BF16)],
        compiler_params=_cparams(("arbitrary",), 40),
        name="inproj",
    )(x, mod, w_in)


def _softmax_av(s, sink, v):
    m = jnp.maximum(jnp.max(s, axis=-1, keepdims=True), sink)
    p = jnp.exp(s - m)
    den = jnp.sum(p, axis=-1, keepdims=True) + jnp.exp(sink - m)
    return _dot(p.astype(BF16), v) / den


def _attn_ctx_kernel(q_ref, k_ref, v_ref, sink_ref, o_ref):
    outs = []
    for h in range(N_HEADS):
        kv = h // GQA
        qh = q_ref[:, h * HD:(h + 1) * HD].astype(BF16)
        kh = k_ref[:, kv * HD:(kv + 1) * HD].astype(BF16)
        vh = v_ref[:, kv * HD:(kv + 1) * HD].astype(BF16)
        s = _dot_nt(qh, kh) * SCALE
        outs.append(_softmax_av(s, sink_ref[h:h + 1, 0:1], vh))
    o_ref[...] = jnp.concatenate(outs, axis=1).astype(BF16)


def _attn_ctx(q, k, v, sink_b):
    return pl.pallas_call(
        _attn_ctx_kernel,
        grid=(N_CTX,),
        in_specs=[pl.BlockSpec((T_CTX, ATTN_W), lambda s: (s, 0)),
                  pl.BlockSpec((T_CTX, KV_W), lambda s: (s, 0)),
                  pl.BlockSpec((T_CTX, KV_W), lambda s: (s, 0)),
                  pl.BlockSpec((N_HEADS, LANES), lambda s: (0, 0))],
        out_specs=pl.BlockSpec((T_CTX, ATTN_W), lambda s: (s, 0)),
        out_shape=jax.ShapeDtypeStruct((ROWS, ATTN_W), BF16),
        compiler_params=_cparams(("arbitrary",), 32),
        name="attn_ctx",
    )(q, k, v, sink_b)


def _rope(x, cos, sin_signed):
    w = x.shape[-1]
    nxt = pltpu.roll(x, w - 1, 1)
    prv = pltpu.roll(x, 1, 1)
    lane = lax.broadcasted_iota(jnp.int32, x.shape, 1)
    swapped = jnp.where((lane & 1) == 0, nxt, prv)
    return x * cos + swapped * sin_signed


def _attn_lat_kernel(q_ref, k_ref, v_ref, ck_ref, cv_ref, cq_ref, sq_ref, ckt_ref, skt_ref, sink_ref, o_ref):
    i = pl.program_id(1)
    n_blk = T_LAT // WINDOW
    q = _rope(q_ref[...], cq_ref[...], sq_ref[...])

    def kv_block(j):
        st = pl.multiple_of(j * WINDOW, WINDOW)
        kk = _rope(k_ref[pl.ds(st, WINDOW), :], ckt_ref[pl.ds(st, WINDOW), :], skt_ref[pl.ds(st, WINDOW), :])
        return kk.astype(BF16), v_ref[pl.ds(st, WINDOW), :].astype(BF16)

    k0, v0 = kv_block(jnp.maximum(i - 1, 0))
    k1, v1 = kv_block(i)
    k2, v2 = kv_block(jnp.minimum(i + 1, n_blk - 1))
    k_all = jnp.concatenate([k0, k1, k2, ck_ref[0, 0].astype(BF16)], axis=0)
    v_all = jnp.concatenate([v0, v1, v2, cv_ref[0, 0].astype(BF16)], axis=0)
    n_keys = 3 * WINDOW + PAST
    rows = GQA * WINDOW
    r = lax.broadcasted_iota(jnp.int32, (rows, n_keys), 0) & (WINDOW - 1)
    col = lax.broadcasted_iota(jnp.int32, (rows, n_keys), 1)
    c = col & (WINDOW - 1)
    far = 2 * WINDOW
    prev_ok = c >= r + jnp.where(i > 0, 0, far)
    next_ok = c + jnp.where(i < n_blk - 1, 0, far) <= r
    valid = ((col >= WINDOW) | prev_ok) & ((col < 2 * WINDOW) | (col >= 3 * WINDOW) | next_ok)
    outs = []
    for kv in range(N_KV):
        qs = jnp.concatenate([q[:, (kv * GQA + g) * HD:(kv * GQA + g + 1) * HD] for g in range(GQA)], axis=0)
        s = _dot_nt(qs.astype(BF16), k_all[:, kv * HD:(kv + 1) * HD]) * SCALE
        s = jnp.where(valid, s, NEG_INF)
        sink = jnp.concatenate(
            [jnp.broadcast_to(sink_ref[kv * GQA + g:kv * GQA + g + 1, 0:1], (WINDOW, 1)) for g in range(GQA)], axis=0)
        o = _softmax_av(s, sink, v_all[:, kv * HD:(kv + 1) * HD])
        outs += [o[g * WINDOW:(g + 1) * WINDOW] for g in range(GQA)]
    o_ref[...] = jnp.concatenate(outs, axis=1).astype(BF16)


def _attn_lat(attn, q, k, v, cache_k, cache_v, tabs, sink_b, l):
    cos_q, sin_q, cos_k, sin_k = tabs
    n_blk = T_LAT // WINDOW
    off = ROWS_CTX // WINDOW
    return pl.pallas_call(
        _chained(_attn_lat_kernel),
        grid=(N_LAT, n_blk),
        in_specs=[_ANY,
                  pl.BlockSpec((WINDOW, ATTN_W), lambda b, i: (off + b * n_blk + i, 0)),
                  pl.BlockSpec((T_LAT, KV_W), lambda b, i: (ROWS_CTX // T_LAT + b, 0)),
                  pl.BlockSpec((T_LAT, KV_W), lambda b, i: (ROWS_CTX // T_LAT + b, 0)),
                  pl.BlockSpec((1, 1, PAST, KV_W), lambda b, i: (b, l, 0, 0)),
                  pl.BlockSpec((1, 1, PAST, KV_W), lambda b, i: (b, l, 0, 0)),
                  pl.BlockSpec((WINDOW, ATTN_W), lambda b, i: (i, 0)),
                  pl.BlockSpec((WINDOW, ATTN_W), lambda b, i: (i, 0)),
                  pl.BlockSpec((T_LAT, KV_W), lambda b, i: (0, 0)),
                  pl.BlockSpec((T_LAT, KV_W), lambda b, i: (0, 0)),
                  pl.BlockSpec((N_HEADS, LANES), lambda b, i: (0, 0))],
        out_specs=pl.BlockSpec((WINDOW, ATTN_W), lambda b, i: (off + b * n_blk + i, 0)),
        out_shape=jax.ShapeDtypeStruct((ROWS, ATTN_W), BF16),
        input_output_aliases={0: 0},
        compiler_params=_cparams(("arbitrary", "arbitrary"), 40),
        name="attn_lat",
    )(attn, q, k, v, cache_k, cache_v, cos_q, sin_q, cos_k, sin_k, sink_b)


def _rope_tables():
    rows = T_LAT // GRID_W
    row = jnp.repeat(jnp.arange(rows, dtype=F32), GRID_W)
    col = jnp.tile(jnp.arange(GRID_W, dtype=F32), rows)
    n_freq = HD // 4
    freqs = ROPE_BASE ** (-jnp.arange(n_freq, dtype=F32) / n_freq)
    ang = jnp.concatenate([row[:, None] * freqs, col[:, None] * freqs], axis=-1)
    cos = jnp.repeat(jnp.cos(ang), 2, axis=-1)
    sign = jnp.tile(jnp.array([-1.0, 1.0], F32), HD // 2)
    sin = jnp.repeat(jnp.sin(ang), 2, axis=-1) * sign
    return (jnp.tile(cos, (1, N_HEADS)), jnp.tile(sin, (1, N_HEADS)),
            jnp.tile(cos, (1, N_KV)), jnp.tile(sin, (1, N_KV)))


def _four_kernel(u_ref, cs_ref, c64_ref, s64_ref, wf_ref, o_ref):
    ub = u_ref[...].astype(BF16)
    uc = _dot(ub, c64_ref[...]).astype(BF16)
    us = _dot(ub, s64_ref[...]).astype(BF16)
    f = _dot(cs_ref[...], jnp.concatenate([uc, us], axis=0))
    o_ref[...] = _dot(f.astype(BF16), wf_ref[0].astype(BF16)).astype(BF16)


def _dft_tables(n):
    j = np.arange(n)
    ang = 2.0 * np.pi * ((j[:, None] * j[None, :]) % n) / n
    cs = np.concatenate([np.cos(ang), -np.sin(ang)], axis=1) / np.sqrt(n)
    return jnp.asarray(cs.astype(np.float32)).astype(BF16)


def _dft_channel_tables():
    j = np.arange(HD)
    ang = 2.0 * np.pi * ((j[:, None] * j[None, :]) % HD) / HD
    eye = np.eye(FOUR_H)
    c = np.kron(eye, np.cos(ang)) / np.sqrt(HD)
    s = np.kron(eye, np.sin(ang)) / np.sqrt(HD)
    return jnp.asarray(c.astype(np.float32)).astype(BF16), jnp.asarray(s.astype(np.float32)).astype(BF16)


def _fourier(prev, uf, cs, c64, s64, w_fourier, l, n, n_seq, blk_off):
    chained = prev is not None
    specs = [pl.BlockSpec((n, FOUR_W), lambda s: (blk_off + s, 0)),
             pl.BlockSpec((n, 2 * n), lambda s: (0, 0)),
             pl.BlockSpec((FOUR_W, FOUR_W), lambda s: (0, 0)),
             pl.BlockSpec((FOUR_W, FOUR_W), lambda s: (0, 0)),
             pl.BlockSpec((1, FOUR_W, FOUR_W), lambda s: (l, 0, 0))]
    args = (uf, cs, c64, s64, w_fourier)
    return pl.pallas_call(
        _chained(_four_kernel) if chained else _four_kernel,
        grid=(n_seq,),
        in_specs=([_ANY] if chained else []) + specs,
        out_specs=pl.BlockSpec((n, FOUR_W), lambda s: (blk_off + s, 0)),
        out_shape=jax.ShapeDtypeStruct((ROWS, FOUR_W), BF16),
        input_output_aliases={0: 0} if chained else {},
        compiler_params=_cparams(("arbitrary",), 40),
        name=f"fourier_{n}",
    )(*(((prev,) if chained else ()) + args))


S5_TW = S5_T * S5_H
S5_HALVES = 2
S5_R = 128
S5_TOK = S5_R * S5_T


def _s5_tables(a_re, a_im, log_dt, b_re, b_im, c_re, c_im):
    f = lambda a: a.astype(F32)
    a_re, a_im, b_re, b_im, c_re, c_im = map(f, (a_re, a_im, b_re, b_im, c_re, c_im))
    dt = jnp.exp(f(log_dt))[..., None]
    x, y = a_re * dt, a_im * dt
    kk = jnp.arange(S5_T + 1, dtype=F32)[None, None, :, None, None]
    mag = jnp.exp(kk * x[:, :, None])
    pr, pi = mag * jnp.cos(kk * y[:, :, None]), mag * jnp.sin(kk * y[:, :, None])
    nr, ni = pr[:, :, 1] - 1.0, pi[:, :, 1]
    den = a_re * a_re + a_im * a_im
    qr, qi = (nr * a_re + ni * a_im) / den, (ni * a_re - nr * a_im) / den
    bb_r = qr[..., None] * b_re - qi[..., None] * b_im
    bb_i = qr[..., None] * b_im + qi[..., None] * b_re
    t = jnp.arange(S5_T)
    pk_r, pk_i = pr[:, :, :S5_T, :, None, :], pi[:, :, :S5_T, :, None, :]
    m_r = c_re[:, :, None] * pk_r - c_im[:, :, None] * pk_i
    m_i = c_re[:, :, None] * pk_i + c_im[:, :, None] * pk_r
    kern = (jnp.einsum('ldkgop,ldgpi->ldkgoi', m_r, bb_r, precision=HIGHEST)
            - jnp.einsum('ldkgop,ldgpi->ldkgoi', m_i, bb_i, precision=HIGHEST))
    kern = jnp.stack([kern[:, 0], kern[:, 1, ::-1]], axis=1)
    ktab = kern.transpose(0, 1, 3, 5, 2, 4).reshape(DEPTH, 2, S5_G, S5_H, S5_TW)
    idx_e = jnp.stack([S5_T - 1 - t, t])
    er = jnp.stack([pr[:, d][:, idx_e[d]] for d in range(2)], axis=1)
    ei = jnp.stack([pi[:, d][:, idx_e[d]] for d in range(2)], axis=1)
    e_r = er[..., None] * bb_r[:, :, None] - ei[..., None] * bb_i[:, :, None]
    e_i = er[..., None] * bb_i[:, :, None] + ei[..., None] * bb_r[:, :, None]
    e_r, e_i = (a.transpose(0, 1, 3, 2, 5, 4) for a in (e_r, e_i))
    shape_e = (DEPTH, 2, S5_G, S5_TW, 2 * S5_P)
    e_op = jnp.concatenate([e_r, e_i], axis=-1).reshape(shape_e).astype(BF16)
    esw_op = jnp.concatenate([e_i, e_r], axis=-1).reshape(shape_e).astype(BF16)
    idx_f = jnp.stack([1 + t, S5_T - t])
    fr = jnp.stack([pr[:, d][:, idx_f[d]] for d in range(2)], axis=1)[:, :, :, :, None, :]
    fi = jnp.stack([pi[:, d][:, idx_f[d]] for d in range(2)], axis=1)[:, :, :, :, None, :]
    w_r = c_re[:, :, None] * fr - c_im[:, :, None] * fi
    w_i = c_re[:, :, None] * fi + c_im[:, :, None] * fr
    f_op = jnp.concatenate([w_r, -w_i], axis=-1).transpose(0, 1, 3, 5, 2, 4)
    f_op = f_op.reshape(DEPTH, 2, S5_G, 2 * S5_P, S5_TW).astype(BF16)
    ar, ai = pr[:, :, S5_T], pi[:, :, S5_T]
    cat = lambda u, v: jnp.concatenate([u, v], axis=-1)
    decay = jnp.stack([cat(ar, ar), cat(-ai, ai), cat(ai, -ai), jnp.zeros_like(cat(ar, ar))], axis=3)
    return ktab, e_op, esw_op, f_op, decay


def _block_transpose(arrs):
    arrs = list(arrs)
    width = arrs[0].shape[1]
    blk = lax.broadcasted_iota(jnp.int32, arrs[0].shape, 1) >> 4
    s = len(arrs) // 2
    while s >= 1:
        keep = (blk & s) == 0
        for i in range(len(arrs)):
            if i & s:
                continue
            lo, hi = arrs[i], arrs[i + s]
            arrs[i] = jnp.where(keep, lo, pltpu.roll(hi, s * S5_H, 1))
            arrs[i + s] = jnp.where(keep, pltpu.roll(lo, width - s * S5_H, 1), hi)
        s //= 2
    return arrs


def _s5_kernel(us_ref, kt_ref, e_ref, esw_ref, f_ref, a_ref, s0_ref, y_ref, fin_ref,
               x_scr, tp_scr, y_scr, loc_scr, lsw_scr, prev_scr, *, nseq, nc):
    rows = pl.ds
    x_by_step = [jnp.concatenate([us_ref[0, rows(tl, S5_R, stride=S5_T), :],
                                  us_ref[1, rows(tl, S5_R, stride=S5_T), :]], axis=1) for tl in range(S5_T)]
    for g, xg in enumerate(_block_transpose(x_by_step)):
        x_scr[g] = xg.astype(BF16)
    lane = lax.broadcasted_iota(jnp.int32, (S5_H, S5_TW), 1)
    for d in range(2):
        for g in range(S5_G):
            kt = kt_ref[0, d, g]
            for ti in range(S5_T):
                if d == 0:
                    sh = S5_H * ti
                    blk_rows = jnp.where(lane >= sh, pltpu.roll(kt, sh, 1) if sh else kt, 0.0)
                else:
                    sh = S5_H * (S5_T - 1 - ti)
                    blk_rows = jnp.where(lane < S5_TW - sh, pltpu.roll(kt, S5_TW - sh, 1) if sh else kt, 0.0)
                tp_scr[g, ti * S5_H:(ti + 1) * S5_H, :] = blk_rows.astype(BF16)
            xg = x_scr[g]
            loc_scr[g] = _dot(xg, e_ref[0, d, g])
            lsw_scr[g] = _dot(xg, esw_ref[0, d, g])
        batch = 8
        for g0 in range(0, S5_G, batch):
            coefs = [[jnp.broadcast_to(a_ref[0, d, g, r:r + 1, :], (nseq, 2 * S5_P)) for r in range(3)]
                     for g in range(g0, g0 + batch)]
            init = tuple((s0_ref[0, d, g, :, :2 * S5_P], s0_ref[0, d, g, :, 2 * S5_P:]) for g in range(g0, g0 + batch))

            def body(j, carry):
                ci = j if d == 0 else nc - 1 - j
                out = []
                for idx in range(batch):
                    g = g0 + idx
                    s, t = carry[idx]
                    ca, cb, cc = coefs[idx]
                    prev_scr[g, rows(ci, nseq, stride=nc), :] = s
                    out.append((s * ca + t * cb + loc_scr[g, rows(ci, nseq, stride=nc), :],
                                t * ca + s * cc + lsw_scr[g, rows(ci, nseq, stride=nc), :]))
                return tuple(out)

            fin = lax.fori_loop(0, nc, body, init)
            for idx in range(batch):
                fin_ref[0, d, g0 + idx] = fin[idx][0]
        for g in range(S5_G):
            yg = _dot(x_scr[g], tp_scr[g]) + _dot(prev_scr[g].astype(BF16), f_ref[0, d, g])
            if d == 0:
                y_scr[g] = yg
            else:
                y_scr[g] += yg
    for to, y_to in enumerate(_block_transpose([y_scr[g] for g in range(S5_G)])):
        y_ref[0, rows(to, S5_R, stride=S5_T), :] = y_to[:, :LANES]
        y_ref[1, rows(to, S5_R, stride=S5_T), :] = y_to[:, LANES:]


def _s5(prev, us, tabs, s0, l, nseq_total, nc, blk_off):
    chained = prev is not None
    nseq = nseq_total // S5_HALVES
    assert nseq * nc == S5_R
    lay = lambda *shape: pl.BlockSpec((1,) + shape, lambda h: (l,) + (0,) * len(shape))
    specs = [_halves_spec(S5_TOK, lambda h: blk_off + h),
             lay(2, S5_G, S5_H, S5_TW), lay(2, S5_G, S5_TW, 2 * S5_P), lay(2, S5_G, S5_TW, 2 * S5_P),
             lay(2, S5_G, 2 * S5_P, S5_TW), lay(2, S5_G, 4, 2 * S5_P),
             pl.BlockSpec((1, 2, S5_G, nseq, 4 * S5_P), lambda h: (h, 0, 0, 0, 0))]
    kern = functools.partial(_s5_kernel, nseq=nseq, nc=nc)
    scan_buf = pltpu.VMEM((S5_G, S5_R, 2 * S5_P), F32)
    return pl.pallas_call(
        _chained(kern) if chained else kern,
        grid=(S5_HALVES,),
        in_specs=([_ANY] if chained else []) + specs,
        out_specs=[_halves_spec(S5_TOK, lambda h: blk_off + h),
                   pl.BlockSpec((1, 2, S5_G, nseq, 2 * S5_P), lambda h: (h, 0, 0, 0, 0))],
        out_shape=[_HALVES_SHAPE,
                   jax.ShapeDtypeStruct((S5_HALVES, 2, S5_G, nseq, 2 * S5_P), F32)],
        scratch_shapes=[pltpu.VMEM((S5_G, S5_R, S5_TW), BF16), pltpu.VMEM((S5_G, S5_TW, S5_TW), BF16),
                        pltpu.VMEM((S5_G, S5_R, S5_TW), F32), scan_buf, scan_buf, scan_buf],
        input_output_aliases={0: 0} if chained else {},
        compiler_params=_cparams(("arbitrary",), 48),
        name=f"s5_{nc}",
    )(*(((prev,) if chained else ()) + (us,) + tuple(tabs) + (s0,)))


def _split_bf16(a):
    hi = a.astype(BF16)
    return hi, (a - hi.astype(F32)).astype(BF16)


def _outproj_kernel(at_ref, fo_ref, ys_ref, us_ref, d_ref, wg_ref, wo_ref, x_ref, mod_ref, g_ref, b_ref, wr_ref,
                    x1_ref, h2_ref, lg_ref, wob_ref):
    @pl.when(pl.program_id(0) == 0)
    def _():
        wob_ref[...] = wo_ref[0].astype(BF16)

    ys = jnp.concatenate([ys_ref[0], ys_ref[1]], axis=1)
    us = jnp.concatenate([us_ref[0], us_ref[1]], axis=1)
    g = jax.nn.gelu(ys + us * d_ref[0])
    s5 = g * jax.nn.sigmoid(_dot(g.astype(BF16), wg_ref[0].astype(BF16)))
    mix = (_dot(at_ref[...], wob_ref[0:ATTN_W, :])
           + _dot(fo_ref[...], wob_ref[ATTN_W:ATTN_W + FOUR_W, :])
           + _dot(s5.astype(BF16), wob_ref[ATTN_W + FOUR_W:, :]))
    m = mod_ref[0, 0]
    x1 = _layer_norm(ALPHA * x_ref[...] + m[2:3] * mix, g_ref[0], b_ref[0])
    x1_ref[...] = x1
    h2 = x1 * (1.0 + m[4:5]) + m[3:4]
    h2_ref[...] = h2.astype(BF16)
    hh, hl = _split_bf16(h2)
    wh, wl = _split_bf16(wr_ref[0])
    lg_ref[...] = _dot_nt(wh, hh) + _dot_nt(wh, hl) + _dot_nt(wl, hh)


def _outproj(attn, four, ys5, us, s5_d, w_glu, w_out, x, mod, ln_g, ln_b, w_router_t, l):
    row = lambda w: pl.BlockSpec((RB, w), lambda i: (i, 0))
    lay = lambda *shape: pl.BlockSpec((1,) + shape, lambda i: (l,) + (0,) * len(shape))
    return pl.pallas_call(
        _outproj_kernel,
        grid=(N_RB,),
        in_specs=[row(ATTN_W), row(FOUR_W), _halves_spec(RB, lambda i: i), _halves_spec(RB, lambda i: i), lay(1, S5_W), lay(S5_W, S5_W), lay(D, D), row(D),
                  pl.BlockSpec((1, 1, 6, D), lambda i: (l, _mod_index(i), 0, 0)),
                  lay(1, D), lay(1, D), lay(N_EXP, D)],
        out_specs=[row(D), row(D), pl.BlockSpec((N_EXP, RB), lambda i: (0, i))],
        out_shape=[jax.ShapeDtypeStruct((ROWS, D), F32), jax.ShapeDtypeStruct((ROWS, D), BF16),
                   jax.ShapeDtypeStruct((N_EXP, ROWS), F32)],
        scratch_shapes=[pltpu.VMEM((D, D), BF16)],
        compiler_params=_cparams(("arbitrary",), 40),
        name="outproj",
    )(attn, four, ys5, us, s5_d, w_glu, w_out, x, mod, ln_g, ln_b, w_router_t)


def _router_kernel(lg_ref, crow_ref, ccol_ref, gcol_ref, *, n, cap):
    lg = lg_ref[...]
    e = jnp.exp(lg - jnp.max(lg, axis=0, keepdims=True))
    aff = e / jnp.sum(e, axis=0, keepdims=True)
    thr_bits = jnp.zeros((N_EXP, 1), jnp.int32)
    for bit in range(30, -1, -1):
        cand = thr_bits | (1 << bit)
        cnt = jnp.sum(jnp.where(aff >= lax.bitcast_convert_type(cand, F32), 1.0, 0.0), axis=1, keepdims=True)
        thr_bits = jnp.where(cnt >= cap, cand, thr_bits)
    thr = lax.bitcast_convert_type(thr_bits, F32)
    above = aff > thr
    tied = aff == thr
    need = cap - jnp.sum(jnp.where(above, 1.0, 0.0), axis=1, keepdims=True)
    r0 = lax.broadcasted_iota(jnp.int32, (n, n), 0)
    r1 = lax.broadcasted_iota(jnp.int32, (n, n), 1)
    before = jnp.where(r0 < r1, 1.0, 0.0).astype(BF16)
    eye = jnp.where(r0 == r1, 1.0, 0.0).astype(BF16)
    tied_rank = _dot(jnp.where(tied, 1.0, 0.0).astype(BF16), before)
    sel = above | (tied & (tied_rank < need))
    slot = _dot(jnp.where(sel, 1.0, 0.0).astype(BF16), before)
    code = jnp.where(sel, slot + 1.0, 0.0)
    crow_ref[...] = code
    pad = jnp.zeros((LANES - N_EXP, n), F32)
    ccol_ref[...] = _dot_nt(eye, jnp.concatenate([code, pad], axis=0).astype(BF16))
    gate = jnp.where(sel, aff, 0.0)
    gcol_ref[...] = _dot_nt(eye, jnp.concatenate([gate, pad], axis=0).astype(BF16))


def _router(lg, n, n_seq, blk_off, cap):
    return pl.pallas_call(
        functools.partial(_router_kernel, n=n, cap=cap),
        grid=(n_seq,),
        in_specs=[pl.BlockSpec((N_EXP, n), lambda s: (0, blk_off + s))],
        out_specs=[pl.BlockSpec((N_EXP, n), lambda s: (0, s)),
                   pl.BlockSpec((n, LANES), lambda s: (s, 0)),
                   pl.BlockSpec((n, LANES), lambda s: (s, 0))],
        out_shape=[jax.ShapeDtypeStruct((N_EXP, n_seq * n), F32),
                   jax.ShapeDtypeStruct((n_seq * n, LANES), F32),
                   jax.ShapeDtypeStruct((n_seq * n, LANES), F32)],
        compiler_params=_cparams(("arbitrary",), 40),
        name=f"router_{n}",
    )(lg)


def _gather_kernel(crow_ref, h_ref, o_ref, *, n, cap):
    code = crow_ref[...]
    slot1 = (lax.broadcasted_iota(jnp.int32, (cap, n), 0) + 1).astype(F32)
    onehot = jnp.concatenate(
        [jnp.where(code[e:e + 1, :] == slot1, 1.0, 0.0).astype(BF16) for e in range(N_EXP)], axis=0)
    xs = _dot(onehot, h_ref[...])
    o_ref[...] = xs.reshape(N_EXP, cap, D).astype(BF16)


def _gather(crow, h2, n, n_seq, blk_off, cap):
    return pl.pallas_call(
        functools.partial(_gather_kernel, n=n, cap=cap),
        grid=(n_seq,),
        in_specs=[pl.BlockSpec((N_EXP, n), lambda s: (0, s)),
                  pl.BlockSpec((n, D), lambda s: (blk_off + s, 0))],
        out_specs=pl.BlockSpec((N_EXP, cap, D), lambda s: (0, s, 0)),
        out_shape=jax.ShapeDtypeStruct((N_EXP, n_seq * cap, D), BF16),
        compiler_params=_cparams(("arbitrary",), 48),
        name=f"gather_{n}",
    )(crow, h2)


FF_BLK = 512
SLOTS = N_CTX * CAP_CTX


def _ffn_kernel(xc_ref, xq_ref, wg_ref, wu_ref, wd_ref, yc_ref, yq_ref, acc_ref):
    f = pl.program_id(1)
    x = jnp.concatenate([xc_ref[0], xq_ref[0]], axis=0)
    g = _dot(x, wg_ref[0, 0].astype(BF16))
    u = _dot(x, wu_ref[0, 0].astype(BF16))
    hid = (g * jax.nn.sigmoid(g) * u).astype(BF16)
    y = _dot(hid, wd_ref[0, 0].astype(BF16))

    @pl.when(f == 0)
    def _():
        acc_ref[...] = y

    @pl.when(f == FF // FF_BLK - 1)
    def _():
        tot = acc_ref[...] + y
        yc_ref[0] = tot[:SLOTS].astype(BF16)
        yq_ref[0] = tot[SLOTS:].astype(BF16)


def _ffn(xs_c, xs_q, w_gate, w_up, w_down, l):
    assert FF // FF_BLK == 2
    xspec = pl.BlockSpec((1, SLOTS, D), lambda e, f: (e, 0, 0))
    return pl.pallas_call(
        _ffn_kernel,
        grid=(N_EXP, FF // FF_BLK),
        in_specs=[xspec, xspec,
                  pl.BlockSpec((1, 1, D, FF_BLK), lambda e, f: (l, e, 0, f)),
                  pl.BlockSpec((1, 1, D, FF_BLK), lambda e, f: (l, e, 0, f)),
                  pl.BlockSpec((1, 1, FF_BLK, D), lambda e, f: (l, e, f, 0))],
        out_specs=[xspec, xspec],
        out_shape=[jax.ShapeDtypeStruct((N_EXP, SLOTS, D), BF16)] * 2,
        scratch_shapes=[pltpu.VMEM((2 * SLOTS, D), F32)],
        compiler_params=_cparams(("arbitrary", "arbitrary"), 48),
        name="ffn",
    )(xs_c, xs_q, w_gate, w_up, w_down)


def _combine_kernel(ccol_ref, gcol_ref, ys_ref, x_ref, mod_ref, g_ref, b_ref, o_ref, *, cap):
    width = N_EXP * cap
    k = lax.broadcasted_iota(jnp.int32, (LANES, width), 0)
    j = lax.broadcasted_iota(jnp.int32, (LANES, width), 1)
    shift = cap.bit_length() - 1
    spread = jnp.where((j >> shift) == k, 1.0, 0.0).astype(BF16)
    code = _dot(ccol_ref[...].astype(BF16), spread)
    gate = _dot(gcol_ref[...].astype(BF16), spread)
    slot1 = ((lax.broadcasted_iota(jnp.int32, (1, width), 1) & (cap - 1)) + 1).astype(F32)
    weights = jnp.where(code == slot1, gate, 0.0).astype(BF16)
    moe = _dot(weights, ys_ref[...].reshape(width, D))
    m = mod_ref[0, 0]
    o_ref[...] = _layer_norm(ALPHA * x_ref[...] + m[5:6] * moe, g_ref[0], b_ref[0])


def _combine(prev, ccol, gcol, ys, x1, mod, ln_g, ln_b, l, cap, n_seq, rb_per_seq, rb_off):
    chained = prev is not None
    lay = lambda *shape: pl.BlockSpec((1,) + shape, lambda s, r: (l,) + (0,) * len(shape))
    specs = [pl.BlockSpec((RB, LANES), lambda s, r: (s * rb_per_seq + r, 0)),
             pl.BlockSpec((RB, LANES), lambda s, r: (s * rb_per_seq + r, 0)),
             pl.BlockSpec((N_EXP, cap, D), lambda s, r: (0, s, 0)),
             pl.BlockSpec((RB, D), lambda s, r: (rb_off + s * rb_per_seq + r, 0)),
             pl.BlockSpec((1, 1, 6, D), lambda s, r: (l, _mod_index(rb_off + s * rb_per_seq + r), 0, 0)),
             lay(1, D), lay(1, D)]
    kern = functools.partial(_combine_kernel, cap=cap)
    return pl.pallas_call(
        _chained(kern) if chained else kern,
        grid=(n_seq, rb_per_seq),
        in_specs=([_ANY] if chained else []) + specs,
        out_specs=pl.BlockSpec((RB, D), lambda s, r: (rb_off + s * rb_per_seq + r, 0)),
        out_shape=jax.ShapeDtypeStruct((ROWS, D), F32),
        input_output_aliases={0: 0} if chained else {},
        compiler_params=_cparams(("arbitrary", "arbitrary"), 40),
        name=f"combine_{cap}",
    )(*(((prev,) if chained else ()) + (ccol, gcol, ys, x1, mod, ln_g, ln_b)))


def kernel(x_prompt, x_sample, cache_k, cache_v, state_s5_re, state_s5_im, c, c_ctx, ln_in_g, ln_in_b, w_ada, b_ada,
           w_in, w_fourier, attn_sink, s5_a_re, s5_a_im, s5_log_dt, s5_b_re, s5_b_im, s5_c_re, s5_c_im, s5_d,
           s5_w_glu, w_out, ln1_g, ln1_b, w_router, w_gate, w_up, w_down, ln2_g, ln2_b):
    x = jnp.concatenate([x_prompt.reshape(ROWS_CTX, D), x_sample.reshape(ROWS - ROWS_CTX, D)], axis=0)
    x = _ln_in(x, ln_in_g, ln_in_b)
    cond8 = jnp.concatenate([c_ctx[None], c, jnp.zeros((8 - 1 - N_LAT, D), F32)], axis=0)
    mod = _ada(cond8, w_ada, b_ada).reshape(DEPTH, 8, 6, D)

    rope_tabs = _rope_tables()
    cs_ctx, cs_lat = _dft_tables(T_CTX), _dft_tables(T_LAT)
    c64, s64 = _dft_channel_tables()
    cache_k = cache_k.reshape(N_LAT, DEPTH, PAST, KV_W)
    cache_v = cache_v.reshape(N_LAT, DEPTH, PAST, KV_W)
    w_router_t = jnp.swapaxes(w_router, 1, 2)
    sink_b = jnp.broadcast_to(attn_sink[:, :, None], (DEPTH, N_HEADS, LANES))
    s5_tabs = _s5_tables(s5_a_re, s5_a_im, s5_log_dt, s5_b_re, s5_b_im, s5_c_re, s5_c_im)
    s0 = jnp.concatenate([state_s5_re, state_s5_im, state_s5_im, state_s5_re], axis=-1)
    s0 = s0.reshape(S5_HALVES, N_LAT // S5_HALVES, DEPTH, 2, S5_G, 4 * S5_P).transpose(2, 0, 3, 4, 1, 5)
    s0_ctx = jnp.zeros((S5_HALVES, 2, S5_G, N_CTX // S5_HALVES, 4 * S5_P), F32)
    ln1_g, ln1_b, ln2_g, ln2_b, s5_d = (a.reshape(DEPTH, 1, -1) for a in (ln1_g, ln1_b, ln2_g, ln2_b, s5_d))

    new_k, new_v, new_s = [], [], []
    for l in range(DEPTH):
        q, k, v, uf, us = _inproj(x, mod, w_in, l)
        new_k.append(k[:ROWS_CTX].reshape(N_CTX, T_CTX, N_KV, HD))
        new_v.append(v[:ROWS_CTX].reshape(N_CTX, T_CTX, N_KV, HD))
        attn = _attn_ctx(q, k, v, sink_b[l])
        attn = _attn_lat(attn, q, k, v, cache_k, cache_v, rope_tabs, sink_b[l], l)
        four = _fourier(None, uf, cs_ctx, c64, s64, w_fourier, l, T_CTX, N_CTX, 0)
        four = _fourier(four, uf, cs_lat, c64, s64, w_fourier, l, T_LAT, N_LAT, ROWS_CTX // T_LAT)
        y5, fin = _s5(None, us, s5_tabs, s0_ctx, l, N_CTX, T_CTX // S5_T, 0)
        y5, _ = _s5(y5, us, s5_tabs, s0[l], l, N_LAT, T_LAT // S5_T, ROWS_CTX // S5_TOK)
        new_s.append(fin.transpose(0, 3, 1, 2, 4).reshape(N_CTX, 2, S5_G, 2 * S5_P))
        x1, h2, lg = _outproj(attn, four, y5, us, s5_d, s5_w_glu, w_out, x, mod, ln1_g, ln1_b, w_router_t, l)
        crow_c, ccol_c, gcol_c = _router(lg, T_CTX, N_CTX, 0, CAP_CTX)
        crow_q, ccol_q, gcol_q = _router(lg, T_LAT, N_LAT, ROWS_CTX // T_LAT, CAP_LAT)
        xs_c = _gather(crow_c, h2, T_CTX, N_CTX, 0, CAP_CTX)
        xs_q = _gather(crow_q, h2, T_LAT, N_LAT, ROWS_CTX // T_LAT, CAP_LAT)
        ys_c, ys_q = _ffn(xs_c, xs_q, w_gate, w_up, w_down, l)
        x = _combine(None, ccol_c, gcol_c, ys_c, x1, mod, ln2_g, ln2_b, l, CAP_CTX, N_CTX, 1, 0)
        x = _combine(x, ccol_q, gcol_q, ys_q, x1, mod, ln2_g, ln2_b, l, CAP_LAT, N_LAT, RB_PER_LAT, N_RB_CTX)

    new_s = jnp.stack(new_s, axis=1)
    return (x[:ROWS_CTX].reshape(N_CTX, T_CTX, D), x[ROWS_CTX:].reshape(N_LAT, T_LAT, D),
            jnp.stack(new_k, axis=1), jnp.stack(new_v, axis=1), new_s[..., :S5_P], new_s[..., S5_P:])
```

```python
import functools

import jax
import jax.numpy as jnp
import numpy as np
from jax import lax
from jax.experimental import pallas as pl
from jax.experimental.pallas import tpu as pltpu

F32 = jnp.float32
BF16 = jnp.bfloat16

D = 1024
N_CTX, T_CTX = 16, 256
N_LAT, T_LAT = 4, 1024
ROWS_CTX = N_CTX * T_CTX
ROWS = ROWS_CTX + N_LAT * T_LAT
DEPTH = 4
PAST = 512
GRID_W = 64
N_HEADS, N_KV, HD = 8, 2, 64
GQA = N_HEADS // N_KV
ATTN_W, KV_W = N_HEADS * HD, N_KV * HD
SCALE = HD ** -0.5
WINDOW = 128
FOUR_H, FOUR_W = 4, 256
S5_G, S5_H, S5_P, S5_W = 16, 16, 64, 256
S5_T = 16
IN_W = ATTN_W + 2 * KV_W + FOUR_W + S5_W
N_EXP, FF = 16, 1024
CAP_CTX, CAP_LAT = 2 * T_CTX // N_EXP, 2 * T_LAT // N_EXP
LN_EPS = 1e-5
NEG_INF = -1e30
ALPHA = (2 * DEPTH) ** 0.25
ROPE_BASE = 10000.0
RB_IN = 1024
RB_OUT = 512
RB_COMBINE_LAT = 512
LANES = 128
MIB = 2 ** 20
HIGHEST = lax.Precision.HIGHEST


def _cparams(sem, vmem_mib):
    return pltpu.CompilerParams(dimension_semantics=sem, vmem_limit_bytes=vmem_mib * MIB)


def _dot(a, b):
    return jnp.dot(a, b, preferred_element_type=F32)


def _dot_nt(a, b):
    return lax.dot_general(a, b, (((1,), (1,)), ((), ())), preferred_element_type=F32)


def _layer_norm(x, g, b):
    mu = jnp.mean(x, axis=-1, keepdims=True)
    xc = x - mu
    var = jnp.mean(xc * xc, axis=-1, keepdims=True)
    return xc * lax.rsqrt(var + LN_EPS) * g + b


def _mod_row(block, block_rows):
    first = block * block_rows
    return jnp.where(first < ROWS_CTX, 0, 1 + (first - ROWS_CTX) // T_LAT)


def _chained(kernel_fn):
    def wrapped(_, *refs):
        kernel_fn(*refs)
    return wrapped


_ANY = pl.BlockSpec(memory_space=pl.ANY)

_HALVES_SHAPE = jax.ShapeDtypeStruct((2, ROWS, LANES), F32)


def _halves_spec(rows, row_block):
    return pl.BlockSpec((2, rows, LANES), lambda *ids: (0, row_block(*ids), 0))


def _ln_in_kernel(x_ref, g_ref, b_ref, o_ref):
    o_ref[...] = _layer_norm(x_ref[...], g_ref[...], b_ref[...])


def _ln_in(x, g, b):
    blk = 512
    return pl.pallas_call(
        _ln_in_kernel,
        grid=(ROWS // blk,),
        in_specs=[pl.BlockSpec((blk, D), lambda i: (i, 0)),
                  pl.BlockSpec((1, D), lambda i: (0, 0)),
                  pl.BlockSpec((1, D), lambda i: (0, 0))],
        out_specs=pl.BlockSpec((blk, D), lambda i: (i, 0)),
        out_shape=jax.ShapeDtypeStruct((ROWS, D), F32),
        compiler_params=_cparams(("arbitrary",), 32),
        name="ln_in",
    )(x, g.reshape(1, D), b.reshape(1, D))


def _ada_kernel(c_ref, w_ref, b_ref, o_ref):
    c = c_ref[...]
    s = (c * jax.nn.sigmoid(c)).astype(BF16)
    o_ref[0] = _dot(s, w_ref[0].astype(BF16)) + b_ref[0]


def _ada(cond8, w_ada, b_ada):
    tn = 1536
    return pl.pallas_call(
        _ada_kernel,
        grid=(DEPTH, 6 * D // tn),
        in_specs=[pl.BlockSpec((8, D), lambda l, j: (0, 0)),
                  pl.BlockSpec((1, D, tn), lambda l, j: (l, 0, j)),
                  pl.BlockSpec((1, 1, tn), lambda l, j: (l, 0, j))],
        out_specs=pl.BlockSpec((1, 8, tn), lambda l, j: (l, 0, j)),
        out_shape=jax.ShapeDtypeStruct((DEPTH, 8, 6 * D), F32),
        compiler_params=_cparams(("arbitrary", "arbitrary"), 40),
        name="adaln",
    )(cond8, w_ada, b_ada.reshape(DEPTH, 1, 6 * D))


def _inproj_kernel(x_ref, mod_ref, w_ref, q_ref, k_ref, v_ref, uf_ref, us_ref, wb_ref):
    @pl.when(pl.program_id(0) == 0)
    def _():
        wb_ref[...] = w_ref[0].astype(BF16)

    m = mod_ref[0, 0]
    h = x_ref[...] * (1.0 + m[1:2]) + m[0:1]
    p = _dot(h.astype(BF16), wb_ref[...])
    q_ref[...] = p[:, :ATTN_W]
    k_ref[...] = p[:, ATTN_W:ATTN_W + KV_W]
    v_ref[...] = p[:, ATTN_W + KV_W:ATTN_W + 2 * KV_W]
    uf_ref[...] = p[:, ATTN_W + 2 * KV_W:ATTN_W + 2 * KV_W + FOUR_W]
    for half in range(S5_W // LANES):
        lo = ATTN_W + 2 * KV_W + FOUR_W + half * LANES
        us_ref[half] = p[:, lo:lo + LANES]


def _inproj(x, mod, w_in, l):
    widths = (ATTN_W, KV_W, KV_W, FOUR_W)
    return pl.pallas_call(
        _inproj_kernel,
        grid=(ROWS // RB_IN,),
        in_specs=[pl.BlockSpec((RB_IN, D), lambda i: (i, 0)),
                  pl.BlockSpec((1, 1, 6, D), lambda i: (l, _mod_row(i, RB_IN), 0, 0)),
                  pl.BlockSpec((1, D, IN_W), lambda i: (l, 0, 0))],
        out_specs=[pl.BlockSpec((RB_IN, w), lambda i: (i, 0)) for w in widths] + [_halves_spec(RB_IN, lambda i: i)],
        out_shape=[jax.ShapeDtypeStruct((ROWS, w), F32) for w in widths] + [_HALVES_SHAPE],
        scratch_shapes=[pltpu.VMEM((D, IN_W), BF16)],
        compiler_params=_cparams(("arbitrary",), 56),
        name="inproj",
    )(x, mod, w_in)


def _softmax_av(s, sink, v):
    m = jnp.maximum(jnp.max(s, axis=-1, keepdims=True), sink)
    p = jnp.exp(s - m)
    den = jnp.sum(p, axis=-1, keepdims=True) + jnp.exp(sink - m)
    return _dot(p.astype(BF16), v) / den


def _attn_ctx_kernel(q_ref, k_ref, v_ref, sink_ref, o_ref):
    outs = []
    for h in range(N_HEADS):
        kv = h // GQA
        qh = q_ref[:, h * HD:(h + 1) * HD].astype(BF16)
        kh = k_ref[:, kv * HD:(kv + 1) * HD].astype(BF16)
        vh = v_ref[:, kv * HD:(kv + 1) * HD].astype(BF16)
        s = _dot_nt(qh, kh) * SCALE
        outs.append(_softmax_av(s, sink_ref[h:h + 1, 0:1], vh))
    o_ref[...] = jnp.concatenate(outs, axis=1).astype(BF16)


def _attn_ctx(q, k, v, sink_b):
    return pl.pallas_call(
        _attn_ctx_kernel,
        grid=(N_CTX,),
        in_specs=[pl.BlockSpec((T_CTX, ATTN_W), lambda s: (s, 0)),
                  pl.BlockSpec((T_CTX, KV_W), lambda s: (s, 0)),
                  pl.BlockSpec((T_CTX, KV_W), lambda s: (s, 0)),
                  pl.BlockSpec((N_HEADS, LANES), lambda s: (0, 0))],
        out_specs=pl.BlockSpec((T_CTX, ATTN_W), lambda s: (s, 0)),
        out_shape=jax.ShapeDtypeStruct((ROWS, ATTN_W), BF16),
        compiler_params=_cparams(("arbitrary",), 32),
        name="attn_ctx",
    )(q, k, v, sink_b)


def _rope(x, cos, sin_signed):
    w = x.shape[-1]
    nxt = pltpu.roll(x, w - 1, 1)
    prv = pltpu.roll(x, 1, 1)
    lane = lax.broadcasted_iota(jnp.int32, x.shape, 1)
    swapped = jnp.where((lane & 1) == 0, nxt, prv)
    return x * cos + swapped * sin_signed


def _attn_lat_kernel(q_ref, k_ref, v_ref, ck_ref, cv_ref, cq_ref, sq_ref, ckt_ref, skt_ref, sink_ref, o_ref):
    i = pl.program_id(1)
    n_blk = T_LAT // WINDOW
    q = _rope(q_ref[...], cq_ref[...], sq_ref[...])

    def kv_block(j):
        st = pl.multiple_of(j * WINDOW, WINDOW)
        kk = _rope(k_ref[pl.ds(st, WINDOW), :], ckt_ref[pl.ds(st, WINDOW), :], skt_ref[pl.ds(st, WINDOW), :])
        return kk.astype(BF16), v_ref[pl.ds(st, WINDOW), :].astype(BF16)

    k0, v0 = kv_block(jnp.maximum(i - 1, 0))
    k1, v1 = kv_block(i)
    k2, v2 = kv_block(jnp.minimum(i + 1, n_blk - 1))
    k_all = jnp.concatenate([k0, k1, k2, ck_ref[0, 0].astype(BF16)], axis=0)
    v_all = jnp.concatenate([v0, v1, v2, cv_ref[0, 0].astype(BF16)], axis=0)
    n_keys = 3 * WINDOW + PAST
    rows = GQA * WINDOW
    r = lax.broadcasted_iota(jnp.int32, (rows, n_keys), 0) & (WINDOW - 1)
    col = lax.broadcasted_iota(jnp.int32, (rows, n_keys), 1)
    c = col & (WINDOW - 1)
    far = 2 * WINDOW
    prev_ok = c >= r + jnp.where(i > 0, 0, far)
    next_ok = c + jnp.where(i < n_blk - 1, 0, far) <= r
    valid = ((col >= WINDOW) | prev_ok) & ((col < 2 * WINDOW) | (col >= 3 * WINDOW) | next_ok)
    outs = []
    for kv in range(N_KV):
        qs = jnp.concatenate([q[:, (kv * GQA + g) * HD:(kv * GQA + g + 1) * HD] for g in range(GQA)], axis=0)
        s = _dot_nt(qs.astype(BF16), k_all[:, kv * HD:(kv + 1) * HD]) * SCALE
        s = jnp.where(valid, s, NEG_INF)
        sink = jnp.concatenate(
            [jnp.broadcast_to(sink_ref[kv * GQA + g:kv * GQA + g + 1, 0:1], (WINDOW, 1)) for g in range(GQA)], axis=0)
        o = _softmax_av(s, sink, v_all[:, kv * HD:(kv + 1) * HD])
        outs += [o[g * WINDOW:(g + 1) * WINDOW] for g in range(GQA)]
    o_ref[...] = jnp.concatenate(outs, axis=1).astype(BF16)


def _attn_lat(attn, q, k, v, cache_k, cache_v, tabs, sink_b, l):
    cos_q, sin_q, cos_k, sin_k = tabs
    n_blk = T_LAT // WINDOW
    off = ROWS_CTX // WINDOW
    return pl.pallas_call(
        _chained(_attn_lat_kernel),
        grid=(N_LAT, n_blk),
        in_specs=[_ANY,
                  pl.BlockSpec((WINDOW, ATTN_W), lambda b, i: (off + b * n_blk + i, 0)),
                  pl.BlockSpec((T_LAT, KV_W), lambda b, i: (ROWS_CTX // T_LAT + b, 0)),
                  pl.BlockSpec((T_LAT, KV_W), lambda b, i: (ROWS_CTX // T_LAT + b, 0)),
                  pl.BlockSpec((1, 1, PAST, KV_W), lambda b, i: (b, l, 0, 0)),
                  pl.BlockSpec((1, 1, PAST, KV_W), lambda b, i: (b, l, 0, 0)),
                  pl.BlockSpec((WINDOW, ATTN_W), lambda b, i: (i, 0)),
                  pl.BlockSpec((WINDOW, ATTN_W), lambda b, i: (i, 0)),
                  pl.BlockSpec((T_LAT, KV_W), lambda b, i: (0, 0)),
                  pl.BlockSpec((T_LAT, KV_W), lambda b, i: (0, 0)),
                  pl.BlockSpec((N_HEADS, LANES), lambda b, i: (0, 0))],
        out_specs=pl.BlockSpec((WINDOW, ATTN_W), lambda b, i: (off + b * n_blk + i, 0)),
        out_shape=jax.ShapeDtypeStruct((ROWS, ATTN_W), BF16),
        input_output_aliases={0: 0},
        compiler_params=_cparams(("arbitrary", "arbitrary"), 40),
        name="attn_lat",
    )(attn, q, k, v, cache_k, cache_v, cos_q, sin_q, cos_k, sin_k, sink_b)


def _rope_tables():
    rows = T_LAT // GRID_W
    row = jnp.repeat(jnp.arange(rows, dtype=F32), GRID_W)
    col = jnp.tile(jnp.arange(GRID_W, dtype=F32), rows)
    n_freq = HD // 4
    freqs = ROPE_BASE ** (-jnp.arange(n_freq, dtype=F32) / n_freq)
    ang = jnp.concatenate([row[:, None] * freqs, col[:, None] * freqs], axis=-1)
    cos = jnp.repeat(jnp.cos(ang), 2, axis=-1)
    sign = jnp.tile(jnp.array([-1.0, 1.0], F32), HD // 2)
    sin = jnp.repeat(jnp.sin(ang), 2, axis=-1) * sign
    return (jnp.tile(cos, (1, N_HEADS)), jnp.tile(sin, (1, N_HEADS)),
            jnp.tile(cos, (1, N_KV)), jnp.tile(sin, (1, N_KV)))


def _four_kernel(u_ref, cs_ref, c64_ref, s64_ref, wf_ref, o_ref):
    ub = u_ref[...].astype(BF16)
    uc = _dot(ub, c64_ref[...]).astype(BF16)
    us = _dot(ub, s64_ref[...]).astype(BF16)
    f = _dot(cs_ref[...], jnp.concatenate([uc, us], axis=0))
    o_ref[...] = _dot(f.astype(BF16), wf_ref[0].astype(BF16)).astype(BF16)


def _dft_tables(n):
    j = np.arange(n)
    ang = 2.0 * np.pi * ((j[:, None] * j[None, :]) % n) / n
    cs = np.concatenate([np.cos(ang), -np.sin(ang)], axis=1) / np.sqrt(n)
    return jnp.asarray(cs.astype(np.float32)).astype(BF16)


def _dft_channel_tables():
    j = np.arange(HD)
    ang = 2.0 * np.pi * ((j[:, None] * j[None, :]) % HD) / HD
    eye = np.eye(FOUR_H)
    c = np.kron(eye, np.cos(ang)) / np.sqrt(HD)
    s = np.kron(eye, np.sin(ang)) / np.sqrt(HD)
    return jnp.asarray(c.astype(np.float32)).astype(BF16), jnp.asarray(s.astype(np.float32)).astype(BF16)


def _fourier(prev, uf, cs, c64, s64, w_fourier, l, n, n_seq, blk_off):
    chained = prev is not None
    specs = [pl.BlockSpec((n, FOUR_W), lambda s: (blk_off + s, 0)),
             pl.BlockSpec((n, 2 * n), lambda s: (0, 0)),
             pl.BlockSpec((FOUR_W, FOUR_W), lambda s: (0, 0)),
             pl.BlockSpec((FOUR_W, FOUR_W), lambda s: (0, 0)),
             pl.BlockSpec((1, FOUR_W, FOUR_W), lambda s: (l, 0, 0))]
    args = (uf, cs, c64, s64, w_fourier)
    return pl.pallas_call(
        _chained(_four_kernel) if chained else _four_kernel,
        grid=(n_seq,),
        in_specs=([_ANY] if chained else []) + specs,
        out_specs=pl.BlockSpec((n, FOUR_W), lambda s: (blk_off + s, 0)),
        out_shape=jax.ShapeDtypeStruct((ROWS, FOUR_W), BF16),
        input_output_aliases={0: 0} if chained else {},
        compiler_params=_cparams(("arbitrary",), 40),
        name=f"fourier_{n}",
    )(*(((prev,) if chained else ()) + args))


S5_TW = S5_T * S5_H
S5_R = 128
S5_TOK = S5_R * S5_T
S5_STEPS_CTX = ROWS_CTX // S5_TOK
S5_STEPS_LAT = (ROWS - ROWS_CTX) // S5_TOK
S5_STEPS = S5_STEPS_CTX + S5_STEPS_LAT
S5_SEQ = N_CTX // S5_STEPS_CTX


def _s5_tables(a_re, a_im, log_dt, b_re, b_im, c_re, c_im):
    f = lambda a: a.astype(F32)
    a_re, a_im, b_re, b_im, c_re, c_im = map(f, (a_re, a_im, b_re, b_im, c_re, c_im))
    dt = jnp.exp(f(log_dt))[..., None]
    x, y = a_re * dt, a_im * dt
    kk = jnp.arange(S5_T + 1, dtype=F32)[None, None, :, None, None]
    mag = jnp.exp(kk * x[:, :, None])
    pr, pi = mag * jnp.cos(kk * y[:, :, None]), mag * jnp.sin(kk * y[:, :, None])
    nr, ni = pr[:, :, 1] - 1.0, pi[:, :, 1]
    den = a_re * a_re + a_im * a_im
    qr, qi = (nr * a_re + ni * a_im) / den, (ni * a_re - nr * a_im) / den
    bb_r = qr[..., None] * b_re - qi[..., None] * b_im
    bb_i = qr[..., None] * b_im + qi[..., None] * b_re
    cat = lambda u, v: jnp.concatenate([u, v], axis=-1)
    prg, pig = pr.transpose(0, 1, 3, 2, 4), pi.transpose(0, 1, 3, 2, 4)
    pows = jnp.stack([cat(prg, prg), cat(-pig, pig), cat(prg, -pig), cat(-pig, -prg), cat(prg, pig), cat(-pig, prg)],
                     axis=3)
    bt_r, bt_i = bb_r.transpose(0, 1, 2, 4, 3), bb_i.transpose(0, 1, 2, 4, 3)
    wrows = jnp.stack([cat(bt_r, bt_i), cat(bt_i, bt_r), cat(bt_r, -bt_i), cat(c_re, c_re), cat(c_im, c_im)],
                      axis=3)
    ar, ai = pr[:, :, S5_T], pi[:, :, S5_T]
    decay = jnp.stack([cat(ar, ar), cat(-ai, ai), cat(ai, -ai), jnp.zeros_like(cat(ar, ar))], axis=3)
    return pows, wrows, decay


def _block_transpose(arrs):
    arrs = list(arrs)
    width = arrs[0].shape[1]
    blk = lax.broadcasted_iota(jnp.int32, arrs[0].shape, 1) >> 4
    s = len(arrs) // 2
    while s >= 1:
        keep = (blk & s) == 0
        for i in range(len(arrs)):
            if i & s:
                continue
            lo, hi = arrs[i], arrs[i + s]
            arrs[i] = jnp.where(keep, lo, pltpu.roll(hi, s * S5_H, 1))
            arrs[i + s] = jnp.where(keep, pltpu.roll(lo, width - s * S5_H, 1), hi)
        s //= 2
    return arrs


def _s5_build_operators(pw_ref, w_ref, e_scr, esw_scr, ft_scr, tp_scr):
    lane = lax.broadcasted_iota(jnp.int32, (S5_H, S5_TW), 1)
    for d in range(2):
        for g in range(S5_G):
            pw = lambda v, k: pw_ref[0, d, g, v, k:k + 1, :]
            b_ri, b_ir, b_conj, c_rr, c_ii = (w_ref[0, d, g, v] for v in range(5))
            lag_rows = []
            for t in range(S5_T):
                blk = slice(t * S5_H, (t + 1) * S5_H)
                ke = S5_T - 1 - t if d == 0 else t
                e_scr[d, g, blk, :] = (b_ri * pw(0, ke) + b_ir * pw(1, ke)).astype(BF16)
                esw_scr[d, g, blk, :] = (b_ir * pw(0, ke) - b_ri * pw(1, ke)).astype(BF16)
                kf = t + 1 if d == 0 else S5_T - t
                ft_scr[d, g, blk, :] = (c_rr * pw(2, kf) + c_ii * pw(3, kf)).astype(BF16)
                kl = t if d == 0 else S5_T - 1 - t
                lag_rows.append(c_rr * pw(4, kl) + c_ii * pw(5, kl))
            mh, ml = _split_bf16(jnp.concatenate(lag_rows, axis=0))
            bh, bl = _split_bf16(b_conj)
            kt = _dot_nt(bh, mh) + _dot_nt(bh, ml) + _dot_nt(bl, mh)
            for ti in range(S5_T):
                if d == 0:
                    sh = S5_H * ti
                    blk_rows = jnp.where(lane >= sh, pltpu.roll(kt, sh, 1) if sh else kt, 0.0)
                else:
                    sh = S5_H * (S5_T - 1 - ti)
                    blk_rows = jnp.where(lane < S5_TW - sh, pltpu.roll(kt, S5_TW - sh, 1) if sh else kt, 0.0)
                tp_scr[d, g, ti * S5_H:(ti + 1) * S5_H, :] = blk_rows.astype(BF16)


def _s5_kernel(us_ref, pw_ref, w_ref, a_ref, s0_ref, y_ref, fin_ref,
               e_scr, esw_scr, ft_scr, tp_scr, x_scr, y_scr, loc_scr, lsw_scr, prev_scr):
    step = pl.program_id(0)

    @pl.when(step == 0)
    def _():
        _s5_build_operators(pw_ref, w_ref, e_scr, esw_scr, ft_scr, tp_scr)

    rows = pl.ds
    x_by_step = [jnp.concatenate([us_ref[0, rows(tl, S5_R, stride=S5_T), :],
                                  us_ref[1, rows(tl, S5_R, stride=S5_T), :]], axis=1) for tl in range(S5_T)]
    for g, xg in enumerate(_block_transpose(x_by_step)):
        x_scr[g] = xg.astype(BF16)
    fin_ref[...] = jnp.zeros(fin_ref.shape, F32)

    def scan(d, nseq, nc):
        batch = 8
        for g0 in range(0, S5_G, batch):
            coefs = [[jnp.broadcast_to(a_ref[0, d, g, r:r + 1, :], (nseq, 2 * S5_P)) for r in range(3)]
                     for g in range(g0, g0 + batch)]
            init = tuple((s0_ref[0, 0, d, g, 0:nseq, :2 * S5_P], s0_ref[0, 0, d, g, 0:nseq, 2 * S5_P:])
                         for g in range(g0, g0 + batch))

            def body(j, carry):
                ci = j if d == 0 else nc - 1 - j
                out = []
                for idx in range(batch):
                    g = g0 + idx
                    s, t = carry[idx]
                    ca, cb, cc = coefs[idx]
                    prev_scr[g, rows(ci, nseq, stride=nc), :] = s
                    out.append((s * ca + t * cb + loc_scr[g, rows(ci, nseq, stride=nc), :],
                                t * ca + s * cc + lsw_scr[g, rows(ci, nseq, stride=nc), :]))
                return tuple(out)

            fin = lax.fori_loop(0, nc, body, init)
            for idx in range(batch):
                fin_ref[0, d, g0 + idx, 0:nseq, :] = fin[idx][0]

    for d in range(2):
        for g in range(S5_G):
            xg = x_scr[g]
            loc_scr[g] = _dot(xg, e_scr[d, g])
            lsw_scr[g] = _dot(xg, esw_scr[d, g])

        @pl.when(step < S5_STEPS_CTX)
        def _():
            scan(d, N_CTX // S5_STEPS_CTX, T_CTX // S5_T)

        @pl.when(step >= S5_STEPS_CTX)
        def _():
            scan(d, N_LAT // S5_STEPS_LAT, T_LAT // S5_T)

        for g in range(S5_G):
            yg = _dot(x_scr[g], tp_scr[d, g]) + _dot_nt(prev_scr[g].astype(BF16), ft_scr[d, g])
            if d == 0:
                y_scr[g] = yg
            else:
                y_scr[g] += yg
    for to, y_to in enumerate(_block_transpose([y_scr[g] for g in range(S5_G)])):
        y_ref[0, rows(to, S5_R, stride=S5_T), :] = y_to[:, :LANES]
        y_ref[1, rows(to, S5_R, stride=S5_T), :] = y_to[:, LANES:]


def _s5(us, tabs, s0, l):
    pows, wrows, decay = tabs
    lay = lambda *shape: pl.BlockSpec((1,) + shape, lambda h: (l,) + (0,) * len(shape))
    op_buf = pltpu.VMEM((2, S5_G, S5_TW, 2 * S5_P), BF16)
    scan_buf = pltpu.VMEM((S5_G, S5_R, 2 * S5_P), F32)
    return pl.pallas_call(
        _s5_kernel,
        grid=(S5_STEPS,),
        in_specs=[_halves_spec(S5_TOK, lambda h: h),
                  lay(2, S5_G, 6, S5_T + 1, 2 * S5_P), lay(2, S5_G, 5, S5_H, 2 * S5_P), lay(2, S5_G, 4, 2 * S5_P),
                  pl.BlockSpec((1, 1, 2, S5_G, S5_SEQ, 4 * S5_P), lambda h: (l, h, 0, 0, 0, 0))],
        out_specs=[_halves_spec(S5_TOK, lambda h: h),
                   pl.BlockSpec((1, 2, S5_G, S5_SEQ, 2 * S5_P), lambda h: (h, 0, 0, 0, 0))],
        out_shape=[_HALVES_SHAPE, jax.ShapeDtypeStruct((S5_STEPS, 2, S5_G, S5_SEQ, 2 * S5_P), F32)],
        scratch_shapes=[op_buf, op_buf, op_buf, pltpu.VMEM((2, S5_G, S5_TW, S5_TW), BF16),
                        pltpu.VMEM((S5_G, S5_R, S5_TW), BF16), pltpu.VMEM((S5_G, S5_R, S5_TW), F32),
                        scan_buf, scan_buf, scan_buf],
        compiler_params=_cparams(("arbitrary",), 48),
        name="s5",
    )(us, pows, wrows, decay, s0)


def _split_bf16(a):
    hi = a.astype(BF16)
    return hi, (a - hi.astype(F32)).astype(BF16)


def _outproj_kernel(at_ref, fo_ref, ys_ref, us_ref, d_ref, wg_ref, wo_ref, x_ref, mod_ref, g_ref, b_ref, wr_ref,
                    x1_ref, h2_ref, lg_ref, wob_ref):
    @pl.when(pl.program_id(0) == 0)
    def _():
        wob_ref[...] = wo_ref[0].astype(BF16)

    ys = jnp.concatenate([ys_ref[0], ys_ref[1]], axis=1)
    us = jnp.concatenate([us_ref[0], us_ref[1]], axis=1)
    g = jax.nn.gelu(ys + us * d_ref[0])
    s5 = g * jax.nn.sigmoid(_dot(g.astype(BF16), wg_ref[0].astype(BF16)))
    mix = (_dot(at_ref[...], wob_ref[0:ATTN_W, :])
           + _dot(fo_ref[...], wob_ref[ATTN_W:ATTN_W + FOUR_W, :])
           + _dot(s5.astype(BF16), wob_ref[ATTN_W + FOUR_W:, :]))
    m = mod_ref[0, 0]
    x1 = _layer_norm(ALPHA * x_ref[...] + m[2:3] * mix, g_ref[0], b_ref[0])
    x1_ref[...] = x1
    h2 = x1 * (1.0 + m[4:5]) + m[3:4]
    h2_ref[...] = h2.astype(BF16)
    hh, hl = _split_bf16(h2)
    wh, wl = _split_bf16(wr_ref[0])
    lg_ref[...] = _dot_nt(wh, hh) + _dot_nt(wh, hl) + _dot_nt(wl, hh)


def _outproj(attn, four, ys5, us, s5_d, w_glu, w_out, x, mod, ln_g, ln_b, w_router_t, l):
    row = lambda w: pl.BlockSpec((RB_OUT, w), lambda i: (i, 0))
    halves = _halves_spec(RB_OUT, lambda i: i)
    lay = lambda *shape: pl.BlockSpec((1,) + shape, lambda i: (l,) + (0,) * len(shape))
    return pl.pallas_call(
        _outproj_kernel,
        grid=(ROWS // RB_OUT,),
        in_specs=[row(ATTN_W), row(FOUR_W), halves, halves, lay(1, S5_W), lay(S5_W, S5_W), lay(D, D), row(D),
                  pl.BlockSpec((1, 1, 6, D), lambda i: (l, _mod_row(i, RB_OUT), 0, 0)),
                  lay(1, D), lay(1, D), lay(N_EXP, D)],
        out_specs=[row(D), row(D), pl.BlockSpec((N_EXP, RB_OUT), lambda i: (0, i))],
        out_shape=[jax.ShapeDtypeStruct((ROWS, D), F32), jax.ShapeDtypeStruct((ROWS, D), BF16),
                   jax.ShapeDtypeStruct((N_EXP, ROWS), F32)],
        scratch_shapes=[pltpu.VMEM((D, D), BF16)],
        compiler_params=_cparams(("arbitrary",), 40),
        name="outproj",
    )(attn, four, ys5, us, s5_d, w_glu, w_out, x, mod, ln_g, ln_b, w_router_t)


SEQ_GROUP = LANES // N_EXP


def _router_kernel(lg_ref, crow_ref, ccol_ref, gcol_ref, *, n, cap, n_seq):
    rows = n_seq * N_EXP
    lg = jnp.concatenate([lg_ref[:, s * n:(s + 1) * n] for s in range(n_seq)], axis=0).reshape(n_seq, N_EXP, n)
    e = jnp.exp(lg - jnp.max(lg, axis=1, keepdims=True))
    aff = (e / jnp.sum(e, axis=1, keepdims=True)).reshape(rows, n)
    thr_bits = jnp.zeros((rows, 1), jnp.int32)
    for bit in range(30, -1, -1):
        cand = thr_bits | (1 << bit)
        cnt = jnp.sum(jnp.where(aff >= lax.bitcast_convert_type(cand, F32), 1.0, 0.0), axis=1, keepdims=True)
        thr_bits = jnp.where(cnt >= cap, cand, thr_bits)
    thr = lax.bitcast_convert_type(thr_bits, F32)
    above = aff > thr
    tied = aff == thr
    need = cap - jnp.sum(jnp.where(above, 1.0, 0.0), axis=1, keepdims=True)
    r0 = lax.broadcasted_iota(jnp.int32, (n, n), 0)
    r1 = lax.broadcasted_iota(jnp.int32, (n, n), 1)
    before = jnp.where(r0 < r1, 1.0, 0.0).astype(BF16)
    eye = jnp.where(r0 == r1, 1.0, 0.0).astype(BF16)
    tied_rank = _dot(jnp.where(tied, 1.0, 0.0).astype(BF16), before)
    sel = above | (tied & (tied_rank < need))
    slot = _dot(jnp.where(sel, 1.0, 0.0).astype(BF16), before)
    code = jnp.where(sel, slot + 1.0, 0.0)
    gate = jnp.where(sel, aff, 0.0)
    for s in range(n_seq):
        crow_ref[:, s * n:(s + 1) * n] = code[s * N_EXP:(s + 1) * N_EXP]
    for grp in range(pl.cdiv(n_seq, SEQ_GROUP)):
        lo, hi = grp * LANES, min((grp + 1) * LANES, rows)
        pad = [jnp.zeros((LANES - (hi - lo), n), F32)] if hi - lo < LANES else []
        ccol_ref[grp] = _dot_nt(eye, jnp.concatenate([code[lo:hi]] + pad, axis=0).astype(BF16))
        gcol_ref[grp] = _dot_nt(eye, jnp.concatenate([gate[lo:hi]] + pad, axis=0).astype(BF16))


def _router(lg, n, n_seq, blk, cap):
    groups = pl.cdiv(n_seq, SEQ_GROUP)
    whole = lambda *shape: pl.BlockSpec(shape, lambda i: (0,) * len(shape))
    return pl.pallas_call(
        functools.partial(_router_kernel, n=n, cap=cap, n_seq=n_seq),
        grid=(1,),
        in_specs=[pl.BlockSpec((N_EXP, n_seq * n), lambda i: (0, blk))],
        out_specs=[whole(N_EXP, n_seq * n), whole(groups, n, LANES), whole(groups, n, LANES)],
        out_shape=[jax.ShapeDtypeStruct((N_EXP, n_seq * n), F32),
                   jax.ShapeDtypeStruct((groups, n, LANES), F32),
                   jax.ShapeDtypeStruct((groups, n, LANES), F32)],
        compiler_params=_cparams(("arbitrary",), 48),
        name=f"router_{n}",
    )(lg)


def _gather_kernel(crow_ref, h_ref, o_ref, *, n, cap):
    code = crow_ref[...]
    slot1 = (lax.broadcasted_iota(jnp.int32, (cap, n), 0) + 1).astype(F32)
    onehot = jnp.concatenate(
        [jnp.where(code[e:e + 1, :] == slot1, 1.0, 0.0).astype(BF16) for e in range(N_EXP)], axis=0)
    xs = _dot(onehot, h_ref[...])
    o_ref[...] = xs.reshape(N_EXP, cap, D).astype(BF16)


def _gather(crow, h2, n, n_seq, blk_off, cap):
    return pl.pallas_call(
        functools.partial(_gather_kernel, n=n, cap=cap),
        grid=(n_seq,),
        in_specs=[pl.BlockSpec((N_EXP, n), lambda s: (0, s)),
                  pl.BlockSpec((n, D), lambda s: (blk_off + s, 0))],
        out_specs=pl.BlockSpec((N_EXP, cap, D), lambda s: (0, s, 0)),
        out_shape=jax.ShapeDtypeStruct((N_EXP, n_seq * cap, D), BF16),
        compiler_params=_cparams(("arbitrary",), 48),
        name=f"gather_{n}",
    )(crow, h2)


FF_BLK = 512
SLOTS = N_CTX * CAP_CTX


def _ffn_kernel(xc_ref, xq_ref, wg_ref, wu_ref, wd_ref, yc_ref, yq_ref, acc_ref):
    f = pl.program_id(1)
    x = jnp.concatenate([xc_ref[0], xq_ref[0]], axis=0)
    g = _dot(x, wg_ref[0, 0].astype(BF16))
    u = _dot(x, wu_ref[0, 0].astype(BF16))
    hid = (g * jax.nn.sigmoid(g) * u).astype(BF16)
    y = _dot(hid, wd_ref[0, 0].astype(BF16))

    @pl.when(f == 0)
    def _():
        acc_ref[...] = y

    @pl.when(f == FF // FF_BLK - 1)
    def _():
        tot = acc_ref[...] + y
        yc_ref[0] = tot[:SLOTS].astype(BF16)
        yq_ref[0] = tot[SLOTS:].astype(BF16)


def _ffn(xs_c, xs_q, w_gate, w_up, w_down, l):
    assert FF // FF_BLK == 2
    xspec = pl.BlockSpec((1, SLOTS, D), lambda e, f: (e, 0, 0))
    return pl.pallas_call(
        _ffn_kernel,
        grid=(N_EXP, FF // FF_BLK),
        in_specs=[xspec, xspec,
                  pl.BlockSpec((1, 1, D, FF_BLK), lambda e, f: (l, e, 0, f)),
                  pl.BlockSpec((1, 1, D, FF_BLK), lambda e, f: (l, e, 0, f)),
                  pl.BlockSpec((1, 1, FF_BLK, D), lambda e, f: (l, e, f, 0))],
        out_specs=[xspec, xspec],
        out_shape=[jax.ShapeDtypeStruct((N_EXP, SLOTS, D), BF16)] * 2,
        scratch_shapes=[pltpu.VMEM((2 * SLOTS, D), F32)],
        compiler_params=_cparams(("arbitrary", "arbitrary"), 48),
        name="ffn",
    )(xs_c, xs_q, w_gate, w_up, w_down)


def _combine_kernel(ccol_ref, gcol_ref, ys_ref, x_ref, mod_ref, g_ref, b_ref, o_ref, *, cap):
    width = N_EXP * cap
    first = (pl.program_id(0) % SEQ_GROUP) * N_EXP
    k = lax.broadcasted_iota(jnp.int32, (LANES, width), 0)
    j = lax.broadcasted_iota(jnp.int32, (LANES, width), 1)
    shift = cap.bit_length() - 1
    spread = jnp.where((j >> shift) + first == k, 1.0, 0.0).astype(BF16)
    code = _dot(ccol_ref[0].astype(BF16), spread)
    gate = _dot(gcol_ref[0].astype(BF16), spread)
    slot1 = ((lax.broadcasted_iota(jnp.int32, (1, width), 1) & (cap - 1)) + 1).astype(F32)
    weights = jnp.where(code == slot1, gate, 0.0).astype(BF16)
    moe = _dot(weights, ys_ref[...].reshape(width, D))
    m = mod_ref[0, 0]
    o_ref[...] = _layer_norm(ALPHA * x_ref[...] + m[5:6] * moe, g_ref[0], b_ref[0])


def _combine(prev, ccol, gcol, ys, x1, mod, ln_g, ln_b, l, cap, n_seq, rb, rb_per_seq, rb_off):
    chained = prev is not None
    lay = lambda *shape: pl.BlockSpec((1,) + shape, lambda s, r: (l,) + (0,) * len(shape))
    table = pl.BlockSpec((1, rb, LANES), lambda s, r: (s // SEQ_GROUP, r, 0))
    block = lambda s, r: rb_off + s * rb_per_seq + r
    specs = [table, table,
             pl.BlockSpec((N_EXP, cap, D), lambda s, r: (0, s, 0)),
             pl.BlockSpec((rb, D), lambda s, r: (block(s, r), 0)),
             pl.BlockSpec((1, 1, 6, D), lambda s, r: (l, _mod_row(block(s, r), rb), 0, 0)),
             lay(1, D), lay(1, D)]
    kern = functools.partial(_combine_kernel, cap=cap)
    return pl.pallas_call(
        _chained(kern) if chained else kern,
        grid=(n_seq, rb_per_seq),
        in_specs=([_ANY] if chained else []) + specs,
        out_specs=pl.BlockSpec((rb, D), lambda s, r: (block(s, r), 0)),
        out_shape=jax.ShapeDtypeStruct((ROWS, D), F32),
        input_output_aliases={0: 0} if chained else {},
        compiler_params=_cparams(("arbitrary", "arbitrary"), 48),
        name=f"combine_{cap}",
    )(*(((prev,) if chained else ()) + (ccol, gcol, ys, x1, mod, ln_g, ln_b)))


def kernel(x_prompt, x_sample, cache_k, cache_v, state_s5_re, state_s5_im, c, c_ctx, ln_in_g, ln_in_b, w_ada, b_ada,
           w_in, w_fourier, attn_sink, s5_a_re, s5_a_im, s5_log_dt, s5_b_re, s5_b_im, s5_c_re, s5_c_im, s5_d,
           s5_w_glu, w_out, ln1_g, ln1_b, w_router, w_gate, w_up, w_down, ln2_g, ln2_b):
    x = jnp.concatenate([x_prompt.reshape(ROWS_CTX, D), x_sample.reshape(ROWS - ROWS_CTX, D)], axis=0)
    x = _ln_in(x, ln_in_g, ln_in_b)
    cond8 = jnp.concatenate([c_ctx[None], c, jnp.zeros((8 - 1 - N_LAT, D), F32)], axis=0)
    mod = _ada(cond8, w_ada, b_ada).reshape(DEPTH, 8, 6, D)

    rope_tabs = _rope_tables()
    cs_ctx, cs_lat = _dft_tables(T_CTX), _dft_tables(T_LAT)
    c64, s64 = _dft_channel_tables()
    cache_k = cache_k.reshape(N_LAT, DEPTH, PAST, KV_W)
    cache_v = cache_v.reshape(N_LAT, DEPTH, PAST, KV_W)
    w_router_t = jnp.swapaxes(w_router, 1, 2)
    sink_b = jnp.broadcast_to(attn_sink[:, :, None], (DEPTH, N_HEADS, LANES))
    s5_tabs = _s5_tables(s5_a_re, s5_a_im, s5_log_dt, s5_b_re, s5_b_im, s5_c_re, s5_c_im)
    lat_per_step = N_LAT // S5_STEPS_LAT
    s0 = jnp.concatenate([state_s5_re, state_s5_im, state_s5_im, state_s5_re], axis=-1)
    s0 = s0.reshape(S5_STEPS_LAT, lat_per_step, DEPTH, 2, S5_G, 4 * S5_P).transpose(2, 0, 3, 4, 1, 5)
    s0 = jnp.pad(s0, ((0, 0), (S5_STEPS_CTX, 0), (0, 0), (0, 0), (0, S5_SEQ - lat_per_step), (0, 0)))
    ln1_g, ln1_b, ln2_g, ln2_b, s5_d = (a.reshape(DEPTH, 1, -1) for a in (ln1_g, ln1_b, ln2_g, ln2_b, s5_d))

    new_k, new_v, new_s = [], [], []
    for l in range(DEPTH):
        q, k, v, uf, us = _inproj(x, mod, w_in, l)
        new_k.append(k[:ROWS_CTX].reshape(N_CTX, T_CTX, N_KV, HD))
        new_v.append(v[:ROWS_CTX].reshape(N_CTX, T_CTX, N_KV, HD))
        attn = _attn_ctx(q, k, v, sink_b[l])
        attn = _attn_lat(attn, q, k, v, cache_k, cache_v, rope_tabs, sink_b[l], l)
        four = _fourier(None, uf, cs_ctx, c64, s64, w_fourier, l, T_CTX, N_CTX, 0)
        four = _fourier(four, uf, cs_lat, c64, s64, w_fourier, l, T_LAT, N_LAT, ROWS_CTX // T_LAT)
        y5, fin = _s5(us, s5_tabs, s0, l)
        new_s.append(fin[:S5_STEPS_CTX].transpose(0, 3, 1, 2, 4).reshape(N_CTX, 2, S5_G, 2 * S5_P))
        x1, h2, lg = _outproj(attn, four, y5, us, s5_d, s5_w_glu, w_out, x, mod, ln1_g, ln1_b, w_router_t, l)
        crow_c, ccol_c, gcol_c = _router(lg, T_CTX, N_CTX, 0, CAP_CTX)
        crow_q, ccol_q, gcol_q = _router(lg, T_LAT, N_LAT, 1, CAP_LAT)
        xs_c = _gather(crow_c, h2, T_CTX, N_CTX, 0, CAP_CTX)
        xs_q = _gather(crow_q, h2, T_LAT, N_LAT, ROWS_CTX // T_LAT, CAP_LAT)
        ys_c, ys_q = _ffn(xs_c, xs_q, w_gate, w_up, w_down, l)
        x = _combine(None, ccol_c, gcol_c, ys_c, x1, mod, ln2_g, ln2_b, l, CAP_CTX, N_CTX, T_CTX, 1, 0)
        x = _combine(x, ccol_q, gcol_q, ys_q, x1, mod, ln2_g, ln2_b, l, CAP_LAT, N_LAT, RB_COMBINE_LAT,
                     T_LAT // RB_COMBINE_LAT, ROWS_CTX // RB_COMBINE_LAT)

    new_s = jnp.stack(new_s, axis=1)
    return (x[:ROWS_CTX].reshape(N_CTX, T_CTX, D), x[ROWS_CTX:].reshape(N_LAT, T_LAT, D),
            jnp.stack(new_k, axis=1), jnp.stack(new_v, axis=1), new_s[..., :S5_P], new_s[..., S5_P:])
```

```python
import functools

import jax
import jax.numpy as jnp
import numpy as np
from jax import lax
from jax.experimental import pallas as pl
from jax.experimental.pallas import tpu as pltpu

F32 = jnp.float32
BF16 = jnp.bfloat16

D = 1024
N_CTX, T_CTX = 16, 256
N_LAT, T_LAT = 4, 1024
ROWS_CTX = N_CTX * T_CTX
ROWS = ROWS_CTX + N_LAT * T_LAT
DEPTH = 4
PAST = 512
GRID_W = 64
N_HEADS, N_KV, HD = 8, 2, 64
GQA = N_HEADS // N_KV
ATTN_W, KV_W = N_HEADS * HD, N_KV * HD
SCALE = HD ** -0.5
WINDOW = 128
FOUR_H, FOUR_W = 4, 256
S5_G, S5_H, S5_P, S5_W = 16, 16, 64, 256
S5_T = 16
IN_W = ATTN_W + 2 * KV_W + FOUR_W + S5_W
N_EXP, FF = 16, 1024
CAP_CTX, CAP_LAT = 2 * T_CTX // N_EXP, 2 * T_LAT // N_EXP
LN_EPS = 1e-5
NEG_INF = -1e30
ALPHA = (2 * DEPTH) ** 0.25
ROPE_BASE = 10000.0
RB_IN = 1024
RB_OUT = 512
RB_COMBINE_LAT = 512
LANES = 128
MIB = 2 ** 20
HIGHEST = lax.Precision.HIGHEST


def _cparams(sem, vmem_mib):
    return pltpu.CompilerParams(dimension_semantics=sem, vmem_limit_bytes=vmem_mib * MIB)


def _dot(a, b):
    return jnp.dot(a, b, preferred_element_type=F32)


def _dot_nt(a, b):
    return lax.dot_general(a, b, (((1,), (1,)), ((), ())), preferred_element_type=F32)


def _layer_norm(x, g, b):
    mu = jnp.mean(x, axis=-1, keepdims=True)
    xc = x - mu
    var = jnp.mean(xc * xc, axis=-1, keepdims=True)
    return xc * lax.rsqrt(var + LN_EPS) * g + b


def _mod_row(block, block_rows):
    first = block * block_rows
    return jnp.where(first < ROWS_CTX, 0, 1 + (first - ROWS_CTX) // T_LAT)


_HALVES_SHAPE = jax.ShapeDtypeStruct((2, ROWS, LANES), F32)


def _halves_spec(rows, row_block):
    return pl.BlockSpec((2, rows, LANES), lambda *ids: (0, row_block(*ids), 0))


def _two_stream_specs(rows, width):
    n_ctx = ROWS_CTX // rows
    return (pl.BlockSpec((rows, width), lambda i: (jnp.minimum(i, n_ctx - 1), 0)),
            pl.BlockSpec((rows, width), lambda i: (jnp.maximum(i - n_ctx, 0), 0)))


def _pick(rows, ctx_ref, lat_ref):
    return lax.cond(pl.program_id(0) < ROWS_CTX // rows, lambda: ctx_ref[...], lambda: lat_ref[...])


def _ln_in_kernel(xc_ref, xq_ref, g_ref, b_ref, o_ref):
    o_ref[...] = _layer_norm(_pick(RB_OUT, xc_ref, xq_ref), g_ref[...], b_ref[...])


def _ln_in(x_ctx, x_lat, g, b):
    return pl.pallas_call(
        _ln_in_kernel,
        grid=(ROWS // RB_OUT,),
        in_specs=[*_two_stream_specs(RB_OUT, D),
                  pl.BlockSpec((1, D), lambda i: (0, 0)),
                  pl.BlockSpec((1, D), lambda i: (0, 0))],
        out_specs=pl.BlockSpec((RB_OUT, D), lambda i: (i, 0)),
        out_shape=jax.ShapeDtypeStruct((ROWS, D), F32),
        compiler_params=_cparams(("arbitrary",), 32),
        name="ln_in",
    )(x_ctx, x_lat, g.reshape(1, D), b.reshape(1, D))


def _ada_kernel(c_ref, w_ref, b_ref, o_ref):
    c = c_ref[...]
    s = (c * jax.nn.sigmoid(c)).astype(BF16)
    o_ref[0] = _dot(s, w_ref[0].astype(BF16)) + b_ref[0]


def _ada(cond8, w_ada, b_ada):
    tn = 1536
    return pl.pallas_call(
        _ada_kernel,
        grid=(DEPTH, 6 * D // tn),
        in_specs=[pl.BlockSpec((8, D), lambda l, j: (0, 0)),
                  pl.BlockSpec((1, D, tn), lambda l, j: (l, 0, j)),
                  pl.BlockSpec((1, 1, tn), lambda l, j: (l, 0, j))],
        out_specs=pl.BlockSpec((1, 8, tn), lambda l, j: (l, 0, j)),
        out_shape=jax.ShapeDtypeStruct((DEPTH, 8, 6 * D), F32),
        compiler_params=_cparams(("arbitrary", "arbitrary"), 40),
        name="adaln",
    )(cond8, w_ada, b_ada.reshape(DEPTH, 1, 6 * D))


def _inproj_kernel(x_ref, mod_ref, w_ref, q_ref, k_ref, v_ref, uf_ref, us_ref, wb_ref):
    @pl.when(pl.program_id(0) == 0)
    def _():
        wb_ref[...] = w_ref[0].astype(BF16)

    m = mod_ref[0, 0]
    h = x_ref[...] * (1.0 + m[1:2]) + m[0:1]
    p = _dot(h.astype(BF16), wb_ref[...])
    q_ref[...] = p[:, :ATTN_W]
    k_ref[...] = p[:, ATTN_W:ATTN_W + KV_W]
    v_ref[...] = p[:, ATTN_W + KV_W:ATTN_W + 2 * KV_W]
    uf_ref[...] = p[:, ATTN_W + 2 * KV_W:ATTN_W + 2 * KV_W + FOUR_W]
    for half in range(S5_W // LANES):
        lo = ATTN_W + 2 * KV_W + FOUR_W + half * LANES
        us_ref[half] = p[:, lo:lo + LANES]


def _inproj(x, mod, w_in, l):
    widths = (ATTN_W, KV_W, KV_W, FOUR_W)
    return pl.pallas_call(
        _inproj_kernel,
        grid=(ROWS // RB_IN,),
        in_specs=[pl.BlockSpec((RB_IN, D), lambda i: (i, 0)),
                  pl.BlockSpec((1, 1, 6, D), lambda i: (l, _mod_row(i, RB_IN), 0, 0)),
                  pl.BlockSpec((1, D, IN_W), lambda i: (l, 0, 0))],
        out_specs=[pl.BlockSpec((RB_IN, w), lambda i: (i, 0)) for w in widths] + [_halves_spec(RB_IN, lambda i: i)],
        out_shape=[jax.ShapeDtypeStruct((ROWS, w), F32) for w in widths] + [_HALVES_SHAPE],
        scratch_shapes=[pltpu.VMEM((D, IN_W), BF16)],
        compiler_params=_cparams(("arbitrary",), 56),
        name="inproj",
    )(x, mod, w_in)


def _softmax_av(score_blocks, sink, value_blocks):
    m = sink
    for s in score_blocks:
        m = jnp.maximum(m, jnp.max(s, axis=-1, keepdims=True))
    acc = None
    for s, v in zip(score_blocks, value_blocks):
        part = _dot(jnp.exp(s - m).astype(BF16), v)
        acc = part if acc is None else acc + part
    den = acc[:, HD:HD + 1] + jnp.exp(sink - m)
    return acc[:, :HD] / den


def _with_ones(v):
    return jnp.concatenate([v, jnp.ones(v.shape, v.dtype)], axis=1)


def _stack_group(q, kv):
    return jnp.concatenate([q[:, (kv * GQA + g) * HD:(kv * GQA + g + 1) * HD] for g in range(GQA)], axis=0)


def _sink_column(sink_ref, kv, rows):
    return jnp.concatenate(
        [jnp.broadcast_to(sink_ref[kv * GQA + g:kv * GQA + g + 1, 0:1], (rows, 1)) for g in range(GQA)], axis=0)


def _attn_ctx_kernel(q_ref, k_ref, v_ref, sink_ref, o_ref):
    q = (q_ref[...] * SCALE).astype(BF16)
    k = k_ref[...].astype(BF16)
    v = v_ref[...].astype(BF16)
    outs = []
    for kv in range(N_KV):
        s = _dot_nt(_stack_group(q, kv), k[:, kv * HD:(kv + 1) * HD])
        o = _softmax_av([s], _sink_column(sink_ref, kv, T_CTX), [_with_ones(v[:, kv * HD:(kv + 1) * HD])])
        outs += [o[g * T_CTX:(g + 1) * T_CTX] for g in range(GQA)]
    o_ref[...] = jnp.concatenate(outs, axis=1).astype(BF16)


def _attn_ctx(q, k, v, sink_b):
    return pl.pallas_call(
        _attn_ctx_kernel,
        grid=(N_CTX,),
        in_specs=[pl.BlockSpec((T_CTX, ATTN_W), lambda s: (s, 0)),
                  pl.BlockSpec((T_CTX, KV_W), lambda s: (s, 0)),
                  pl.BlockSpec((T_CTX, KV_W), lambda s: (s, 0)),
                  pl.BlockSpec((N_HEADS, LANES), lambda s: (0, 0))],
        out_specs=pl.BlockSpec((T_CTX, ATTN_W), lambda s: (s, 0)),
        out_shape=jax.ShapeDtypeStruct((ROWS_CTX, ATTN_W), BF16),
        compiler_params=_cparams(("arbitrary",), 32),
        name="attn_ctx",
    )(q, k, v, sink_b)


def _rope(x, cos, sin_signed):
    w = x.shape[-1]
    nxt = pltpu.roll(x, w - 1, 1)
    prv = pltpu.roll(x, 1, 1)
    lane = lax.broadcasted_iota(jnp.int32, x.shape, 1)
    swapped = jnp.where((lane & 1) == 0, nxt, prv)
    return x * cos + swapped * sin_signed


def _attn_lat_kernel(q_ref, k_ref, v_ref, ck_ref, cv_ref, cq_ref, sq_ref, ckt_ref, skt_ref, sink_ref, o_ref):
    i = pl.program_id(1)
    n_blk = T_LAT // WINDOW
    q = (_rope(q_ref[...], cq_ref[...], sq_ref[...]) * SCALE).astype(BF16)

    def kv_block(j):
        st = pl.multiple_of(j * WINDOW, WINDOW)
        kk = _rope(k_ref[pl.ds(st, WINDOW), :], ckt_ref[pl.ds(st, WINDOW), :], skt_ref[pl.ds(st, WINDOW), :])
        return kk.astype(BF16), v_ref[pl.ds(st, WINDOW), :].astype(BF16)

    k0, v0 = kv_block(jnp.maximum(i - 1, 0))
    k1, v1 = kv_block(i)
    k2, v2 = kv_block(jnp.minimum(i + 1, n_blk - 1))
    k_win = jnp.concatenate([k0, k1, k2], axis=0)
    v_win = jnp.concatenate([v0, v1, v2], axis=0)
    k_ctx = ck_ref[0, 0].astype(BF16)
    v_ctx = cv_ref[0, 0].astype(BF16)
    rows = GQA * WINDOW
    r = lax.broadcasted_iota(jnp.int32, (rows, WINDOW), 0) & (WINDOW - 1)
    c = lax.broadcasted_iota(jnp.int32, (rows, WINDOW), 1)
    far = 2 * WINDOW
    prev_ok = c >= r + jnp.where(i > 0, 0, far)
    next_ok = c + jnp.where(i < n_blk - 1, 0, far) <= r
    outs = []
    for kv in range(N_KV):
        head = slice(kv * HD, (kv + 1) * HD)
        qs = _stack_group(q, kv)
        s_win = _dot_nt(qs, k_win[:, head])
        s_prev = jnp.where(prev_ok, s_win[:, :WINDOW], NEG_INF)
        s_next = jnp.where(next_ok, s_win[:, 2 * WINDOW:], NEG_INF)
        scores = [jnp.concatenate([s_prev, s_win[:, WINDOW:2 * WINDOW], s_next], axis=1), _dot_nt(qs, k_ctx[:, head])]
        o = _softmax_av(scores, _sink_column(sink_ref, kv, WINDOW), [_with_ones(v_win[:, head]), _with_ones(v_ctx[:, head])])
        outs += [o[g * WINDOW:(g + 1) * WINDOW] for g in range(GQA)]
    o_ref[...] = jnp.concatenate(outs, axis=1).astype(BF16)


def _attn_lat(q, k, v, cache_k, cache_v, tabs, sink_b, l):
    cos_q, sin_q, cos_k, sin_k = tabs
    n_blk = T_LAT // WINDOW
    off = ROWS_CTX // WINDOW
    return pl.pallas_call(
        _attn_lat_kernel,
        grid=(N_LAT, n_blk),
        in_specs=[pl.BlockSpec((WINDOW, ATTN_W), lambda b, i: (off + b * n_blk + i, 0)),
                  pl.BlockSpec((T_LAT, KV_W), lambda b, i: (ROWS_CTX // T_LAT + b, 0)),
                  pl.BlockSpec((T_LAT, KV_W), lambda b, i: (ROWS_CTX // T_LAT + b, 0)),
                  pl.BlockSpec((1, 1, PAST, KV_W), lambda b, i: (b, l, 0, 0)),
                  pl.BlockSpec((1, 1, PAST, KV_W), lambda b, i: (b, l, 0, 0)),
                  pl.BlockSpec((WINDOW, ATTN_W), lambda b, i: (i, 0)),
                  pl.BlockSpec((WINDOW, ATTN_W), lambda b, i: (i, 0)),
                  pl.BlockSpec((T_LAT, KV_W), lambda b, i: (0, 0)),
                  pl.BlockSpec((T_LAT, KV_W), lambda b, i: (0, 0)),
                  pl.BlockSpec((N_HEADS, LANES), lambda b, i: (0, 0))],
        out_specs=pl.BlockSpec((WINDOW, ATTN_W), lambda b, i: (b * n_blk + i, 0)),
        out_shape=jax.ShapeDtypeStruct((ROWS - ROWS_CTX, ATTN_W), BF16),
        compiler_params=_cparams(("arbitrary", "arbitrary"), 40),
        name="attn_lat",
    )(q, k, v, cache_k, cache_v, cos_q, sin_q, cos_k, sin_k, sink_b)


def _rope_tables():
    rows = T_LAT // GRID_W
    row = jnp.repeat(jnp.arange(rows, dtype=F32), GRID_W)
    col = jnp.tile(jnp.arange(GRID_W, dtype=F32), rows)
    n_freq = HD // 4
    freqs = ROPE_BASE ** (-jnp.arange(n_freq, dtype=F32) / n_freq)
    ang = jnp.concatenate([row[:, None] * freqs, col[:, None] * freqs], axis=-1)
    cos = jnp.repeat(jnp.cos(ang), 2, axis=-1)
    sign = jnp.tile(jnp.array([-1.0, 1.0], F32), HD // 2)
    sin = jnp.repeat(jnp.sin(ang), 2, axis=-1) * sign
    return (jnp.tile(cos, (1, N_HEADS)), jnp.tile(sin, (1, N_HEADS)),
            jnp.tile(cos, (1, N_KV)), jnp.tile(sin, (1, N_KV)))


def _four_kernel(u_ref, cs_ref, c64_ref, s64_ref, wf_ref, o_ref):
    ub = u_ref[...].astype(BF16)
    uc = _dot(ub, c64_ref[...]).astype(BF16)
    us = _dot(ub, s64_ref[...]).astype(BF16)
    f = _dot(cs_ref[...], jnp.concatenate([uc, us], axis=0))
    o_ref[...] = _dot(f.astype(BF16), wf_ref[0].astype(BF16)).astype(BF16)


def _dft_tables(n):
    j = np.arange(n)
    ang = 2.0 * np.pi * ((j[:, None] * j[None, :]) % n) / n
    cs = np.concatenate([np.cos(ang), -np.sin(ang)], axis=1) / np.sqrt(n)
    return jnp.asarray(cs.astype(np.float32)).astype(BF16)


def _dft_channel_tables():
    j = np.arange(HD)
    ang = 2.0 * np.pi * ((j[:, None] * j[None, :]) % HD) / HD
    eye = np.eye(FOUR_H)
    c = np.kron(eye, np.cos(ang)) / np.sqrt(HD)
    s = np.kron(eye, np.sin(ang)) / np.sqrt(HD)
    return jnp.asarray(c.astype(np.float32)).astype(BF16), jnp.asarray(s.astype(np.float32)).astype(BF16)


def _fourier(uf, cs, c64, s64, w_fourier, l, n, n_seq, blk_off):
    return pl.pallas_call(
        _four_kernel,
        grid=(n_seq,),
        in_specs=[pl.BlockSpec((n, FOUR_W), lambda s: (blk_off + s, 0)),
                  pl.BlockSpec((n, 2 * n), lambda s: (0, 0)),
                  pl.BlockSpec((FOUR_W, FOUR_W), lambda s: (0, 0)),
                  pl.BlockSpec((FOUR_W, FOUR_W), lambda s: (0, 0)),
                  pl.BlockSpec((1, FOUR_W, FOUR_W), lambda s: (l, 0, 0))],
        out_specs=pl.BlockSpec((n, FOUR_W), lambda s: (s, 0)),
        out_shape=jax.ShapeDtypeStruct((n_seq * n, FOUR_W), BF16),
        compiler_params=_cparams(("arbitrary",), 40),
        name=f"fourier_{n}",
    )(uf, cs, c64, s64, w_fourier)


S5_TW = S5_T * S5_H
S5_R = 128
S5_TOK = S5_R * S5_T
S5_STEPS_CTX = ROWS_CTX // S5_TOK
S5_STEPS_LAT = (ROWS - ROWS_CTX) // S5_TOK
S5_STEPS = S5_STEPS_CTX + S5_STEPS_LAT
S5_SEQ = N_CTX // S5_STEPS_CTX


def _s5_tables(a_re, a_im, log_dt, b_re, b_im, c_re, c_im):
    f = lambda a: a.astype(F32)
    a_re, a_im, b_re, b_im, c_re, c_im = map(f, (a_re, a_im, b_re, b_im, c_re, c_im))
    dt = jnp.exp(f(log_dt))[..., None]
    x, y = a_re * dt, a_im * dt
    kk = jnp.arange(S5_T + 1, dtype=F32)[None, None, :, None, None]
    mag = jnp.exp(kk * x[:, :, None])
    pr, pi = mag * jnp.cos(kk * y[:, :, None]), mag * jnp.sin(kk * y[:, :, None])
    nr, ni = pr[:, :, 1] - 1.0, pi[:, :, 1]
    den = a_re * a_re + a_im * a_im
    qr, qi = (nr * a_re + ni * a_im) / den, (ni * a_re - nr * a_im) / den
    bb_r = qr[..., None] * b_re - qi[..., None] * b_im
    bb_i = qr[..., None] * b_im + qi[..., None] * b_re
    cat = lambda u, v: jnp.concatenate([u, v], axis=-1)
    prg, pig = pr.transpose(0, 1, 3, 2, 4), pi.transpose(0, 1, 3, 2, 4)
    pows = jnp.stack([cat(prg, prg), cat(-pig, pig), cat(prg, -pig), cat(-pig, -prg), cat(prg, pig), cat(-pig, prg)],
                     axis=3)
    bt_r, bt_i = bb_r.transpose(0, 1, 2, 4, 3), bb_i.transpose(0, 1, 2, 4, 3)
    wrows = jnp.stack([cat(bt_r, bt_i), cat(bt_i, bt_r), cat(bt_r, -bt_i), cat(c_re, c_re), cat(c_im, c_im)],
                      axis=3)
    ar, ai = pr[:, :, S5_T], pi[:, :, S5_T]
    decay = jnp.stack([cat(ar, ar), cat(-ai, ai), cat(ai, -ai), jnp.zeros_like(cat(ar, ar))], axis=3)
    return pows, wrows, decay


def _block_transpose(arrs):
    arrs = list(arrs)
    width = arrs[0].shape[1]
    blk = lax.broadcasted_iota(jnp.int32, arrs[0].shape, 1) >> 4
    s = len(arrs) // 2
    while s >= 1:
        keep = (blk & s) == 0
        for i in range(len(arrs)):
            if i & s:
                continue
            lo, hi = arrs[i], arrs[i + s]
            arrs[i] = jnp.where(keep, lo, pltpu.roll(hi, s * S5_H, 1))
            arrs[i + s] = jnp.where(keep, pltpu.roll(lo, width - s * S5_H, 1), hi)
        s //= 2
    return arrs


def _s5_build_operators(pw_ref, w_ref, e_scr, esw_scr, ft_scr, tp_scr):
    lane = lax.broadcasted_iota(jnp.int32, (S5_H, S5_TW), 1)
    for d in range(2):
        for g in range(S5_G):
            pw = lambda v, k: pw_ref[0, d, g, v, k:k + 1, :]
            b_ri, b_ir, b_conj, c_rr, c_ii = (w_ref[0, d, g, v] for v in range(5))
            lag_rows = []
            for t in range(S5_T):
                blk = slice(t * S5_H, (t + 1) * S5_H)
                ke = S5_T - 1 - t if d == 0 else t
                e_scr[d, g, blk, :] = (b_ri * pw(0, ke) + b_ir * pw(1, ke)).astype(BF16)
                esw_scr[d, g, blk, :] = (b_ir * pw(0, ke) - b_ri * pw(1, ke)).astype(BF16)
                kf = t + 1 if d == 0 else S5_T - t
                ft_scr[d, g, blk, :] = (c_rr * pw(2, kf) + c_ii * pw(3, kf)).astype(BF16)
                kl = t if d == 0 else S5_T - 1 - t
                lag_rows.append(c_rr * pw(4, kl) + c_ii * pw(5, kl))
            mh, ml = _split_bf16(jnp.concatenate(lag_rows, axis=0))
            bh, bl = _split_bf16(b_conj)
            kt = _dot_nt(bh, mh) + _dot_nt(bh, ml) + _dot_nt(bl, mh)
            for ti in range(S5_T):
                if d == 0:
                    sh = S5_H * ti
                    blk_rows = jnp.where(lane >= sh, pltpu.roll(kt, sh, 1) if sh else kt, 0.0)
                else:
                    sh = S5_H * (S5_T - 1 - ti)
                    blk_rows = jnp.where(lane < S5_TW - sh, pltpu.roll(kt, S5_TW - sh, 1) if sh else kt, 0.0)
                tp_scr[d, g, ti * S5_H:(ti + 1) * S5_H, :] = blk_rows.astype(BF16)


def _s5_kernel(us_ref, pw_ref, w_ref, a_ref, s0_ref, y_ref, fin_ref,
               e_scr, esw_scr, ft_scr, tp_scr, x_scr, y_scr, loc_scr, lsw_scr, prev_scr):
    step = pl.program_id(0)

    @pl.when(step == 0)
    def _():
        _s5_build_operators(pw_ref, w_ref, e_scr, esw_scr, ft_scr, tp_scr)

    rows = pl.ds
    x_by_step = [jnp.concatenate([us_ref[0, rows(tl, S5_R, stride=S5_T), :],
                                  us_ref[1, rows(tl, S5_R, stride=S5_T), :]], axis=1) for tl in range(S5_T)]
    for g, xg in enumerate(_block_transpose(x_by_step)):
        x_scr[g] = xg.astype(BF16)
    fin_ref[...] = jnp.zeros(fin_ref.shape, F32)

    def scan(d, nseq, nc):
        batch = 8
        for g0 in range(0, S5_G, batch):
            coefs = [[jnp.broadcast_to(a_ref[0, d, g, r:r + 1, :], (nseq, 2 * S5_P)) for r in range(3)]
                     for g in range(g0, g0 + batch)]
            init = tuple((s0_ref[0, 0, d, g, 0:nseq, :2 * S5_P], s0_ref[0, 0, d, g, 0:nseq, 2 * S5_P:])
                         for g in range(g0, g0 + batch))

            def body(j, carry):
                ci = j if d == 0 else nc - 1 - j
                out = []
                for idx in range(batch):
                    g = g0 + idx
                    s, t = carry[idx]
                    ca, cb, cc = coefs[idx]
                    prev_scr[g, rows(ci, nseq, stride=nc), :] = s
                    out.append((s * ca + t * cb + loc_scr[g, rows(ci, nseq, stride=nc), :],
                                t * ca + s * cc + lsw_scr[g, rows(ci, nseq, stride=nc), :]))
                return tuple(out)

            fin = lax.fori_loop(0, nc, body, init)
            for idx in range(batch):
                fin_ref[0, d, g0 + idx, 0:nseq, :] = fin[idx][0]

    for d in range(2):
        for g in range(S5_G):
            xg = x_scr[g]
            loc_scr[g] = _dot(xg, e_scr[d, g])
            lsw_scr[g] = _dot(xg, esw_scr[d, g])

        @pl.when(step < S5_STEPS_CTX)
        def _():
            scan(d, N_CTX // S5_STEPS_CTX, T_CTX // S5_T)

        @pl.when(step >= S5_STEPS_CTX)
        def _():
            scan(d, N_LAT // S5_STEPS_LAT, T_LAT // S5_T)

        for g in range(S5_G):
            yg = _dot(x_scr[g], tp_scr[d, g]) + _dot_nt(prev_scr[g].astype(BF16), ft_scr[d, g])
            if d == 0:
                y_scr[g] = yg
            else:
                y_scr[g] += yg
    for to, y_to in enumerate(_block_transpose([y_scr[g] for g in range(S5_G)])):
        y_ref[0, rows(to, S5_R, stride=S5_T), :] = y_to[:, :LANES]
        y_ref[1, rows(to, S5_R, stride=S5_T), :] = y_to[:, LANES:]


def _s5(us, tabs, s0, l):
    pows, wrows, decay = tabs
    lay = lambda *shape: pl.BlockSpec((1,) + shape, lambda h: (l,) + (0,) * len(shape))
    op_buf = pltpu.VMEM((2, S5_G, S5_TW, 2 * S5_P), BF16)
    scan_buf = pltpu.VMEM((S5_G, S5_R, 2 * S5_P), F32)
    return pl.pallas_call(
        _s5_kernel,
        grid=(S5_STEPS,),
        in_specs=[_halves_spec(S5_TOK, lambda h: h),
                  lay(2, S5_G, 6, S5_T + 1, 2 * S5_P), lay(2, S5_G, 5, S5_H, 2 * S5_P), lay(2, S5_G, 4, 2 * S5_P),
                  pl.BlockSpec((1, 1, 2, S5_G, S5_SEQ, 4 * S5_P), lambda h: (l, h, 0, 0, 0, 0))],
        out_specs=[_halves_spec(S5_TOK, lambda h: h),
                   pl.BlockSpec((1, 2, S5_G, S5_SEQ, 2 * S5_P), lambda h: (h, 0, 0, 0, 0))],
        out_shape=[_HALVES_SHAPE, jax.ShapeDtypeStruct((S5_STEPS, 2, S5_G, S5_SEQ, 2 * S5_P), F32)],
        scratch_shapes=[op_buf, op_buf, op_buf, pltpu.VMEM((2, S5_G, S5_TW, S5_TW), BF16),
                        pltpu.VMEM((S5_G, S5_R, S5_TW), BF16), pltpu.VMEM((S5_G, S5_R, S5_TW), F32),
                        scan_buf, scan_buf, scan_buf],
        compiler_params=_cparams(("arbitrary",), 48),
        name="s5",
    )(us, pows, wrows, decay, s0)


def _split_bf16(a):
    hi = a.astype(BF16)
    return hi, (a - hi.astype(F32)).astype(BF16)


def _outproj_kernel(atc_ref, atq_ref, foc_ref, foq_ref, ys_ref, us_ref, d_ref, wg_ref, wo_ref, x_ref, mod_ref,
                    g_ref, b_ref, wr_ref, x1_ref, h2_ref, lg_ref, wob_ref):
    @pl.when(pl.program_id(0) == 0)
    def _():
        wob_ref[...] = wo_ref[0].astype(BF16)

    ys = jnp.concatenate([ys_ref[0], ys_ref[1]], axis=1)
    us = jnp.concatenate([us_ref[0], us_ref[1]], axis=1)
    g = jax.nn.gelu(ys + us * d_ref[0])
    s5 = g * jax.nn.sigmoid(_dot(g.astype(BF16), wg_ref[0].astype(BF16)))
    mix = (_dot(_pick(RB_OUT, atc_ref, atq_ref), wob_ref[0:ATTN_W, :])
           + _dot(_pick(RB_OUT, foc_ref, foq_ref), wob_ref[ATTN_W:ATTN_W + FOUR_W, :])
           + _dot(s5.astype(BF16), wob_ref[ATTN_W + FOUR_W:, :]))
    m = mod_ref[0, 0]
    x1 = _layer_norm(ALPHA * x_ref[...] + m[2:3] * mix, g_ref[0], b_ref[0])
    x1_ref[...] = x1
    h2 = x1 * (1.0 + m[4:5]) + m[3:4]
    h2_ref[...] = h2.astype(BF16)
    hh, hl = _split_bf16(h2)
    wh, wl = _split_bf16(wr_ref[0])
    lg_ref[...] = _dot_nt(wh, hh) + _dot_nt(wh, hl) + _dot_nt(wl, hh)


def _outproj(attn, four, ys5, us, s5_d, w_glu, w_out, x, mod, ln_g, ln_b, w_router_t, l):
    row = lambda w: pl.BlockSpec((RB_OUT, w), lambda i: (i, 0))
    halves = _halves_spec(RB_OUT, lambda i: i)
    lay = lambda *shape: pl.BlockSpec((1,) + shape, lambda i: (l,) + (0,) * len(shape))
    return pl.pallas_call(
        _outproj_kernel,
        grid=(ROWS // RB_OUT,),
        in_specs=[*_two_stream_specs(RB_OUT, ATTN_W), *_two_stream_specs(RB_OUT, FOUR_W), halves, halves,
                  lay(1, S5_W), lay(S5_W, S5_W), lay(D, D), row(D),
                  pl.BlockSpec((1, 1, 6, D), lambda i: (l, _mod_row(i, RB_OUT), 0, 0)),
                  lay(1, D), lay(1, D), lay(N_EXP, D)],
        out_specs=[row(D), row(D), pl.BlockSpec((N_EXP, RB_OUT), lambda i: (0, i))],
        out_shape=[jax.ShapeDtypeStruct((ROWS, D), F32), jax.ShapeDtypeStruct((ROWS, D), BF16),
                   jax.ShapeDtypeStruct((N_EXP, ROWS), F32)],
        scratch_shapes=[pltpu.VMEM((D, D), BF16)],
        compiler_params=_cparams(("arbitrary",), 40),
        name="outproj",
    )(*attn, *four, ys5, us, s5_d, w_glu, w_out, x, mod, ln_g, ln_b, w_router_t)


SEQ_GROUP = LANES // N_EXP


def _router_kernel(lg_ref, crow_ref, ccol_ref, gcol_ref, *, n, cap, n_seq):
    rows = n_seq * N_EXP
    lg = jnp.concatenate([lg_ref[:, s * n:(s + 1) * n] for s in range(n_seq)], axis=0).reshape(n_seq, N_EXP, n)
    e = jnp.exp(lg - jnp.max(lg, axis=1, keepdims=True))
    aff = (e / jnp.sum(e, axis=1, keepdims=True)).reshape(rows, n)
    thr_bits = jnp.zeros((rows, 1), jnp.int32)
    for bit in range(30, -1, -1):
        cand = thr_bits | (1 << bit)
        cnt = jnp.sum(jnp.where(aff >= lax.bitcast_convert_type(cand, F32), 1.0, 0.0), axis=1, keepdims=True)
        thr_bits = jnp.where(cnt >= cap, cand, thr_bits)
    thr = lax.bitcast_convert_type(thr_bits, F32)
    above = aff > thr
    tied = aff == thr
    need = cap - jnp.sum(jnp.where(above, 1.0, 0.0), axis=1, keepdims=True)
    r0 = lax.broadcasted_iota(jnp.int32, (n, n), 0)
    r1 = lax.broadcasted_iota(jnp.int32, (n, n), 1)
    before = jnp.where(r0 < r1, 1.0, 0.0).astype(BF16)
    eye = jnp.where(r0 == r1, 1.0, 0.0).astype(BF16)
    tied_rank = _dot(jnp.where(tied, 1.0, 0.0).astype(BF16), before)
    sel = above | (tied & (tied_rank < need))
    slot = _dot(jnp.where(sel, 1.0, 0.0).astype(BF16), before)
    code = jnp.where(sel, slot + 1.0, 0.0)
    gate = jnp.where(sel, aff, 0.0)
    for s in range(n_seq):
        crow_ref[:, s * n:(s + 1) * n] = code[s * N_EXP:(s + 1) * N_EXP]
    for grp in range(pl.cdiv(n_seq, SEQ_GROUP)):
        lo, hi = grp * LANES, min((grp + 1) * LANES, rows)
        pad = [jnp.zeros((LANES - (hi - lo), n), F32)] if hi - lo < LANES else []
        ccol_ref[grp] = _dot_nt(eye, jnp.concatenate([code[lo:hi]] + pad, axis=0).astype(BF16))
        gcol_ref[grp] = _dot_nt(eye, jnp.concatenate([gate[lo:hi]] + pad, axis=0).astype(BF16))


def _router(lg, n, n_seq, blk, cap):
    groups = pl.cdiv(n_seq, SEQ_GROUP)
    whole = lambda *shape: pl.BlockSpec(shape, lambda i: (0,) * len(shape))
    return pl.pallas_call(
        functools.partial(_router_kernel, n=n, cap=cap, n_seq=n_seq),
        grid=(1,),
        in_specs=[pl.BlockSpec((N_EXP, n_seq * n), lambda i: (0, blk))],
        out_specs=[whole(N_EXP, n_seq * n), whole(groups, n, LANES), whole(groups, n, LANES)],
        out_shape=[jax.ShapeDtypeStruct((N_EXP, n_seq * n), F32),
                   jax.ShapeDtypeStruct((groups, n, LANES), F32),
                   jax.ShapeDtypeStruct((groups, n, LANES), F32)],
        compiler_params=_cparams(("arbitrary",), 48),
        name=f"router_{n}",
    )(lg)


def _gather_kernel(crow_ref, h_ref, o_ref, *, n, cap):
    code = crow_ref[...]
    slot1 = (lax.broadcasted_iota(jnp.int32, (cap, n), 0) + 1).astype(F32)
    onehot = jnp.concatenate(
        [jnp.where(code[e:e + 1, :] == slot1, 1.0, 0.0).astype(BF16) for e in range(N_EXP)], axis=0)
    xs = _dot(onehot, h_ref[...])
    o_ref[...] = xs.reshape(N_EXP, cap, D).astype(BF16)


def _gather(crow, h2, n, n_seq, blk_off, cap):
    return pl.pallas_call(
        functools.partial(_gather_kernel, n=n, cap=cap),
        grid=(n_seq,),
        in_specs=[pl.BlockSpec((N_EXP, n), lambda s: (0, s)),
                  pl.BlockSpec((n, D), lambda s: (blk_off + s, 0))],
        out_specs=pl.BlockSpec((N_EXP, cap, D), lambda s: (0, s, 0)),
        out_shape=jax.ShapeDtypeStruct((N_EXP, n_seq * cap, D), BF16),
        compiler_params=_cparams(("arbitrary",), 48),
        name=f"gather_{n}",
    )(crow, h2)


FF_BLK = 512
SLOTS = N_CTX * CAP_CTX


def _ffn_kernel(xc_ref, xq_ref, wg_ref, wu_ref, wd_ref, yc_ref, yq_ref, acc_ref):
    f = pl.program_id(1)
    x = jnp.concatenate([xc_ref[0], xq_ref[0]], axis=0)
    g = _dot(x, wg_ref[0, 0].astype(BF16))
    u = _dot(x, wu_ref[0, 0].astype(BF16))
    hid = (g * jax.nn.sigmoid(g) * u).astype(BF16)
    y = _dot(hid, wd_ref[0, 0].astype(BF16))

    @pl.when(f == 0)
    def _():
        acc_ref[...] = y

    @pl.when(f == FF // FF_BLK - 1)
    def _():
        tot = acc_ref[...] + y
        yc_ref[0] = tot[:SLOTS].astype(BF16)
        yq_ref[0] = tot[SLOTS:].astype(BF16)


def _ffn(xs_c, xs_q, w_gate, w_up, w_down, l):
    assert FF // FF_BLK == 2
    xspec = pl.BlockSpec((1, SLOTS, D), lambda e, f: (e, 0, 0))
    return pl.pallas_call(
        _ffn_kernel,
        grid=(N_EXP, FF // FF_BLK),
        in_specs=[xspec, xspec,
                  pl.BlockSpec((1, 1, D, FF_BLK), lambda e, f: (l, e, 0, f)),
                  pl.BlockSpec((1, 1, D, FF_BLK), lambda e, f: (l, e, 0, f)),
                  pl.BlockSpec((1, 1, FF_BLK, D), lambda e, f: (l, e, f, 0))],
        out_specs=[xspec, xspec],
        out_shape=[jax.ShapeDtypeStruct((N_EXP, SLOTS, D), BF16)] * 2,
        scratch_shapes=[pltpu.VMEM((2 * SLOTS, D), F32)],
        compiler_params=_cparams(("arbitrary", "arbitrary"), 48),
        name="ffn",
    )(xs_c, xs_q, w_gate, w_up, w_down)


def _combine_kernel(ccol_ref, gcol_ref, ys_ref, x_ref, mod_ref, g_ref, b_ref, o_ref, *, cap):
    width = N_EXP * cap
    first = (pl.program_id(0) % SEQ_GROUP) * N_EXP
    k = lax.broadcasted_iota(jnp.int32, (LANES, width), 0)
    j = lax.broadcasted_iota(jnp.int32, (LANES, width), 1)
    shift = cap.bit_length() - 1
    spread = jnp.where((j >> shift) + first == k, 1.0, 0.0).astype(BF16)
    code = _dot(ccol_ref[0].astype(BF16), spread)
    gate = _dot(gcol_ref[0].astype(BF16), spread)
    slot1 = ((lax.broadcasted_iota(jnp.int32, (1, width), 1) & (cap - 1)) + 1).astype(F32)
    weights = jnp.where(code == slot1, gate, 0.0).astype(BF16)
    moe = _dot(weights, ys_ref[...].reshape(width, D))
    m = mod_ref[0, 0]
    o_ref[...] = _layer_norm(ALPHA * x_ref[...] + m[5:6] * moe, g_ref[0], b_ref[0])


def _combine(ccol, gcol, ys, x1, mod, ln_g, ln_b, l, cap, n_seq, rb, rb_per_seq, rb_off, in_place):
    lay = lambda *shape: pl.BlockSpec((1,) + shape, lambda s, r: (l,) + (0,) * len(shape))
    table = pl.BlockSpec((1, rb, LANES), lambda s, r: (s // SEQ_GROUP, r, 0))
    block = lambda s, r: rb_off + s * rb_per_seq + r
    out_off = rb_off if in_place else 0
    return pl.pallas_call(
        functools.partial(_combine_kernel, cap=cap),
        grid=(n_seq, rb_per_seq),
        in_specs=[table, table,
                  pl.BlockSpec((N_EXP, cap, D), lambda s, r: (0, s, 0)),
                  pl.BlockSpec((rb, D), lambda s, r: (block(s, r), 0)),
                  pl.BlockSpec((1, 1, 6, D), lambda s, r: (l, _mod_row(block(s, r), rb), 0, 0)),
                  lay(1, D), lay(1, D)],
        out_specs=pl.BlockSpec((rb, D), lambda s, r: (block(s, r) - rb_off + out_off, 0)),
        out_shape=jax.ShapeDtypeStruct((ROWS if in_place else n_seq * rb_per_seq * rb, D), F32),
        input_output_aliases={3: 0} if in_place else {},
        compiler_params=_cparams(("arbitrary", "arbitrary"), 48),
        name=f"combine_{cap}",
    )(ccol, gcol, ys, x1, mod, ln_g, ln_b)


def kernel(x_prompt, x_sample, cache_k, cache_v, state_s5_re, state_s5_im, c, c_ctx, ln_in_g, ln_in_b, w_ada, b_ada,
           w_in, w_fourier, attn_sink, s5_a_re, s5_a_im, s5_log_dt, s5_b_re, s5_b_im, s5_c_re, s5_c_im, s5_d,
           s5_w_glu, w_out, ln1_g, ln1_b, w_router, w_gate, w_up, w_down, ln2_g, ln2_b):
    x = _ln_in(x_prompt.reshape(ROWS_CTX, D), x_sample.reshape(ROWS - ROWS_CTX, D), ln_in_g, ln_in_b)
    cond8 = jnp.concatenate([c_ctx[None], c, jnp.zeros((8 - 1 - N_LAT, D), F32)], axis=0)
    mod = _ada(cond8, w_ada, b_ada).reshape(DEPTH, 8, 6, D)

    rope_tabs = _rope_tables()
    cs_ctx, cs_lat = _dft_tables(T_CTX), _dft_tables(T_LAT)
    c64, s64 = _dft_channel_tables()
    cache_k = cache_k.reshape(N_LAT, DEPTH, PAST, KV_W)
    cache_v = cache_v.reshape(N_LAT, DEPTH, PAST, KV_W)
    w_router_t = jnp.swapaxes(w_router, 1, 2)
    sink_b = jnp.broadcast_to(attn_sink[:, :, None], (DEPTH, N_HEADS, LANES))
    s5_tabs = _s5_tables(s5_a_re, s5_a_im, s5_log_dt, s5_b_re, s5_b_im, s5_c_re, s5_c_im)
    lat_per_step = N_LAT // S5_STEPS_LAT
    s0 = jnp.concatenate([state_s5_re, state_s5_im, state_s5_im, state_s5_re], axis=-1)
    s0 = s0.reshape(S5_STEPS_LAT, lat_per_step, DEPTH, 2, S5_G, 4 * S5_P).transpose(2, 0, 3, 4, 1, 5)
    s0 = jnp.pad(s0, ((0, 0), (S5_STEPS_CTX, 0), (0, 0), (0, 0), (0, S5_SEQ - lat_per_step), (0, 0)))
    ln1_g, ln1_b, ln2_g, ln2_b, s5_d = (a.reshape(DEPTH, 1, -1) for a in (ln1_g, ln1_b, ln2_g, ln2_b, s5_d))

    new_k, new_v, new_s = [], [], []
    for l in range(DEPTH):
        q, k, v, uf, us = _inproj(x, mod, w_in, l)
        new_k.append(k[:ROWS_CTX].reshape(N_CTX, T_CTX, N_KV, HD))
        new_v.append(v[:ROWS_CTX].reshape(N_CTX, T_CTX, N_KV, HD))
        attn = (_attn_ctx(q, k, v, sink_b[l]), _attn_lat(q, k, v, cache_k, cache_v, rope_tabs, sink_b[l], l))
        four = (_fourier(uf, cs_ctx, c64, s64, w_fourier, l, T_CTX, N_CTX, 0),
                _fourier(uf, cs_lat, c64, s64, w_fourier, l, T_LAT, N_LAT, ROWS_CTX // T_LAT))
        y5, fin = _s5(us, s5_tabs, s0, l)
        new_s.append(fin[:S5_STEPS_CTX].transpose(0, 3, 1, 2, 4).reshape(N_CTX, 2, S5_G, 2 * S5_P))
        x1, h2, lg = _outproj(attn, four, y5, us, s5_d, s5_w_glu, w_out, x, mod, ln1_g, ln1_b, w_router_t, l)
        crow_c, ccol_c, gcol_c = _router(lg, T_CTX, N_CTX, 0, CAP_CTX)
        crow_q, ccol_q, gcol_q = _router(lg, T_LAT, N_LAT, 1, CAP_LAT)
        xs_c = _gather(crow_c, h2, T_CTX, N_CTX, 0, CAP_CTX)
        xs_q = _gather(crow_q, h2, T_LAT, N_LAT, ROWS_CTX // T_LAT, CAP_LAT)
        ys_c, ys_q = _ffn(xs_c, xs_q, w_gate, w_up, w_down, l)
        args_c = (ccol_c, gcol_c, ys_c), (mod, ln2_g, ln2_b, l, CAP_CTX, N_CTX, T_CTX, 1, 0)
        args_q = (ccol_q, gcol_q, ys_q), (mod, ln2_g, ln2_b, l, CAP_LAT, N_LAT, RB_COMBINE_LAT,
                                          T_LAT // RB_COMBINE_LAT, ROWS_CTX // RB_COMBINE_LAT)
        if l < DEPTH - 1:
            x = _combine(*args_c[0], x1, *args_c[1], in_place=True)
            x = _combine(*args_q[0], x, *args_q[1], in_place=True)
        else:
            y_ctx = _combine(*args_c[0], x1, *args_c[1], in_place=False)
            y_lat = _combine(*args_q[0], x1, *args_q[1], in_place=False)

    new_s = jnp.stack(new_s, axis=1)
    return (y_ctx.reshape(N_CTX, T_CTX, D), y_lat.reshape(N_LAT, T_LAT, D),
            jnp.stack(new_k, axis=1), jnp.stack(new_v, axis=1), new_s[..., :S5_P], new_s[..., S5_P:])
```

```python
import functools

import jax
import jax.numpy as jnp
import numpy as np
from jax import lax
from jax.experimental import pallas as pl
from jax.experimental.pallas import tpu as pltpu

F32 = jnp.float32
BF16 = jnp.bfloat16

D = 1024
N_CTX, T_CTX = 16, 256
N_LAT, T_LAT = 4, 1024
ROWS_CTX = N_CTX * T_CTX
ROWS = ROWS_CTX + N_LAT * T_LAT
DEPTH = 4
PAST = 512
GRID_W = 64
N_HEADS, N_KV, HD = 8, 2, 64
GQA = N_HEADS // N_KV
ATTN_W, KV_W = N_HEADS * HD, N_KV * HD
LOG2E = 1.4426950408889634
Q_SCALE = HD ** -0.5 * LOG2E
WINDOW = 128
FOUR_H, FOUR_W = 4, 256
S5_G, S5_H, S5_P, S5_W = 16, 16, 64, 256
S5_T = 16
IN_W = ATTN_W + 2 * KV_W + FOUR_W + S5_W
N_EXP, FF = 16, 1024
CAP_CTX, CAP_LAT = 2 * T_CTX // N_EXP, 2 * T_LAT // N_EXP
LN_EPS = 1e-5
NEG_INF = -1e30
ALPHA = (2 * DEPTH) ** 0.25
ROPE_BASE = 10000.0
RB_IN = 1024
RB_OUT = 512
RB_COMBINE_LAT = 512
LANES = 128
MIB = 2 ** 20
HIGHEST = lax.Precision.HIGHEST


def _cparams(sem, vmem_mib):
    return pltpu.CompilerParams(dimension_semantics=sem, vmem_limit_bytes=vmem_mib * MIB)


def _dot(a, b):
    return jnp.dot(a, b, preferred_element_type=F32)


def _dot_nt(a, b):
    return lax.dot_general(a, b, (((1,), (1,)), ((), ())), preferred_element_type=F32)


def _dot_tn(a, b):
    return lax.dot_general(a, b, (((0,), (0,)), ((), ())), preferred_element_type=F32)


def _layer_norm(x, g, b):
    mu = jnp.mean(x, axis=-1, keepdims=True)
    xc = x - mu
    var = jnp.mean(xc * xc, axis=-1, keepdims=True)
    return xc * lax.rsqrt(var + LN_EPS) * g + b


def _mod_row(block, block_rows):
    first = block * block_rows
    return jnp.where(first < ROWS_CTX, 0, 1 + (first - ROWS_CTX) // T_LAT)


_HALVES_SHAPE = jax.ShapeDtypeStruct((2, ROWS, LANES), F32)


def _halves_spec(rows, row_block):
    return pl.BlockSpec((2, rows, LANES), lambda *ids: (0, row_block(*ids), 0))


def _two_stream_specs(rows, width):
    n_ctx = ROWS_CTX // rows
    return (pl.BlockSpec((rows, width), lambda i: (jnp.minimum(i, n_ctx - 1), 0)),
            pl.BlockSpec((rows, width), lambda i: (jnp.maximum(i - n_ctx, 0), 0)))


def _pick(rows, ctx_ref, lat_ref):
    return lax.cond(pl.program_id(0) < ROWS_CTX // rows, lambda: ctx_ref[...], lambda: lat_ref[...])


def _ln_in_kernel(xc_ref, xq_ref, g_ref, b_ref, o_ref):
    o_ref[...] = _layer_norm(_pick(RB_OUT, xc_ref, xq_ref), g_ref[...], b_ref[...])


def _ln_in(x_ctx, x_lat, g, b):
    return pl.pallas_call(
        _ln_in_kernel,
        grid=(ROWS // RB_OUT,),
        in_specs=[*_two_stream_specs(RB_OUT, D),
                  pl.BlockSpec((1, D), lambda i: (0, 0)),
                  pl.BlockSpec((1, D), lambda i: (0, 0))],
        out_specs=pl.BlockSpec((RB_OUT, D), lambda i: (i, 0)),
        out_shape=jax.ShapeDtypeStruct((ROWS, D), F32),
        compiler_params=_cparams(("arbitrary",), 32),
        name="ln_in",
    )(x_ctx, x_lat, g.reshape(1, D), b.reshape(1, D))


def _ada_kernel(c_ref, w_ref, b_ref, o_ref):
    c = c_ref[...]
    s = (c * jax.nn.sigmoid(c)).astype(BF16)
    o_ref[0] = _dot(s, w_ref[0].astype(BF16)) + b_ref[0]


def _ada(cond8, w_ada, b_ada):
    tn = 1536
    return pl.pallas_call(
        _ada_kernel,
        grid=(DEPTH, 6 * D // tn),
        in_specs=[pl.BlockSpec((8, D), lambda l, j: (0, 0)),
                  pl.BlockSpec((1, D, tn), lambda l, j: (l, 0, j)),
                  pl.BlockSpec((1, 1, tn), lambda l, j: (l, 0, j))],
        out_specs=pl.BlockSpec((1, 8, tn), lambda l, j: (l, 0, j)),
        out_shape=jax.ShapeDtypeStruct((DEPTH, 8, 6 * D), F32),
        compiler_params=_cparams(("arbitrary", "arbitrary"), 40),
        name="adaln",
    )(cond8, w_ada, b_ada.reshape(DEPTH, 1, 6 * D))


def _inproj_kernel(x_ref, mod_ref, w_ref, cq_ref, sq_ref, ck_ref, sk_ref, q_ref, k_ref, v_ref, uf_ref, us_ref, wb_ref):
    @pl.when(pl.program_id(0) == 0)
    def _():
        wb_ref[...] = w_ref[0].astype(BF16)

    m = mod_ref[0, 0]
    h = x_ref[...] * (1.0 + m[1:2]) + m[0:1]
    p = _dot(h.astype(BF16), wb_ref[...])
    q = p[:, :ATTN_W]
    k = p[:, ATTN_W:ATTN_W + KV_W]
    q, k = lax.cond(pl.program_id(0) >= ROWS_CTX // RB_IN,
                    lambda: (_rope(q, cq_ref[...], sq_ref[...]), _rope(k, ck_ref[...], sk_ref[...])),
                    lambda: (q, k))
    q_ref[...] = (q * Q_SCALE).astype(BF16)
    k_ref[...] = k
    v_ref[...] = p[:, ATTN_W + KV_W:ATTN_W + 2 * KV_W]
    uf_ref[...] = p[:, ATTN_W + 2 * KV_W:ATTN_W + 2 * KV_W + FOUR_W].astype(BF16)
    for half in range(S5_W // LANES):
        lo = ATTN_W + 2 * KV_W + FOUR_W + half * LANES
        us_ref[half] = p[:, lo:lo + LANES]


def _inproj(x, mod, w_in, rope_tabs, l):
    assert RB_IN == T_LAT
    outs = ((ATTN_W, BF16), (KV_W, F32), (KV_W, F32), (FOUR_W, BF16))
    table = lambda w: pl.BlockSpec((T_LAT, w), lambda i: (0, 0))
    return pl.pallas_call(
        _inproj_kernel,
        grid=(ROWS // RB_IN,),
        in_specs=[pl.BlockSpec((RB_IN, D), lambda i: (i, 0)),
                  pl.BlockSpec((1, 1, 6, D), lambda i: (l, _mod_row(i, RB_IN), 0, 0)),
                  pl.BlockSpec((1, D, IN_W), lambda i: (l, 0, 0)),
                  table(ATTN_W), table(ATTN_W), table(KV_W), table(KV_W)],
        out_specs=[pl.BlockSpec((RB_IN, w), lambda i: (i, 0)) for w, _ in outs] + [_halves_spec(RB_IN, lambda i: i)],
        out_shape=[jax.ShapeDtypeStruct((ROWS, w), dt) for w, dt in outs] + [_HALVES_SHAPE],
        scratch_shapes=[pltpu.VMEM((D, IN_W), BF16)],
        compiler_params=_cparams(("arbitrary",), 56),
        name="inproj",
    )(x, mod, w_in, *rope_tabs)


SINK_ROWS = 16


def _softmax_av(score_blocks, sink, value_blocks):
    m = sink
    for s in score_blocks:
        m = jnp.maximum(m, jnp.max(s, axis=0, keepdims=True))
    first = lax.broadcasted_iota(jnp.int32, (SINK_ROWS, 1), 0) == 0
    probs = [jnp.exp2(s - m).astype(BF16) for s in score_blocks]
    probs.append(jnp.where(first, jnp.exp2(sink - m), 0.0).astype(BF16))
    ext = []
    for v in value_blocks:
        ones = jnp.ones(v.shape, BF16)
        ext.append(jnp.concatenate([v, v, ones, ones], axis=1))
    row = lax.broadcasted_iota(jnp.int32, (SINK_ROWS, 4 * HD), 0)
    lane = lax.broadcasted_iota(jnp.int32, (SINK_ROWS, 4 * HD), 1)
    ext.append(jnp.where((row == 0) & (lane >= 2 * HD), 1.0, 0.0).astype(BF16))
    acc = _dot_tn(jnp.concatenate(probs, axis=0), jnp.concatenate(ext, axis=0))
    return acc[:, :2 * HD] / acc[:, 2 * HD:]


def _stack_group(q, kv):
    return jnp.concatenate([q[:, (kv * GQA + g) * HD:(kv * GQA + g + 1) * HD] for g in range(GQA)], axis=0)


def _sink_row(sink_ref, kv, rows):
    return LOG2E * jnp.concatenate(
        [jnp.broadcast_to(sink_ref[kv * GQA + g:kv * GQA + g + 1, 0:1], (1, rows)) for g in range(GQA)], axis=1)


def _unstack_group(o, rows):
    lane = lax.broadcasted_iota(jnp.int32, (rows, 2 * HD), 1)
    return [jnp.where(lane < HD, o[2 * j * rows:(2 * j + 1) * rows], o[(2 * j + 1) * rows:(2 * j + 2) * rows])
            for j in range(GQA // 2)]


def _attn_ctx_kernel(q_ref, k_ref, v_ref, sink_ref, o_ref):
    q = q_ref[...]
    k = k_ref[...].astype(BF16)
    v = v_ref[...].astype(BF16)
    outs = []
    for kv in range(N_KV):
        head = slice(kv * HD, (kv + 1) * HD)
        s = _dot_nt(k[:, head], _stack_group(q, kv))
        o = _softmax_av([s], _sink_row(sink_ref, kv, T_CTX), [v[:, head]])
        outs += _unstack_group(o, T_CTX)
    o_ref[...] = jnp.concatenate(outs, axis=1).astype(BF16)


def _attn_ctx(q, k, v, sink_b):
    return pl.pallas_call(
        _attn_ctx_kernel,
        grid=(N_CTX,),
        in_specs=[pl.BlockSpec((T_CTX, ATTN_W), lambda s: (s, 0)),
                  pl.BlockSpec((T_CTX, KV_W), lambda s: (s, 0)),
                  pl.BlockSpec((T_CTX, KV_W), lambda s: (s, 0)),
                  pl.BlockSpec((N_HEADS, LANES), lambda s: (0, 0))],
        out_specs=pl.BlockSpec((T_CTX, ATTN_W), lambda s: (s, 0)),
        out_shape=jax.ShapeDtypeStruct((ROWS, ATTN_W), BF16),
        input_output_aliases={0: 0},
        compiler_params=_cparams(("arbitrary",), 32),
        name="attn_ctx",
    )(q, k, v, sink_b)


def _rope(x, cos, sin_signed):
    w = x.shape[-1]
    nxt = pltpu.roll(x, w - 1, 1)
    prv = pltpu.roll(x, 1, 1)
    lane = lax.broadcasted_iota(jnp.int32, x.shape, 1)
    swapped = jnp.where((lane & 1) == 0, nxt, prv)
    return x * cos + swapped * sin_signed


def _attn_lat_kernel(q_ref, k_ref, v_ref, ck_ref, cv_ref, sink_ref, o_ref):
    i = pl.program_id(1)
    n_blk = T_LAT // WINDOW
    q = q_ref[...]

    def kv_block(j):
        st = pl.multiple_of(j * WINDOW, WINDOW)
        return k_ref[pl.ds(st, WINDOW), :].astype(BF16), v_ref[pl.ds(st, WINDOW), :].astype(BF16)

    k0, v0 = kv_block(jnp.maximum(i - 1, 0))
    k1, v1 = kv_block(i)
    k2, v2 = kv_block(jnp.minimum(i + 1, n_blk - 1))
    k_win = jnp.concatenate([k0, k1, k2], axis=0)
    k_ctx = ck_ref[0, 0].astype(BF16)
    v_ctx = cv_ref[0, 0].astype(BF16)
    rows = GQA * WINDOW
    c = lax.broadcasted_iota(jnp.int32, (WINDOW, rows), 0)
    r = lax.broadcasted_iota(jnp.int32, (WINDOW, rows), 1) & (WINDOW - 1)
    far = 2 * WINDOW
    prev_ok = c >= r + jnp.where(i > 0, 0, far)
    next_ok = c + jnp.where(i < n_blk - 1, 0, far) <= r
    outs = []
    for kv in range(N_KV):
        head = slice(kv * HD, (kv + 1) * HD)
        qs = _stack_group(q, kv)
        s_win = _dot_nt(k_win[:, head], qs)
        scores = [jnp.where(prev_ok, s_win[:WINDOW], NEG_INF), s_win[WINDOW:2 * WINDOW],
                  jnp.where(next_ok, s_win[2 * WINDOW:], NEG_INF), _dot_nt(k_ctx[:, head], qs)]
        values = [v0[:, head], v1[:, head], v2[:, head], v_ctx[:, head]]
        outs += _unstack_group(_softmax_av(scores, _sink_row(sink_ref, kv, WINDOW), values), WINDOW)
    o_ref[...] = jnp.concatenate(outs, axis=1).astype(BF16)


def _attn_lat(q, k, v, cache_k, cache_v, sink_b, l):
    n_blk = T_LAT // WINDOW
    off = ROWS_CTX // WINDOW
    return pl.pallas_call(
        _attn_lat_kernel,
        grid=(N_LAT, n_blk),
        in_specs=[pl.BlockSpec((WINDOW, ATTN_W), lambda b, i: (off + b * n_blk + i, 0)),
                  pl.BlockSpec((T_LAT, KV_W), lambda b, i: (ROWS_CTX // T_LAT + b, 0)),
                  pl.BlockSpec((T_LAT, KV_W), lambda b, i: (ROWS_CTX // T_LAT + b, 0)),
                  pl.BlockSpec((1, 1, PAST, KV_W), lambda b, i: (b, l, 0, 0)),
                  pl.BlockSpec((1, 1, PAST, KV_W), lambda b, i: (b, l, 0, 0)),
                  pl.BlockSpec((N_HEADS, LANES), lambda b, i: (0, 0))],
        out_specs=pl.BlockSpec((WINDOW, ATTN_W), lambda b, i: (off + b * n_blk + i, 0)),
        out_shape=jax.ShapeDtypeStruct((ROWS, ATTN_W), BF16),
        input_output_aliases={0: 0},
        compiler_params=_cparams(("arbitrary", "arbitrary"), 40),
        name="attn_lat",
    )(q, k, v, cache_k, cache_v, sink_b)


def _rope_tables():
    rows = T_LAT // GRID_W
    row = jnp.repeat(jnp.arange(rows, dtype=F32), GRID_W)
    col = jnp.tile(jnp.arange(GRID_W, dtype=F32), rows)
    n_freq = HD // 4
    freqs = ROPE_BASE ** (-jnp.arange(n_freq, dtype=F32) / n_freq)
    ang = jnp.concatenate([row[:, None] * freqs, col[:, None] * freqs], axis=-1)
    cos = jnp.repeat(jnp.cos(ang), 2, axis=-1)
    sign = jnp.tile(jnp.array([-1.0, 1.0], F32), HD // 2)
    sin = jnp.repeat(jnp.sin(ang), 2, axis=-1) * sign
    return (jnp.tile(cos, (1, N_HEADS)), jnp.tile(sin, (1, N_HEADS)),
            jnp.tile(cos, (1, N_KV)), jnp.tile(sin, (1, N_KV)))


def _four_kernel(u_ref, cs_ref, c64_ref, s64_ref, wf_ref, o_ref):
    ub = u_ref[...]
    uc = _dot(ub, c64_ref[...]).astype(BF16)
    us = _dot(ub, s64_ref[...]).astype(BF16)
    f = _dot(cs_ref[...], jnp.concatenate([uc, us], axis=0))
    o_ref[...] = _dot(f.astype(BF16), wf_ref[0].astype(BF16)).astype(BF16)


def _dft_tables(n):
    j = np.arange(n)
    ang = 2.0 * np.pi * ((j[:, None] * j[None, :]) % n) / n
    cs = np.concatenate([np.cos(ang), -np.sin(ang)], axis=1) / np.sqrt(n)
    return jnp.asarray(cs.astype(np.float32)).astype(BF16)


def _dft_channel_tables():
    j = np.arange(HD)
    ang = 2.0 * np.pi * ((j[:, None] * j[None, :]) % HD) / HD
    eye = np.eye(FOUR_H)
    c = np.kron(eye, np.cos(ang)) / np.sqrt(HD)
    s = np.kron(eye, np.sin(ang)) / np.sqrt(HD)
    return jnp.asarray(c.astype(np.float32)).astype(BF16), jnp.asarray(s.astype(np.float32)).astype(BF16)


def _fourier(uf, cs, c64, s64, w_fourier, l, n, n_seq, blk_off):
    return pl.pallas_call(
        _four_kernel,
        grid=(n_seq,),
        in_specs=[pl.BlockSpec((n, FOUR_W), lambda s: (blk_off + s, 0)),
                  pl.BlockSpec((n, 2 * n), lambda s: (0, 0)),
                  pl.BlockSpec((FOUR_W, FOUR_W), lambda s: (0, 0)),
                  pl.BlockSpec((FOUR_W, FOUR_W), lambda s: (0, 0)),
                  pl.BlockSpec((1, FOUR_W, FOUR_W), lambda s: (l, 0, 0))],
        out_specs=pl.BlockSpec((n, FOUR_W), lambda s: (blk_off + s, 0)),
        out_shape=jax.ShapeDtypeStruct((ROWS, FOUR_W), BF16),
        input_output_aliases={0: 0},
        compiler_params=_cparams(("arbitrary",), 40),
        name=f"fourier_{n}",
    )(uf, cs, c64, s64, w_fourier)


S5_TW = S5_T * S5_H
S5_R = 128
S5_TOK = S5_R * S5_T
S5_STEPS_CTX = ROWS_CTX // S5_TOK
S5_STEPS_LAT = (ROWS - ROWS_CTX) // S5_TOK
S5_STEPS = S5_STEPS_CTX + S5_STEPS_LAT
S5_SEQ = N_CTX // S5_STEPS_CTX


def _s5_tables(a_re, a_im, log_dt, b_re, b_im, c_re, c_im):
    f = lambda a: a.astype(F32)
    a_re, a_im, b_re, b_im, c_re, c_im = map(f, (a_re, a_im, b_re, b_im, c_re, c_im))
    dt = jnp.exp(f(log_dt))[..., None]
    x, y = a_re * dt, a_im * dt
    kk = jnp.arange(S5_T + 1, dtype=F32)[None, None, :, None, None]
    mag = jnp.exp(kk * x[:, :, None])
    pr, pi = mag * jnp.cos(kk * y[:, :, None]), mag * jnp.sin(kk * y[:, :, None])
    nr, ni = pr[:, :, 1] - 1.0, pi[:, :, 1]
    den = a_re * a_re + a_im * a_im
    qr, qi = (nr * a_re + ni * a_im) / den, (ni * a_re - nr * a_im) / den
    bb_r = qr[..., None] * b_re - qi[..., None] * b_im
    bb_i = qr[..., None] * b_im + qi[..., None] * b_re
    cat = lambda u, v: jnp.concatenate([u, v], axis=-1)
    prg, pig = pr.transpose(0, 1, 3, 2, 4), pi.transpose(0, 1, 3, 2, 4)
    pows = jnp.stack([cat(prg, prg), cat(-pig, pig), cat(prg, -pig), cat(-pig, -prg), cat(prg, pig), cat(-pig, prg)],
                     axis=3)
    bt_r, bt_i = bb_r.transpose(0, 1, 2, 4, 3), bb_i.transpose(0, 1, 2, 4, 3)
    wrows = jnp.stack([cat(bt_r, bt_i), cat(bt_i, bt_r), cat(bt_r, -bt_i), cat(c_re, c_re), cat(c_im, c_im)],
                      axis=3)
    ar, ai = pr[:, :, S5_T], pi[:, :, S5_T]
    decay = jnp.stack([cat(ar, ar), cat(-ai, ai), cat(ai, -ai), jnp.zeros_like(cat(ar, ar))], axis=3)
    return pows, wrows, decay


def _block_transpose(arrs):
    arrs = list(arrs)
    width = arrs[0].shape[1]
    blk = lax.broadcasted_iota(jnp.int32, arrs[0].shape, 1) >> 4
    s = len(arrs) // 2
    while s >= 1:
        keep = (blk & s) == 0
        for i in range(len(arrs)):
            if i & s:
                continue
            lo, hi = arrs[i], arrs[i + s]
            arrs[i] = jnp.where(keep, lo, pltpu.roll(hi, s * S5_H, 1))
            arrs[i + s] = jnp.where(keep, pltpu.roll(lo, width - s * S5_H, 1), hi)
        s //= 2
    return arrs


def _s5_build_operators(pw_ref, w_ref, e_scr, esw_scr, ft_scr, tp_scr):
    lane = lax.broadcasted_iota(jnp.int32, (S5_H, S5_TW), 1)
    for d in range(2):
        for g in range(S5_G):
            pw = lambda v, k: pw_ref[0, d, g, v, k:k + 1, :]
            b_ri, b_ir, b_conj, c_rr, c_ii = (w_ref[0, d, g, v] for v in range(5))
            lag_rows = []
            for t in range(S5_T):
                blk = slice(t * S5_H, (t + 1) * S5_H)
                ke = S5_T - 1 - t if d == 0 else t
                e_scr[d, g, blk, :] = (b_ri * pw(0, ke) + b_ir * pw(1, ke)).astype(BF16)
                esw_scr[d, g, blk, :] = (b_ir * pw(0, ke) - b_ri * pw(1, ke)).astype(BF16)
                kf = t + 1 if d == 0 else S5_T - t
                ft_scr[d, g, blk, :] = (c_rr * pw(2, kf) + c_ii * pw(3, kf)).astype(BF16)
                kl = t if d == 0 else S5_T - 1 - t
                lag_rows.append(c_rr * pw(4, kl) + c_ii * pw(5, kl))
            mh, ml = _split_bf16(jnp.concatenate(lag_rows, axis=0))
            bh, bl = _split_bf16(b_conj)
            kt = _dot_nt(bh, mh) + _dot_nt(bh, ml) + _dot_nt(bl, mh)
            for ti in range(S5_T):
                if d == 0:
                    sh = S5_H * ti
                    blk_rows = jnp.where(lane >= sh, pltpu.roll(kt, sh, 1) if sh else kt, 0.0)
                else:
                    sh = S5_H * (S5_T - 1 - ti)
                    blk_rows = jnp.where(lane < S5_TW - sh, pltpu.roll(kt, S5_TW - sh, 1) if sh else kt, 0.0)
                tp_scr[d, g, ti * S5_H:(ti + 1) * S5_H, :] = blk_rows.astype(BF16)


def _s5_kernel(us_ref, pw_ref, w_ref, a_ref, s0_ref, y_ref, fin_ref,
               e_scr, esw_scr, ft_scr, tp_scr, x_scr, y_scr, loc_scr, lsw_scr, prev_scr):
    step = pl.program_id(0)

    @pl.when(step == 0)
    def _():
        _s5_build_operators(pw_ref, w_ref, e_scr, esw_scr, ft_scr, tp_scr)

    rows = pl.ds
    x_by_step = [jnp.concatenate([us_ref[0, rows(tl, S5_R, stride=S5_T), :],
                                  us_ref[1, rows(tl, S5_R, stride=S5_T), :]], axis=1) for tl in range(S5_T)]
    for g, xg in enumerate(_block_transpose(x_by_step)):
        x_scr[g] = xg.astype(BF16)
    fin_ref[...] = jnp.zeros(fin_ref.shape, F32)

    def scan(d, nseq, nc):
        batch = 8
        for g0 in range(0, S5_G, batch):
            coefs = [[jnp.broadcast_to(a_ref[0, d, g, r:r + 1, :], (nseq, 2 * S5_P)) for r in range(3)]
                     for g in range(g0, g0 + batch)]
            init = tuple((s0_ref[0, 0, d, g, 0:nseq, :2 * S5_P], s0_ref[0, 0, d, g, 0:nseq, 2 * S5_P:])
                         for g in range(g0, g0 + batch))

            def body(j, carry):
                ci = j if d == 0 else nc - 1 - j
                out = []
                for idx in range(batch):
                    g = g0 + idx
                    s, t = carry[idx]
                    ca, cb, cc = coefs[idx]
                    prev_scr[g, rows(ci, nseq, stride=nc), :] = s
                    out.append((s * ca + t * cb + loc_scr[g, rows(ci, nseq, stride=nc), :],
                                t * ca + s * cc + lsw_scr[g, rows(ci, nseq, stride=nc), :]))
                return tuple(out)

            fin = lax.fori_loop(0, nc, body, init)
            for idx in range(batch):
                fin_ref[0, d, g0 + idx, 0:nseq, :] = fin[idx][0]

    for d in range(2):
        for g in range(S5_G):
            xg = x_scr[g]
            loc_scr[g] = _dot(xg, e_scr[d, g])
            lsw_scr[g] = _dot(xg, esw_scr[d, g])

        @pl.when(step < S5_STEPS_CTX)
        def _():
            scan(d, N_CTX // S5_STEPS_CTX, T_CTX // S5_T)

        @pl.when(step >= S5_STEPS_CTX)
        def _():
            scan(d, N_LAT // S5_STEPS_LAT, T_LAT // S5_T)

        for g in range(S5_G):
            yg = _dot(x_scr[g], tp_scr[d, g]) + _dot_nt(prev_scr[g].astype(BF16), ft_scr[d, g])
            if d == 0:
                y_scr[g] = yg
            else:
                y_scr[g] += yg
    for to, y_to in enumerate(_block_transpose([y_scr[g] for g in range(S5_G)])):
        y_ref[0, rows(to, S5_R, stride=S5_T), :] = y_to[:, :LANES]
        y_ref[1, rows(to, S5_R, stride=S5_T), :] = y_to[:, LANES:]


def _s5(us, tabs, s0, l):
    pows, wrows, decay = tabs
    lay = lambda *shape: pl.BlockSpec((1,) + shape, lambda h: (l,) + (0,) * len(shape))
    op_buf = pltpu.VMEM((2, S5_G, S5_TW, 2 * S5_P), BF16)
    scan_buf = pltpu.VMEM((S5_G, S5_R, 2 * S5_P), F32)
    return pl.pallas_call(
        _s5_kernel,
        grid=(S5_STEPS,),
        in_specs=[_halves_spec(S5_TOK, lambda h: h),
                  lay(2, S5_G, 6, S5_T + 1, 2 * S5_P), lay(2, S5_G, 5, S5_H, 2 * S5_P), lay(2, S5_G, 4, 2 * S5_P),
                  pl.BlockSpec((1, 1, 2, S5_G, S5_SEQ, 4 * S5_P), lambda h: (l, h, 0, 0, 0, 0))],
        out_specs=[_halves_spec(S5_TOK, lambda h: h),
                   pl.BlockSpec((1, 2, S5_G, S5_SEQ, 2 * S5_P), lambda h: (h, 0, 0, 0, 0))],
        out_shape=[_HALVES_SHAPE, jax.ShapeDtypeStruct((S5_STEPS, 2, S5_G, S5_SEQ, 2 * S5_P), F32)],
        scratch_shapes=[op_buf, op_buf, op_buf, pltpu.VMEM((2, S5_G, S5_TW, S5_TW), BF16),
                        pltpu.VMEM((S5_G, S5_R, S5_TW), BF16), pltpu.VMEM((S5_G, S5_R, S5_TW), F32),
                        scan_buf, scan_buf, scan_buf],
        compiler_params=_cparams(("arbitrary",), 48),
        name="s5",
    )(us, pows, wrows, decay, s0)


def _split_bf16(a):
    hi = a.astype(BF16)
    return hi, (a - hi.astype(F32)).astype(BF16)


def _outproj_kernel(at_ref, fo_ref, ys_ref, us_ref, d_ref, wg_ref, wo_ref, x_ref, mod_ref,
                    g_ref, b_ref, wr_ref, x1_ref, h2_ref, lg_ref, wob_ref):
    @pl.when(pl.program_id(0) == 0)
    def _():
        wob_ref[...] = wo_ref[0].astype(BF16)

    ys = jnp.concatenate([ys_ref[0], ys_ref[1]], axis=1)
    us = jnp.concatenate([us_ref[0], us_ref[1]], axis=1)
    g = jax.nn.gelu(ys + us * d_ref[0])
    s5 = g * jax.nn.sigmoid(_dot(g.astype(BF16), wg_ref[0].astype(BF16)))
    mix = (_dot(at_ref[...], wob_ref[0:ATTN_W, :])
           + _dot(fo_ref[...], wob_ref[ATTN_W:ATTN_W + FOUR_W, :])
           + _dot(s5.astype(BF16), wob_ref[ATTN_W + FOUR_W:, :]))
    m = mod_ref[0, 0]
    x1 = _layer_norm(ALPHA * x_ref[...] + m[2:3] * mix, g_ref[0], b_ref[0])
    x1_ref[...] = x1
    h2 = x1 * (1.0 + m[4:5]) + m[3:4]
    hb = h2.astype(BF16)
    h2_ref[...] = hb
    lg_ref[...] = _dot_nt(wr_ref[0].astype(BF16), hb)


def _outproj(attn, four, ys5, us, s5_d, w_glu, w_out, x, mod, ln_g, ln_b, w_router_t, l):
    row = lambda w: pl.BlockSpec((RB_OUT, w), lambda i: (i, 0))
    halves = _halves_spec(RB_OUT, lambda i: i)
    lay = lambda *shape: pl.BlockSpec((1,) + shape, lambda i: (l,) + (0,) * len(shape))
    return pl.pallas_call(
        _outproj_kernel,
        grid=(ROWS // RB_OUT,),
        in_specs=[row(ATTN_W), row(FOUR_W), halves, halves, lay(1, S5_W), lay(S5_W, S5_W), lay(D, D), row(D),
                  pl.BlockSpec((1, 1, 6, D), lambda i: (l, _mod_row(i, RB_OUT), 0, 0)),
                  lay(1, D), lay(1, D), lay(N_EXP, D)],
        out_specs=[row(D), row(D), pl.BlockSpec((N_EXP, RB_OUT), lambda i: (0, i))],
        out_shape=[jax.ShapeDtypeStruct((ROWS, D), F32), jax.ShapeDtypeStruct((ROWS, D), BF16),
                   jax.ShapeDtypeStruct((N_EXP, ROWS), F32)],
        scratch_shapes=[pltpu.VMEM((D, D), BF16)],
        compiler_params=_cparams(("arbitrary",), 40),
        name="outproj",
    )(attn, four, ys5, us, s5_d, w_glu, w_out, x, mod, ln_g, ln_b, w_router_t)


SEQ_GROUP = LANES // N_EXP


def _router_kernel(lg_ref, crow_ref, ccol_ref, gcol_ref, *, n, cap, n_seq):
    rows = n_seq * N_EXP
    lg = jnp.concatenate([lg_ref[:, s * n:(s + 1) * n] for s in range(n_seq)], axis=0).reshape(n_seq, N_EXP, n)
    e = jnp.exp(lg - jnp.max(lg, axis=1, keepdims=True))
    aff = (e / jnp.sum(e, axis=1, keepdims=True)).reshape(rows, n)
    thr_bits = jnp.zeros((rows, 1), jnp.int32)
    for bit in range(30, -1, -1):
        cand = thr_bits | (1 << bit)
        cnt = jnp.sum(jnp.where(aff >= lax.bitcast_convert_type(cand, F32), 1.0, 0.0), axis=1, keepdims=True)
        thr_bits = jnp.where(cnt >= cap, cand, thr_bits)
    thr = lax.bitcast_convert_type(thr_bits, F32)
    above = aff > thr
    tied = aff == thr
    need = cap - jnp.sum(jnp.where(above, 1.0, 0.0), axis=1, keepdims=True)
    r0 = lax.broadcasted_iota(jnp.int32, (n, n), 0)
    r1 = lax.broadcasted_iota(jnp.int32, (n, n), 1)
    before = jnp.where(r0 < r1, 1.0, 0.0).astype(BF16)
    eye = jnp.where(r0 == r1, 1.0, 0.0).astype(BF16)
    tied_rank = _dot(jnp.where(tied, 1.0, 0.0).astype(BF16), before)
    sel = above | (tied & (tied_rank < need))
    slot = _dot(jnp.where(sel, 1.0, 0.0).astype(BF16), before)
    code = jnp.where(sel, slot + 1.0, 0.0)
    gate = jnp.where(sel, aff, 0.0)
    for s in range(n_seq):
        crow_ref[:, s * n:(s + 1) * n] = code[s * N_EXP:(s + 1) * N_EXP]
    for grp in range(pl.cdiv(n_seq, SEQ_GROUP)):
        lo, hi = grp * LANES, min((grp + 1) * LANES, rows)
        pad = [jnp.zeros((LANES - (hi - lo), n), F32)] if hi - lo < LANES else []
        ccol_ref[grp] = _dot_nt(eye, jnp.concatenate([code[lo:hi]] + pad, axis=0).astype(BF16))
        gcol_ref[grp] = _dot_nt(eye, jnp.concatenate([gate[lo:hi]] + pad, axis=0).astype(BF16))


def _router(lg, n, n_seq, blk, cap):
    groups = pl.cdiv(n_seq, SEQ_GROUP)
    whole = lambda *shape: pl.BlockSpec(shape, lambda i: (0,) * len(shape))
    return pl.pallas_call(
        functools.partial(_router_kernel, n=n, cap=cap, n_seq=n_seq),
        grid=(1,),
        in_specs=[pl.BlockSpec((N_EXP, n_seq * n), lambda i: (0, blk))],
        out_specs=[whole(N_EXP, n_seq * n), whole(groups, n, LANES), whole(groups, n, LANES)],
        out_shape=[jax.ShapeDtypeStruct((N_EXP, n_seq * n), F32),
                   jax.ShapeDtypeStruct((groups, n, LANES), F32),
                   jax.ShapeDtypeStruct((groups, n, LANES), F32)],
        compiler_params=_cparams(("arbitrary",), 48),
        name=f"router_{n}",
    )(lg)


def _gather_kernel(crow_ref, h_ref, o_ref, *, n, cap):
    code = crow_ref[...]
    slot1 = (lax.broadcasted_iota(jnp.int32, (cap, n), 0) + 1).astype(F32)
    onehot = jnp.concatenate(
        [jnp.where(code[e:e + 1, :] == slot1, 1.0, 0.0).astype(BF16) for e in range(N_EXP)], axis=0)
    xs = _dot(onehot, h_ref[...])
    o_ref[...] = xs.reshape(N_EXP, cap, D).astype(BF16)


def _gather(crow, h2, n, n_seq, blk_off, cap):
    return pl.pallas_call(
        functools.partial(_gather_kernel, n=n, cap=cap),
        grid=(n_seq,),
        in_specs=[pl.BlockSpec((N_EXP, n), lambda s: (0, s)),
                  pl.BlockSpec((n, D), lambda s: (blk_off + s, 0))],
        out_specs=pl.BlockSpec((N_EXP, cap, D), lambda s: (0, s, 0)),
        out_shape=jax.ShapeDtypeStruct((N_EXP, n_seq * cap, D), BF16),
        compiler_params=_cparams(("arbitrary",), 48),
        name=f"gather_{n}",
    )(crow, h2)


FF_BLK = 512
SLOTS = N_CTX * CAP_CTX


def _ffn_kernel(xc_ref, xq_ref, wg_ref, wu_ref, wd_ref, yc_ref, yq_ref, acc_ref):
    f = pl.program_id(1)
    x = jnp.concatenate([xc_ref[0], xq_ref[0]], axis=0)
    g = _dot(x, wg_ref[0, 0].astype(BF16))
    u = _dot(x, wu_ref[0, 0].astype(BF16))
    hid = (g * jax.nn.sigmoid(g) * u).astype(BF16)
    y = _dot(hid, wd_ref[0, 0].astype(BF16))

    @pl.when(f == 0)
    def _():
        acc_ref[...] = y

    @pl.when(f == FF // FF_BLK - 1)
    def _():
        tot = acc_ref[...] + y
        yc_ref[0] = tot[:SLOTS].astype(BF16)
        yq_ref[0] = tot[SLOTS:].astype(BF16)


def _ffn(xs_c, xs_q, w_gate, w_up, w_down, l):
    assert FF // FF_BLK == 2
    xspec = pl.BlockSpec((1, SLOTS, D), lambda e, f: (e, 0, 0))
    return pl.pallas_call(
        _ffn_kernel,
        grid=(N_EXP, FF // FF_BLK),
        in_specs=[xspec, xspec,
                  pl.BlockSpec((1, 1, D, FF_BLK), lambda e, f: (l, e, 0, f)),
                  pl.BlockSpec((1, 1, D, FF_BLK), lambda e, f: (l, e, 0, f)),
                  pl.BlockSpec((1, 1, FF_BLK, D), lambda e, f: (l, e, f, 0))],
        out_specs=[xspec, xspec],
        out_shape=[jax.ShapeDtypeStruct((N_EXP, SLOTS, D), BF16)] * 2,
        scratch_shapes=[pltpu.VMEM((2 * SLOTS, D), F32)],
        compiler_params=_cparams(("arbitrary", "arbitrary"), 48),
        name="ffn",
    )(xs_c, xs_q, w_gate, w_up, w_down)


def _combine_kernel(ccol_ref, gcol_ref, ys_ref, x_ref, mod_ref, g_ref, b_ref, o_ref, *, cap):
    width = N_EXP * cap
    first = (pl.program_id(0) % SEQ_GROUP) * N_EXP
    k = lax.broadcasted_iota(jnp.int32, (LANES, width), 0)
    j = lax.broadcasted_iota(jnp.int32, (LANES, width), 1)
    shift = cap.bit_length() - 1
    spread = jnp.where((j >> shift) + first == k, 1.0, 0.0).astype(BF16)
    code = _dot(ccol_ref[0].astype(BF16), spread)
    gate = _dot(gcol_ref[0].astype(BF16), spread)
    slot1 = ((lax.broadcasted_iota(jnp.int32, (1, width), 1) & (cap - 1)) + 1).astype(F32)
    weights = jnp.where(code == slot1, gate, 0.0).astype(BF16)
    moe = _dot(weights, ys_ref[...].reshape(width, D))
    m = mod_ref[0, 0]
    o_ref[...] = _layer_norm(ALPHA * x_ref[...] + m[5:6] * moe, g_ref[0], b_ref[0])


def _combine(ccol, gcol, ys, x1, mod, ln_g, ln_b, l, cap, n_seq, rb, rb_per_seq, rb_off, in_place):
    lay = lambda *shape: pl.BlockSpec((1,) + shape, lambda s, r: (l,) + (0,) * len(shape))
    table = pl.BlockSpec((1, rb, LANES), lambda s, r: (s // SEQ_GROUP, r, 0))
    block = lambda s, r: rb_off + s * rb_per_seq + r
    out_off = rb_off if in_place else 0
    return pl.pallas_call(
        functools.partial(_combine_kernel, cap=cap),
        grid=(n_seq, rb_per_seq),
        in_specs=[table, table,
                  pl.BlockSpec((N_EXP, cap, D), lambda s, r: (0, s, 0)),
                  pl.BlockSpec((rb, D), lambda s, r: (block(s, r), 0)),
                  pl.BlockSpec((1, 1, 6, D), lambda s, r: (l, _mod_row(block(s, r), rb), 0, 0)),
                  lay(1, D), lay(1, D)],
        out_specs=pl.BlockSpec((rb, D), lambda s, r: (block(s, r) - rb_off + out_off, 0)),
        out_shape=jax.ShapeDtypeStruct((ROWS if in_place else n_seq * rb_per_seq * rb, D), F32),
        input_output_aliases={3: 0} if in_place else {},
        compiler_params=_cparams(("arbitrary", "arbitrary"), 48),
        name=f"combine_{cap}",
    )(ccol, gcol, ys, x1, mod, ln_g, ln_b)


def kernel(x_prompt, x_sample, cache_k, cache_v, state_s5_re, state_s5_im, c, c_ctx, ln_in_g, ln_in_b, w_ada, b_ada,
           w_in, w_fourier, attn_sink, s5_a_re, s5_a_im, s5_log_dt, s5_b_re, s5_b_im, s5_c_re, s5_c_im, s5_d,
           s5_w_glu, w_out, ln1_g, ln1_b, w_router, w_gate, w_up, w_down, ln2_g, ln2_b):
    x = _ln_in(x_prompt.reshape(ROWS_CTX, D), x_sample.reshape(ROWS - ROWS_CTX, D), ln_in_g, ln_in_b)
    cond8 = jnp.concatenate([c_ctx[None], c, jnp.zeros((8 - 1 - N_LAT, D), F32)], axis=0)
    mod = _ada(cond8, w_ada, b_ada).reshape(DEPTH, 8, 6, D)

    rope_tabs = _rope_tables()
    cs_ctx, cs_lat = _dft_tables(T_CTX), _dft_tables(T_LAT)
    c64, s64 = _dft_channel_tables()
    cache_k = cache_k.reshape(N_LAT, DEPTH, PAST, KV_W)
    cache_v = cache_v.reshape(N_LAT, DEPTH, PAST, KV_W)
    w_router_t = jnp.swapaxes(w_router, 1, 2)
    sink_b = jnp.broadcast_to(attn_sink[:, :, None], (DEPTH, N_HEADS, LANES))
    s5_tabs = _s5_tables(s5_a_re, s5_a_im, s5_log_dt, s5_b_re, s5_b_im, s5_c_re, s5_c_im)
    lat_per_step = N_LAT // S5_STEPS_LAT
    s0 = jnp.concatenate([state_s5_re, state_s5_im, state_s5_im, state_s5_re], axis=-1)
    s0 = s0.reshape(S5_STEPS_LAT, lat_per_step, DEPTH, 2, S5_G, 4 * S5_P).transpose(2, 0, 3, 4, 1, 5)
    s0 = jnp.pad(s0, ((0, 0), (S5_STEPS_CTX, 0), (0, 0), (0, 0), (0, S5_SEQ - lat_per_step), (0, 0)))
    ln1_g, ln1_b, ln2_g, ln2_b, s5_d = (a.reshape(DEPTH, 1, -1) for a in (ln1_g, ln1_b, ln2_g, ln2_b, s5_d))

    new_k, new_v, new_s = [], [], []
    for l in range(DEPTH):
        q, k, v, uf, us = _inproj(x, mod, w_in, rope_tabs, l)
        new_k.append(k[:ROWS_CTX].reshape(N_CTX, T_CTX, N_KV, HD))
        new_v.append(v[:ROWS_CTX].reshape(N_CTX, T_CTX, N_KV, HD))
        attn = _attn_lat(_attn_ctx(q, k, v, sink_b[l]), k, v, cache_k, cache_v, sink_b[l], l)
        four = _fourier(uf, cs_ctx, c64, s64, w_fourier, l, T_CTX, N_CTX, 0)
        four = _fourier(four, cs_lat, c64, s64, w_fourier, l, T_LAT, N_LAT, ROWS_CTX // T_LAT)
        y5, fin = _s5(us, s5_tabs, s0, l)
        new_s.append(fin[:S5_STEPS_CTX].transpose(0, 3, 1, 2, 4).reshape(N_CTX, 2, S5_G, 2 * S5_P))
        x1, h2, lg = _outproj(attn, four, y5, us, s5_d, s5_w_glu, w_out, x, mod, ln1_g, ln1_b, w_router_t, l)
        crow_c, ccol_c, gcol_c = _router(lg, T_CTX, N_CTX, 0, CAP_CTX)
        crow_q, ccol_q, gcol_q = _router(lg, T_LAT, N_LAT, 1, CAP_LAT)
        xs_c = _gather(crow_c, h2, T_CTX, N_CTX, 0, CAP_CTX)
        xs_q = _gather(crow_q, h2, T_LAT, N_LAT, ROWS_CTX // T_LAT, CAP_LAT)
        ys_c, ys_q = _ffn(xs_c, xs_q, w_gate, w_up, w_down, l)
        args_c = (ccol_c, gcol_c, ys_c), (mod, ln2_g, ln2_b, l, CAP_CTX, N_CTX, T_CTX, 1, 0)
        args_q = (ccol_q, gcol_q, ys_q), (mod, ln2_g, ln2_b, l, CAP_LAT, N_LAT, RB_COMBINE_LAT,
                                          T_LAT // RB_COMBINE_LAT, ROWS_CTX // RB_COMBINE_LAT)
        if l < DEPTH - 1:
            x = _combine(*args_c[0], x1, *args_c[1], in_place=True)
            x = _combine(*args_q[0], x, *args_q[1], in_place=True)
        else:
            y_ctx = _combine(*args_c[0], x1, *args_c[1], in_place=False)
            y_lat = _combine(*args_q[0], x1, *args_q[1], in_place=False)

    new_s = jnp.stack(new_s, axis=1)
    return (y_ctx.reshape(N_CTX, T_CTX, D), y_lat.reshape(N_LAT, T_LAT, D),
            jnp.stack(new_k, axis=1), jnp.stack(new_v, axis=1), new_s[..., :S5_P], new_s[..., S5_P:])
```

```python
import functools

import jax
import jax.numpy as jnp
import numpy as np
from jax import lax
from jax.experimental import pallas as pl
from jax.experimental.pallas import tpu as pltpu

F32 = jnp.float32
BF16 = jnp.bfloat16

D = 1024
N_CTX, T_CTX = 16, 256
N_LAT, T_LAT = 4, 1024
ROWS_CTX = N_CTX * T_CTX
ROWS = ROWS_CTX + N_LAT * T_LAT
DEPTH = 4
PAST = 512
GRID_W = 64
N_HEADS, N_KV, HD = 8, 2, 64
GQA = N_HEADS // N_KV
ATTN_W, KV_W = N_HEADS * HD, N_KV * HD
LOG2E = 1.4426950408889634
Q_SCALE = HD ** -0.5 * LOG2E
WINDOW = 128
FOUR_H, FOUR_W = 4, 256
S5_G, S5_H, S5_P, S5_W = 16, 16, 64, 256
S5_T = 16
IN_W = ATTN_W + 2 * KV_W + FOUR_W + S5_W
N_EXP, FF = 16, 1024
CAP_CTX, CAP_LAT = 2 * T_CTX // N_EXP, 2 * T_LAT // N_EXP
LN_EPS = 1e-5
NEG_INF = -1e30
ALPHA = (2 * DEPTH) ** 0.25
ROPE_BASE = 10000.0
RB_IN = 1024
RB_OUT = 512
RB_COMBINE_LAT = 512
LANES = 128
MIB = 2 ** 20
HIGHEST = lax.Precision.HIGHEST


def _cparams(sem, vmem_mib):
    return pltpu.CompilerParams(dimension_semantics=sem, vmem_limit_bytes=vmem_mib * MIB)


def _dot(a, b):
    return jnp.dot(a, b, preferred_element_type=F32)


def _dot_nt(a, b):
    return lax.dot_general(a, b, (((1,), (1,)), ((), ())), preferred_element_type=F32)


def _dot_tn(a, b):
    return lax.dot_general(a, b, (((0,), (0,)), ((), ())), preferred_element_type=F32)


def _layer_norm(x, g, b):
    mu = jnp.mean(x, axis=-1, keepdims=True)
    xc = x - mu
    var = jnp.mean(xc * xc, axis=-1, keepdims=True)
    return xc * lax.rsqrt(var + LN_EPS) * g + b


def _mod_row(block, block_rows):
    first = block * block_rows
    return jnp.where(first < ROWS_CTX, 0, 1 + (first - ROWS_CTX) // T_LAT)


_HALVES_SHAPE = jax.ShapeDtypeStruct((2, ROWS, LANES), F32)


def _halves_spec(rows, row_block):
    return pl.BlockSpec((2, rows, LANES), lambda *ids: (0, row_block(*ids), 0))


def _two_stream_specs(rows, width):
    n_ctx = ROWS_CTX // rows
    return (pl.BlockSpec((rows, width), lambda i: (jnp.minimum(i, n_ctx - 1), 0)),
            pl.BlockSpec((rows, width), lambda i: (jnp.maximum(i - n_ctx, 0), 0)))


def _pick(rows, ctx_ref, lat_ref):
    return lax.cond(pl.program_id(0) < ROWS_CTX // rows, lambda: ctx_ref[...], lambda: lat_ref[...])


def _ln_in_kernel(xc_ref, xq_ref, g_ref, b_ref, o_ref):
    o_ref[...] = _layer_norm(_pick(RB_OUT, xc_ref, xq_ref), g_ref[...], b_ref[...])


def _ln_in(x_ctx, x_lat, g, b):
    return pl.pallas_call(
        _ln_in_kernel,
        grid=(ROWS // RB_OUT,),
        in_specs=[*_two_stream_specs(RB_OUT, D),
                  pl.BlockSpec((1, D), lambda i: (0, 0)),
                  pl.BlockSpec((1, D), lambda i: (0, 0))],
        out_specs=pl.BlockSpec((RB_OUT, D), lambda i: (i, 0)),
        out_shape=jax.ShapeDtypeStruct((ROWS, D), F32),
        compiler_params=_cparams(("arbitrary",), 32),
        name="ln_in",
    )(x_ctx, x_lat, g.reshape(1, D), b.reshape(1, D))


def _ada_kernel(c_ref, w_ref, b_ref, o_ref):
    c = c_ref[...]
    s = (c * jax.nn.sigmoid(c)).astype(BF16)
    o_ref[0] = _dot(s, w_ref[0].astype(BF16)) + b_ref[0]


def _ada(cond8, w_ada, b_ada):
    tn = 1536
    return pl.pallas_call(
        _ada_kernel,
        grid=(DEPTH, 6 * D // tn),
        in_specs=[pl.BlockSpec((8, D), lambda l, j: (0, 0)),
                  pl.BlockSpec((1, D, tn), lambda l, j: (l, 0, j)),
                  pl.BlockSpec((1, 1, tn), lambda l, j: (l, 0, j))],
        out_specs=pl.BlockSpec((1, 8, tn), lambda l, j: (l, 0, j)),
        out_shape=jax.ShapeDtypeStruct((DEPTH, 8, 6 * D), F32),
        compiler_params=_cparams(("arbitrary", "arbitrary"), 40),
        name="adaln",
    )(cond8, w_ada, b_ada.reshape(DEPTH, 1, 6 * D))


def _inproj_kernel(x_ref, mod_ref, w_ref, wsw_ref, cos_ref, sin_ref, q_ref, k_ref, v_ref, uf_ref, us_ref, kt_ref, vt_ref,
                   wb_ref, wswb_ref):
    step = pl.program_id(0)
    n_ctx_steps = ROWS_CTX // RB_IN

    @pl.when(step == 0)
    def _():
        wb_ref[...] = w_ref[0].astype(BF16)
        wswb_ref[...] = wsw_ref[0].astype(BF16)

    m = mod_ref[0, 0]
    hb = (x_ref[...] * (1.0 + m[1:2]) + m[0:1]).astype(BF16)
    p = _dot(hb, wb_ref[...])
    q = p[:, :ATTN_W]
    k = p[:, ATTN_W:ATTN_W + KV_W]
    v = p[:, ATTN_W + KV_W:ATTN_W + 2 * KV_W]

    psw = _dot(hb, wswb_ref[...])
    cos, sin = cos_ref[...], sin_ref[...]
    wide = lambda t: jnp.concatenate([t] * (ATTN_W // KV_W), axis=1)
    latent = step >= n_ctx_steps
    q = jnp.where(latent, q * wide(cos) + psw[:, :ATTN_W] * wide(sin), q)
    k = jnp.where(latent, k * cos + psw[:, ATTN_W:] * sin, k)
    q_ref[...] = (q * Q_SCALE).astype(BF16)
    k_ref[...] = k
    v_ref[...] = v

    @pl.when(step < n_ctx_steps)
    def _():
        kt_ref[...] = k.T
        vt_ref[...] = v.T

    uf_ref[...] = p[:, ATTN_W + 2 * KV_W:ATTN_W + 2 * KV_W + FOUR_W].astype(BF16)
    for half in range(S5_W // LANES):
        lo = ATTN_W + 2 * KV_W + FOUR_W + half * LANES
        us_ref[half] = p[:, lo:lo + LANES]


def _inproj(x, mod, w_in, w_in_swapped, rope_tabs, l):
    assert RB_IN == T_LAT
    outs = ((ATTN_W, BF16), (KV_W, F32), (KV_W, F32), (FOUR_W, BF16))
    table = pl.BlockSpec((T_LAT, KV_W), lambda i: (0, 0))
    feat_major = pl.BlockSpec((KV_W, RB_IN), lambda i: (0, jnp.minimum(i, ROWS_CTX // RB_IN - 1)))
    return pl.pallas_call(
        _inproj_kernel,
        grid=(ROWS // RB_IN,),
        in_specs=[pl.BlockSpec((RB_IN, D), lambda i: (i, 0)),
                  pl.BlockSpec((1, 1, 6, D), lambda i: (l, _mod_row(i, RB_IN), 0, 0)),
                  pl.BlockSpec((1, D, IN_W), lambda i: (l, 0, 0)),
                  pl.BlockSpec((1, D, ATTN_W + KV_W), lambda i: (l, 0, 0)),
                  table, table],
        out_specs=([pl.BlockSpec((RB_IN, w), lambda i: (i, 0)) for w, _ in outs] + [_halves_spec(RB_IN, lambda i: i)]
                   + [feat_major, feat_major]),
        out_shape=([jax.ShapeDtypeStruct((ROWS, w), dt) for w, dt in outs] + [_HALVES_SHAPE]
                   + [jax.ShapeDtypeStruct((KV_W, ROWS_CTX), F32)] * 2),
        scratch_shapes=[pltpu.VMEM((D, IN_W), BF16), pltpu.VMEM((D, ATTN_W + KV_W), BF16)],
        compiler_params=_cparams(("arbitrary",), 56),
        name="inproj",
    )(x, mod, w_in, w_in_swapped, *rope_tabs)


SINK_ROWS = 16


def _softmax_av(score_blocks, sink, value_blocks):
    m = sink
    for s in score_blocks:
        m = jnp.maximum(m, jnp.max(s, axis=0, keepdims=True))
    first = lax.broadcasted_iota(jnp.int32, (SINK_ROWS, 1), 0) == 0
    probs = [jnp.exp2(s - m).astype(BF16) for s in score_blocks]
    probs.append(jnp.where(first, jnp.exp2(sink - m), 0.0).astype(BF16))
    ext = []
    for v in value_blocks:
        ones = jnp.ones(v.shape, BF16)
        ext.append(jnp.concatenate([v, v, ones, ones], axis=1))
    row = lax.broadcasted_iota(jnp.int32, (SINK_ROWS, 4 * HD), 0)
    lane = lax.broadcasted_iota(jnp.int32, (SINK_ROWS, 4 * HD), 1)
    ext.append(jnp.where((row == 0) & (lane >= 2 * HD), 1.0, 0.0).astype(BF16))
    acc = _dot_tn(jnp.concatenate(probs, axis=0), jnp.concatenate(ext, axis=0))
    return acc[:, :2 * HD] / acc[:, 2 * HD:]


def _stack_group(q, kv):
    return jnp.concatenate([q[:, (kv * GQA + g) * HD:(kv * GQA + g + 1) * HD] for g in range(GQA)], axis=0)


def _sink_row(sink_ref, kv, rows):
    return LOG2E * jnp.concatenate(
        [jnp.broadcast_to(sink_ref[kv * GQA + g:kv * GQA + g + 1, 0:1], (1, rows)) for g in range(GQA)], axis=1)


def _unstack_group(o, rows):
    lane = lax.broadcasted_iota(jnp.int32, (rows, 2 * HD), 1)
    return [jnp.where(lane < HD, o[2 * j * rows:(2 * j + 1) * rows], o[(2 * j + 1) * rows:(2 * j + 2) * rows])
            for j in range(GQA // 2)]


def _attn_ctx_kernel(q_ref, k_ref, v_ref, sink_ref, o_ref):
    q = q_ref[...]
    k = k_ref[...].astype(BF16)
    v = v_ref[...].astype(BF16)
    outs = []
    for kv in range(N_KV):
        head = slice(kv * HD, (kv + 1) * HD)
        s = _dot_nt(k[:, head], _stack_group(q, kv))
        o = _softmax_av([s], _sink_row(sink_ref, kv, T_CTX), [v[:, head]])
        outs += _unstack_group(o, T_CTX)
    o_ref[...] = jnp.concatenate(outs, axis=1).astype(BF16)


def _attn_ctx(q, k, v, sink_b):
    return pl.pallas_call(
        _attn_ctx_kernel,
        grid=(N_CTX,),
        in_specs=[pl.BlockSpec((T_CTX, ATTN_W), lambda s: (s, 0)),
                  pl.BlockSpec((T_CTX, KV_W), lambda s: (s, 0)),
                  pl.BlockSpec((T_CTX, KV_W), lambda s: (s, 0)),
                  pl.BlockSpec((N_HEADS, LANES), lambda s: (0, 0))],
        out_specs=pl.BlockSpec((T_CTX, ATTN_W), lambda s: (s, 0)),
        out_shape=jax.ShapeDtypeStruct((ROWS, ATTN_W), BF16),
        input_output_aliases={0: 0},
        compiler_params=_cparams(("arbitrary",), 32),
        name="attn_ctx",
    )(q, k, v, sink_b)


def _attn_lat_kernel(q_ref, k_ref, v_ref, ck_ref, cv_ref, sink_ref, o_ref):
    i = pl.program_id(1)
    n_blk = T_LAT // WINDOW
    q = q_ref[...]

    def kv_block(j):
        st = pl.multiple_of(j * WINDOW, WINDOW)
        return k_ref[pl.ds(st, WINDOW), :].astype(BF16), v_ref[pl.ds(st, WINDOW), :].astype(BF16)

    k0, v0 = kv_block(jnp.maximum(i - 1, 0))
    k1, v1 = kv_block(i)
    k2, v2 = kv_block(jnp.minimum(i + 1, n_blk - 1))
    k_win = jnp.concatenate([k0, k1, k2], axis=0)
    k_ctx = ck_ref[0, 0].astype(BF16)
    v_ctx = cv_ref[0, 0].astype(BF16)
    rows = GQA * WINDOW
    c = lax.broadcasted_iota(jnp.int32, (WINDOW, rows), 0)
    r = lax.broadcasted_iota(jnp.int32, (WINDOW, rows), 1) & (WINDOW - 1)
    far = 2 * WINDOW
    prev_ok = c >= r + jnp.where(i > 0, 0, far)
    next_ok = c + jnp.where(i < n_blk - 1, 0, far) <= r
    outs = []
    for kv in range(N_KV):
        head = slice(kv * HD, (kv + 1) * HD)
        qs = _stack_group(q, kv)
        s_win = _dot_nt(k_win[:, head], qs)
        scores = [jnp.where(prev_ok, s_win[:WINDOW], NEG_INF), s_win[WINDOW:2 * WINDOW],
                  jnp.where(next_ok, s_win[2 * WINDOW:], NEG_INF), _dot_nt(k_ctx[:, head], qs)]
        values = [v0[:, head], v1[:, head], v2[:, head], v_ctx[:, head]]
        outs += _unstack_group(_softmax_av(scores, _sink_row(sink_ref, kv, WINDOW), values), WINDOW)
    o_ref[...] = jnp.concatenate(outs, axis=1).astype(BF16)


def _attn_lat(q, k, v, cache_k, cache_v, sink_b, l):
    n_blk = T_LAT // WINDOW
    off = ROWS_CTX // WINDOW
    return pl.pallas_call(
        _attn_lat_kernel,
        grid=(N_LAT, n_blk),
        in_specs=[pl.BlockSpec((WINDOW, ATTN_W), lambda b, i: (off + b * n_blk + i, 0)),
                  pl.BlockSpec((T_LAT, KV_W), lambda b, i: (ROWS_CTX // T_LAT + b, 0)),
                  pl.BlockSpec((T_LAT, KV_W), lambda b, i: (ROWS_CTX // T_LAT + b, 0)),
                  pl.BlockSpec((1, 1, PAST, KV_W), lambda b, i: (b, l, 0, 0)),
                  pl.BlockSpec((1, 1, PAST, KV_W), lambda b, i: (b, l, 0, 0)),
                  pl.BlockSpec((N_HEADS, LANES), lambda b, i: (0, 0))],
        out_specs=pl.BlockSpec((WINDOW, ATTN_W), lambda b, i: (off + b * n_blk + i, 0)),
        out_shape=jax.ShapeDtypeStruct((ROWS, ATTN_W), BF16),
        input_output_aliases={0: 0},
        compiler_params=_cparams(("arbitrary", "arbitrary"), 40),
        name="attn_lat",
    )(q, k, v, cache_k, cache_v, sink_b)


def _rope_tables():
    rows = T_LAT // GRID_W
    row = jnp.repeat(jnp.arange(rows, dtype=F32), GRID_W)
    col = jnp.tile(jnp.arange(GRID_W, dtype=F32), rows)
    n_freq = HD // 4
    freqs = ROPE_BASE ** (-jnp.arange(n_freq, dtype=F32) / n_freq)
    ang = jnp.concatenate([row[:, None] * freqs, col[:, None] * freqs], axis=-1)
    cos = jnp.repeat(jnp.cos(ang), 2, axis=-1)
    sign = jnp.tile(jnp.array([-1.0, 1.0], F32), HD // 2)
    sin = jnp.repeat(jnp.sin(ang), 2, axis=-1) * sign
    return jnp.tile(cos, (1, N_KV)), jnp.tile(sin, (1, N_KV))


def _four_kernel(u_ref, cs_ref, c64_ref, s64_ref, wf_ref, o_ref):
    ub = u_ref[...]
    uc = _dot(ub, c64_ref[...]).astype(BF16)
    us = _dot(ub, s64_ref[...]).astype(BF16)
    f = _dot(cs_ref[...], jnp.concatenate([uc, us], axis=0))
    o_ref[...] = _dot(f.astype(BF16), wf_ref[0].astype(BF16)).astype(BF16)


def _dft_tables(n):
    j = np.arange(n)
    ang = 2.0 * np.pi * ((j[:, None] * j[None, :]) % n) / n
    cs = np.concatenate([np.cos(ang), -np.sin(ang)], axis=1) / np.sqrt(n)
    return jnp.asarray(cs.astype(np.float32)).astype(BF16)


def _dft_channel_tables():
    j = np.arange(HD)
    ang = 2.0 * np.pi * ((j[:, None] * j[None, :]) % HD) / HD
    eye = np.eye(FOUR_H)
    c = np.kron(eye, np.cos(ang)) / np.sqrt(HD)
    s = np.kron(eye, np.sin(ang)) / np.sqrt(HD)
    return jnp.asarray(c.astype(np.float32)).astype(BF16), jnp.asarray(s.astype(np.float32)).astype(BF16)


def _fourier(uf, cs, c64, s64, w_fourier, l, n, n_seq, blk_off):
    return pl.pallas_call(
        _four_kernel,
        grid=(n_seq,),
        in_specs=[pl.BlockSpec((n, FOUR_W), lambda s: (blk_off + s, 0)),
                  pl.BlockSpec((n, 2 * n), lambda s: (0, 0)),
                  pl.BlockSpec((FOUR_W, FOUR_W), lambda s: (0, 0)),
                  pl.BlockSpec((FOUR_W, FOUR_W), lambda s: (0, 0)),
                  pl.BlockSpec((1, FOUR_W, FOUR_W), lambda s: (l, 0, 0))],
        out_specs=pl.BlockSpec((n, FOUR_W), lambda s: (blk_off + s, 0)),
        out_shape=jax.ShapeDtypeStruct((ROWS, FOUR_W), BF16),
        input_output_aliases={0: 0},
        compiler_params=_cparams(("arbitrary",), 40),
        name=f"fourier_{n}",
    )(uf, cs, c64, s64, w_fourier)


S5_TW = S5_T * S5_H
S5_R = 128
S5_TOK = S5_R * S5_T
S5_STEPS_CTX = ROWS_CTX // S5_TOK
S5_STEPS_LAT = (ROWS - ROWS_CTX) // S5_TOK
S5_STEPS = S5_STEPS_CTX + S5_STEPS_LAT
S5_SEQ = N_CTX // S5_STEPS_CTX


def _s5_tables(a_re, a_im, log_dt, b_re, b_im, c_re, c_im):
    f = lambda a: a.astype(F32)
    a_re, a_im, b_re, b_im, c_re, c_im = map(f, (a_re, a_im, b_re, b_im, c_re, c_im))
    dt = jnp.exp(f(log_dt))[..., None]
    x, y = a_re * dt, a_im * dt
    kk = jnp.arange(S5_T + 1, dtype=F32)[None, None, :, None, None]
    mag = jnp.exp(kk * x[:, :, None])
    pr, pi = mag * jnp.cos(kk * y[:, :, None]), mag * jnp.sin(kk * y[:, :, None])
    nr, ni = pr[:, :, 1] - 1.0, pi[:, :, 1]
    den = a_re * a_re + a_im * a_im
    qr, qi = (nr * a_re + ni * a_im) / den, (ni * a_re - nr * a_im) / den
    bb_r = qr[..., None] * b_re - qi[..., None] * b_im
    bb_i = qr[..., None] * b_im + qi[..., None] * b_re
    cat = lambda u, v: jnp.concatenate([u, v], axis=-1)
    prg, pig = pr.transpose(0, 1, 3, 2, 4), pi.transpose(0, 1, 3, 2, 4)
    pows = jnp.stack([cat(prg, prg), cat(-pig, pig), cat(prg, -pig), cat(-pig, -prg), cat(prg, pig), cat(-pig, prg)],
                     axis=3)
    bt_r, bt_i = bb_r.transpose(0, 1, 2, 4, 3), bb_i.transpose(0, 1, 2, 4, 3)
    wrows = jnp.stack([cat(bt_r, bt_i), cat(bt_i, bt_r), cat(bt_r, -bt_i), cat(c_re, c_re), cat(c_im, c_im)],
                      axis=3)
    ar, ai = pr[:, :, S5_T], pi[:, :, S5_T]
    decay = jnp.stack([cat(ar, ar), cat(-ai, ai), cat(ai, -ai), jnp.zeros_like(cat(ar, ar))], axis=3)
    return pows, wrows, decay


def _block_transpose(arrs):
    arrs = list(arrs)
    width = arrs[0].shape[1]
    blk = lax.broadcasted_iota(jnp.int32, arrs[0].shape, 1) >> 4
    s = len(arrs) // 2
    while s >= 1:
        keep = (blk & s) == 0
        for i in range(len(arrs)):
            if i & s:
                continue
            lo, hi = arrs[i], arrs[i + s]
            arrs[i] = jnp.where(keep, lo, pltpu.roll(hi, s * S5_H, 1))
            arrs[i + s] = jnp.where(keep, pltpu.roll(lo, width - s * S5_H, 1), hi)
        s //= 2
    return arrs


def _s5_build_operators(pw_ref, w_ref, e_scr, esw_scr, ft_scr, tp_scr):
    lane = lax.broadcasted_iota(jnp.int32, (S5_H, S5_TW), 1)
    for d in range(2):
        for g in range(S5_G):
            pw = lambda v, k: pw_ref[0, d, g, v, k:k + 1, :]
            b_ri, b_ir, b_conj, c_rr, c_ii = (w_ref[0, d, g, v] for v in range(5))
            lag_rows = []
            for t in range(S5_T):
                blk = slice(t * S5_H, (t + 1) * S5_H)
                ke = S5_T - 1 - t if d == 0 else t
                e_scr[d, g, blk, :] = (b_ri * pw(0, ke) + b_ir * pw(1, ke)).astype(BF16)
                esw_scr[d, g, blk, :] = (b_ir * pw(0, ke) - b_ri * pw(1, ke)).astype(BF16)
                kf = t + 1 if d == 0 else S5_T - t
                ft_scr[d, g, blk, :] = (c_rr * pw(2, kf) + c_ii * pw(3, kf)).astype(BF16)
                kl = t if d == 0 else S5_T - 1 - t
                lag_rows.append(c_rr * pw(4, kl) + c_ii * pw(5, kl))
            mh, ml = _split_bf16(jnp.concatenate(lag_rows, axis=0))
            bh, bl = _split_bf16(b_conj)
            kt = _dot_nt(bh, mh) + _dot_nt(bh, ml) + _dot_nt(bl, mh)
            for ti in range(S5_T):
                if d == 0:
                    sh = S5_H * ti
                    blk_rows = jnp.where(lane >= sh, pltpu.roll(kt, sh, 1) if sh else kt, 0.0)
                else:
                    sh = S5_H * (S5_T - 1 - ti)
                    blk_rows = jnp.where(lane < S5_TW - sh, pltpu.roll(kt, S5_TW - sh, 1) if sh else kt, 0.0)
                tp_scr[d, g, ti * S5_H:(ti + 1) * S5_H, :] = blk_rows.astype(BF16)


def _s5_kernel(us_ref, pw_ref, w_ref, a_ref, s0_ref, y_ref, fin_ref,
               e_scr, esw_scr, ft_scr, tp_scr, x_scr, y_scr, loc_scr, lsw_scr, prev_scr):
    step = pl.program_id(0)

    @pl.when(step == 0)
    def _():
        _s5_build_operators(pw_ref, w_ref, e_scr, esw_scr, ft_scr, tp_scr)

    rows = pl.ds
    x_by_step = [jnp.concatenate([us_ref[0, rows(tl, S5_R, stride=S5_T), :],
                                  us_ref[1, rows(tl, S5_R, stride=S5_T), :]], axis=1) for tl in range(S5_T)]
    for g, xg in enumerate(_block_transpose(x_by_step)):
        x_scr[g] = xg.astype(BF16)
    fin_ref[...] = jnp.zeros(fin_ref.shape, F32)

    def scan(d, nseq, nc):
        batch = 8
        for g0 in range(0, S5_G, batch):
            coefs = [[jnp.broadcast_to(a_ref[0, d, g, r:r + 1, :], (nseq, 2 * S5_P)) for r in range(3)]
                     for g in range(g0, g0 + batch)]
            init = tuple((s0_ref[0, 0, d, g, 0:nseq, :2 * S5_P], s0_ref[0, 0, d, g, 0:nseq, 2 * S5_P:])
                         for g in range(g0, g0 + batch))

            def body(j, carry):
                ci = j if d == 0 else nc - 1 - j
                out = []
                for idx in range(batch):
                    g = g0 + idx
                    s, t = carry[idx]
                    ca, cb, cc = coefs[idx]
                    prev_scr[g, rows(ci, nseq, stride=nc), :] = s
                    out.append((s * ca + t * cb + loc_scr[g, rows(ci, nseq, stride=nc), :],
                                t * ca + s * cc + lsw_scr[g, rows(ci, nseq, stride=nc), :]))
                return tuple(out)

            fin = lax.fori_loop(0, nc, body, init)
            for idx in range(batch):
                fin_ref[0, d, g0 + idx, 0:nseq, :] = fin[idx][0]

    for d in range(2):
        for g in range(S5_G):
            xg = x_scr[g]
            loc_scr[g] = _dot(xg, e_scr[d, g])
            lsw_scr[g] = _dot(xg, esw_scr[d, g])

        @pl.when(step < S5_STEPS_CTX)
        def _():
            scan(d, N_CTX // S5_STEPS_CTX, T_CTX // S5_T)

        @pl.when(step >= S5_STEPS_CTX)
        def _():
            scan(d, N_LAT // S5_STEPS_LAT, T_LAT // S5_T)

        for g in range(S5_G):
            yg = _dot(x_scr[g], tp_scr[d, g]) + _dot_nt(prev_scr[g].astype(BF16), ft_scr[d, g])
            if d == 0:
                y_scr[g] = yg
            else:
                y_scr[g] += yg
    for to, y_to in enumerate(_block_transpose([y_scr[g] for g in range(S5_G)])):
        y_ref[0, rows(to, S5_R, stride=S5_T), :] = y_to[:, :LANES]
        y_ref[1, rows(to, S5_R, stride=S5_T), :] = y_to[:, LANES:]


def _s5(us, tabs, s0, l):
    pows, wrows, decay = tabs
    lay = lambda *shape: pl.BlockSpec((1,) + shape, lambda h: (l,) + (0,) * len(shape))
    op_buf = pltpu.VMEM((2, S5_G, S5_TW, 2 * S5_P), BF16)
    scan_buf = pltpu.VMEM((S5_G, S5_R, 2 * S5_P), F32)
    return pl.pallas_call(
        _s5_kernel,
        grid=(S5_STEPS,),
        in_specs=[_halves_spec(S5_TOK, lambda h: h),
                  lay(2, S5_G, 6, S5_T + 1, 2 * S5_P), lay(2, S5_G, 5, S5_H, 2 * S5_P), lay(2, S5_G, 4, 2 * S5_P),
                  pl.BlockSpec((1, 1, 2, S5_G, S5_SEQ, 4 * S5_P), lambda h: (l, h, 0, 0, 0, 0))],
        out_specs=[_halves_spec(S5_TOK, lambda h: h),
                   pl.BlockSpec((1, 2, S5_G, S5_SEQ, 2 * S5_P), lambda h: (h, 0, 0, 0, 0))],
        out_shape=[_HALVES_SHAPE, jax.ShapeDtypeStruct((S5_STEPS, 2, S5_G, S5_SEQ, 2 * S5_P), F32)],
        scratch_shapes=[op_buf, op_buf, op_buf, pltpu.VMEM((2, S5_G, S5_TW, S5_TW), BF16),
                        pltpu.VMEM((S5_G, S5_R, S5_TW), BF16), pltpu.VMEM((S5_G, S5_R, S5_TW), F32),
                        scan_buf, scan_buf, scan_buf],
        compiler_params=_cparams(("arbitrary",), 56),
        name="s5",
    )(us, pows, wrows, decay, s0)


def _split_bf16(a):
    hi = a.astype(BF16)
    return hi, (a - hi.astype(F32)).astype(BF16)


def _outproj_kernel(at_ref, fo_ref, ys_ref, us_ref, d_ref, wg_ref, wo_ref, x_ref, mod_ref,
                    g_ref, b_ref, wr_ref, x1_ref, h2_ref, lg_ref, wob_ref):
    @pl.when(pl.program_id(0) == 0)
    def _():
        wob_ref[...] = wo_ref[0].astype(BF16)

    ys = jnp.concatenate([ys_ref[0], ys_ref[1]], axis=1)
    us = jnp.concatenate([us_ref[0], us_ref[1]], axis=1)
    g = jax.nn.gelu(ys + us * d_ref[0])
    s5 = g * jax.nn.sigmoid(_dot(g.astype(BF16), wg_ref[0].astype(BF16)))
    mix = (_dot(at_ref[...], wob_ref[0:ATTN_W, :])
           + _dot(fo_ref[...], wob_ref[ATTN_W:ATTN_W + FOUR_W, :])
           + _dot(s5.astype(BF16), wob_ref[ATTN_W + FOUR_W:, :]))
    m = mod_ref[0, 0]
    x1 = _layer_norm(ALPHA * x_ref[...] + m[2:3] * mix, g_ref[0], b_ref[0])
    x1_ref[...] = x1
    h2 = x1 * (1.0 + m[4:5]) + m[3:4]
    hb = h2.astype(BF16)
    h2_ref[...] = hb
    lg_ref[...] = _dot_nt(wr_ref[0].astype(BF16), hb)


def _outproj(attn, four, ys5, us, s5_d, w_glu, w_out, x, mod, ln_g, ln_b, w_router_t, l):
    row = lambda w: pl.BlockSpec((RB_OUT, w), lambda i: (i, 0))
    halves = _halves_spec(RB_OUT, lambda i: i)
    lay = lambda *shape: pl.BlockSpec((1,) + shape, lambda i: (l,) + (0,) * len(shape))
    return pl.pallas_call(
        _outproj_kernel,
        grid=(ROWS // RB_OUT,),
        in_specs=[row(ATTN_W), row(FOUR_W), halves, halves, lay(1, S5_W), lay(S5_W, S5_W), lay(D, D), row(D),
                  pl.BlockSpec((1, 1, 6, D), lambda i: (l, _mod_row(i, RB_OUT), 0, 0)),
                  lay(1, D), lay(1, D), lay(N_EXP, D)],
        out_specs=[row(D), row(D), pl.BlockSpec((N_EXP, RB_OUT), lambda i: (0, i))],
        out_shape=[jax.ShapeDtypeStruct((ROWS, D), F32), jax.ShapeDtypeStruct((ROWS, D), BF16),
                   jax.ShapeDtypeStruct((N_EXP, ROWS), F32)],
        scratch_shapes=[pltpu.VMEM((D, D), BF16)],
        compiler_params=_cparams(("arbitrary",), 40),
        name="outproj",
    )(attn, four, ys5, us, s5_d, w_glu, w_out, x, mod, ln_g, ln_b, w_router_t)


SEQ_GROUP = LANES // N_EXP


def _router_kernel(lg_ref, crow_ref, ccol_ref, gcol_ref, *, n, cap, n_seq):
    rows = n_seq * N_EXP
    lg = jnp.concatenate([lg_ref[:, s * n:(s + 1) * n] for s in range(n_seq)], axis=0).reshape(n_seq, N_EXP, n)
    e = jnp.exp(lg - jnp.max(lg, axis=1, keepdims=True))
    aff = (e / jnp.sum(e, axis=1, keepdims=True)).reshape(rows, n)
    thr_bits = jnp.zeros((rows, 1), jnp.int32)
    for bit in range(30, -1, -1):
        cand = thr_bits | (1 << bit)
        cnt = jnp.sum(jnp.where(aff >= lax.bitcast_convert_type(cand, F32), 1.0, 0.0), axis=1, keepdims=True)
        thr_bits = jnp.where(cnt >= cap, cand, thr_bits)
    thr = lax.bitcast_convert_type(thr_bits, F32)
    above = aff > thr
    tied = aff == thr
    need = cap - jnp.sum(jnp.where(above, 1.0, 0.0), axis=1, keepdims=True)
    r0 = lax.broadcasted_iota(jnp.int32, (n, n), 0)
    r1 = lax.broadcasted_iota(jnp.int32, (n, n), 1)
    before = jnp.where(r0 < r1, 1.0, 0.0).astype(BF16)
    eye = jnp.where(r0 == r1, 1.0, 0.0).astype(BF16)
    tied_rank = _dot(jnp.where(tied, 1.0, 0.0).astype(BF16), before)
    sel = above | (tied & (tied_rank < need))
    slot = _dot(jnp.where(sel, 1.0, 0.0).astype(BF16), before)
    code = jnp.where(sel, slot + 1.0, 0.0)
    gate = jnp.where(sel, aff, 0.0)
    for s in range(n_seq):
        crow_ref[:, s * n:(s + 1) * n] = code[s * N_EXP:(s + 1) * N_EXP]
    for grp in range(pl.cdiv(n_seq, SEQ_GROUP)):
        lo, hi = grp * LANES, min((grp + 1) * LANES, rows)
        pad = [jnp.zeros((LANES - (hi - lo), n), F32)] if hi - lo < LANES else []
        ccol_ref[grp] = _dot_nt(eye, jnp.concatenate([code[lo:hi]] + pad, axis=0).astype(BF16))
        gcol_ref[grp] = _dot_nt(eye, jnp.concatenate([gate[lo:hi]] + pad, axis=0).astype(BF16))


def _router(lg, n, n_seq, blk, cap):
    groups = pl.cdiv(n_seq, SEQ_GROUP)
    whole = lambda *shape: pl.BlockSpec(shape, lambda i: (0,) * len(shape))
    return pl.pallas_call(
        functools.partial(_router_kernel, n=n, cap=cap, n_seq=n_seq),
        grid=(1,),
        in_specs=[pl.BlockSpec((N_EXP, n_seq * n), lambda i: (0, blk))],
        out_specs=[whole(N_EXP, n_seq * n), whole(groups, n, LANES), whole(groups, n, LANES)],
        out_shape=[jax.ShapeDtypeStruct((N_EXP, n_seq * n), F32),
                   jax.ShapeDtypeStruct((groups, n, LANES), F32),
                   jax.ShapeDtypeStruct((groups, n, LANES), F32)],
        compiler_params=_cparams(("arbitrary",), 48),
        name=f"router_{n}",
    )(lg)


def _gather_kernel(crow_ref, h_ref, o_ref, *, n, cap):
    code = crow_ref[...]
    slot1 = (lax.broadcasted_iota(jnp.int32, (cap, n), 0) + 1).astype(F32)
    onehot = jnp.concatenate(
        [jnp.where(code[e:e + 1, :] == slot1, 1.0, 0.0).astype(BF16) for e in range(N_EXP)], axis=0)
    xs = _dot(onehot, h_ref[...])
    o_ref[...] = xs.reshape(N_EXP, cap, D).astype(BF16)


def _gather(crow, h2, n, n_seq, blk_off, cap):
    return pl.pallas_call(
        functools.partial(_gather_kernel, n=n, cap=cap),
        grid=(n_seq,),
        in_specs=[pl.BlockSpec((N_EXP, n), lambda s: (0, s)),
                  pl.BlockSpec((n, D), lambda s: (blk_off + s, 0))],
        out_specs=pl.BlockSpec((N_EXP, cap, D), lambda s: (0, s, 0)),
        out_shape=jax.ShapeDtypeStruct((N_EXP, n_seq * cap, D), BF16),
        compiler_params=_cparams(("arbitrary",), 48),
        name=f"gather_{n}",
    )(crow, h2)


FF_BLK = 512
SLOTS = N_CTX * CAP_CTX


def _ffn_kernel(xc_ref, xq_ref, wg_ref, wu_ref, wd_ref, yc_ref, yq_ref, acc_ref):
    f = pl.program_id(1)
    x = jnp.concatenate([xc_ref[0], xq_ref[0]], axis=0)
    g = _dot(x, wg_ref[0, 0].astype(BF16))
    u = _dot(x, wu_ref[0, 0].astype(BF16))
    hid = (g * jax.nn.sigmoid(g) * u).astype(BF16)
    y = _dot(hid, wd_ref[0, 0].astype(BF16))

    @pl.when(f == 0)
    def _():
        acc_ref[...] = y

    @pl.when(f == FF // FF_BLK - 1)
    def _():
        tot = acc_ref[...] + y
        yc_ref[0] = tot[:SLOTS].astype(BF16)
        yq_ref[0] = tot[SLOTS:].astype(BF16)


def _ffn(xs_c, xs_q, w_gate, w_up, w_down, l):
    assert FF // FF_BLK == 2
    xspec = pl.BlockSpec((1, SLOTS, D), lambda e, f: (e, 0, 0))
    return pl.pallas_call(
        _ffn_kernel,
        grid=(N_EXP, FF // FF_BLK),
        in_specs=[xspec, xspec,
                  pl.BlockSpec((1, 1, D, FF_BLK), lambda e, f: (l, e, 0, f)),
                  pl.BlockSpec((1, 1, D, FF_BLK), lambda e, f: (l, e, 0, f)),
                  pl.BlockSpec((1, 1, FF_BLK, D), lambda e, f: (l, e, f, 0))],
        out_specs=[xspec, xspec],
        out_shape=[jax.ShapeDtypeStruct((N_EXP, SLOTS, D), BF16)] * 2,
        scratch_shapes=[pltpu.VMEM((2 * SLOTS, D), F32)],
        compiler_params=_cparams(("arbitrary", "arbitrary"), 48),
        name="ffn",
    )(xs_c, xs_q, w_gate, w_up, w_down)


def _combine_kernel(ccol_ref, gcol_ref, ys_ref, x_ref, mod_ref, g_ref, b_ref, o_ref, *, cap):
    width = N_EXP * cap
    first = (pl.program_id(0) % SEQ_GROUP) * N_EXP
    k = lax.broadcasted_iota(jnp.int32, (LANES, width), 0)
    j = lax.broadcasted_iota(jnp.int32, (LANES, width), 1)
    shift = cap.bit_length() - 1
    spread = jnp.where((j >> shift) + first == k, 1.0, 0.0).astype(BF16)
    code = _dot(ccol_ref[0].astype(BF16), spread)
    gate = _dot(gcol_ref[0].astype(BF16), spread)
    slot1 = ((lax.broadcasted_iota(jnp.int32, (1, width), 1) & (cap - 1)) + 1).astype(F32)
    weights = jnp.where(code == slot1, gate, 0.0).astype(BF16)
    moe = _dot(weights, ys_ref[...].reshape(width, D))
    m = mod_ref[0, 0]
    o_ref[...] = _layer_norm(ALPHA * x_ref[...] + m[5:6] * moe, g_ref[0], b_ref[0])


def _combine(ccol, gcol, ys, x1, mod, ln_g, ln_b, l, cap, n_seq, rb, rb_per_seq, rb_off, in_place):
    lay = lambda *shape: pl.BlockSpec((1,) + shape, lambda s, r: (l,) + (0,) * len(shape))
    table = pl.BlockSpec((1, rb, LANES), lambda s, r: (s // SEQ_GROUP, r, 0))
    block = lambda s, r: rb_off + s * rb_per_seq + r
    out_off = rb_off if in_place else 0
    return pl.pallas_call(
        functools.partial(_combine_kernel, cap=cap),
        grid=(n_seq, rb_per_seq),
        in_specs=[table, table,
                  pl.BlockSpec((N_EXP, cap, D), lambda s, r: (0, s, 0)),
                  pl.BlockSpec((rb, D), lambda s, r: (block(s, r), 0)),
                  pl.BlockSpec((1, 1, 6, D), lambda s, r: (l, _mod_row(block(s, r), rb), 0, 0)),
                  lay(1, D), lay(1, D)],
        out_specs=pl.BlockSpec((rb, D), lambda s, r: (block(s, r) - rb_off + out_off, 0)),
        out_shape=jax.ShapeDtypeStruct((ROWS if in_place else n_seq * rb_per_seq * rb, D), F32),
        input_output_aliases={3: 0} if in_place else {},
        compiler_params=_cparams(("arbitrary", "arbitrary"), 48),
        name=f"combine_{cap}",
    )(ccol, gcol, ys, x1, mod, ln_g, ln_b)


def kernel(x_prompt, x_sample, cache_k, cache_v, state_s5_re, state_s5_im, c, c_ctx, ln_in_g, ln_in_b, w_ada, b_ada,
           w_in, w_fourier, attn_sink, s5_a_re, s5_a_im, s5_log_dt, s5_b_re, s5_b_im, s5_c_re, s5_c_im, s5_d,
           s5_w_glu, w_out, ln1_g, ln1_b, w_router, w_gate, w_up, w_down, ln2_g, ln2_b):
    x = _ln_in(x_prompt.reshape(ROWS_CTX, D), x_sample.reshape(ROWS - ROWS_CTX, D), ln_in_g, ln_in_b)
    cond8 = jnp.concatenate([c_ctx[None], c, jnp.zeros((8 - 1 - N_LAT, D), F32)], axis=0)
    mod = _ada(cond8, w_ada, b_ada).reshape(DEPTH, 8, 6, D)

    rope_tabs = _rope_tables()
    cs_ctx, cs_lat = _dft_tables(T_CTX), _dft_tables(T_LAT)
    c64, s64 = _dft_channel_tables()
    cache_k = cache_k.reshape(N_LAT, DEPTH, PAST, KV_W)
    cache_v = cache_v.reshape(N_LAT, DEPTH, PAST, KV_W)
    w_router_t = jnp.swapaxes(w_router, 1, 2)
    n_qk = ATTN_W + KV_W
    w_in_swapped = w_in[:, :, :n_qk].reshape(DEPTH, D, n_qk // 2, 2)[..., ::-1].reshape(DEPTH, D, n_qk)
    sink_b = jnp.broadcast_to(attn_sink[:, :, None], (DEPTH, N_HEADS, LANES))
    s5_tabs = _s5_tables(s5_a_re, s5_a_im, s5_log_dt, s5_b_re, s5_b_im, s5_c_re, s5_c_im)
    lat_per_step = N_LAT // S5_STEPS_LAT
    s0 = jnp.concatenate([state_s5_re, state_s5_im, state_s5_im, state_s5_re], axis=-1)
    s0 = s0.reshape(S5_STEPS_LAT, lat_per_step, DEPTH, 2, S5_G, 4 * S5_P).transpose(2, 0, 3, 4, 1, 5)
    s0 = jnp.pad(s0, ((0, 0), (S5_STEPS_CTX, 0), (0, 0), (0, 0), (0, S5_SEQ - lat_per_step), (0, 0)))
    ln1_g, ln1_b, ln2_g, ln2_b, s5_d = (a.reshape(DEPTH, 1, -1) for a in (ln1_g, ln1_b, ln2_g, ln2_b, s5_d))

    new_k, new_v, new_s = [], [], []
    for l in range(DEPTH):
        q, k, v, uf, us, k_t, v_t = _inproj(x, mod, w_in, w_in_swapped, rope_tabs, l)
        new_k.append(k_t)
        new_v.append(v_t)
        attn = _attn_lat(_attn_ctx(q, k, v, sink_b[l]), k, v, cache_k, cache_v, sink_b[l], l)
        four = _fourier(uf, cs_ctx, c64, s64, w_fourier, l, T_CTX, N_CTX, 0)
        four = _fourier(four, cs_lat, c64, s64, w_fourier, l, T_LAT, N_LAT, ROWS_CTX // T_LAT)
        y5, fin = _s5(us, s5_tabs, s0, l)
        new_s.append(fin[:S5_STEPS_CTX].transpose(0, 3, 1, 2, 4).reshape(N_CTX, 2, S5_G, 2 * S5_P))
        x1, h2, lg = _outproj(attn, four, y5, us, s5_d, s5_w_glu, w_out, x, mod, ln1_g, ln1_b, w_router_t, l)
        crow_c, ccol_c, gcol_c = _router(lg, T_CTX, N_CTX, 0, CAP_CTX)
        crow_q, ccol_q, gcol_q = _router(lg, T_LAT, N_LAT, 1, CAP_LAT)
        xs_c = _gather(crow_c, h2, T_CTX, N_CTX, 0, CAP_CTX)
        xs_q = _gather(crow_q, h2, T_LAT, N_LAT, ROWS_CTX // T_LAT, CAP_LAT)
        ys_c, ys_q = _ffn(xs_c, xs_q, w_gate, w_up, w_down, l)
        args_c = (ccol_c, gcol_c, ys_c), (mod, ln2_g, ln2_b, l, CAP_CTX, N_CTX, T_CTX, 1, 0)
        args_q = (ccol_q, gcol_q, ys_q), (mod, ln2_g, ln2_b, l, CAP_LAT, N_LAT, RB_COMBINE_LAT,
                                          T_LAT // RB_COMBINE_LAT, ROWS_CTX // RB_COMBINE_LAT)
        if l < DEPTH - 1:
            x = _combine(*args_c[0], x1, *args_c[1], in_place=True)
            x = _combine(*args_q[0], x, *args_q[1], in_place=True)
        else:
            y_ctx = _combine(*args_c[0], x1, *args_c[1], in_place=False)
            y_lat = _combine(*args_q[0], x1, *args_q[1], in_place=False)

    new_s = jnp.stack(new_s, axis=1)

    def cache(feat_major):
        return jnp.stack(feat_major).reshape(DEPTH, N_KV, HD, N_CTX, T_CTX).transpose(3, 0, 4, 1, 2)

    return (y_ctx.reshape(N_CTX, T_CTX, D), y_lat.reshape(N_LAT, T_LAT, D),
            cache(new_k), cache(new_v), new_s[..., :S5_P], new_s[..., S5_P:])
```

```python
import functools

import jax
import jax.numpy as jnp
import numpy as np
from jax import lax
from jax.experimental import pallas as pl
from jax.experimental.pallas import tpu as pltpu

F32 = jnp.float32
BF16 = jnp.bfloat16

D = 1024
N_CTX, T_CTX = 16, 256
N_LAT, T_LAT = 4, 1024
ROWS_CTX = N_CTX * T_CTX
ROWS = ROWS_CTX + N_LAT * T_LAT
DEPTH = 4
PAST = 512
GRID_W = 64
N_HEADS, N_KV, HD = 8, 2, 64
GQA = N_HEADS // N_KV
ATTN_W, KV_W = N_HEADS * HD, N_KV * HD
LOG2E = 1.4426950408889634
Q_SCALE = HD ** -0.5 * LOG2E
WINDOW = 128
FOUR_H, FOUR_W = 4, 256
S5_G, S5_H, S5_P, S5_W = 16, 16, 64, 256
S5_T = 16
IN_W = ATTN_W + 2 * KV_W + FOUR_W + S5_W
N_EXP, FF = 16, 1024
CAP_CTX, CAP_LAT = 2 * T_CTX // N_EXP, 2 * T_LAT // N_EXP
LN_EPS = 1e-5
NEG_INF = -1e30
ALPHA = (2 * DEPTH) ** 0.25
ROPE_BASE = 10000.0
RB_IN = 1024
RB_OUT = 512
RB_COMBINE_LAT = 512
LANES = 128
MIB = 2 ** 20
HIGHEST = lax.Precision.HIGHEST


def _cparams(sem, vmem_mib):
    return pltpu.CompilerParams(dimension_semantics=sem, vmem_limit_bytes=vmem_mib * MIB)


def _dot(a, b):
    return jnp.dot(a, b, preferred_element_type=F32)


def _dot_nt(a, b):
    return lax.dot_general(a, b, (((1,), (1,)), ((), ())), preferred_element_type=F32)


def _dot_tn(a, b):
    return lax.dot_general(a, b, (((0,), (0,)), ((), ())), preferred_element_type=F32)


def _layer_norm(x, g, b):
    mu = jnp.mean(x, axis=-1, keepdims=True)
    xc = x - mu
    var = jnp.mean(xc * xc, axis=-1, keepdims=True)
    return xc * lax.rsqrt(var + LN_EPS) * g + b


def _mod_row(block, block_rows):
    first = block * block_rows
    return jnp.where(first < ROWS_CTX, 0, 1 + (first - ROWS_CTX) // T_LAT)


_HALVES_SHAPE = jax.ShapeDtypeStruct((2, ROWS, LANES), F32)


def _halves_spec(rows, row_block):
    return pl.BlockSpec((2, rows, LANES), lambda *ids: (0, row_block(*ids), 0))


def _two_stream_specs(rows, width):
    n_ctx = ROWS_CTX // rows
    return (pl.BlockSpec((rows, width), lambda i: (jnp.minimum(i, n_ctx - 1), 0)),
            pl.BlockSpec((rows, width), lambda i: (jnp.maximum(i - n_ctx, 0), 0)))


def _pick(rows, ctx_ref, lat_ref):
    return lax.cond(pl.program_id(0) < ROWS_CTX // rows, lambda: ctx_ref[...], lambda: lat_ref[...])


def _ln_in_kernel(xc_ref, xq_ref, g_ref, b_ref, o_ref):
    o_ref[...] = _layer_norm(_pick(RB_OUT, xc_ref, xq_ref), g_ref[...], b_ref[...])


def _ln_in(x_ctx, x_lat, g, b):
    return pl.pallas_call(
        _ln_in_kernel,
        grid=(ROWS // RB_OUT,),
        in_specs=[*_two_stream_specs(RB_OUT, D),
                  pl.BlockSpec((1, D), lambda i: (0, 0)),
                  pl.BlockSpec((1, D), lambda i: (0, 0))],
        out_specs=pl.BlockSpec((RB_OUT, D), lambda i: (i, 0)),
        out_shape=jax.ShapeDtypeStruct((ROWS, D), F32),
        compiler_params=_cparams(("arbitrary",), 32),
        name="ln_in",
    )(x_ctx, x_lat, g.reshape(1, D), b.reshape(1, D))


def _ada_kernel(c_ref, w_ref, b_ref, o_ref):
    c = c_ref[...]
    s = (c * jax.nn.sigmoid(c)).astype(BF16)
    o_ref[0] = _dot(s, w_ref[0].astype(BF16)) + b_ref[0]


def _ada(cond8, w_ada, b_ada):
    tn = 1536
    return pl.pallas_call(
        _ada_kernel,
        grid=(DEPTH, 6 * D // tn),
        in_specs=[pl.BlockSpec((8, D), lambda l, j: (0, 0)),
                  pl.BlockSpec((1, D, tn), lambda l, j: (l, 0, j)),
                  pl.BlockSpec((1, 1, tn), lambda l, j: (l, 0, j))],
        out_specs=pl.BlockSpec((1, 8, tn), lambda l, j: (l, 0, j)),
        out_shape=jax.ShapeDtypeStruct((DEPTH, 8, 6 * D), F32),
        compiler_params=_cparams(("arbitrary", "arbitrary"), 40),
        name="adaln",
    )(cond8, w_ada, b_ada.reshape(DEPTH, 1, 6 * D))


def _inproj_kernel(x_ref, mod_ref, w_ref, cos_ref, sin_ref, q_ref, k_ref, v_ref, uf_ref, us_ref, kt_ref, vt_ref,
                   wb_ref, wswb_ref):
    step = pl.program_id(0)
    n_ctx_steps = ROWS_CTX // RB_IN

    @pl.when(step == 0)
    def _():
        wb = w_ref[0].astype(BF16)
        wb_ref[...] = wb
        n_qk = ATTN_W + KV_W
        src = lax.broadcasted_iota(jnp.int32, (n_qk, n_qk), 0)
        dst = lax.broadcasted_iota(jnp.int32, (n_qk, n_qk), 1)
        swap = jnp.where(src == (dst ^ 1), 1.0, 0.0).astype(BF16)
        wswb_ref[...] = _dot(wb[:, :n_qk], swap).astype(BF16)

    m = mod_ref[0, 0]
    hb = (x_ref[...] * (1.0 + m[1:2]) + m[0:1]).astype(BF16)
    p = _dot(hb, wb_ref[...])
    q = p[:, :ATTN_W]
    k = p[:, ATTN_W:ATTN_W + KV_W]
    v = p[:, ATTN_W + KV_W:ATTN_W + 2 * KV_W]

    psw = _dot(hb, wswb_ref[...])
    cos, sin = cos_ref[...], sin_ref[...]
    wide = lambda t: jnp.concatenate([t] * (ATTN_W // KV_W), axis=1)
    latent = step >= n_ctx_steps
    q = jnp.where(latent, q * wide(cos) + psw[:, :ATTN_W] * wide(sin), q)
    k = jnp.where(latent, k * cos + psw[:, ATTN_W:] * sin, k)
    q_ref[...] = (q * Q_SCALE).astype(BF16)
    k_ref[...] = k
    v_ref[...] = v

    @pl.when(step < n_ctx_steps)
    def _():
        kt_ref[...] = k.T
        vt_ref[...] = v.T

    uf_ref[...] = p[:, ATTN_W + 2 * KV_W:ATTN_W + 2 * KV_W + FOUR_W].astype(BF16)
    for half in range(S5_W // LANES):
        lo = ATTN_W + 2 * KV_W + FOUR_W + half * LANES
        us_ref[half] = p[:, lo:lo + LANES]


def _inproj(x, mod, w_in, rope_tabs, l):
    assert RB_IN == T_LAT
    outs = ((ATTN_W, BF16), (KV_W, F32), (KV_W, F32), (FOUR_W, BF16))
    table = pl.BlockSpec((T_LAT, KV_W), lambda i: (0, 0))
    feat_major = pl.BlockSpec((KV_W, RB_IN), lambda i: (0, jnp.minimum(i, ROWS_CTX // RB_IN - 1)))
    return pl.pallas_call(
        _inproj_kernel,
        grid=(ROWS // RB_IN,),
        in_specs=[pl.BlockSpec((RB_IN, D), lambda i: (i, 0)),
                  pl.BlockSpec((1, 1, 6, D), lambda i: (l, _mod_row(i, RB_IN), 0, 0)),
                  pl.BlockSpec((1, D, IN_W), lambda i: (l, 0, 0)),
                  table, table],
        out_specs=([pl.BlockSpec((RB_IN, w), lambda i: (i, 0)) for w, _ in outs] + [_halves_spec(RB_IN, lambda i: i)]
                   + [feat_major, feat_major]),
        out_shape=([jax.ShapeDtypeStruct((ROWS, w), dt) for w, dt in outs] + [_HALVES_SHAPE]
                   + [jax.ShapeDtypeStruct((KV_W, ROWS_CTX), F32)] * 2),
        scratch_shapes=[pltpu.VMEM((D, IN_W), BF16), pltpu.VMEM((D, ATTN_W + KV_W), BF16)],
        compiler_params=_cparams(("arbitrary",), 56),
        name="inproj",
    )(x, mod, w_in, *rope_tabs)


SINK_ROWS = 16


def _softmax_av(score_blocks, sink, value_blocks):
    m = sink
    for s in score_blocks:
        m = jnp.maximum(m, jnp.max(s, axis=0, keepdims=True))
    first = lax.broadcasted_iota(jnp.int32, (SINK_ROWS, 1), 0) == 0
    probs = [jnp.exp2(s - m).astype(BF16) for s in score_blocks]
    probs.append(jnp.where(first, jnp.exp2(sink - m), 0.0).astype(BF16))
    ext = []
    for v in value_blocks:
        ones = jnp.ones(v.shape, BF16)
        ext.append(jnp.concatenate([v, v, ones, ones], axis=1))
    row = lax.broadcasted_iota(jnp.int32, (SINK_ROWS, 4 * HD), 0)
    lane = lax.broadcasted_iota(jnp.int32, (SINK_ROWS, 4 * HD), 1)
    ext.append(jnp.where((row == 0) & (lane >= 2 * HD), 1.0, 0.0).astype(BF16))
    acc = _dot_tn(jnp.concatenate(probs, axis=0), jnp.concatenate(ext, axis=0))
    return acc[:, :2 * HD] / acc[:, 2 * HD:]


def _stack_group(q, kv):
    return jnp.concatenate([q[:, (kv * GQA + g) * HD:(kv * GQA + g + 1) * HD] for g in range(GQA)], axis=0)


def _sink_row(sink_ref, kv, rows):
    return LOG2E * jnp.concatenate(
        [jnp.broadcast_to(sink_ref[kv * GQA + g:kv * GQA + g + 1, 0:1], (1, rows)) for g in range(GQA)], axis=1)


def _unstack_group(o, rows):
    lane = lax.broadcasted_iota(jnp.int32, (rows, 2 * HD), 1)
    return [jnp.where(lane < HD, o[2 * j * rows:(2 * j + 1) * rows], o[(2 * j + 1) * rows:(2 * j + 2) * rows])
            for j in range(GQA // 2)]


def _attn_ctx_kernel(q_ref, k_ref, v_ref, sink_ref, o_ref):
    q = q_ref[...]
    k = k_ref[...].astype(BF16)
    v = v_ref[...].astype(BF16)
    outs = []
    for kv in range(N_KV):
        head = slice(kv * HD, (kv + 1) * HD)
        s = _dot_nt(k[:, head], _stack_group(q, kv))
        o = _softmax_av([s], _sink_row(sink_ref, kv, T_CTX), [v[:, head]])
        outs += _unstack_group(o, T_CTX)
    o_ref[...] = jnp.concatenate(outs, axis=1).astype(BF16)


def _attn_ctx(q, k, v, sink_b):
    return pl.pallas_call(
        _attn_ctx_kernel,
        grid=(N_CTX,),
        in_specs=[pl.BlockSpec((T_CTX, ATTN_W), lambda s: (s, 0)),
                  pl.BlockSpec((T_CTX, KV_W), lambda s: (s, 0)),
                  pl.BlockSpec((T_CTX, KV_W), lambda s: (s, 0)),
                  pl.BlockSpec((N_HEADS, LANES), lambda s: (0, 0))],
        out_specs=pl.BlockSpec((T_CTX, ATTN_W), lambda s: (s, 0)),
        out_shape=jax.ShapeDtypeStruct((ROWS, ATTN_W), BF16),
        input_output_aliases={0: 0},
        compiler_params=_cparams(("arbitrary",), 32),
        name="attn_ctx",
    )(q, k, v, sink_b)


def _attn_lat_kernel(q_ref, k_ref, v_ref, ck_ref, cv_ref, sink_ref, o_ref):
    i = pl.program_id(1)
    n_blk = T_LAT // WINDOW
    q = q_ref[...]

    def kv_block(j):
        st = pl.multiple_of(j * WINDOW, WINDOW)
        return k_ref[pl.ds(st, WINDOW), :].astype(BF16), v_ref[pl.ds(st, WINDOW), :].astype(BF16)

    k0, v0 = kv_block(jnp.maximum(i - 1, 0))
    k1, v1 = kv_block(i)
    k2, v2 = kv_block(jnp.minimum(i + 1, n_blk - 1))
    k_win = jnp.concatenate([k0, k1, k2], axis=0)
    k_ctx = ck_ref[0, 0].astype(BF16)
    v_ctx = cv_ref[0, 0].astype(BF16)
    rows = GQA * WINDOW
    c = lax.broadcasted_iota(jnp.int32, (WINDOW, rows), 0)
    r = lax.broadcasted_iota(jnp.int32, (WINDOW, rows), 1) & (WINDOW - 1)
    far = 2 * WINDOW
    prev_ok = c >= r + jnp.where(i > 0, 0, far)
    next_ok = c + jnp.where(i < n_blk - 1, 0, far) <= r
    outs = []
    for kv in range(N_KV):
        head = slice(kv * HD, (kv + 1) * HD)
        qs = _stack_group(q, kv)
        s_win = _dot_nt(k_win[:, head], qs)
        scores = [jnp.where(prev_ok, s_win[:WINDOW], NEG_INF), s_win[WINDOW:2 * WINDOW],
                  jnp.where(next_ok, s_win[2 * WINDOW:], NEG_INF), _dot_nt(k_ctx[:, head], qs)]
        values = [v0[:, head], v1[:, head], v2[:, head], v_ctx[:, head]]
        outs += _unstack_group(_softmax_av(scores, _sink_row(sink_ref, kv, WINDOW), values), WINDOW)
    o_ref[...] = jnp.concatenate(outs, axis=1).astype(BF16)


def _attn_lat(q, k, v, cache_k, cache_v, sink_b, l):
    n_blk = T_LAT // WINDOW
    off = ROWS_CTX // WINDOW
    return pl.pallas_call(
        _attn_lat_kernel,
        grid=(N_LAT, n_blk),
        in_specs=[pl.BlockSpec((WINDOW, ATTN_W), lambda b, i: (off + b * n_blk + i, 0)),
                  pl.BlockSpec((T_LAT, KV_W), lambda b, i: (ROWS_CTX // T_LAT + b, 0)),
                  pl.BlockSpec((T_LAT, KV_W), lambda b, i: (ROWS_CTX // T_LAT + b, 0)),
                  pl.BlockSpec((1, 1, PAST, KV_W), lambda b, i: (b, l, 0, 0)),
                  pl.BlockSpec((1, 1, PAST, KV_W), lambda b, i: (b, l, 0, 0)),
                  pl.BlockSpec((N_HEADS, LANES), lambda b, i: (0, 0))],
        out_specs=pl.BlockSpec((WINDOW, ATTN_W), lambda b, i: (off + b * n_blk + i, 0)),
        out_shape=jax.ShapeDtypeStruct((ROWS, ATTN_W), BF16),
        input_output_aliases={0: 0},
        compiler_params=_cparams(("arbitrary", "arbitrary"), 40),
        name="attn_lat",
    )(q, k, v, cache_k, cache_v, sink_b)


def _rope_tables():
    rows = T_LAT // GRID_W
    row = jnp.repeat(jnp.arange(rows, dtype=F32), GRID_W)
    col = jnp.tile(jnp.arange(GRID_W, dtype=F32), rows)
    n_freq = HD // 4
    freqs = ROPE_BASE ** (-jnp.arange(n_freq, dtype=F32) / n_freq)
    ang = jnp.concatenate([row[:, None] * freqs, col[:, None] * freqs], axis=-1)
    cos = jnp.repeat(jnp.cos(ang), 2, axis=-1)
    sign = jnp.tile(jnp.array([-1.0, 1.0], F32), HD // 2)
    sin = jnp.repeat(jnp.sin(ang), 2, axis=-1) * sign
    return jnp.tile(cos, (1, N_KV)), jnp.tile(sin, (1, N_KV))


SEQ_PER_STEP_CTX = 4


def _four_kernel(u_ref, cs_ref, c64_ref, s64_ref, wf_ref, o_ref, *, n, n_sub):
    wf = wf_ref[0].astype(BF16)
    for j in range(n_sub):
        rows = slice(j * n, (j + 1) * n)
        ub = u_ref[rows, :]
        uc = _dot(ub, c64_ref[...]).astype(BF16)
        us = _dot(ub, s64_ref[...]).astype(BF16)
        f = _dot(cs_ref[...], jnp.concatenate([uc, us], axis=0))
        o_ref[rows, :] = _dot(f.astype(BF16), wf).astype(BF16)


def _dft_tables(n):
    j = np.arange(n)
    ang = 2.0 * np.pi * ((j[:, None] * j[None, :]) % n) / n
    cs = np.concatenate([np.cos(ang), -np.sin(ang)], axis=1) / np.sqrt(n)
    return jnp.asarray(cs.astype(np.float32)).astype(BF16)


def _dft_channel_tables():
    j = np.arange(HD)
    ang = 2.0 * np.pi * ((j[:, None] * j[None, :]) % HD) / HD
    eye = np.eye(FOUR_H)
    c = np.kron(eye, np.cos(ang)) / np.sqrt(HD)
    s = np.kron(eye, np.sin(ang)) / np.sqrt(HD)
    return jnp.asarray(c.astype(np.float32)).astype(BF16), jnp.asarray(s.astype(np.float32)).astype(BF16)


def _fourier(uf, cs, c64, s64, w_fourier, l, n, n_seq, n_sub, blk_off):
    rows = n_sub * n
    return pl.pallas_call(
        functools.partial(_four_kernel, n=n, n_sub=n_sub),
        grid=(n_seq // n_sub,),
        in_specs=[pl.BlockSpec((rows, FOUR_W), lambda s: (blk_off + s, 0)),
                  pl.BlockSpec((n, 2 * n), lambda s: (0, 0)),
                  pl.BlockSpec((FOUR_W, FOUR_W), lambda s: (0, 0)),
                  pl.BlockSpec((FOUR_W, FOUR_W), lambda s: (0, 0)),
                  pl.BlockSpec((1, FOUR_W, FOUR_W), lambda s: (l, 0, 0))],
        out_specs=pl.BlockSpec((rows, FOUR_W), lambda s: (blk_off + s, 0)),
        out_shape=jax.ShapeDtypeStruct((ROWS, FOUR_W), BF16),
        input_output_aliases={0: 0},
        compiler_params=_cparams(("arbitrary",), 40),
        name=f"fourier_{n}",
    )(uf, cs, c64, s64, w_fourier)


S5_TW = S5_T * S5_H
S5_R = 128
S5_TOK = S5_R * S5_T
S5_STEPS_CTX = ROWS_CTX // S5_TOK
S5_STEPS_LAT = (ROWS - ROWS_CTX) // S5_TOK
S5_STEPS = S5_STEPS_CTX + S5_STEPS_LAT
S5_SEQ = N_CTX // S5_STEPS_CTX


def _s5_tables(a_re, a_im, log_dt, b_re, b_im, c_re, c_im):
    f = lambda a: a.astype(F32)
    a_re, a_im, b_re, b_im, c_re, c_im = map(f, (a_re, a_im, b_re, b_im, c_re, c_im))
    dt = jnp.exp(f(log_dt))[..., None]
    x, y = a_re * dt, a_im * dt
    kk = jnp.arange(S5_T + 1, dtype=F32)[None, None, :, None, None]
    mag = jnp.exp(kk * x[:, :, None])
    pr, pi = mag * jnp.cos(kk * y[:, :, None]), mag * jnp.sin(kk * y[:, :, None])
    nr, ni = pr[:, :, 1] - 1.0, pi[:, :, 1]
    den = a_re * a_re + a_im * a_im
    qr, qi = (nr * a_re + ni * a_im) / den, (ni * a_re - nr * a_im) / den
    bb_r = qr[..., None] * b_re - qi[..., None] * b_im
    bb_i = qr[..., None] * b_im + qi[..., None] * b_re
    cat = lambda u, v: jnp.concatenate([u, v], axis=-1)
    prg, pig = pr.transpose(0, 1, 3, 2, 4), pi.transpose(0, 1, 3, 2, 4)
    pows = jnp.stack([cat(prg, prg), cat(-pig, pig), cat(prg, -pig), cat(-pig, -prg), cat(prg, pig), cat(-pig, prg)],
                     axis=3)
    bt_r, bt_i = bb_r.transpose(0, 1, 2, 4, 3), bb_i.transpose(0, 1, 2, 4, 3)
    wrows = jnp.stack([cat(bt_r, bt_i), cat(bt_i, bt_r), cat(bt_r, -bt_i), cat(c_re, c_re), cat(c_im, c_im)],
                      axis=3)
    ar, ai = pr[:, :, S5_T], pi[:, :, S5_T]
    decay = jnp.stack([cat(ar, ar), cat(-ai, ai), cat(ai, -ai), jnp.zeros_like(cat(ar, ar))], axis=3)
    return pows, wrows, decay


def _block_transpose(arrs):
    arrs = list(arrs)
    width = arrs[0].shape[1]
    blk = lax.broadcasted_iota(jnp.int32, arrs[0].shape, 1) >> 4
    s = len(arrs) // 2
    while s >= 1:
        keep = (blk & s) == 0
        for i in range(len(arrs)):
            if i & s:
                continue
            lo, hi = arrs[i], arrs[i + s]
            arrs[i] = jnp.where(keep, lo, pltpu.roll(hi, s * S5_H, 1))
            arrs[i + s] = jnp.where(keep, pltpu.roll(lo, width - s * S5_H, 1), hi)
        s //= 2
    return arrs


def _s5_build_operators(pw_ref, w_ref, e_scr, esw_scr, ft_scr, tp_scr):
    lane = lax.broadcasted_iota(jnp.int32, (S5_H, S5_TW), 1)
    for d in range(2):
        for g in range(S5_G):
            pw = lambda v, k: pw_ref[0, d, g, v, k:k + 1, :]
            b_ri, b_ir, b_conj, c_rr, c_ii = (w_ref[0, d, g, v] for v in range(5))
            lag_rows = []
            for t in range(S5_T):
                blk = slice(t * S5_H, (t + 1) * S5_H)
                ke = S5_T - 1 - t if d == 0 else t
                e_scr[d, g, blk, :] = (b_ri * pw(0, ke) + b_ir * pw(1, ke)).astype(BF16)
                esw_scr[d, g, blk, :] = (b_ir * pw(0, ke) - b_ri * pw(1, ke)).astype(BF16)
                kf = t + 1 if d == 0 else S5_T - t
                ft_scr[d, g, blk, :] = (c_rr * pw(2, kf) + c_ii * pw(3, kf)).astype(BF16)
                kl = t if d == 0 else S5_T - 1 - t
                lag_rows.append(c_rr * pw(4, kl) + c_ii * pw(5, kl))
            mh, ml = _split_bf16(jnp.concatenate(lag_rows, axis=0))
            bh, bl = _split_bf16(b_conj)
            kt = _dot_nt(bh, mh) + _dot_nt(bh, ml) + _dot_nt(bl, mh)
            for ti in range(S5_T):
                if d == 0:
                    sh = S5_H * ti
                    blk_rows = jnp.where(lane >= sh, pltpu.roll(kt, sh, 1) if sh else kt, 0.0)
                else:
                    sh = S5_H * (S5_T - 1 - ti)
                    blk_rows = jnp.where(lane < S5_TW - sh, pltpu.roll(kt, S5_TW - sh, 1) if sh else kt, 0.0)
                tp_scr[d, g, ti * S5_H:(ti + 1) * S5_H, :] = blk_rows.astype(BF16)


def _s5_kernel(us_ref, pw_ref, w_ref, a_ref, s0_ref, y_ref, fin_ref,
               e_scr, esw_scr, ft_scr, tp_scr, x_scr, y_scr, loc_scr, lsw_scr, prev_scr):
    step = pl.program_id(0)

    @pl.when(step == 0)
    def _():
        _s5_build_operators(pw_ref, w_ref, e_scr, esw_scr, ft_scr, tp_scr)

    rows = pl.ds
    x_by_step = [jnp.concatenate([us_ref[0, rows(tl, S5_R, stride=S5_T), :],
                                  us_ref[1, rows(tl, S5_R, stride=S5_T), :]], axis=1) for tl in range(S5_T)]
    for g, xg in enumerate(_block_transpose(x_by_step)):
        x_scr[g] = xg.astype(BF16)
    fin_ref[...] = jnp.zeros(fin_ref.shape, F32)

    def scan(d, nseq, nc):
        batch = 8
        for g0 in range(0, S5_G, batch):
            coefs = [[jnp.broadcast_to(a_ref[0, d, g, r:r + 1, :], (nseq, 2 * S5_P)) for r in range(3)]
                     for g in range(g0, g0 + batch)]
            init = tuple((s0_ref[0, 0, d, g, 0:nseq, :2 * S5_P], s0_ref[0, 0, d, g, 0:nseq, 2 * S5_P:])
                         for g in range(g0, g0 + batch))

            def body(j, carry):
                ci = j if d == 0 else nc - 1 - j
                out = []
                for idx in range(batch):
                    g = g0 + idx
                    s, t = carry[idx]
                    ca, cb, cc = coefs[idx]
                    prev_scr[g, rows(ci, nseq, stride=nc), :] = s
                    out.append((s * ca + t * cb + loc_scr[g, rows(ci, nseq, stride=nc), :],
                                t * ca + s * cc + lsw_scr[g, rows(ci, nseq, stride=nc), :]))
                return tuple(out)

            fin = lax.fori_loop(0, nc, body, init)
            for idx in range(batch):
                fin_ref[0, d, g0 + idx, 0:nseq, :] = fin[idx][0]

    for d in range(2):
        for g in range(S5_G):
            xg = x_scr[g]
            loc_scr[g] = _dot(xg, e_scr[d, g])
            lsw_scr[g] = _dot(xg, esw_scr[d, g])

        @pl.when(step < S5_STEPS_CTX)
        def _():
            scan(d, N_CTX // S5_STEPS_CTX, T_CTX // S5_T)

        @pl.when(step >= S5_STEPS_CTX)
        def _():
            scan(d, N_LAT // S5_STEPS_LAT, T_LAT // S5_T)

        for g in range(S5_G):
            yg = _dot(x_scr[g], tp_scr[d, g]) + _dot_nt(prev_scr[g].astype(BF16), ft_scr[d, g])
            if d == 0:
                y_scr[g] = yg
            else:
                y_scr[g] += yg
    for to, y_to in enumerate(_block_transpose([y_scr[g] for g in range(S5_G)])):
        y_ref[0, rows(to, S5_R, stride=S5_T), :] = y_to[:, :LANES]
        y_ref[1, rows(to, S5_R, stride=S5_T), :] = y_to[:, LANES:]


def _s5(us, tabs, s0, l):
    pows, wrows, decay = tabs
    lay = lambda *shape: pl.BlockSpec((1,) + shape, lambda h: (l,) + (0,) * len(shape))
    op_buf = pltpu.VMEM((2, S5_G, S5_TW, 2 * S5_P), BF16)
    scan_buf = pltpu.VMEM((S5_G, S5_R, 2 * S5_P), F32)
    return pl.pallas_call(
        _s5_kernel,
        grid=(S5_STEPS,),
        in_specs=[_halves_spec(S5_TOK, lambda h: h),
                  lay(2, S5_G, 6, S5_T + 1, 2 * S5_P), lay(2, S5_G, 5, S5_H, 2 * S5_P), lay(2, S5_G, 4, 2 * S5_P),
                  pl.BlockSpec((1, 1, 2, S5_G, S5_SEQ, 4 * S5_P), lambda h: (l, h, 0, 0, 0, 0))],
        out_specs=[_halves_spec(S5_TOK, lambda h: h),
                   pl.BlockSpec((1, 2, S5_G, S5_SEQ, 2 * S5_P), lambda h: (h, 0, 0, 0, 0))],
        out_shape=[_HALVES_SHAPE, jax.ShapeDtypeStruct((S5_STEPS, 2, S5_G, S5_SEQ, 2 * S5_P), F32)],
        scratch_shapes=[op_buf, op_buf, op_buf, pltpu.VMEM((2, S5_G, S5_TW, S5_TW), BF16),
                        pltpu.VMEM((S5_G, S5_R, S5_TW), BF16), pltpu.VMEM((S5_G, S5_R, S5_TW), F32),
                        scan_buf, scan_buf, scan_buf],
        compiler_params=_cparams(("arbitrary",), 56),
        name="s5",
    )(us, pows, wrows, decay, s0)


def _split_bf16(a):
    hi = a.astype(BF16)
    return hi, (a - hi.astype(F32)).astype(BF16)


def _outproj_kernel(at_ref, fo_ref, ys_ref, us_ref, d_ref, wg_ref, wo_ref, x_ref, mod_ref,
                    g_ref, b_ref, wr_ref, x1_ref, h2_ref, lg_ref, wob_ref):
    @pl.when(pl.program_id(0) == 0)
    def _():
        wob_ref[...] = wo_ref[0].astype(BF16)

    ys = jnp.concatenate([ys_ref[0], ys_ref[1]], axis=1)
    us = jnp.concatenate([us_ref[0], us_ref[1]], axis=1)
    g = jax.nn.gelu(ys + us * d_ref[0])
    s5 = g * jax.nn.sigmoid(_dot(g.astype(BF16), wg_ref[0].astype(BF16)))
    mix = (_dot(at_ref[...], wob_ref[0:ATTN_W, :])
           + _dot(fo_ref[...], wob_ref[ATTN_W:ATTN_W + FOUR_W, :])
           + _dot(s5.astype(BF16), wob_ref[ATTN_W + FOUR_W:, :]))
    m = mod_ref[0, 0]
    r = ALPHA * x_ref[...] + m[2:3] * mix
    rc = r - jnp.mean(r, axis=-1, keepdims=True)
    t = rc * lax.rsqrt(jnp.mean(rc * rc, axis=-1, keepdims=True) + LN_EPS)
    gain, bias = g_ref[0], b_ref[0]
    x1_ref[...] = t * gain + bias
    up = 1.0 + m[4:5]
    hb = (t * (gain * up) + (bias * up + m[3:4])).astype(BF16)
    h2_ref[...] = hb
    lg_ref[...] = _dot_nt(wr_ref[0].astype(BF16), hb)


def _outproj(attn, four, ys5, us, s5_d, w_glu, w_out, x, mod, ln_g, ln_b, w_router_t, l):
    row = lambda w: pl.BlockSpec((RB_OUT, w), lambda i: (i, 0))
    halves = _halves_spec(RB_OUT, lambda i: i)
    lay = lambda *shape: pl.BlockSpec((1,) + shape, lambda i: (l,) + (0,) * len(shape))
    return pl.pallas_call(
        _outproj_kernel,
        grid=(ROWS // RB_OUT,),
        in_specs=[row(ATTN_W), row(FOUR_W), halves, halves, lay(1, S5_W), lay(S5_W, S5_W), lay(D, D), row(D),
                  pl.BlockSpec((1, 1, 6, D), lambda i: (l, _mod_row(i, RB_OUT), 0, 0)),
                  lay(1, D), lay(1, D), lay(N_EXP, D)],
        out_specs=[row(D), row(D), pl.BlockSpec((N_EXP, RB_OUT), lambda i: (0, i))],
        out_shape=[jax.ShapeDtypeStruct((ROWS, D), F32), jax.ShapeDtypeStruct((ROWS, D), BF16),
                   jax.ShapeDtypeStruct((N_EXP, ROWS), F32)],
        scratch_shapes=[pltpu.VMEM((D, D), BF16)],
        compiler_params=_cparams(("arbitrary",), 40),
        name="outproj",
    )(attn, four, ys5, us, s5_d, w_glu, w_out, x, mod, ln_g, ln_b, w_router_t)


SEQ_GROUP = LANES // N_EXP


def _router_kernel(lg_ref, crow_ref, ccol_ref, gcol_ref, *, n, cap, n_seq):
    rows = n_seq * N_EXP
    lg = jnp.concatenate([lg_ref[:, s * n:(s + 1) * n] for s in range(n_seq)], axis=0).reshape(n_seq, N_EXP, n)
    e = jnp.exp(lg - jnp.max(lg, axis=1, keepdims=True))
    aff = (e / jnp.sum(e, axis=1, keepdims=True)).reshape(rows, n)
    thr_bits = jnp.zeros((rows, 1), jnp.int32)
    for bit in range(30, -1, -1):
        cand = thr_bits | (1 << bit)
        cnt = jnp.sum(jnp.where(aff >= lax.bitcast_convert_type(cand, F32), 1.0, 0.0), axis=1, keepdims=True)
        thr_bits = jnp.where(cnt >= cap, cand, thr_bits)
    thr = lax.bitcast_convert_type(thr_bits, F32)
    above = aff > thr
    tied = aff == thr
    need = cap - jnp.sum(jnp.where(above, 1.0, 0.0), axis=1, keepdims=True)
    r0 = lax.broadcasted_iota(jnp.int32, (n, n), 0)
    r1 = lax.broadcasted_iota(jnp.int32, (n, n), 1)
    before = jnp.where(r0 < r1, 1.0, 0.0).astype(BF16)
    eye = jnp.where(r0 == r1, 1.0, 0.0).astype(BF16)
    tied_rank = _dot(jnp.where(tied, 1.0, 0.0).astype(BF16), before)
    sel = above | (tied & (tied_rank < need))
    slot = _dot(jnp.where(sel, 1.0, 0.0).astype(BF16), before)
    code = jnp.where(sel, slot + 1.0, 0.0)
    gate = jnp.where(sel, aff, 0.0)
    for s in range(n_seq):
        crow_ref[:, s * n:(s + 1) * n] = code[s * N_EXP:(s + 1) * N_EXP]
    for grp in range(pl.cdiv(n_seq, SEQ_GROUP)):
        lo, hi = grp * LANES, min((grp + 1) * LANES, rows)
        pad = [jnp.zeros((LANES - (hi - lo), n), F32)] if hi - lo < LANES else []
        ccol_ref[grp] = _dot_nt(eye, jnp.concatenate([code[lo:hi]] + pad, axis=0).astype(BF16))
        gcol_ref[grp] = _dot_nt(eye, jnp.concatenate([gate[lo:hi]] + pad, axis=0).astype(BF16))


def _router(lg, n, n_seq, blk, cap):
    groups = pl.cdiv(n_seq, SEQ_GROUP)
    whole = lambda *shape: pl.BlockSpec(shape, lambda i: (0,) * len(shape))
    return pl.pallas_call(
        functools.partial(_router_kernel, n=n, cap=cap, n_seq=n_seq),
        grid=(1,),
        in_specs=[pl.BlockSpec((N_EXP, n_seq * n), lambda i: (0, blk))],
        out_specs=[whole(N_EXP, n_seq * n), whole(groups, n, LANES), whole(groups, n, LANES)],
        out_shape=[jax.ShapeDtypeStruct((N_EXP, n_seq * n), F32),
                   jax.ShapeDtypeStruct((groups, n, LANES), F32),
                   jax.ShapeDtypeStruct((groups, n, LANES), F32)],
        compiler_params=_cparams(("arbitrary",), 48),
        name=f"router_{n}",
    )(lg)


def _gather_kernel(crow_ref, h_ref, o_ref, *, n, cap, n_sub):
    slot1 = (lax.broadcasted_iota(jnp.int32, (cap, n), 0) + 1).astype(F32)
    for j in range(n_sub):
        code = crow_ref[:, j * n:(j + 1) * n]
        onehot = jnp.concatenate(
            [jnp.where(code[e:e + 1, :] == slot1, 1.0, 0.0).astype(BF16) for e in range(N_EXP)], axis=0)
        xs = _dot(onehot, h_ref[j * n:(j + 1) * n, :])
        o_ref[:, j * cap:(j + 1) * cap, :] = xs.reshape(N_EXP, cap, D).astype(BF16)


def _gather(crow, h2, n, n_seq, n_sub, blk_off, cap):
    return pl.pallas_call(
        functools.partial(_gather_kernel, n=n, cap=cap, n_sub=n_sub),
        grid=(n_seq // n_sub,),
        in_specs=[pl.BlockSpec((N_EXP, n_sub * n), lambda s: (0, s)),
                  pl.BlockSpec((n_sub * n, D), lambda s: (blk_off + s, 0))],
        out_specs=pl.BlockSpec((N_EXP, n_sub * cap, D), lambda s: (0, s, 0)),
        out_shape=jax.ShapeDtypeStruct((N_EXP, n_seq * cap, D), BF16),
        compiler_params=_cparams(("arbitrary",), 48),
        name=f"gather_{n}",
    )(crow, h2)


FF_BLK = 512
SLOTS = N_CTX * CAP_CTX


def _ffn_kernel(xc_ref, xq_ref, wg_ref, wu_ref, wd_ref, yc_ref, yq_ref, acc_ref):
    f = pl.program_id(1)
    x = jnp.concatenate([xc_ref[0], xq_ref[0]], axis=0)
    g = _dot(x, wg_ref[0, 0].astype(BF16))
    u = _dot(x, wu_ref[0, 0].astype(BF16))
    hid = (g * jax.nn.sigmoid(g) * u).astype(BF16)
    y = _dot(hid, wd_ref[0, 0].astype(BF16))

    @pl.when(f == 0)
    def _():
        acc_ref[...] = y

    @pl.when(f == FF // FF_BLK - 1)
    def _():
        tot = acc_ref[...] + y
        yc_ref[0] = tot[:SLOTS].astype(BF16)
        yq_ref[0] = tot[SLOTS:].astype(BF16)


def _ffn(xs_c, xs_q, w_gate, w_up, w_down, l):
    assert FF // FF_BLK == 2
    xspec = pl.BlockSpec((1, SLOTS, D), lambda e, f: (e, 0, 0))
    return pl.pallas_call(
        _ffn_kernel,
        grid=(N_EXP, FF // FF_BLK),
        in_specs=[xspec, xspec,
                  pl.BlockSpec((1, 1, D, FF_BLK), lambda e, f: (l, e, 0, f)),
                  pl.BlockSpec((1, 1, D, FF_BLK), lambda e, f: (l, e, 0, f)),
                  pl.BlockSpec((1, 1, FF_BLK, D), lambda e, f: (l, e, f, 0))],
        out_specs=[xspec, xspec],
        out_shape=[jax.ShapeDtypeStruct((N_EXP, SLOTS, D), BF16)] * 2,
        scratch_shapes=[pltpu.VMEM((2 * SLOTS, D), F32)],
        compiler_params=_cparams(("arbitrary", "arbitrary"), 48),
        name="ffn",
    )(xs_c, xs_q, w_gate, w_up, w_down)


def _combine_kernel(ccol_ref, gcol_ref, ys_ref, x_ref, mod_ref, g_ref, b_ref, o_ref, *, cap, rb, n_sub):
    width = N_EXP * cap
    k = lax.broadcasted_iota(jnp.int32, (LANES, width), 0)
    col = lax.broadcasted_iota(jnp.int32, (LANES, width), 1)
    shift = cap.bit_length() - 1
    slot1 = ((lax.broadcasted_iota(jnp.int32, (1, width), 1) & (cap - 1)) + 1).astype(F32)
    code_t, gate_t = ccol_ref[0].astype(BF16), gcol_ref[0].astype(BF16)
    m = mod_ref[0, 0]
    for j in range(n_sub):
        seq = pl.program_id(0) * n_sub + j
        first = (seq % SEQ_GROUP) * N_EXP
        spread = jnp.where((col >> shift) + first == k, 1.0, 0.0).astype(BF16)
        code = _dot(code_t, spread)
        gate = _dot(gate_t, spread)
        weights = jnp.where(code == slot1, gate, 0.0).astype(BF16)
        moe = _dot(weights, ys_ref[:, j * cap:(j + 1) * cap, :].reshape(width, D))
        rows = slice(j * rb, (j + 1) * rb)
        o_ref[rows, :] = _layer_norm(ALPHA * x_ref[rows, :] + m[5:6] * moe, g_ref[0], b_ref[0])


def _combine(ccol, gcol, ys, x1, mod, ln_g, ln_b, l, cap, n_seq, n_sub, rb, rb_per_seq, rb_off, in_place):
    assert n_sub == 1 or rb_per_seq == 1
    rows = n_sub * rb
    lay = lambda *shape: pl.BlockSpec((1,) + shape, lambda s, r: (l,) + (0,) * len(shape))
    table = pl.BlockSpec((1, rb, LANES), lambda s, r: (s * n_sub // SEQ_GROUP, r, 0))
    block = lambda s, r: rb_off + s * rb_per_seq + r
    out_off = rb_off if in_place else 0
    return pl.pallas_call(
        functools.partial(_combine_kernel, cap=cap, rb=rb, n_sub=n_sub),
        grid=(n_seq // n_sub, rb_per_seq),
        in_specs=[table, table,
                  pl.BlockSpec((N_EXP, n_sub * cap, D), lambda s, r: (0, s, 0)),
                  pl.BlockSpec((rows, D), lambda s, r: (block(s, r), 0)),
                  pl.BlockSpec((1, 1, 6, D), lambda s, r: (l, _mod_row(block(s, r), rows), 0, 0)),
                  lay(1, D), lay(1, D)],
        out_specs=pl.BlockSpec((rows, D), lambda s, r: (block(s, r) - rb_off + out_off, 0)),
        out_shape=jax.ShapeDtypeStruct((ROWS if in_place else n_seq * rb_per_seq * rb, D), F32),
        input_output_aliases={3: 0} if in_place else {},
        compiler_params=_cparams(("arbitrary", "arbitrary"), 48),
        name=f"combine_{cap}",
    )(ccol, gcol, ys, x1, mod, ln_g, ln_b)


def kernel(x_prompt, x_sample, cache_k, cache_v, state_s5_re, state_s5_im, c, c_ctx, ln_in_g, ln_in_b, w_ada, b_ada,
           w_in, w_fourier, attn_sink, s5_a_re, s5_a_im, s5_log_dt, s5_b_re, s5_b_im, s5_c_re, s5_c_im, s5_d,
           s5_w_glu, w_out, ln1_g, ln1_b, w_router, w_gate, w_up, w_down, ln2_g, ln2_b):
    x = _ln_in(x_prompt.reshape(ROWS_CTX, D), x_sample.reshape(ROWS - ROWS_CTX, D), ln_in_g, ln_in_b)
    cond8 = jnp.concatenate([c_ctx[None], c, jnp.zeros((8 - 1 - N_LAT, D), F32)], axis=0)
    mod = _ada(cond8, w_ada, b_ada).reshape(DEPTH, 8, 6, D)

    rope_tabs = _rope_tables()
    cs_ctx, cs_lat = _dft_tables(T_CTX), _dft_tables(T_LAT)
    c64, s64 = _dft_channel_tables()
    cache_k = cache_k.reshape(N_LAT, DEPTH, PAST, KV_W)
    cache_v = cache_v.reshape(N_LAT, DEPTH, PAST, KV_W)
    w_router_t = jnp.swapaxes(w_router, 1, 2)
    sink_b = jnp.broadcast_to(attn_sink[:, :, None], (DEPTH, N_HEADS, LANES))
    s5_tabs = _s5_tables(s5_a_re, s5_a_im, s5_log_dt, s5_b_re, s5_b_im, s5_c_re, s5_c_im)
    lat_per_step = N_LAT // S5_STEPS_LAT
    s0 = jnp.concatenate([state_s5_re, state_s5_im, state_s5_im, state_s5_re], axis=-1)
    s0 = s0.reshape(S5_STEPS_LAT, lat_per_step, DEPTH, 2, S5_G, 4 * S5_P).transpose(2, 0, 3, 4, 1, 5)
    s0 = jnp.pad(s0, ((0, 0), (S5_STEPS_CTX, 0), (0, 0), (0, 0), (0, S5_SEQ - lat_per_step), (0, 0)))
    ln1_g, ln1_b, ln2_g, ln2_b, s5_d = (a.reshape(DEPTH, 1, -1) for a in (ln1_g, ln1_b, ln2_g, ln2_b, s5_d))

    new_k, new_v, new_s = [], [], []
    for l in range(DEPTH):
        q, k, v, uf, us, k_t, v_t = _inproj(x, mod, w_in, rope_tabs, l)
        new_k.append(k_t)
        new_v.append(v_t)
        attn = _attn_lat(_attn_ctx(q, k, v, sink_b[l]), k, v, cache_k, cache_v, sink_b[l], l)
        four = _fourier(uf, cs_ctx, c64, s64, w_fourier, l, T_CTX, N_CTX, SEQ_PER_STEP_CTX, 0)
        four = _fourier(four, cs_lat, c64, s64, w_fourier, l, T_LAT, N_LAT, 1, ROWS_CTX // T_LAT)
        y5, fin = _s5(us, s5_tabs, s0, l)
        new_s.append(fin[:S5_STEPS_CTX].transpose(0, 3, 1, 2, 4).reshape(N_CTX, 2, S5_G, 2 * S5_P))
        x1, h2, lg = _outproj(attn, four, y5, us, s5_d, s5_w_glu, w_out, x, mod, ln1_g, ln1_b, w_router_t, l)
        crow_c, ccol_c, gcol_c = _router(lg, T_CTX, N_CTX, 0, CAP_CTX)
        crow_q, ccol_q, gcol_q = _router(lg, T_LAT, N_LAT, 1, CAP_LAT)
        xs_c = _gather(crow_c, h2, T_CTX, N_CTX, SEQ_PER_STEP_CTX, 0, CAP_CTX)
        xs_q = _gather(crow_q, h2, T_LAT, N_LAT, 1, ROWS_CTX // T_LAT, CAP_LAT)
        ys_c, ys_q = _ffn(xs_c, xs_q, w_gate, w_up, w_down, l)
        args_c = (ccol_c, gcol_c, ys_c), (mod, ln2_g, ln2_b, l, CAP_CTX, N_CTX, SEQ_PER_STEP_CTX, T_CTX, 1, 0)
        args_q = (ccol_q, gcol_q, ys_q), (mod, ln2_g, ln2_b, l, CAP_LAT, N_LAT, 1, RB_COMBINE_LAT,
                                          T_LAT // RB_COMBINE_LAT, ROWS_CTX // RB_COMBINE_LAT)
        if l < DEPTH - 1:
            x = _combine(*args_c[0], x1, *args_c[1], in_place=True)
            x = _combine(*args_q[0], x, *args_q[1], in_place=True)
        else:
            y_ctx = _combine(*args_c[0], x1, *args_c[1], in_place=False)
            y_lat = _combine(*args_q[0], x1, *args_q[1], in_place=False)

    new_s = jnp.stack(new_s, axis=1)

    def cache(feat_major):
        return jnp.stack(feat_major).reshape(DEPTH, N_KV, HD, N_CTX, T_CTX).transpose(3, 0, 4, 1, 2)

    return (y_ctx.reshape(N_CTX, T_CTX, D), y_lat.reshape(N_LAT, T_LAT, D),
            cache(new_k), cache(new_v), new_s[..., :S5_P], new_s[..., S5_P:])
```

```python
import functools

import jax
import jax.numpy as jnp
import numpy as np
from jax import lax
from jax.experimental import pallas as pl
from jax.experimental.pallas import tpu as pltpu

F32 = jnp.float32
BF16 = jnp.bfloat16

D = 1024
N_CTX, T_CTX = 16, 256
N_LAT, T_LAT = 4, 1024
ROWS_CTX = N_CTX * T_CTX
ROWS = ROWS_CTX + N_LAT * T_LAT
DEPTH = 4
PAST = 512
GRID_W = 64
N_HEADS, N_KV, HD = 8, 2, 64
GQA = N_HEADS // N_KV
ATTN_W, KV_W = N_HEADS * HD, N_KV * HD
LOG2E = 1.4426950408889634
Q_SCALE = HD ** -0.5 * LOG2E
WINDOW = 128
FOUR_H, FOUR_W = 4, 256
S5_G, S5_H, S5_P, S5_W = 16, 16, 64, 256
S5_T = 16
IN_W = ATTN_W + 2 * KV_W + FOUR_W + S5_W
N_EXP, FF = 16, 1024
CAP_CTX, CAP_LAT = 2 * T_CTX // N_EXP, 2 * T_LAT // N_EXP
LN_EPS = 1e-5
NEG_INF = -1e30
ALPHA = (2 * DEPTH) ** 0.25
ROPE_BASE = 10000.0
RB_IN = 1024
RB_OUT = 512
RB_COMBINE_LAT = 512
LANES = 128
MIB = 2 ** 20
HIGHEST = lax.Precision.HIGHEST


def _cparams(sem, vmem_mib):
    return pltpu.CompilerParams(dimension_semantics=sem, vmem_limit_bytes=vmem_mib * MIB)


def _dot(a, b):
    return jnp.dot(a, b, preferred_element_type=F32)


def _dot_nt(a, b):
    return lax.dot_general(a, b, (((1,), (1,)), ((), ())), preferred_element_type=F32)


def _dot_tn(a, b):
    return lax.dot_general(a, b, (((0,), (0,)), ((), ())), preferred_element_type=F32)


def _layer_norm(x, g, b):
    mu = jnp.mean(x, axis=-1, keepdims=True)
    xc = x - mu
    var = jnp.mean(xc * xc, axis=-1, keepdims=True)
    return xc * lax.rsqrt(var + LN_EPS) * g + b


def _mod_row(block, block_rows):
    first = block * block_rows
    return jnp.where(first < ROWS_CTX, 0, 1 + (first - ROWS_CTX) // T_LAT)


_HALVES_SHAPE = jax.ShapeDtypeStruct((2, ROWS, LANES), F32)


def _halves_spec(rows, row_block):
    return pl.BlockSpec((2, rows, LANES), lambda *ids: (0, row_block(*ids), 0))


def _two_stream_specs(rows, width):
    n_ctx = ROWS_CTX // rows
    return (pl.BlockSpec((rows, width), lambda i: (jnp.minimum(i, n_ctx - 1), 0)),
            pl.BlockSpec((rows, width), lambda i: (jnp.maximum(i - n_ctx, 0), 0)))


def _pick(rows, ctx_ref, lat_ref):
    return lax.cond(pl.program_id(0) < ROWS_CTX // rows, lambda: ctx_ref[...], lambda: lat_ref[...])


def _ln_in_kernel(xc_ref, xq_ref, g_ref, b_ref, o_ref):
    o_ref[...] = _layer_norm(_pick(RB_OUT, xc_ref, xq_ref), g_ref[...], b_ref[...])


def _ln_in(x_ctx, x_lat, g, b):
    return pl.pallas_call(
        _ln_in_kernel,
        grid=(ROWS // RB_OUT,),
        in_specs=[*_two_stream_specs(RB_OUT, D),
                  pl.BlockSpec((1, D), lambda i: (0, 0)),
                  pl.BlockSpec((1, D), lambda i: (0, 0))],
        out_specs=pl.BlockSpec((RB_OUT, D), lambda i: (i, 0)),
        out_shape=jax.ShapeDtypeStruct((ROWS, D), F32),
        compiler_params=_cparams(("arbitrary",), 32),
        name="ln_in",
    )(x_ctx, x_lat, g.reshape(1, D), b.reshape(1, D))


def _ada_kernel(c_ref, w_ref, b_ref, o_ref):
    c = c_ref[...]
    s = (c * jax.nn.sigmoid(c)).astype(BF16)
    o_ref[0] = _dot(s, w_ref[0].astype(BF16)) + b_ref[0]


def _ada(cond8, w_ada, b_ada):
    tn = 1536
    return pl.pallas_call(
        _ada_kernel,
        grid=(DEPTH, 6 * D // tn),
        in_specs=[pl.BlockSpec((8, D), lambda l, j: (0, 0)),
                  pl.BlockSpec((1, D, tn), lambda l, j: (l, 0, j)),
                  pl.BlockSpec((1, 1, tn), lambda l, j: (l, 0, j))],
        out_specs=pl.BlockSpec((1, 8, tn), lambda l, j: (l, 0, j)),
        out_shape=jax.ShapeDtypeStruct((DEPTH, 8, 6 * D), F32),
        compiler_params=_cparams(("arbitrary", "arbitrary"), 40),
        name="adaln",
    )(cond8, w_ada, b_ada.reshape(DEPTH, 1, 6 * D))


def _inproj_kernel(x_ref, mod_ref, w_ref, cos_ref, sin_ref, q_ref, k_ref, v_ref, uf_ref, us_ref, kt_ref, vt_ref,
                   wb_ref, wswb_ref):
    step = pl.program_id(0)
    n_ctx_steps = ROWS_CTX // RB_IN

    @pl.when(step == 0)
    def _():
        wb = w_ref[0].astype(BF16)
        wb_ref[...] = wb
        n_qk = ATTN_W + KV_W
        src = lax.broadcasted_iota(jnp.int32, (n_qk, n_qk), 0)
        dst = lax.broadcasted_iota(jnp.int32, (n_qk, n_qk), 1)
        swap = jnp.where(src == (dst ^ 1), 1.0, 0.0).astype(BF16)
        wswb_ref[...] = _dot(wb[:, :n_qk], swap).astype(BF16)

    m = mod_ref[0, 0]
    hb = (x_ref[...] * (1.0 + m[1:2]) + m[0:1]).astype(BF16)
    p = _dot(hb, wb_ref[...])
    q = p[:, :ATTN_W]
    k = p[:, ATTN_W:ATTN_W + KV_W]
    v = p[:, ATTN_W + KV_W:ATTN_W + 2 * KV_W]

    psw = _dot(hb, wswb_ref[...])
    cos, sin = cos_ref[...], sin_ref[...]
    wide = lambda t: jnp.concatenate([t] * (ATTN_W // KV_W), axis=1)
    latent = step >= n_ctx_steps
    q = jnp.where(latent, q * wide(cos) + psw[:, :ATTN_W] * wide(sin), q)
    k = jnp.where(latent, k * cos + psw[:, ATTN_W:] * sin, k)
    q_ref[...] = (q * Q_SCALE).astype(BF16)
    k_ref[...] = k
    v_ref[...] = v

    @pl.when(step < n_ctx_steps)
    def _():
        kt_ref[...] = k.T
        vt_ref[...] = v.T

    uf_ref[...] = p[:, ATTN_W + 2 * KV_W:ATTN_W + 2 * KV_W + FOUR_W].astype(BF16)
    for half in range(S5_W // LANES):
        lo = ATTN_W + 2 * KV_W + FOUR_W + half * LANES
        us_ref[half] = p[:, lo:lo + LANES]


def _inproj(x, mod, w_in, rope_tabs, l):
    assert RB_IN == T_LAT
    outs = ((ATTN_W, BF16), (KV_W, F32), (KV_W, F32), (FOUR_W, BF16))
    table = pl.BlockSpec((T_LAT, KV_W), lambda i: (0, 0))
    feat_major = pl.BlockSpec((KV_W, RB_IN), lambda i: (0, jnp.minimum(i, ROWS_CTX // RB_IN - 1)))
    return pl.pallas_call(
        _inproj_kernel,
        grid=(ROWS // RB_IN,),
        in_specs=[pl.BlockSpec((RB_IN, D), lambda i: (i, 0)),
                  pl.BlockSpec((1, 1, 6, D), lambda i: (l, _mod_row(i, RB_IN), 0, 0)),
                  pl.BlockSpec((1, D, IN_W), lambda i: (l, 0, 0)),
                  table, table],
        out_specs=([pl.BlockSpec((RB_IN, w), lambda i: (i, 0)) for w, _ in outs] + [_halves_spec(RB_IN, lambda i: i)]
                   + [feat_major, feat_major]),
        out_shape=([jax.ShapeDtypeStruct((ROWS, w), dt) for w, dt in outs] + [_HALVES_SHAPE]
                   + [jax.ShapeDtypeStruct((KV_W, ROWS_CTX), F32)] * 2),
        scratch_shapes=[pltpu.VMEM((D, IN_W), BF16), pltpu.VMEM((D, ATTN_W + KV_W), BF16)],
        compiler_params=_cparams(("arbitrary",), 56),
        name="inproj",
    )(x, mod, w_in, *rope_tabs)


SINK_ROWS = 16


def _softmax_av(score_blocks, sink, value_blocks):
    m = sink
    for s in score_blocks:
        m = jnp.maximum(m, jnp.max(s, axis=0, keepdims=True))
    first = lax.broadcasted_iota(jnp.int32, (SINK_ROWS, 1), 0) == 0
    probs = [jnp.exp2(s - m).astype(BF16) for s in score_blocks]
    probs.append(jnp.where(first, jnp.exp2(sink - m), 0.0).astype(BF16))
    ext = []
    for v in value_blocks:
        ones = jnp.ones(v.shape, BF16)
        ext.append(jnp.concatenate([v, v, ones, ones], axis=1))
    row = lax.broadcasted_iota(jnp.int32, (SINK_ROWS, 4 * HD), 0)
    lane = lax.broadcasted_iota(jnp.int32, (SINK_ROWS, 4 * HD), 1)
    ext.append(jnp.where((row == 0) & (lane >= 2 * HD), 1.0, 0.0).astype(BF16))
    acc = _dot_tn(jnp.concatenate(probs, axis=0), jnp.concatenate(ext, axis=0))
    return acc[:, :2 * HD] / acc[:, 2 * HD:]


def _stack_group(q, kv):
    return jnp.concatenate([q[:, (kv * GQA + g) * HD:(kv * GQA + g + 1) * HD] for g in range(GQA)], axis=0)


def _sink_row(sink_ref, kv, rows):
    return LOG2E * jnp.concatenate(
        [jnp.broadcast_to(sink_ref[kv * GQA + g:kv * GQA + g + 1, 0:1], (1, rows)) for g in range(GQA)], axis=1)


def _unstack_group(o, rows):
    lane = lax.broadcasted_iota(jnp.int32, (rows, 2 * HD), 1)
    return [jnp.where(lane < HD, o[2 * j * rows:(2 * j + 1) * rows], o[(2 * j + 1) * rows:(2 * j + 2) * rows])
            for j in range(GQA // 2)]


def _attn_ctx_kernel(q_ref, k_ref, v_ref, sink_ref, o_ref):
    q = q_ref[...]
    k = k_ref[...].astype(BF16)
    v = v_ref[...].astype(BF16)
    outs = []
    for kv in range(N_KV):
        head = slice(kv * HD, (kv + 1) * HD)
        s = _dot_nt(k[:, head], _stack_group(q, kv))
        o = _softmax_av([s], _sink_row(sink_ref, kv, T_CTX), [v[:, head]])
        outs += _unstack_group(o, T_CTX)
    o_ref[...] = jnp.concatenate(outs, axis=1).astype(BF16)


def _attn_ctx(q, k, v, sink_b):
    return pl.pallas_call(
        _attn_ctx_kernel,
        grid=(N_CTX,),
        in_specs=[pl.BlockSpec((T_CTX, ATTN_W), lambda s: (s, 0)),
                  pl.BlockSpec((T_CTX, KV_W), lambda s: (s, 0)),
                  pl.BlockSpec((T_CTX, KV_W), lambda s: (s, 0)),
                  pl.BlockSpec((N_HEADS, LANES), lambda s: (0, 0))],
        out_specs=pl.BlockSpec((T_CTX, ATTN_W), lambda s: (s, 0)),
        out_shape=jax.ShapeDtypeStruct((ROWS, ATTN_W), BF16),
        input_output_aliases={0: 0},
        compiler_params=_cparams(("arbitrary",), 32),
        name="attn_ctx",
    )(q, k, v, sink_b)


def _attn_lat_kernel(q_ref, k_ref, v_ref, ck_ref, cv_ref, sink_ref, o_ref):
    i = pl.program_id(1)
    n_blk = T_LAT // WINDOW
    q = q_ref[...]

    def kv_block(j):
        st = pl.multiple_of(j * WINDOW, WINDOW)
        return k_ref[pl.ds(st, WINDOW), :].astype(BF16), v_ref[pl.ds(st, WINDOW), :].astype(BF16)

    k0, v0 = kv_block(jnp.maximum(i - 1, 0))
    k1, v1 = kv_block(i)
    k2, v2 = kv_block(jnp.minimum(i + 1, n_blk - 1))
    k_win = jnp.concatenate([k0, k1, k2], axis=0)
    k_ctx = ck_ref[0, 0].astype(BF16)
    v_ctx = cv_ref[0, 0].astype(BF16)
    rows = GQA * WINDOW
    c = lax.broadcasted_iota(jnp.int32, (WINDOW, rows), 0)
    r = lax.broadcasted_iota(jnp.int32, (WINDOW, rows), 1) & (WINDOW - 1)
    far = 2 * WINDOW
    prev_ok = c >= r + jnp.where(i > 0, 0, far)
    next_ok = c + jnp.where(i < n_blk - 1, 0, far) <= r
    outs = []
    for kv in range(N_KV):
        head = slice(kv * HD, (kv + 1) * HD)
        qs = _stack_group(q, kv)
        s_win = _dot_nt(k_win[:, head], qs)
        scores = [jnp.where(prev_ok, s_win[:WINDOW], NEG_INF), s_win[WINDOW:2 * WINDOW],
                  jnp.where(next_ok, s_win[2 * WINDOW:], NEG_INF), _dot_nt(k_ctx[:, head], qs)]
        values = [v0[:, head], v1[:, head], v2[:, head], v_ctx[:, head]]
        outs += _unstack_group(_softmax_av(scores, _sink_row(sink_ref, kv, WINDOW), values), WINDOW)
    o_ref[...] = jnp.concatenate(outs, axis=1).astype(BF16)


def _attn_lat(q, k, v, cache_k, cache_v, sink_b, l):
    n_blk = T_LAT // WINDOW
    off = ROWS_CTX // WINDOW
    return pl.pallas_call(
        _attn_lat_kernel,
        grid=(N_LAT, n_blk),
        in_specs=[pl.BlockSpec((WINDOW, ATTN_W), lambda b, i: (off + b * n_blk + i, 0)),
                  pl.BlockSpec((T_LAT, KV_W), lambda b, i: (ROWS_CTX // T_LAT + b, 0)),
                  pl.BlockSpec((T_LAT, KV_W), lambda b, i: (ROWS_CTX // T_LAT + b, 0)),
                  pl.BlockSpec((1, 1, PAST, KV_W), lambda b, i: (b, l, 0, 0)),
                  pl.BlockSpec((1, 1, PAST, KV_W), lambda b, i: (b, l, 0, 0)),
                  pl.BlockSpec((N_HEADS, LANES), lambda b, i: (0, 0))],
        out_specs=pl.BlockSpec((WINDOW, ATTN_W), lambda b, i: (off + b * n_blk + i, 0)),
        out_shape=jax.ShapeDtypeStruct((ROWS, ATTN_W), BF16),
        input_output_aliases={0: 0},
        compiler_params=_cparams(("arbitrary", "arbitrary"), 40),
        name="attn_lat",
    )(q, k, v, cache_k, cache_v, sink_b)


def _rope_tables():
    rows = T_LAT // GRID_W
    row = jnp.repeat(jnp.arange(rows, dtype=F32), GRID_W)
    col = jnp.tile(jnp.arange(GRID_W, dtype=F32), rows)
    n_freq = HD // 4
    freqs = ROPE_BASE ** (-jnp.arange(n_freq, dtype=F32) / n_freq)
    ang = jnp.concatenate([row[:, None] * freqs, col[:, None] * freqs], axis=-1)
    cos = jnp.repeat(jnp.cos(ang), 2, axis=-1)
    sign = jnp.tile(jnp.array([-1.0, 1.0], F32), HD // 2)
    sin = jnp.repeat(jnp.sin(ang), 2, axis=-1) * sign
    return jnp.tile(cos, (1, N_KV)), jnp.tile(sin, (1, N_KV))


SEQ_PER_STEP_CTX = 4


def _four_kernel(u_ref, cs_ref, c64_ref, s64_ref, wf_ref, o_ref, *, n, n_sub):
    wf = wf_ref[0].astype(BF16)
    for j in range(n_sub):
        rows = slice(j * n, (j + 1) * n)
        ub = u_ref[rows, :]
        uc = _dot(ub, c64_ref[...]).astype(BF16)
        us = _dot(ub, s64_ref[...]).astype(BF16)
        f = _dot(cs_ref[...], jnp.concatenate([uc, us], axis=0))
        o_ref[rows, :] = _dot(f.astype(BF16), wf).astype(BF16)


def _dft_tables(n):
    j = np.arange(n)
    ang = 2.0 * np.pi * ((j[:, None] * j[None, :]) % n) / n
    cs = np.concatenate([np.cos(ang), -np.sin(ang)], axis=1) / np.sqrt(n)
    return jnp.asarray(cs.astype(np.float32)).astype(BF16)


def _dft_channel_tables():
    j = np.arange(HD)
    ang = 2.0 * np.pi * ((j[:, None] * j[None, :]) % HD) / HD
    eye = np.eye(FOUR_H)
    c = np.kron(eye, np.cos(ang)) / np.sqrt(HD)
    s = np.kron(eye, np.sin(ang)) / np.sqrt(HD)
    return jnp.asarray(c.astype(np.float32)).astype(BF16), jnp.asarray(s.astype(np.float32)).astype(BF16)


def _fourier(uf, cs, c64, s64, w_fourier, l, n, n_seq, n_sub, blk_off):
    rows = n_sub * n
    return pl.pallas_call(
        functools.partial(_four_kernel, n=n, n_sub=n_sub),
        grid=(n_seq // n_sub,),
        in_specs=[pl.BlockSpec((rows, FOUR_W), lambda s: (blk_off + s, 0)),
                  pl.BlockSpec((n, 2 * n), lambda s: (0, 0)),
                  pl.BlockSpec((FOUR_W, FOUR_W), lambda s: (0, 0)),
                  pl.BlockSpec((FOUR_W, FOUR_W), lambda s: (0, 0)),
                  pl.BlockSpec((1, FOUR_W, FOUR_W), lambda s: (l, 0, 0))],
        out_specs=pl.BlockSpec((rows, FOUR_W), lambda s: (blk_off + s, 0)),
        out_shape=jax.ShapeDtypeStruct((ROWS, FOUR_W), BF16),
        input_output_aliases={0: 0},
        compiler_params=_cparams(("arbitrary",), 40),
        name=f"fourier_{n}",
    )(uf, cs, c64, s64, w_fourier)


S5_TW = S5_T * S5_H
S5_R = 128
S5_TOK = S5_R * S5_T
S5_STEPS_CTX = ROWS_CTX // S5_TOK
S5_STEPS_LAT = (ROWS - ROWS_CTX) // S5_TOK
S5_STEPS = S5_STEPS_CTX + S5_STEPS_LAT
S5_SEQ = N_CTX // S5_STEPS_CTX


def _s5_tables(a_re, a_im, log_dt, b_re, b_im, c_re, c_im):
    f = lambda a: a.astype(F32)
    a_re, a_im, b_re, b_im, c_re, c_im = map(f, (a_re, a_im, b_re, b_im, c_re, c_im))
    dt = jnp.exp(f(log_dt))[..., None]
    x, y = a_re * dt, a_im * dt
    kk = jnp.arange(S5_T + 1, dtype=F32)[None, None, :, None, None]
    mag = jnp.exp(kk * x[:, :, None])
    pr, pi = mag * jnp.cos(kk * y[:, :, None]), mag * jnp.sin(kk * y[:, :, None])
    nr, ni = pr[:, :, 1] - 1.0, pi[:, :, 1]
    den = a_re * a_re + a_im * a_im
    qr, qi = (nr * a_re + ni * a_im) / den, (ni * a_re - nr * a_im) / den
    bb_r = qr[..., None] * b_re - qi[..., None] * b_im
    bb_i = qr[..., None] * b_im + qi[..., None] * b_re
    cat = lambda u, v: jnp.concatenate([u, v], axis=-1)
    prg, pig = pr.transpose(0, 1, 3, 2, 4), pi.transpose(0, 1, 3, 2, 4)
    pows = jnp.stack([cat(prg, prg), cat(-pig, pig), cat(prg, -pig), cat(-pig, -prg), cat(prg, pig), cat(-pig, prg)],
                     axis=3)
    bt_r, bt_i = bb_r.transpose(0, 1, 2, 4, 3), bb_i.transpose(0, 1, 2, 4, 3)
    wrows = jnp.stack([cat(bt_r, bt_i), cat(bt_i, bt_r), cat(bt_r, -bt_i), cat(c_re, c_re), cat(c_im, c_im)],
                      axis=3)
    ar, ai = pr[:, :, S5_T], pi[:, :, S5_T]
    decay = jnp.stack([cat(ar, ar), cat(-ai, ai), cat(ai, -ai), jnp.zeros_like(cat(ar, ar))], axis=3)
    return pows, wrows, decay


def _block_transpose(arrs):
    arrs = list(arrs)
    width = arrs[0].shape[1]
    blk = lax.broadcasted_iota(jnp.int32, arrs[0].shape, 1) >> 4
    s = len(arrs) // 2
    while s >= 1:
        keep = (blk & s) == 0
        for i in range(len(arrs)):
            if i & s:
                continue
            lo, hi = arrs[i], arrs[i + s]
            arrs[i] = jnp.where(keep, lo, pltpu.roll(hi, s * S5_H, 1))
            arrs[i + s] = jnp.where(keep, pltpu.roll(lo, width - s * S5_H, 1), hi)
        s //= 2
    return arrs


def _s5_build_operators(pw_ref, w_ref, e_scr, esw_scr, ft_scr, tp_scr):
    lane = lax.broadcasted_iota(jnp.int32, (S5_H, S5_TW), 1)
    for d in range(2):
        for g in range(S5_G):
            pw = lambda v, k: pw_ref[0, d, g, v, k:k + 1, :]
            b_ri, b_ir, b_conj, c_rr, c_ii = (w_ref[0, d, g, v] for v in range(5))
            lag_rows = []
            for t in range(S5_T):
                blk = slice(t * S5_H, (t + 1) * S5_H)
                ke = S5_T - 1 - t if d == 0 else t
                e_scr[d, g, blk, :] = (b_ri * pw(0, ke) + b_ir * pw(1, ke)).astype(BF16)
                esw_scr[d, g, blk, :] = (b_ir * pw(0, ke) - b_ri * pw(1, ke)).astype(BF16)
                kf = t + 1 if d == 0 else S5_T - t
                ft_scr[d, g, blk, :] = (c_rr * pw(2, kf) + c_ii * pw(3, kf)).astype(BF16)
                kl = t if d == 0 else S5_T - 1 - t
                lag_rows.append(c_rr * pw(4, kl) + c_ii * pw(5, kl))
            mh, ml = _split_bf16(jnp.concatenate(lag_rows, axis=0))
            bh, bl = _split_bf16(b_conj)
            kt = _dot_nt(bh, mh) + _dot_nt(bh, ml) + _dot_nt(bl, mh)
            for ti in range(S5_T):
                if d == 0:
                    sh = S5_H * ti
                    blk_rows = jnp.where(lane >= sh, pltpu.roll(kt, sh, 1) if sh else kt, 0.0)
                else:
                    sh = S5_H * (S5_T - 1 - ti)
                    blk_rows = jnp.where(lane < S5_TW - sh, pltpu.roll(kt, S5_TW - sh, 1) if sh else kt, 0.0)
                tp_scr[d, g, ti * S5_H:(ti + 1) * S5_H, :] = blk_rows.astype(BF16)


def _s5_kernel(us_ref, pw_ref, w_ref, a_ref, s0_ref, y_ref, fin_ref,
               e_scr, esw_scr, ft_scr, tp_scr, x_scr, y_scr, loc_scr, lsw_scr, prev_scr):
    step = pl.program_id(0)

    @pl.when(step == 0)
    def _():
        _s5_build_operators(pw_ref, w_ref, e_scr, esw_scr, ft_scr, tp_scr)

    rows = pl.ds
    x_by_step = [jnp.concatenate([us_ref[0, rows(tl, S5_R, stride=S5_T), :],
                                  us_ref[1, rows(tl, S5_R, stride=S5_T), :]], axis=1) for tl in range(S5_T)]
    for g, xg in enumerate(_block_transpose(x_by_step)):
        x_scr[g] = xg.astype(BF16)
    fin_ref[...] = jnp.zeros(fin_ref.shape, F32)

    def scan(d, nseq, nc):
        batch = 8
        for g0 in range(0, S5_G, batch):
            coefs = [[jnp.broadcast_to(a_ref[0, d, g, r:r + 1, :], (nseq, 2 * S5_P)) for r in range(3)]
                     for g in range(g0, g0 + batch)]
            init = tuple((s0_ref[0, 0, d, g, 0:nseq, :2 * S5_P], s0_ref[0, 0, d, g, 0:nseq, 2 * S5_P:])
                         for g in range(g0, g0 + batch))

            def body(j, carry):
                ci = j if d == 0 else nc - 1 - j
                out = []
                for idx in range(batch):
                    g = g0 + idx
                    s, t = carry[idx]
                    ca, cb, cc = coefs[idx]
                    prev_scr[g, rows(ci, nseq, stride=nc), :] = s
                    out.append((s * ca + t * cb + loc_scr[g, rows(ci, nseq, stride=nc), :],
                                t * ca + s * cc + lsw_scr[g, rows(ci, nseq, stride=nc), :]))
                return tuple(out)

            fin = lax.fori_loop(0, nc, body, init)
            for idx in range(batch):
                fin_ref[0, d, g0 + idx, 0:nseq, :] = fin[idx][0]

    for d in range(2):
        for g in range(S5_G):
            xg = x_scr[g]
            loc_scr[g] = _dot(xg, e_scr[d, g])
            lsw_scr[g] = _dot(xg, esw_scr[d, g])

        @pl.when(step < S5_STEPS_CTX)
        def _():
            scan(d, N_CTX // S5_STEPS_CTX, T_CTX // S5_T)

        @pl.when(step >= S5_STEPS_CTX)
        def _():
            scan(d, N_LAT // S5_STEPS_LAT, T_LAT // S5_T)

        for g in range(S5_G):
            yg = _dot(x_scr[g], tp_scr[d, g]) + _dot_nt(prev_scr[g].astype(BF16), ft_scr[d, g])
            if d == 0:
                y_scr[g] = yg
            else:
                y_scr[g] += yg
    for to, y_to in enumerate(_block_transpose([y_scr[g] for g in range(S5_G)])):
        y_ref[0, rows(to, S5_R, stride=S5_T), :] = y_to[:, :LANES]
        y_ref[1, rows(to, S5_R, stride=S5_T), :] = y_to[:, LANES:]


def _s5(us, tabs, s0, l):
    pows, wrows, decay = tabs
    lay = lambda *shape: pl.BlockSpec((1,) + shape, lambda h: (l,) + (0,) * len(shape))
    op_buf = pltpu.VMEM((2, S5_G, S5_TW, 2 * S5_P), BF16)
    scan_buf = pltpu.VMEM((S5_G, S5_R, 2 * S5_P), F32)
    return pl.pallas_call(
        _s5_kernel,
        grid=(S5_STEPS,),
        in_specs=[_halves_spec(S5_TOK, lambda h: h),
                  lay(2, S5_G, 6, S5_T + 1, 2 * S5_P), lay(2, S5_G, 5, S5_H, 2 * S5_P), lay(2, S5_G, 4, 2 * S5_P),
                  pl.BlockSpec((1, 1, 2, S5_G, S5_SEQ, 4 * S5_P), lambda h: (l, h, 0, 0, 0, 0))],
        out_specs=[_halves_spec(S5_TOK, lambda h: h),
                   pl.BlockSpec((1, 2, S5_G, S5_SEQ, 2 * S5_P), lambda h: (h, 0, 0, 0, 0))],
        out_shape=[_HALVES_SHAPE, jax.ShapeDtypeStruct((S5_STEPS, 2, S5_G, S5_SEQ, 2 * S5_P), F32)],
        scratch_shapes=[op_buf, op_buf, op_buf, pltpu.VMEM((2, S5_G, S5_TW, S5_TW), BF16),
                        pltpu.VMEM((S5_G, S5_R, S5_TW), BF16), pltpu.VMEM((S5_G, S5_R, S5_TW), F32),
                        scan_buf, scan_buf, scan_buf],
        compiler_params=_cparams(("arbitrary",), 56),
        name="s5",
    )(us, pows, wrows, decay, s0)


def _split_bf16(a):
    hi = a.astype(BF16)
    return hi, (a - hi.astype(F32)).astype(BF16)


def _outproj_kernel(at_ref, fo_ref, ys_ref, us_ref, d_ref, wg_ref, wo_ref, x_ref, mod_ref,
                    g_ref, b_ref, wr_ref, x1_ref, h2_ref, lg_ref, wob_ref):
    @pl.when(pl.program_id(0) == 0)
    def _():
        wob_ref[...] = wo_ref[0].astype(BF16)

    ys = jnp.concatenate([ys_ref[0], ys_ref[1]], axis=1)
    us = jnp.concatenate([us_ref[0], us_ref[1]], axis=1)
    g = jax.nn.gelu(ys + us * d_ref[0])
    s5 = g * jax.nn.sigmoid(_dot(g.astype(BF16), wg_ref[0].astype(BF16)))
    mix = (_dot(at_ref[...], wob_ref[0:ATTN_W, :])
           + _dot(fo_ref[...], wob_ref[ATTN_W:ATTN_W + FOUR_W, :])
           + _dot(s5.astype(BF16), wob_ref[ATTN_W + FOUR_W:, :]))
    m = mod_ref[0, 0]
    r = ALPHA * x_ref[...] + m[2:3] * mix
    rc = r - jnp.mean(r, axis=-1, keepdims=True)
    t = rc * lax.rsqrt(jnp.mean(rc * rc, axis=-1, keepdims=True) + LN_EPS)
    gain, bias = g_ref[0], b_ref[0]
    x1_ref[...] = t * gain + bias
    up = 1.0 + m[4:5]
    hb = (t * (gain * up) + (bias * up + m[3:4])).astype(BF16)
    h2_ref[...] = hb
    lg_ref[...] = _dot_nt(wr_ref[0].astype(BF16), hb)


def _outproj(attn, four, ys5, us, s5_d, w_glu, w_out, x, mod, ln_g, ln_b, w_router_t, l):
    row = lambda w: pl.BlockSpec((RB_OUT, w), lambda i: (i, 0))
    halves = _halves_spec(RB_OUT, lambda i: i)
    lay = lambda *shape: pl.BlockSpec((1,) + shape, lambda i: (l,) + (0,) * len(shape))
    return pl.pallas_call(
        _outproj_kernel,
        grid=(ROWS // RB_OUT,),
        in_specs=[row(ATTN_W), row(FOUR_W), halves, halves, lay(1, S5_W), lay(S5_W, S5_W), lay(D, D), row(D),
                  pl.BlockSpec((1, 1, 6, D), lambda i: (l, _mod_row(i, RB_OUT), 0, 0)),
                  lay(1, D), lay(1, D), lay(N_EXP, D)],
        out_specs=[row(D), row(D), pl.BlockSpec((N_EXP, RB_OUT), lambda i: (0, i))],
        out_shape=[jax.ShapeDtypeStruct((ROWS, D), F32), jax.ShapeDtypeStruct((ROWS, D), BF16),
                   jax.ShapeDtypeStruct((N_EXP, ROWS), F32)],
        scratch_shapes=[pltpu.VMEM((D, D), BF16)],
        compiler_params=_cparams(("arbitrary",), 40),
        name="outproj",
    )(attn, four, ys5, us, s5_d, w_glu, w_out, x, mod, ln_g, ln_b, w_router_t)


SEQ_GROUP = LANES // N_EXP


def _router_kernel(lg_ref, crow_ref, ccol_ref, gcol_ref, *, n, cap, n_seq):
    rows = n_seq * N_EXP
    lg = jnp.concatenate([lg_ref[:, s * n:(s + 1) * n] for s in range(n_seq)], axis=0).reshape(n_seq, N_EXP, n)
    e = jnp.exp(lg - jnp.max(lg, axis=1, keepdims=True))
    aff = (e / jnp.sum(e, axis=1, keepdims=True)).reshape(rows, n)
    thr_bits = jnp.zeros((rows, 1), jnp.int32)
    for bit in range(30, -1, -1):
        cand = thr_bits | (1 << bit)
        cnt = jnp.sum(jnp.where(aff >= lax.bitcast_convert_type(cand, F32), 1.0, 0.0), axis=1, keepdims=True)
        thr_bits = jnp.where(cnt >= cap, cand, thr_bits)
    thr = lax.bitcast_convert_type(thr_bits, F32)
    above = aff > thr
    tied = aff == thr
    need = cap - jnp.sum(jnp.where(above, 1.0, 0.0), axis=1, keepdims=True)
    r0 = lax.broadcasted_iota(jnp.int32, (n, n), 0)
    r1 = lax.broadcasted_iota(jnp.int32, (n, n), 1)
    before = jnp.where(r0 < r1, 1.0, 0.0).astype(BF16)
    eye = jnp.where(r0 == r1, 1.0, 0.0).astype(BF16)
    tied_rank = _dot(jnp.where(tied, 1.0, 0.0).astype(BF16), before)
    sel = above | (tied & (tied_rank < need))
    slot = _dot(jnp.where(sel, 1.0, 0.0).astype(BF16), before)
    code = jnp.where(sel, slot + 1.0, 0.0)
    gate = jnp.where(sel, aff, 0.0)
    for s in range(n_seq):
        crow_ref[:, s * n:(s + 1) * n] = code[s * N_EXP:(s + 1) * N_EXP]
    for grp in range(pl.cdiv(n_seq, SEQ_GROUP)):
        lo, hi = grp * LANES, min((grp + 1) * LANES, rows)
        pad = [jnp.zeros((LANES - (hi - lo), n), F32)] if hi - lo < LANES else []
        ccol_ref[grp] = _dot_nt(eye, jnp.concatenate([code[lo:hi]] + pad, axis=0).astype(BF16))
        gcol_ref[grp] = _dot_nt(eye, jnp.concatenate([gate[lo:hi]] + pad, axis=0).astype(BF16))


def _router(lg, n, n_seq, blk, cap):
    groups = pl.cdiv(n_seq, SEQ_GROUP)
    whole = lambda *shape: pl.BlockSpec(shape, lambda i: (0,) * len(shape))
    return pl.pallas_call(
        functools.partial(_router_kernel, n=n, cap=cap, n_seq=n_seq),
        grid=(1,),
        in_specs=[pl.BlockSpec((N_EXP, n_seq * n), lambda i: (0, blk))],
        out_specs=[whole(N_EXP, n_seq * n), whole(groups, n, LANES), whole(groups, n, LANES)],
        out_shape=[jax.ShapeDtypeStruct((N_EXP, n_seq * n), F32),
                   jax.ShapeDtypeStruct((groups, n, LANES), F32),
                   jax.ShapeDtypeStruct((groups, n, LANES), F32)],
        compiler_params=_cparams(("arbitrary",), 48),
        name=f"router_{n}",
    )(lg)


def _gather_kernel(crow_ref, h_ref, o_ref, *, n, cap, n_sub):
    slot1 = (lax.broadcasted_iota(jnp.int32, (cap, n), 0) + 1).astype(F32)
    for j in range(n_sub):
        code = crow_ref[:, j * n:(j + 1) * n]
        onehot = jnp.concatenate(
            [jnp.where(code[e:e + 1, :] == slot1, 1.0, 0.0).astype(BF16) for e in range(N_EXP)], axis=0)
        xs = _dot(onehot, h_ref[j * n:(j + 1) * n, :])
        o_ref[:, j * cap:(j + 1) * cap, :] = xs.reshape(N_EXP, cap, D).astype(BF16)


def _gather(crow, h2, n, n_seq, n_sub, blk_off, cap):
    return pl.pallas_call(
        functools.partial(_gather_kernel, n=n, cap=cap, n_sub=n_sub),
        grid=(n_seq // n_sub,),
        in_specs=[pl.BlockSpec((N_EXP, n_sub * n), lambda s: (0, s)),
                  pl.BlockSpec((n_sub * n, D), lambda s: (blk_off + s, 0))],
        out_specs=pl.BlockSpec((N_EXP, n_sub * cap, D), lambda s: (0, s, 0)),
        out_shape=jax.ShapeDtypeStruct((N_EXP, n_seq * cap, D), BF16),
        compiler_params=_cparams(("arbitrary",), 48),
        name=f"gather_{n}",
    )(crow, h2)


SLOTS = N_CTX * CAP_CTX


def _ffn_kernel(xc_ref, xq_ref, wg_ref, wu_ref, wd_ref, yc_ref, yq_ref, wgb_ref, wub_ref, wdb_ref):
    stream = pl.program_id(1)

    @pl.when(stream == 0)
    def _():
        wgb_ref[...] = wg_ref[0, 0].astype(BF16)
        wub_ref[...] = wu_ref[0, 0].astype(BF16)
        wdb_ref[...] = wd_ref[0, 0].astype(BF16)

    def swiglu(x_ref, y_ref):
        x = x_ref[0]
        g = _dot(x, wgb_ref[...])
        u = _dot(x, wub_ref[...])
        hid = (g * jax.nn.sigmoid(g) * u).astype(BF16)
        y_ref[0] = _dot(hid, wdb_ref[...]).astype(BF16)

    @pl.when(stream == 0)
    def _():
        swiglu(xc_ref, yc_ref)

    @pl.when(stream == 1)
    def _():
        swiglu(xq_ref, yq_ref)


def _ffn(xs_c, xs_q, w_gate, w_up, w_down, l):
    xspec = pl.BlockSpec((1, SLOTS, D), lambda e, s: (e, 0, 0))
    wspec = lambda rows, cols: pl.BlockSpec((1, 1, rows, cols), lambda e, s: (l, e, 0, 0))
    return pl.pallas_call(
        _ffn_kernel,
        grid=(N_EXP, 2),
        in_specs=[xspec, xspec, wspec(D, FF), wspec(D, FF), wspec(FF, D)],
        out_specs=[xspec, xspec],
        out_shape=[jax.ShapeDtypeStruct((N_EXP, SLOTS, D), BF16)] * 2,
        scratch_shapes=[pltpu.VMEM((D, FF), BF16), pltpu.VMEM((D, FF), BF16), pltpu.VMEM((FF, D), BF16)],
        compiler_params=_cparams(("arbitrary", "arbitrary"), 56),
        name="ffn",
    )(xs_c, xs_q, w_gate, w_up, w_down)


def _combine_kernel(ccol_ref, gcol_ref, ys_ref, x_ref, mod_ref, g_ref, b_ref, o_ref, *, cap, rb, n_sub):
    width = N_EXP * cap
    k = lax.broadcasted_iota(jnp.int32, (LANES, width), 0)
    col = lax.broadcasted_iota(jnp.int32, (LANES, width), 1)
    shift = cap.bit_length() - 1
    slot1 = ((lax.broadcasted_iota(jnp.int32, (1, width), 1) & (cap - 1)) + 1).astype(F32)
    code_t, gate_t = ccol_ref[0].astype(BF16), gcol_ref[0].astype(BF16)
    m = mod_ref[0, 0]
    for j in range(n_sub):
        seq = pl.program_id(0) * n_sub + j
        first = (seq % SEQ_GROUP) * N_EXP
        spread = jnp.where((col >> shift) + first == k, 1.0, 0.0).astype(BF16)
        code = _dot(code_t, spread)
        gate = _dot(gate_t, spread)
        weights = jnp.where(code == slot1, gate, 0.0).astype(BF16)
        moe = _dot(weights, ys_ref[:, j * cap:(j + 1) * cap, :].reshape(width, D))
        rows = slice(j * rb, (j + 1) * rb)
        o_ref[rows, :] = _layer_norm(ALPHA * x_ref[rows, :] + m[5:6] * moe, g_ref[0], b_ref[0])


def _combine(ccol, gcol, ys, x1, mod, ln_g, ln_b, l, cap, n_seq, n_sub, rb, rb_per_seq, rb_off, in_place):
    assert n_sub == 1 or rb_per_seq == 1
    rows = n_sub * rb
    lay = lambda *shape: pl.BlockSpec((1,) + shape, lambda s, r: (l,) + (0,) * len(shape))
    table = pl.BlockSpec((1, rb, LANES), lambda s, r: (s * n_sub // SEQ_GROUP, r, 0))
    block = lambda s, r: rb_off + s * rb_per_seq + r
    out_off = rb_off if in_place else 0
    return pl.pallas_call(
        functools.partial(_combine_kernel, cap=cap, rb=rb, n_sub=n_sub),
        grid=(n_seq // n_sub, rb_per_seq),
        in_specs=[table, table,
                  pl.BlockSpec((N_EXP, n_sub * cap, D), lambda s, r: (0, s, 0)),
                  pl.BlockSpec((rows, D), lambda s, r: (block(s, r), 0)),
                  pl.BlockSpec((1, 1, 6, D), lambda s, r: (l, _mod_row(block(s, r), rows), 0, 0)),
                  lay(1, D), lay(1, D)],
        out_specs=pl.BlockSpec((rows, D), lambda s, r: (block(s, r) - rb_off + out_off, 0)),
        out_shape=jax.ShapeDtypeStruct((ROWS if in_place else n_seq * rb_per_seq * rb, D), F32),
        input_output_aliases={3: 0} if in_place else {},
        compiler_params=_cparams(("arbitrary", "arbitrary"), 48),
        name=f"combine_{cap}",
    )(ccol, gcol, ys, x1, mod, ln_g, ln_b)


def kernel(x_prompt, x_sample, cache_k, cache_v, state_s5_re, state_s5_im, c, c_ctx, ln_in_g, ln_in_b, w_ada, b_ada,
           w_in, w_fourier, attn_sink, s5_a_re, s5_a_im, s5_log_dt, s5_b_re, s5_b_im, s5_c_re, s5_c_im, s5_d,
           s5_w_glu, w_out, ln1_g, ln1_b, w_router, w_gate, w_up, w_down, ln2_g, ln2_b):
    x = _ln_in(x_prompt.reshape(ROWS_CTX, D), x_sample.reshape(ROWS - ROWS_CTX, D), ln_in_g, ln_in_b)
    cond8 = jnp.concatenate([c_ctx[None], c, jnp.zeros((8 - 1 - N_LAT, D), F32)], axis=0)
    mod = _ada(cond8, w_ada, b_ada).reshape(DEPTH, 8, 6, D)

    rope_tabs = _rope_tables()
    cs_ctx, cs_lat = _dft_tables(T_CTX), _dft_tables(T_LAT)
    c64, s64 = _dft_channel_tables()
    cache_k = cache_k.reshape(N_LAT, DEPTH, PAST, KV_W)
    cache_v = cache_v.reshape(N_LAT, DEPTH, PAST, KV_W)
    w_router_t = jnp.swapaxes(w_router, 1, 2)
    sink_b = jnp.broadcast_to(attn_sink[:, :, None], (DEPTH, N_HEADS, LANES))
    s5_tabs = _s5_tables(s5_a_re, s5_a_im, s5_log_dt, s5_b_re, s5_b_im, s5_c_re, s5_c_im)
    lat_per_step = N_LAT // S5_STEPS_LAT
    s0 = jnp.concatenate([state_s5_re, state_s5_im, state_s5_im, state_s5_re], axis=-1)
    s0 = s0.reshape(S5_STEPS_LAT, lat_per_step, DEPTH, 2, S5_G, 4 * S5_P).transpose(2, 0, 3, 4, 1, 5)
    s0 = jnp.pad(s0, ((0, 0), (S5_STEPS_CTX, 0), (0, 0), (0, 0), (0, S5_SEQ - lat_per_step), (0, 0)))
    ln1_g, ln1_b, ln2_g, ln2_b, s5_d = (a.reshape(DEPTH, 1, -1) for a in (ln1_g, ln1_b, ln2_g, ln2_b, s5_d))

    new_k, new_v, new_s = [], [], []
    for l in range(DEPTH):
        q, k, v, uf, us, k_t, v_t = _inproj(x, mod, w_in, rope_tabs, l)
        new_k.append(k_t)
        new_v.append(v_t)
        attn = _attn_lat(_attn_ctx(q, k, v, sink_b[l]), k, v, cache_k, cache_v, sink_b[l], l)
        four = _fourier(uf, cs_ctx, c64, s64, w_fourier, l, T_CTX, N_CTX, SEQ_PER_STEP_CTX, 0)
        four = _fourier(four, cs_lat, c64, s64, w_fourier, l, T_LAT, N_LAT, 1, ROWS_CTX // T_LAT)
        y5, fin = _s5(us, s5_tabs, s0, l)
        new_s.append(fin[:S5_STEPS_CTX].transpose(0, 3, 1, 2, 4).reshape(N_CTX, 2, S5_G, 2 * S5_P))
        x1, h2, lg = _outproj(attn, four, y5, us, s5_d, s5_w_glu, w_out, x, mod, ln1_g, ln1_b, w_router_t, l)
        crow_c, ccol_c, gcol_c = _router(lg, T_CTX, N_CTX, 0, CAP_CTX)
        crow_q, ccol_q, gcol_q = _router(lg, T_LAT, N_LAT, 1, CAP_LAT)
        xs_c = _gather(crow_c, h2, T_CTX, N_CTX, SEQ_PER_STEP_CTX, 0, CAP_CTX)
        xs_q = _gather(crow_q, h2, T_LAT, N_LAT, 1, ROWS_CTX // T_LAT, CAP_LAT)
        ys_c, ys_q = _ffn(xs_c, xs_q, w_gate, w_up, w_down, l)
        args_c = (ccol_c, gcol_c, ys_c), (mod, ln2_g, ln2_b, l, CAP_CTX, N_CTX, SEQ_PER_STEP_CTX, T_CTX, 1, 0)
        args_q = (ccol_q, gcol_q, ys_q), (mod, ln2_g, ln2_b, l, CAP_LAT, N_LAT, 1, RB_COMBINE_LAT,
                                          T_LAT // RB_COMBINE_LAT, ROWS_CTX // RB_COMBINE_LAT)
        if l < DEPTH - 1:
            x = _combine(*args_c[0], x1, *args_c[1], in_place=True)
            x = _combine(*args_q[0], x, *args_q[1], in_place=True)
        else:
            y_ctx = _combine(*args_c[0], x1, *args_c[1], in_place=False)
            y_lat = _combine(*args_q[0], x1, *args_q[1], in_place=False)

    new_s = jnp.stack(new_s, axis=1)

    def cache(feat_major):
        return jnp.stack(feat_major).reshape(DEPTH, N_KV, HD, N_CTX, T_CTX).transpose(3, 0, 4, 1, 2)

    return (y_ctx.reshape(N_CTX, T_CTX, D), y_lat.reshape(N_LAT, T_LAT, D),
            cache(new_k), cache(new_v), new_s[..., :S5_P], new_s[..., S5_P:])
```

```python
import functools

import jax
import jax.numpy as jnp
import numpy as np
from jax import lax
from jax.experimental import pallas as pl
from jax.experimental.pallas import tpu as pltpu

F32 = jnp.float32
BF16 = jnp.bfloat16

D = 1024
N_CTX, T_CTX = 16, 256
N_LAT, T_LAT = 4, 1024
ROWS_CTX = N_CTX * T_CTX
ROWS = ROWS_CTX + N_LAT * T_LAT
DEPTH = 4
PAST = 512
GRID_W = 64
N_HEADS, N_KV, HD = 8, 2, 64
GQA = N_HEADS // N_KV
ATTN_W, KV_W = N_HEADS * HD, N_KV * HD
LOG2E = 1.4426950408889634
Q_SCALE = HD ** -0.5 * LOG2E
WINDOW = 128
FOUR_H, FOUR_W = 4, 256
S5_G, S5_H, S5_P, S5_W = 16, 16, 64, 256
S5_T = 16
IN_W = ATTN_W + 2 * KV_W + FOUR_W + S5_W
N_EXP, FF = 16, 1024
CAP_CTX, CAP_LAT = 2 * T_CTX // N_EXP, 2 * T_LAT // N_EXP
LN_EPS = 1e-5
NEG_INF = -1e30
ALPHA = (2 * DEPTH) ** 0.25
ROPE_BASE = 10000.0
RB_IN = 1024
RB_OUT = 512
RB_COMBINE_LAT = 512
LANES = 128
MIB = 2 ** 20
HIGHEST = lax.Precision.HIGHEST


def _cparams(sem, vmem_mib):
    return pltpu.CompilerParams(dimension_semantics=sem, vmem_limit_bytes=vmem_mib * MIB)


def _dot(a, b):
    return jnp.dot(a, b, preferred_element_type=F32)


def _dot_nt(a, b):
    return lax.dot_general(a, b, (((1,), (1,)), ((), ())), preferred_element_type=F32)


def _dot_tn(a, b):
    return lax.dot_general(a, b, (((0,), (0,)), ((), ())), preferred_element_type=F32)


def _layer_norm(x, g, b):
    mu = jnp.mean(x, axis=-1, keepdims=True)
    xc = x - mu
    var = jnp.mean(xc * xc, axis=-1, keepdims=True)
    return xc * lax.rsqrt(var + LN_EPS) * g + b


def _mod_row(block, block_rows):
    first = block * block_rows
    return jnp.where(first < ROWS_CTX, 0, 1 + (first - ROWS_CTX) // T_LAT)


_HALVES_SHAPE = jax.ShapeDtypeStruct((2, ROWS, LANES), F32)


def _halves_spec(rows, row_block):
    return pl.BlockSpec((2, rows, LANES), lambda *ids: (0, row_block(*ids), 0))


def _two_stream_specs(rows, width):
    n_ctx = ROWS_CTX // rows
    return (pl.BlockSpec((rows, width), lambda i: (jnp.minimum(i, n_ctx - 1), 0)),
            pl.BlockSpec((rows, width), lambda i: (jnp.maximum(i - n_ctx, 0), 0)))


def _pick(rows, ctx_ref, lat_ref):
    return lax.cond(pl.program_id(0) < ROWS_CTX // rows, lambda: ctx_ref[...], lambda: lat_ref[...])


def _ln_in_kernel(xc_ref, xq_ref, g_ref, b_ref, o_ref):
    o_ref[...] = _layer_norm(_pick(RB_OUT, xc_ref, xq_ref), g_ref[...], b_ref[...])


def _ln_in(x_ctx, x_lat, g, b):
    return pl.pallas_call(
        _ln_in_kernel,
        grid=(ROWS // RB_OUT,),
        in_specs=[*_two_stream_specs(RB_OUT, D),
                  pl.BlockSpec((1, D), lambda i: (0, 0)),
                  pl.BlockSpec((1, D), lambda i: (0, 0))],
        out_specs=pl.BlockSpec((RB_OUT, D), lambda i: (i, 0)),
        out_shape=jax.ShapeDtypeStruct((ROWS, D), F32),
        compiler_params=_cparams(("arbitrary",), 32),
        name="ln_in",
    )(x_ctx, x_lat, g.reshape(1, D), b.reshape(1, D))


def _ada_kernel(c_ref, w_ref, b_ref, o_ref):
    c = c_ref[...]
    s = (c * jax.nn.sigmoid(c)).astype(BF16)
    o_ref[0] = _dot(s, w_ref[0].astype(BF16)) + b_ref[0]


def _ada(cond8, w_ada, b_ada):
    tn = 1536
    return pl.pallas_call(
        _ada_kernel,
        grid=(DEPTH, 6 * D // tn),
        in_specs=[pl.BlockSpec((8, D), lambda l, j: (0, 0)),
                  pl.BlockSpec((1, D, tn), lambda l, j: (l, 0, j)),
                  pl.BlockSpec((1, 1, tn), lambda l, j: (l, 0, j))],
        out_specs=pl.BlockSpec((1, 8, tn), lambda l, j: (l, 0, j)),
        out_shape=jax.ShapeDtypeStruct((DEPTH, 8, 6 * D), F32),
        compiler_params=_cparams(("arbitrary", "arbitrary"), 40),
        name="adaln",
    )(cond8, w_ada, b_ada.reshape(DEPTH, 1, 6 * D))


def _inproj_kernel(x_ref, mod_ref, w_ref, cos_ref, sin_ref, q_ref, k_ref, v_ref, uf_ref, us_ref, kt_ref, vt_ref,
                   wb_ref, wswb_ref):
    step = pl.program_id(0)
    n_ctx_steps = ROWS_CTX // RB_IN

    @pl.when(step == 0)
    def _():
        wb = w_ref[0].astype(BF16)
        wb_ref[...] = wb
        n_qk = ATTN_W + KV_W
        src = lax.broadcasted_iota(jnp.int32, (n_qk, n_qk), 0)
        dst = lax.broadcasted_iota(jnp.int32, (n_qk, n_qk), 1)
        swap = jnp.where(src == (dst ^ 1), 1.0, 0.0).astype(BF16)
        wswb_ref[...] = _dot(wb[:, :n_qk], swap).astype(BF16)

    m = mod_ref[0, 0]
    hb = (x_ref[...] * (1.0 + m[1:2]) + m[0:1]).astype(BF16)
    p = _dot(hb, wb_ref[...])
    q = p[:, :ATTN_W]
    k = p[:, ATTN_W:ATTN_W + KV_W]
    v = p[:, ATTN_W + KV_W:ATTN_W + 2 * KV_W]

    psw = _dot(hb, wswb_ref[...])
    cos, sin = cos_ref[...], sin_ref[...]
    wide = lambda t: jnp.concatenate([t] * (ATTN_W // KV_W), axis=1)
    latent = step >= n_ctx_steps
    q = jnp.where(latent, q * wide(cos) + psw[:, :ATTN_W] * wide(sin), q)
    k = jnp.where(latent, k * cos + psw[:, ATTN_W:] * sin, k)
    q_ref[...] = (q * Q_SCALE).astype(BF16)
    k_ref[...] = k
    v_ref[...] = v

    @pl.when(step < n_ctx_steps)
    def _():
        kt_ref[...] = k.T
        vt_ref[...] = v.T

    uf_ref[...] = p[:, ATTN_W + 2 * KV_W:ATTN_W + 2 * KV_W + FOUR_W].astype(BF16)
    for half in range(S5_W // LANES):
        lo = ATTN_W + 2 * KV_W + FOUR_W + half * LANES
        us_ref[half] = p[:, lo:lo + LANES]


def _inproj(x, mod, w_in, rope_tabs, l):
    assert RB_IN == T_LAT
    outs = ((ATTN_W, BF16), (KV_W, F32), (KV_W, F32), (FOUR_W, BF16))
    table = pl.BlockSpec((T_LAT, KV_W), lambda i: (0, 0))
    feat_major = pl.BlockSpec((KV_W, RB_IN), lambda i: (0, jnp.minimum(i, ROWS_CTX // RB_IN - 1)))
    return pl.pallas_call(
        _inproj_kernel,
        grid=(ROWS // RB_IN,),
        in_specs=[pl.BlockSpec((RB_IN, D), lambda i: (i, 0)),
                  pl.BlockSpec((1, 1, 6, D), lambda i: (l, _mod_row(i, RB_IN), 0, 0)),
                  pl.BlockSpec((1, D, IN_W), lambda i: (l, 0, 0)),
                  table, table],
        out_specs=([pl.BlockSpec((RB_IN, w), lambda i: (i, 0)) for w, _ in outs] + [_halves_spec(RB_IN, lambda i: i)]
                   + [feat_major, feat_major]),
        out_shape=([jax.ShapeDtypeStruct((ROWS, w), dt) for w, dt in outs] + [_HALVES_SHAPE]
                   + [jax.ShapeDtypeStruct((KV_W, ROWS_CTX), F32)] * 2),
        scratch_shapes=[pltpu.VMEM((D, IN_W), BF16), pltpu.VMEM((D, ATTN_W + KV_W), BF16)],
        compiler_params=_cparams(("arbitrary",), 56),
        name="inproj",
    )(x, mod, w_in, *rope_tabs)


SINK_ROWS = 16


def _softmax_av(score_blocks, sink, value_blocks):
    m = sink
    for s in score_blocks:
        m = jnp.maximum(m, jnp.max(s, axis=0, keepdims=True))
    first = lax.broadcasted_iota(jnp.int32, (SINK_ROWS, 1), 0) == 0
    probs = [jnp.exp2(s - m).astype(BF16) for s in score_blocks]
    probs.append(jnp.where(first, jnp.exp2(sink - m), 0.0).astype(BF16))
    ext = []
    for v in value_blocks:
        ones = jnp.ones(v.shape, BF16)
        ext.append(jnp.concatenate([v, v, ones, ones], axis=1))
    row = lax.broadcasted_iota(jnp.int32, (SINK_ROWS, 4 * HD), 0)
    lane = lax.broadcasted_iota(jnp.int32, (SINK_ROWS, 4 * HD), 1)
    ext.append(jnp.where((row == 0) & (lane >= 2 * HD), 1.0, 0.0).astype(BF16))
    acc = _dot_tn(jnp.concatenate(probs, axis=0), jnp.concatenate(ext, axis=0))
    return acc[:, :2 * HD] / acc[:, 2 * HD:]


def _stack_group(q, kv):
    return jnp.concatenate([q[:, (kv * GQA + g) * HD:(kv * GQA + g + 1) * HD] for g in range(GQA)], axis=0)


def _sink_row(sink_ref, kv, rows):
    return LOG2E * jnp.concatenate(
        [jnp.broadcast_to(sink_ref[kv * GQA + g:kv * GQA + g + 1, 0:1], (1, rows)) for g in range(GQA)], axis=1)


def _unstack_group(o, rows):
    lane = lax.broadcasted_iota(jnp.int32, (rows, 2 * HD), 1)
    return [jnp.where(lane < HD, o[2 * j * rows:(2 * j + 1) * rows], o[(2 * j + 1) * rows:(2 * j + 2) * rows])
            for j in range(GQA // 2)]


def _attn_ctx_kernel(q_ref, k_ref, v_ref, sink_ref, o_ref):
    q = q_ref[...]
    k = k_ref[...].astype(BF16)
    v = v_ref[...].astype(BF16)
    outs = []
    for kv in range(N_KV):
        head = slice(kv * HD, (kv + 1) * HD)
        s = _dot_nt(k[:, head], _stack_group(q, kv))
        o = _softmax_av([s], _sink_row(sink_ref, kv, T_CTX), [v[:, head]])
        outs += _unstack_group(o, T_CTX)
    o_ref[...] = jnp.concatenate(outs, axis=1).astype(BF16)


def _attn_ctx(q, k, v, sink_b):
    return pl.pallas_call(
        _attn_ctx_kernel,
        grid=(N_CTX,),
        in_specs=[pl.BlockSpec((T_CTX, ATTN_W), lambda s: (s, 0)),
                  pl.BlockSpec((T_CTX, KV_W), lambda s: (s, 0)),
                  pl.BlockSpec((T_CTX, KV_W), lambda s: (s, 0)),
                  pl.BlockSpec((N_HEADS, LANES), lambda s: (0, 0))],
        out_specs=pl.BlockSpec((T_CTX, ATTN_W), lambda s: (s, 0)),
        out_shape=jax.ShapeDtypeStruct((ROWS, ATTN_W), BF16),
        input_output_aliases={0: 0},
        compiler_params=_cparams(("arbitrary",), 32),
        name="attn_ctx",
    )(q, k, v, sink_b)


def _attn_lat_kernel(q_ref, k_ref, v_ref, ck_ref, cv_ref, sink_ref, o_ref):
    i = pl.program_id(1)
    n_blk = T_LAT // WINDOW
    q = q_ref[...]

    def kv_block(j):
        st = pl.multiple_of(j * WINDOW, WINDOW)
        return k_ref[pl.ds(st, WINDOW), :].astype(BF16), v_ref[pl.ds(st, WINDOW), :].astype(BF16)

    k0, v0 = kv_block(jnp.maximum(i - 1, 0))
    k1, v1 = kv_block(i)
    k2, v2 = kv_block(jnp.minimum(i + 1, n_blk - 1))
    k_win = jnp.concatenate([k0, k1, k2], axis=0)
    k_ctx = ck_ref[0, 0].astype(BF16)
    v_ctx = cv_ref[0, 0].astype(BF16)
    rows = GQA * WINDOW
    c = lax.broadcasted_iota(jnp.int32, (WINDOW, rows), 0)
    r = lax.broadcasted_iota(jnp.int32, (WINDOW, rows), 1) & (WINDOW - 1)
    far = 2 * WINDOW
    prev_ok = c >= r + jnp.where(i > 0, 0, far)
    next_ok = c + jnp.where(i < n_blk - 1, 0, far) <= r
    outs = []
    for kv in range(N_KV):
        head = slice(kv * HD, (kv + 1) * HD)
        qs = _stack_group(q, kv)
        s_win = _dot_nt(k_win[:, head], qs)
        scores = [jnp.where(prev_ok, s_win[:WINDOW], NEG_INF), s_win[WINDOW:2 * WINDOW],
                  jnp.where(next_ok, s_win[2 * WINDOW:], NEG_INF), _dot_nt(k_ctx[:, head], qs)]
        values = [v0[:, head], v1[:, head], v2[:, head], v_ctx[:, head]]
        outs += _unstack_group(_softmax_av(scores, _sink_row(sink_ref, kv, WINDOW), values), WINDOW)
    o_ref[...] = jnp.concatenate(outs, axis=1).astype(BF16)


def _attn_lat(q, k, v, cache_k, cache_v, sink_b, l):
    n_blk = T_LAT // WINDOW
    off = ROWS_CTX // WINDOW
    return pl.pallas_call(
        _attn_lat_kernel,
        grid=(N_LAT, n_blk),
        in_specs=[pl.BlockSpec((WINDOW, ATTN_W), lambda b, i: (off + b * n_blk + i, 0)),
                  pl.BlockSpec((T_LAT, KV_W), lambda b, i: (ROWS_CTX // T_LAT + b, 0)),
                  pl.BlockSpec((T_LAT, KV_W), lambda b, i: (ROWS_CTX // T_LAT + b, 0)),
                  pl.BlockSpec((1, 1, PAST, KV_W), lambda b, i: (b, l, 0, 0)),
                  pl.BlockSpec((1, 1, PAST, KV_W), lambda b, i: (b, l, 0, 0)),
                  pl.BlockSpec((N_HEADS, LANES), lambda b, i: (0, 0))],
        out_specs=pl.BlockSpec((WINDOW, ATTN_W), lambda b, i: (off + b * n_blk + i, 0)),
        out_shape=jax.ShapeDtypeStruct((ROWS, ATTN_W), BF16),
        input_output_aliases={0: 0},
        compiler_params=_cparams(("arbitrary", "arbitrary"), 40),
        name="attn_lat",
    )(q, k, v, cache_k, cache_v, sink_b)


def _rope_tables():
    rows = T_LAT // GRID_W
    row = jnp.repeat(jnp.arange(rows, dtype=F32), GRID_W)
    col = jnp.tile(jnp.arange(GRID_W, dtype=F32), rows)
    n_freq = HD // 4
    freqs = ROPE_BASE ** (-jnp.arange(n_freq, dtype=F32) / n_freq)
    ang = jnp.concatenate([row[:, None] * freqs, col[:, None] * freqs], axis=-1)
    cos = jnp.repeat(jnp.cos(ang), 2, axis=-1)
    sign = jnp.tile(jnp.array([-1.0, 1.0], F32), HD // 2)
    sin = jnp.repeat(jnp.sin(ang), 2, axis=-1) * sign
    return jnp.tile(cos, (1, N_KV)), jnp.tile(sin, (1, N_KV))


SEQ_PER_STEP_CTX = 4


def _four_kernel(u_ref, cs_ref, c64_ref, s64_ref, wf_ref, o_ref, *, n, n_sub):
    wf = wf_ref[0].astype(BF16)
    for j in range(n_sub):
        rows = slice(j * n, (j + 1) * n)
        ub = u_ref[rows, :]
        uc = _dot(ub, c64_ref[...]).astype(BF16)
        us = _dot(ub, s64_ref[...]).astype(BF16)
        f = _dot(cs_ref[...], jnp.concatenate([uc, us], axis=0))
        o_ref[rows, :] = _dot(f.astype(BF16), wf).astype(BF16)


def _dft_tables(n):
    j = np.arange(n)
    ang = 2.0 * np.pi * ((j[:, None] * j[None, :]) % n) / n
    cs = np.concatenate([np.cos(ang), -np.sin(ang)], axis=1) / np.sqrt(n)
    return jnp.asarray(cs.astype(np.float32)).astype(BF16)


def _dft_channel_tables():
    j = np.arange(HD)
    ang = 2.0 * np.pi * ((j[:, None] * j[None, :]) % HD) / HD
    eye = np.eye(FOUR_H)
    c = np.kron(eye, np.cos(ang)) / np.sqrt(HD)
    s = np.kron(eye, np.sin(ang)) / np.sqrt(HD)
    return jnp.asarray(c.astype(np.float32)).astype(BF16), jnp.asarray(s.astype(np.float32)).astype(BF16)


def _fourier(uf, cs, c64, s64, w_fourier, l, n, n_seq, n_sub, blk_off):
    rows = n_sub * n
    return pl.pallas_call(
        functools.partial(_four_kernel, n=n, n_sub=n_sub),
        grid=(n_seq // n_sub,),
        in_specs=[pl.BlockSpec((rows, FOUR_W), lambda s: (blk_off + s, 0)),
                  pl.BlockSpec((n, 2 * n), lambda s: (0, 0)),
                  pl.BlockSpec((FOUR_W, FOUR_W), lambda s: (0, 0)),
                  pl.BlockSpec((FOUR_W, FOUR_W), lambda s: (0, 0)),
                  pl.BlockSpec((1, FOUR_W, FOUR_W), lambda s: (l, 0, 0))],
        out_specs=pl.BlockSpec((rows, FOUR_W), lambda s: (blk_off + s, 0)),
        out_shape=jax.ShapeDtypeStruct((ROWS, FOUR_W), BF16),
        input_output_aliases={0: 0},
        compiler_params=_cparams(("arbitrary",), 40),
        name=f"fourier_{n}",
    )(uf, cs, c64, s64, w_fourier)


S5_TW = S5_T * S5_H
S5_R = 128
S5_TOK = S5_R * S5_T
S5_STEPS_CTX = ROWS_CTX // S5_TOK
S5_STEPS_LAT = (ROWS - ROWS_CTX) // S5_TOK
S5_STEPS = S5_STEPS_CTX + S5_STEPS_LAT
S5_SEQ = N_CTX // S5_STEPS_CTX


def _s5_tables(a_re, a_im, log_dt, b_re, b_im, c_re, c_im):
    f = lambda a: a.astype(F32)
    a_re, a_im, b_re, b_im, c_re, c_im = map(f, (a_re, a_im, b_re, b_im, c_re, c_im))
    dt = jnp.exp(f(log_dt))[..., None]
    x, y = a_re * dt, a_im * dt
    kk = jnp.arange(S5_T + 1, dtype=F32)[None, None, :, None, None]
    mag = jnp.exp(kk * x[:, :, None])
    pr, pi = mag * jnp.cos(kk * y[:, :, None]), mag * jnp.sin(kk * y[:, :, None])
    nr, ni = pr[:, :, 1] - 1.0, pi[:, :, 1]
    den = a_re * a_re + a_im * a_im
    qr, qi = (nr * a_re + ni * a_im) / den, (ni * a_re - nr * a_im) / den
    bb_r = qr[..., None] * b_re - qi[..., None] * b_im
    bb_i = qr[..., None] * b_im + qi[..., None] * b_re
    cat = lambda u, v: jnp.concatenate([u, v], axis=-1)
    prg, pig = pr.transpose(0, 1, 3, 2, 4), pi.transpose(0, 1, 3, 2, 4)
    pows = jnp.stack([cat(prg, prg), cat(-pig, pig), cat(prg, -pig), cat(-pig, -prg), cat(prg, pig), cat(-pig, prg)],
                     axis=3)
    bt_r, bt_i = bb_r.transpose(0, 1, 2, 4, 3), bb_i.transpose(0, 1, 2, 4, 3)
    wrows = jnp.stack([cat(bt_r, bt_i), cat(bt_i, bt_r), cat(bt_r, -bt_i), cat(c_re, c_re), cat(c_im, c_im)],
                      axis=3)
    ar, ai = pr[:, :, S5_T], pi[:, :, S5_T]
    decay = jnp.stack([cat(ar, ar), cat(-ai, ai), cat(ai, -ai), jnp.zeros_like(cat(ar, ar))], axis=3)
    return pows, wrows, decay


def _block_transpose(arrs):
    arrs = list(arrs)
    width = arrs[0].shape[1]
    blk = lax.broadcasted_iota(jnp.int32, arrs[0].shape, 1) >> 4
    s = len(arrs) // 2
    while s >= 1:
        keep = (blk & s) == 0
        for i in range(len(arrs)):
            if i & s:
                continue
            lo, hi = arrs[i], arrs[i + s]
            arrs[i] = jnp.where(keep, lo, pltpu.roll(hi, s * S5_H, 1))
            arrs[i + s] = jnp.where(keep, pltpu.roll(lo, width - s * S5_H, 1), hi)
        s //= 2
    return arrs


def _s5_build_operators(pw_ref, w_ref, e_scr, esw_scr, ft_scr, tp_scr):
    lane = lax.broadcasted_iota(jnp.int32, (S5_H, S5_TW), 1)
    for d in range(2):
        for g in range(S5_G):
            pw = lambda v, k: pw_ref[0, d, g, v, k:k + 1, :]
            b_ri, b_ir, b_conj, c_rr, c_ii = (w_ref[0, d, g, v] for v in range(5))
            lag_rows = []
            for t in range(S5_T):
                blk = slice(t * S5_H, (t + 1) * S5_H)
                ke = S5_T - 1 - t if d == 0 else t
                e_scr[d, g, blk, :] = (b_ri * pw(0, ke) + b_ir * pw(1, ke)).astype(BF16)
                esw_scr[d, g, blk, :] = (b_ir * pw(0, ke) - b_ri * pw(1, ke)).astype(BF16)
                kf = t + 1 if d == 0 else S5_T - t
                ft_scr[d, g, blk, :] = (c_rr * pw(2, kf) + c_ii * pw(3, kf)).astype(BF16)
                kl = t if d == 0 else S5_T - 1 - t
                lag_rows.append(c_rr * pw(4, kl) + c_ii * pw(5, kl))
            mh, ml = _split_bf16(jnp.concatenate(lag_rows, axis=0))
            bh, bl = _split_bf16(b_conj)
            kt = _dot_nt(bh, mh) + _dot_nt(bh, ml) + _dot_nt(bl, mh)
            for ti in range(S5_T):
                if d == 0:
                    sh = S5_H * ti
                    blk_rows = jnp.where(lane >= sh, pltpu.roll(kt, sh, 1) if sh else kt, 0.0)
                else:
                    sh = S5_H * (S5_T - 1 - ti)
                    blk_rows = jnp.where(lane < S5_TW - sh, pltpu.roll(kt, S5_TW - sh, 1) if sh else kt, 0.0)
                tp_scr[d, g, ti * S5_H:(ti + 1) * S5_H, :] = blk_rows.astype(BF16)


def _s5_kernel(us_ref, pw_ref, w_ref, a_ref, s0_ref, y_ref, fin_ref,
               e_scr, esw_scr, ft_scr, tp_scr, x_scr, y_scr, loc_scr, lsw_scr, prev_scr):
    step = pl.program_id(0)

    @pl.when(step == 0)
    def _():
        _s5_build_operators(pw_ref, w_ref, e_scr, esw_scr, ft_scr, tp_scr)

    rows = pl.ds
    x_by_step = [jnp.concatenate([us_ref[0, rows(tl, S5_R, stride=S5_T), :],
                                  us_ref[1, rows(tl, S5_R, stride=S5_T), :]], axis=1) for tl in range(S5_T)]
    for g, xg in enumerate(_block_transpose(x_by_step)):
        x_scr[g] = xg.astype(BF16)
    fin_ref[...] = jnp.zeros(fin_ref.shape, F32)

    def scan(d, nseq, nc):
        batch = 8
        for g0 in range(0, S5_G, batch):
            coefs = [[jnp.broadcast_to(a_ref[0, d, g, r:r + 1, :], (nseq, 2 * S5_P)) for r in range(3)]
                     for g in range(g0, g0 + batch)]
            init = tuple((s0_ref[0, 0, d, g, 0:nseq, :2 * S5_P], s0_ref[0, 0, d, g, 0:nseq, 2 * S5_P:])
                         for g in range(g0, g0 + batch))

            def body(j, carry):
                ci = j if d == 0 else nc - 1 - j
                out = []
                for idx in range(batch):
                    g = g0 + idx
                    s, t = carry[idx]
                    ca, cb, cc = coefs[idx]
                    prev_scr[g, rows(ci, nseq, stride=nc), :] = s
                    out.append((s * ca + t * cb + loc_scr[g, rows(ci, nseq, stride=nc), :],
                                t * ca + s * cc + lsw_scr[g, rows(ci, nseq, stride=nc), :]))
                return tuple(out)

            fin = lax.fori_loop(0, nc, body, init)
            for idx in range(batch):
                fin_ref[0, d, g0 + idx, 0:nseq, :] = fin[idx][0]

    for d in range(2):
        for g in range(S5_G):
            xg = x_scr[g]
            loc_scr[g] = _dot(xg, e_scr[d, g])
            lsw_scr[g] = _dot(xg, esw_scr[d, g])

        @pl.when(step < S5_STEPS_CTX)
        def _():
            scan(d, N_CTX // S5_STEPS_CTX, T_CTX // S5_T)

        @pl.when(step >= S5_STEPS_CTX)
        def _():
            scan(d, N_LAT // S5_STEPS_LAT, T_LAT // S5_T)

        for g in range(S5_G):
            yg = _dot(x_scr[g], tp_scr[d, g]) + _dot_nt(prev_scr[g].astype(BF16), ft_scr[d, g])
            if d == 0:
                y_scr[g] = yg
            else:
                y_scr[g] += yg
    for to, y_to in enumerate(_block_transpose([y_scr[g] for g in range(S5_G)])):
        y_ref[0, rows(to, S5_R, stride=S5_T), :] = y_to[:, :LANES]
        y_ref[1, rows(to, S5_R, stride=S5_T), :] = y_to[:, LANES:]


def _s5(us, tabs, s0, l):
    pows, wrows, decay = tabs
    lay = lambda *shape: pl.BlockSpec((1,) + shape, lambda h: (l,) + (0,) * len(shape))
    op_buf = pltpu.VMEM((2, S5_G, S5_TW, 2 * S5_P), BF16)
    scan_buf = pltpu.VMEM((S5_G, S5_R, 2 * S5_P), F32)
    return pl.pallas_call(
        _s5_kernel,
        grid=(S5_STEPS,),
        in_specs=[_halves_spec(S5_TOK, lambda h: h),
                  lay(2, S5_G, 6, S5_T + 1, 2 * S5_P), lay(2, S5_G, 5, S5_H, 2 * S5_P), lay(2, S5_G, 4, 2 * S5_P),
                  pl.BlockSpec((1, 1, 2, S5_G, S5_SEQ, 4 * S5_P), lambda h: (l, h, 0, 0, 0, 0))],
        out_specs=[_halves_spec(S5_TOK, lambda h: h),
                   pl.BlockSpec((1, 2, S5_G, S5_SEQ, 2 * S5_P), lambda h: (h, 0, 0, 0, 0))],
        out_shape=[_HALVES_SHAPE, jax.ShapeDtypeStruct((S5_STEPS, 2, S5_G, S5_SEQ, 2 * S5_P), F32)],
        scratch_shapes=[op_buf, op_buf, op_buf, pltpu.VMEM((2, S5_G, S5_TW, S5_TW), BF16),
                        pltpu.VMEM((S5_G, S5_R, S5_TW), BF16), pltpu.VMEM((S5_G, S5_R, S5_TW), F32),
                        scan_buf, scan_buf, scan_buf],
        compiler_params=_cparams(("arbitrary",), 56),
        name="s5",
    )(us, pows, wrows, decay, s0)


def _split_bf16(a):
    hi = a.astype(BF16)
    return hi, (a - hi.astype(F32)).astype(BF16)


def _outproj_kernel(at_ref, fo_ref, ys_ref, us_ref, d_ref, wg_ref, wo_ref, x_ref, mod_ref,
                    g_ref, b_ref, wr_ref, x1_ref, h2_ref, lg_ref, wob_ref):
    @pl.when(pl.program_id(0) == 0)
    def _():
        wob_ref[...] = wo_ref[0].astype(BF16)

    ys = jnp.concatenate([ys_ref[0], ys_ref[1]], axis=1)
    us = jnp.concatenate([us_ref[0], us_ref[1]], axis=1)
    g = jax.nn.gelu(ys + us * d_ref[0])
    s5 = g * jax.nn.sigmoid(_dot(g.astype(BF16), wg_ref[0].astype(BF16)))
    mix = (_dot(at_ref[...], wob_ref[0:ATTN_W, :])
           + _dot(fo_ref[...], wob_ref[ATTN_W:ATTN_W + FOUR_W, :])
           + _dot(s5.astype(BF16), wob_ref[ATTN_W + FOUR_W:, :]))
    m = mod_ref[0, 0]
    r = ALPHA * x_ref[...] + m[2:3] * mix
    rc = r - jnp.mean(r, axis=-1, keepdims=True)
    t = rc * lax.rsqrt(jnp.mean(rc * rc, axis=-1, keepdims=True) + LN_EPS)
    gain, bias = g_ref[0], b_ref[0]
    x1_ref[...] = t * gain + bias
    up = 1.0 + m[4:5]
    hb = (t * (gain * up) + (bias * up + m[3:4])).astype(BF16)
    h2_ref[...] = hb
    lg_ref[...] = _dot_nt(wr_ref[0].astype(BF16), hb)


def _outproj(attn, four, ys5, us, s5_d, w_glu, w_out, x, mod, ln_g, ln_b, w_router_t, l):
    row = lambda w: pl.BlockSpec((RB_OUT, w), lambda i: (i, 0))
    halves = _halves_spec(RB_OUT, lambda i: i)
    lay = lambda *shape: pl.BlockSpec((1,) + shape, lambda i: (l,) + (0,) * len(shape))
    return pl.pallas_call(
        _outproj_kernel,
        grid=(ROWS // RB_OUT,),
        in_specs=[row(ATTN_W), row(FOUR_W), halves, halves, lay(1, S5_W), lay(S5_W, S5_W), lay(D, D), row(D),
                  pl.BlockSpec((1, 1, 6, D), lambda i: (l, _mod_row(i, RB_OUT), 0, 0)),
                  lay(1, D), lay(1, D), lay(N_EXP, D)],
        out_specs=[row(D), row(D), pl.BlockSpec((N_EXP, RB_OUT), lambda i: (0, i))],
        out_shape=[jax.ShapeDtypeStruct((ROWS, D), F32), jax.ShapeDtypeStruct((ROWS, D), BF16),
                   jax.ShapeDtypeStruct((N_EXP, ROWS), F32)],
        scratch_shapes=[pltpu.VMEM((D, D), BF16)],
        compiler_params=_cparams(("arbitrary",), 40),
        name="outproj",
    )(attn, four, ys5, us, s5_d, w_glu, w_out, x, mod, ln_g, ln_b, w_router_t)


SEQ_GROUP = LANES // N_EXP


def _router_kernel(lg_ref, crow_ref, ccol_ref, gcol_ref, *, n, cap, n_seq):
    rows = n_seq * N_EXP
    lg = jnp.concatenate([lg_ref[:, s * n:(s + 1) * n] for s in range(n_seq)], axis=0).reshape(n_seq, N_EXP, n)
    e = jnp.exp(lg - jnp.max(lg, axis=1, keepdims=True))
    aff = (e / jnp.sum(e, axis=1, keepdims=True)).reshape(rows, n)
    thr_bits = jnp.zeros((rows, 1), jnp.int32)
    for bit in range(30, -1, -1):
        cand = thr_bits | (1 << bit)
        cnt = jnp.sum(jnp.where(aff >= lax.bitcast_convert_type(cand, F32), 1.0, 0.0), axis=1, keepdims=True)
        thr_bits = jnp.where(cnt >= cap, cand, thr_bits)
    thr = lax.bitcast_convert_type(thr_bits, F32)
    above = aff > thr
    tied = aff == thr
    need = cap - jnp.sum(jnp.where(above, 1.0, 0.0), axis=1, keepdims=True)
    r0 = lax.broadcasted_iota(jnp.int32, (n, n), 0)
    r1 = lax.broadcasted_iota(jnp.int32, (n, n), 1)
    before = jnp.where(r0 < r1, 1.0, 0.0).astype(BF16)
    eye = jnp.where(r0 == r1, 1.0, 0.0).astype(BF16)
    tied_rank = _dot(jnp.where(tied, 1.0, 0.0).astype(BF16), before)
    sel = above | (tied & (tied_rank < need))
    slot = _dot(jnp.where(sel, 1.0, 0.0).astype(BF16), before)
    code = jnp.where(sel, slot + 1.0, 0.0)
    gate = jnp.where(sel, aff, 0.0)
    for s in range(n_seq):
        crow_ref[:, s * n:(s + 1) * n] = code[s * N_EXP:(s + 1) * N_EXP]
    for grp in range(pl.cdiv(n_seq, SEQ_GROUP)):
        lo, hi = grp * LANES, min((grp + 1) * LANES, rows)
        pad = [jnp.zeros((LANES - (hi - lo), n), F32)] if hi - lo < LANES else []
        ccol_ref[grp] = _dot_nt(eye, jnp.concatenate([code[lo:hi]] + pad, axis=0).astype(BF16))
        gcol_ref[grp] = _dot_nt(eye, jnp.concatenate([gate[lo:hi]] + pad, axis=0).astype(BF16))


def _router(lg, n, n_seq, blk, cap):
    groups = pl.cdiv(n_seq, SEQ_GROUP)
    whole = lambda *shape: pl.BlockSpec(shape, lambda i: (0,) * len(shape))
    return pl.pallas_call(
        functools.partial(_router_kernel, n=n, cap=cap, n_seq=n_seq),
        grid=(1,),
        in_specs=[pl.BlockSpec((N_EXP, n_seq * n), lambda i: (0, blk))],
        out_specs=[whole(N_EXP, n_seq * n), whole(groups, n, LANES), whole(groups, n, LANES)],
        out_shape=[jax.ShapeDtypeStruct((N_EXP, n_seq * n), F32),
                   jax.ShapeDtypeStruct((groups, n, LANES), F32),
                   jax.ShapeDtypeStruct((groups, n, LANES), F32)],
        compiler_params=_cparams(("arbitrary",), 48),
        name=f"router_{n}",
    )(lg)


def _gather_kernel(crow_ref, h_ref, o_ref, *, n, cap, n_sub):
    slot1 = (lax.broadcasted_iota(jnp.int32, (cap, n), 0) + 1).astype(F32)
    for j in range(n_sub):
        code = crow_ref[:, j * n:(j + 1) * n]
        onehot = jnp.concatenate(
            [jnp.where(code[e:e + 1, :] == slot1, 1.0, 0.0).astype(BF16) for e in range(N_EXP)], axis=0)
        xs = _dot(onehot, h_ref[j * n:(j + 1) * n, :])
        o_ref[:, j * cap:(j + 1) * cap, :] = xs.reshape(N_EXP, cap, D).astype(BF16)


def _gather(crow, h2, n, n_seq, n_sub, blk_off, cap):
    return pl.pallas_call(
        functools.partial(_gather_kernel, n=n, cap=cap, n_sub=n_sub),
        grid=(n_seq // n_sub,),
        in_specs=[pl.BlockSpec((N_EXP, n_sub * n), lambda s: (0, s)),
                  pl.BlockSpec((n_sub * n, D), lambda s: (blk_off + s, 0))],
        out_specs=pl.BlockSpec((N_EXP, n_sub * cap, D), lambda s: (0, s, 0)),
        out_shape=jax.ShapeDtypeStruct((N_EXP, n_seq * cap, D), BF16),
        compiler_params=_cparams(("arbitrary",), 48),
        name=f"gather_{n}",
    )(crow, h2)


SLOTS = N_CTX * CAP_CTX


def _ffn_kernel(xc_ref, xq_ref, wg_ref, wu0_ref, wu1_ref, wd_ref, yc_ref, yq_ref, wgb_ref, wub_ref, wdb_ref):
    stream = pl.program_id(1)

    @pl.when(stream == 0)
    def _():
        wgb_ref[...] = wg_ref[0, 0].astype(BF16)
        wub_ref[:, :FF // 2] = wu0_ref[0, 0].astype(BF16)
        wub_ref[:, FF // 2:] = wu1_ref[0, 0].astype(BF16)
        wdb_ref[...] = wd_ref[0, 0].astype(BF16)

    def swiglu(x_ref, y_ref):
        x = x_ref[0]
        g = _dot(x, wgb_ref[...])
        u = _dot(x, wub_ref[...])
        hid = (g * jax.nn.sigmoid(g) * u).astype(BF16)
        y_ref[0] = _dot(hid, wdb_ref[...]).astype(BF16)

    @pl.when(stream == 0)
    def _():
        swiglu(xc_ref, yc_ref)

    @pl.when(stream == 1)
    def _():
        swiglu(xq_ref, yq_ref)


def _ffn(xs_c, xs_q, w_gate, w_up, w_down, l):
    xspec = pl.BlockSpec((1, SLOTS, D), lambda e, s: (e, 0, 0))
    ahead = lambda e, s: jnp.minimum(e + s, N_EXP - 1)
    return pl.pallas_call(
        _ffn_kernel,
        grid=(N_EXP, 2),
        in_specs=[xspec, xspec,
                  pl.BlockSpec((1, 1, D, FF), lambda e, s: (l, ahead(e, s), 0, 0)),
                  pl.BlockSpec((1, 1, D, FF // 2), lambda e, s: (l, ahead(e, s), 0, 0)),
                  pl.BlockSpec((1, 1, D, FF // 2), lambda e, s: (l, e, 0, 1)),
                  pl.BlockSpec((1, 1, FF, D), lambda e, s: (l, e, 0, 0))],
        out_specs=[xspec, xspec],
        out_shape=[jax.ShapeDtypeStruct((N_EXP, SLOTS, D), BF16)] * 2,
        scratch_shapes=[pltpu.VMEM((D, FF), BF16), pltpu.VMEM((D, FF), BF16), pltpu.VMEM((FF, D), BF16)],
        compiler_params=_cparams(("arbitrary", "arbitrary"), 56),
        name="ffn",
    )(xs_c, xs_q, w_gate, w_up, w_up, w_down)


def _combine_kernel(ccol_ref, gcol_ref, ys_ref, x_ref, mod_ref, g_ref, b_ref, o_ref, *, cap, rb, n_sub):
    width = N_EXP * cap
    k = lax.broadcasted_iota(jnp.int32, (LANES, width), 0)
    col = lax.broadcasted_iota(jnp.int32, (LANES, width), 1)
    shift = cap.bit_length() - 1
    slot1 = ((lax.broadcasted_iota(jnp.int32, (1, width), 1) & (cap - 1)) + 1).astype(F32)
    code_t, gate_t = ccol_ref[0].astype(BF16), gcol_ref[0].astype(BF16)
    m = mod_ref[0, 0]
    for j in range(n_sub):
        seq = pl.program_id(0) * n_sub + j
        first = (seq % SEQ_GROUP) * N_EXP
        spread = jnp.where((col >> shift) + first == k, 1.0, 0.0).astype(BF16)
        code = _dot(code_t, spread)
        gate = _dot(gate_t, spread)
        weights = jnp.where(code == slot1, gate, 0.0).astype(BF16)
        moe = _dot(weights, ys_ref[:, j * cap:(j + 1) * cap, :].reshape(width, D))
        rows = slice(j * rb, (j + 1) * rb)
        o_ref[rows, :] = _layer_norm(ALPHA * x_ref[rows, :] + m[5:6] * moe, g_ref[0], b_ref[0])


def _combine(ccol, gcol, ys, x1, mod, ln_g, ln_b, l, cap, n_seq, n_sub, rb, rb_per_seq, rb_off, in_place):
    assert n_sub == 1 or rb_per_seq == 1
    rows = n_sub * rb
    lay = lambda *shape: pl.BlockSpec((1,) + shape, lambda s, r: (l,) + (0,) * len(shape))
    table = pl.BlockSpec((1, rb, LANES), lambda s, r: (s * n_sub // SEQ_GROUP, r, 0))
    block = lambda s, r: rb_off + s * rb_per_seq + r
    out_off = rb_off if in_place else 0
    return pl.pallas_call(
        functools.partial(_combine_kernel, cap=cap, rb=rb, n_sub=n_sub),
        grid=(n_seq // n_sub, rb_per_seq),
        in_specs=[table, table,
                  pl.BlockSpec((N_EXP, n_sub * cap, D), lambda s, r: (0, s, 0)),
                  pl.BlockSpec((rows, D), lambda s, r: (block(s, r), 0)),
                  pl.BlockSpec((1, 1, 6, D), lambda s, r: (l, _mod_row(block(s, r), rows), 0, 0)),
                  lay(1, D), lay(1, D)],
        out_specs=pl.BlockSpec((rows, D), lambda s, r: (block(s, r) - rb_off + out_off, 0)),
        out_shape=jax.ShapeDtypeStruct((ROWS if in_place else n_seq * rb_per_seq * rb, D), F32),
        input_output_aliases={3: 0} if in_place else {},
        compiler_params=_cparams(("arbitrary", "arbitrary"), 48),
        name=f"combine_{cap}",
    )(ccol, gcol, ys, x1, mod, ln_g, ln_b)


def kernel(x_prompt, x_sample, cache_k, cache_v, state_s5_re, state_s5_im, c, c_ctx, ln_in_g, ln_in_b, w_ada, b_ada,
           w_in, w_fourier, attn_sink, s5_a_re, s5_a_im, s5_log_dt, s5_b_re, s5_b_im, s5_c_re, s5_c_im, s5_d,
           s5_w_glu, w_out, ln1_g, ln1_b, w_router, w_gate, w_up, w_down, ln2_g, ln2_b):
    x = _ln_in(x_prompt.reshape(ROWS_CTX, D), x_sample.reshape(ROWS - ROWS_CTX, D), ln_in_g, ln_in_b)
    cond8 = jnp.concatenate([c_ctx[None], c, jnp.zeros((8 - 1 - N_LAT, D), F32)], axis=0)
    mod = _ada(cond8, w_ada, b_ada).reshape(DEPTH, 8, 6, D)

    rope_tabs = _rope_tables()
    cs_ctx, cs_lat = _dft_tables(T_CTX), _dft_tables(T_LAT)
    c64, s64 = _dft_channel_tables()
    cache_k = cache_k.reshape(N_LAT, DEPTH, PAST, KV_W)
    cache_v = cache_v.reshape(N_LAT, DEPTH, PAST, KV_W)
    w_router_t = jnp.swapaxes(w_router, 1, 2)
    sink_b = jnp.broadcast_to(attn_sink[:, :, None], (DEPTH, N_HEADS, LANES))
    s5_tabs = _s5_tables(s5_a_re, s5_a_im, s5_log_dt, s5_b_re, s5_b_im, s5_c_re, s5_c_im)
    lat_per_step = N_LAT // S5_STEPS_LAT
    s0 = jnp.concatenate([state_s5_re, state_s5_im, state_s5_im, state_s5_re], axis=-1)
    s0 = s0.reshape(S5_STEPS_LAT, lat_per_step, DEPTH, 2, S5_G, 4 * S5_P).transpose(2, 0, 3, 4, 1, 5)
    s0 = jnp.pad(s0, ((0, 0), (S5_STEPS_CTX, 0), (0, 0), (0, 0), (0, S5_SEQ - lat_per_step), (0, 0)))
    ln1_g, ln1_b, ln2_g, ln2_b, s5_d = (a.reshape(DEPTH, 1, -1) for a in (ln1_g, ln1_b, ln2_g, ln2_b, s5_d))

    new_k, new_v, new_s = [], [], []
    for l in range(DEPTH):
        q, k, v, uf, us, k_t, v_t = _inproj(x, mod, w_in, rope_tabs, l)
        new_k.append(k_t)
        new_v.append(v_t)
        attn = _attn_lat(_attn_ctx(q, k, v, sink_b[l]), k, v, cache_k, cache_v, sink_b[l], l)
        four = _fourier(uf, cs_ctx, c64, s64, w_fourier, l, T_CTX, N_CTX, SEQ_PER_STEP_CTX, 0)
        four = _fourier(four, cs_lat, c64, s64, w_fourier, l, T_LAT, N_LAT, 1, ROWS_CTX // T_LAT)
        y5, fin = _s5(us, s5_tabs, s0, l)
        new_s.append(fin[:S5_STEPS_CTX].transpose(0, 3, 1, 2, 4).reshape(N_CTX, 2, S5_G, 2 * S5_P))
        x1, h2, lg = _outproj(attn, four, y5, us, s5_d, s5_w_glu, w_out, x, mod, ln1_g, ln1_b, w_router_t, l)
        crow_c, ccol_c, gcol_c = _router(lg, T_CTX, N_CTX, 0, CAP_CTX)
        crow_q, ccol_q, gcol_q = _router(lg, T_LAT, N_LAT, 1, CAP_LAT)
        xs_c = _gather(crow_c, h2, T_CTX, N_CTX, SEQ_PER_STEP_CTX, 0, CAP_CTX)
        xs_q = _gather(crow_q, h2, T_LAT, N_LAT, 1, ROWS_CTX // T_LAT, CAP_LAT)
        ys_c, ys_q = _ffn(xs_c, xs_q, w_gate, w_up, w_down, l)
        args_c = (ccol_c, gcol_c, ys_c), (mod, ln2_g, ln2_b, l, CAP_CTX, N_CTX, SEQ_PER_STEP_CTX, T_CTX, 1, 0)
        args_q = (ccol_q, gcol_q, ys_q), (mod, ln2_g, ln2_b, l, CAP_LAT, N_LAT, 1, RB_COMBINE_LAT,
                                          T_LAT // RB_COMBINE_LAT, ROWS_CTX // RB_COMBINE_LAT)
        if l < DEPTH - 1:
            x = _combine(*args_c[0], x1, *args_c[1], in_place=True)
            x = _combine(*args_q[0], x, *args_q[1], in_place=True)
        else:
            y_ctx = _combine(*args_c[0], x1, *args_c[1], in_place=False)
            y_lat = _combine(*args_q[0], x1, *args_q[1], in_place=False)

    new_s = jnp.stack(new_s, axis=1)

    def cache(feat_major):
        return jnp.stack(feat_major).reshape(DEPTH, N_KV, HD, N_CTX, T_CTX).transpose(3, 0, 4, 1, 2)

    return (y_ctx.reshape(N_CTX, T_CTX, D), y_lat.reshape(N_LAT, T_LAT, D),
            cache(new_k), cache(new_v), new_s[..., :S5_P], new_s[..., S5_P:])
```

```python
import functools

import jax
import jax.numpy as jnp
import numpy as np
from jax import lax
from jax.experimental import pallas as pl
from jax.experimental.pallas import tpu as pltpu

F32 = jnp.float32
BF16 = jnp.bfloat16

D = 1024
N_CTX, T_CTX = 16, 256
N_LAT, T_LAT = 4, 1024
ROWS_CTX = N_CTX * T_CTX
ROWS = ROWS_CTX + N_LAT * T_LAT
DEPTH = 4
PAST = 512
GRID_W = 64
N_HEADS, N_KV, HD = 8, 2, 64
GQA = N_HEADS // N_KV
ATTN_W, KV_W = N_HEADS * HD, N_KV * HD
LOG2E = 1.4426950408889634
Q_SCALE = HD ** -0.5 * LOG2E
WINDOW = 128
FOUR_H, FOUR_W = 4, 256
S5_G, S5_H, S5_P, S5_W = 16, 16, 64, 256
S5_T = 16
IN_W = ATTN_W + 2 * KV_W + FOUR_W + S5_W
N_EXP, FF = 16, 1024
CAP_CTX, CAP_LAT = 2 * T_CTX // N_EXP, 2 * T_LAT // N_EXP
LN_EPS = 1e-5
NEG_INF = -1e30
ALPHA = (2 * DEPTH) ** 0.25
ROPE_BASE = 10000.0
RB_IN = 1024
RB_OUT = 512
RB_COMBINE_LAT = 512
LANES = 128
MIB = 2 ** 20
HIGHEST = lax.Precision.HIGHEST


def _cparams(sem, vmem_mib):
    return pltpu.CompilerParams(dimension_semantics=sem, vmem_limit_bytes=vmem_mib * MIB)


def _dot(a, b):
    return jnp.dot(a, b, preferred_element_type=F32)


def _dot_nt(a, b):
    return lax.dot_general(a, b, (((1,), (1,)), ((), ())), preferred_element_type=F32)


def _dot_tn(a, b):
    return lax.dot_general(a, b, (((0,), (0,)), ((), ())), preferred_element_type=F32)


def _layer_norm(x, g, b):
    mu = jnp.mean(x, axis=-1, keepdims=True)
    xc = x - mu
    var = jnp.mean(xc * xc, axis=-1, keepdims=True)
    return xc * lax.rsqrt(var + LN_EPS) * g + b


def _mod_row(block, block_rows):
    first = block * block_rows
    return jnp.where(first < ROWS_CTX, 0, 1 + (first - ROWS_CTX) // T_LAT)


_HALVES_SHAPE = jax.ShapeDtypeStruct((2, ROWS, LANES), F32)


def _halves_spec(rows, row_block):
    return pl.BlockSpec((2, rows, LANES), lambda *ids: (0, row_block(*ids), 0))


def _stream_specs(x, rows):
    if not isinstance(x, tuple):
        return [pl.BlockSpec((rows, D), lambda i: (i, 0))]
    n_ctx = ROWS_CTX // rows
    vec = pl.BlockSpec((1, D), lambda i: (0, 0))
    return [pl.BlockSpec((rows, D), lambda i: (jnp.minimum(i, n_ctx - 1), 0)),
            pl.BlockSpec((rows, D), lambda i: (jnp.maximum(i - n_ctx, 0), 0)), vec, vec]


def _stream_args(x):
    return x if isinstance(x, tuple) else (x,)


def _stream_block(refs, rows):
    if len(refs) == 1:
        return refs[0][...]
    xc_ref, xq_ref, g_ref, b_ref = refs
    raw = jnp.where(pl.program_id(0) < ROWS_CTX // rows, xc_ref[...], xq_ref[...])
    return _layer_norm(raw, g_ref[...], b_ref[...])


def _ada_kernel(c_ref, w_ref, b_ref, o_ref):
    c = c_ref[...]
    s = (c * jax.nn.sigmoid(c)).astype(BF16)
    o_ref[0] = _dot(s, w_ref[0].astype(BF16)) + b_ref[0]


def _ada(cond8, w_ada, b_ada):
    tn = 1536
    return pl.pallas_call(
        _ada_kernel,
        grid=(DEPTH, 6 * D // tn),
        in_specs=[pl.BlockSpec((8, D), lambda l, j: (0, 0)),
                  pl.BlockSpec((1, D, tn), lambda l, j: (l, 0, j)),
                  pl.BlockSpec((1, 1, tn), lambda l, j: (l, 0, j))],
        out_specs=pl.BlockSpec((1, 8, tn), lambda l, j: (l, 0, j)),
        out_shape=jax.ShapeDtypeStruct((DEPTH, 8, 6 * D), F32),
        compiler_params=_cparams(("arbitrary", "arbitrary"), 40),
        name="adaln",
    )(cond8, w_ada, b_ada.reshape(DEPTH, 1, 6 * D))


def _inproj_kernel(*refs, n_x):
    x_refs = refs[:n_x]
    mod_ref, w_ref, cos_ref, sin_ref, q_ref, k_ref, v_ref, uf_ref, us_ref, kt_ref, vt_ref, wb_ref, wswb_ref = refs[n_x:]
    step = pl.program_id(0)
    n_ctx_steps = ROWS_CTX // RB_IN

    @pl.when(step == 0)
    def _():
        wb = w_ref[0].astype(BF16)
        wb_ref[...] = wb
        n_qk = ATTN_W + KV_W
        src = lax.broadcasted_iota(jnp.int32, (n_qk, n_qk), 0)
        dst = lax.broadcasted_iota(jnp.int32, (n_qk, n_qk), 1)
        swap = jnp.where(src == (dst ^ 1), 1.0, 0.0).astype(BF16)
        wswb_ref[...] = _dot(wb[:, :n_qk], swap).astype(BF16)

    m = mod_ref[0, 0]
    hb = (_stream_block(x_refs, RB_IN) * (1.0 + m[1:2]) + m[0:1]).astype(BF16)
    p = _dot(hb, wb_ref[...])
    q = p[:, :ATTN_W]
    k = p[:, ATTN_W:ATTN_W + KV_W]
    v = p[:, ATTN_W + KV_W:ATTN_W + 2 * KV_W]

    psw = _dot(hb, wswb_ref[...])
    cos, sin = cos_ref[...], sin_ref[...]
    wide = lambda t: jnp.concatenate([t] * (ATTN_W // KV_W), axis=1)
    latent = step >= n_ctx_steps
    q = jnp.where(latent, q * wide(cos) + psw[:, :ATTN_W] * wide(sin), q)
    k = jnp.where(latent, k * cos + psw[:, ATTN_W:] * sin, k)
    q_ref[...] = (q * Q_SCALE).astype(BF16)
    k_ref[...] = k
    v_ref[...] = v

    @pl.when(step < n_ctx_steps)
    def _():
        kt_ref[...] = k.T
        vt_ref[...] = v.T

    uf_ref[...] = p[:, ATTN_W + 2 * KV_W:ATTN_W + 2 * KV_W + FOUR_W].astype(BF16)
    for half in range(S5_W // LANES):
        lo = ATTN_W + 2 * KV_W + FOUR_W + half * LANES
        us_ref[half] = p[:, lo:lo + LANES]


def _inproj(x, mod, w_in, rope_tabs, l):
    assert RB_IN == T_LAT
    outs = ((ATTN_W, BF16), (KV_W, F32), (KV_W, F32), (FOUR_W, BF16))
    table = pl.BlockSpec((T_LAT, KV_W), lambda i: (0, 0))
    feat_major = pl.BlockSpec((KV_W, RB_IN), lambda i: (0, jnp.minimum(i, ROWS_CTX // RB_IN - 1)))
    x_specs = _stream_specs(x, RB_IN)
    return pl.pallas_call(
        functools.partial(_inproj_kernel, n_x=len(x_specs)),
        grid=(ROWS // RB_IN,),
        in_specs=x_specs + [pl.BlockSpec((1, 1, 6, D), lambda i: (l, _mod_row(i, RB_IN), 0, 0)),
                            pl.BlockSpec((1, D, IN_W), lambda i: (l, 0, 0)),
                            table, table],
        out_specs=([pl.BlockSpec((RB_IN, w), lambda i: (i, 0)) for w, _ in outs] + [_halves_spec(RB_IN, lambda i: i)]
                   + [feat_major, feat_major]),
        out_shape=([jax.ShapeDtypeStruct((ROWS, w), dt) for w, dt in outs] + [_HALVES_SHAPE]
                   + [jax.ShapeDtypeStruct((KV_W, ROWS_CTX), F32)] * 2),
        scratch_shapes=[pltpu.VMEM((D, IN_W), BF16), pltpu.VMEM((D, ATTN_W + KV_W), BF16)],
        compiler_params=_cparams(("arbitrary",), 56),
        name="inproj",
    )(*_stream_args(x), mod, w_in, *rope_tabs)


SINK_ROWS = 16


def _softmax_av(score_blocks, sink, value_blocks):
    m = sink
    for s in score_blocks:
        m = jnp.maximum(m, jnp.max(s, axis=0, keepdims=True))
    first = lax.broadcasted_iota(jnp.int32, (SINK_ROWS, 1), 0) == 0
    probs = [jnp.exp2(s - m).astype(BF16) for s in score_blocks]
    probs.append(jnp.where(first, jnp.exp2(sink - m), 0.0).astype(BF16))
    ext = []
    for v in value_blocks:
        ones = jnp.ones(v.shape, BF16)
        ext.append(jnp.concatenate([v, v, ones, ones], axis=1))
    row = lax.broadcasted_iota(jnp.int32, (SINK_ROWS, 4 * HD), 0)
    lane = lax.broadcasted_iota(jnp.int32, (SINK_ROWS, 4 * HD), 1)
    ext.append(jnp.where((row == 0) & (lane >= 2 * HD), 1.0, 0.0).astype(BF16))
    acc = _dot_tn(jnp.concatenate(probs, axis=0), jnp.concatenate(ext, axis=0))
    return acc[:, :2 * HD] / acc[:, 2 * HD:]


def _stack_group(q, kv):
    return jnp.concatenate([q[:, (kv * GQA + g) * HD:(kv * GQA + g + 1) * HD] for g in range(GQA)], axis=0)


def _sink_row(sink_ref, kv, rows):
    return LOG2E * jnp.concatenate(
        [jnp.broadcast_to(sink_ref[kv * GQA + g:kv * GQA + g + 1, 0:1], (1, rows)) for g in range(GQA)], axis=1)


def _unstack_group(o, rows):
    lane = lax.broadcasted_iota(jnp.int32, (rows, 2 * HD), 1)
    return [jnp.where(lane < HD, o[2 * j * rows:(2 * j + 1) * rows], o[(2 * j + 1) * rows:(2 * j + 2) * rows])
            for j in range(GQA // 2)]


def _attn_ctx_kernel(q_ref, k_ref, v_ref, sink_ref, o_ref):
    q = q_ref[...]
    k = k_ref[...].astype(BF16)
    v = v_ref[...].astype(BF16)
    outs = []
    for kv in range(N_KV):
        head = slice(kv * HD, (kv + 1) * HD)
        s = _dot_nt(k[:, head], _stack_group(q, kv))
        o = _softmax_av([s], _sink_row(sink_ref, kv, T_CTX), [v[:, head]])
        outs += _unstack_group(o, T_CTX)
    o_ref[...] = jnp.concatenate(outs, axis=1).astype(BF16)


def _attn_ctx(q, k, v, sink_b):
    return pl.pallas_call(
        _attn_ctx_kernel,
        grid=(N_CTX,),
        in_specs=[pl.BlockSpec((T_CTX, ATTN_W), lambda s: (s, 0)),
                  pl.BlockSpec((T_CTX, KV_W), lambda s: (s, 0)),
                  pl.BlockSpec((T_CTX, KV_W), lambda s: (s, 0)),
                  pl.BlockSpec((N_HEADS, LANES), lambda s: (0, 0))],
        out_specs=pl.BlockSpec((T_CTX, ATTN_W), lambda s: (s, 0)),
        out_shape=jax.ShapeDtypeStruct((ROWS, ATTN_W), BF16),
        input_output_aliases={0: 0},
        compiler_params=_cparams(("arbitrary",), 32),
        name="attn_ctx",
    )(q, k, v, sink_b)


def _attn_lat_kernel(q_ref, k_ref, v_ref, ck_ref, cv_ref, sink_ref, o_ref):
    i = pl.program_id(1)
    n_blk = T_LAT // WINDOW
    q = q_ref[...]

    def kv_block(j):
        st = pl.multiple_of(j * WINDOW, WINDOW)
        return k_ref[pl.ds(st, WINDOW), :].astype(BF16), v_ref[pl.ds(st, WINDOW), :].astype(BF16)

    k0, v0 = kv_block(jnp.maximum(i - 1, 0))
    k1, v1 = kv_block(i)
    k2, v2 = kv_block(jnp.minimum(i + 1, n_blk - 1))
    k_win = jnp.concatenate([k0, k1, k2], axis=0)
    k_ctx = ck_ref[0, 0].astype(BF16)
    v_ctx = cv_ref[0, 0].astype(BF16)
    rows = GQA * WINDOW
    c = lax.broadcasted_iota(jnp.int32, (WINDOW, rows), 0)
    r = lax.broadcasted_iota(jnp.int32, (WINDOW, rows), 1) & (WINDOW - 1)
    far = 2 * WINDOW
    prev_ok = c >= r + jnp.where(i > 0, 0, far)
    next_ok = c + jnp.where(i < n_blk - 1, 0, far) <= r
    outs = []
    for kv in range(N_KV):
        head = slice(kv * HD, (kv + 1) * HD)
        qs = _stack_group(q, kv)
        s_win = _dot_nt(k_win[:, head], qs)
        scores = [jnp.where(prev_ok, s_win[:WINDOW], NEG_INF), s_win[WINDOW:2 * WINDOW],
                  jnp.where(next_ok, s_win[2 * WINDOW:], NEG_INF), _dot_nt(k_ctx[:, head], qs)]
        values = [v0[:, head], v1[:, head], v2[:, head], v_ctx[:, head]]
        outs += _unstack_group(_softmax_av(scores, _sink_row(sink_ref, kv, WINDOW), values), WINDOW)
    o_ref[...] = jnp.concatenate(outs, axis=1).astype(BF16)


def _attn_lat(q, k, v, cache_k, cache_v, sink_b, l):
    n_blk = T_LAT // WINDOW
    off = ROWS_CTX // WINDOW
    return pl.pallas_call(
        _attn_lat_kernel,
        grid=(N_LAT, n_blk),
        in_specs=[pl.BlockSpec((WINDOW, ATTN_W), lambda b, i: (off + b * n_blk + i, 0)),
                  pl.BlockSpec((T_LAT, KV_W), lambda b, i: (ROWS_CTX // T_LAT + b, 0)),
                  pl.BlockSpec((T_LAT, KV_W), lambda b, i: (ROWS_CTX // T_LAT + b, 0)),
                  pl.BlockSpec((1, 1, PAST, KV_W), lambda b, i: (b, l, 0, 0)),
                  pl.BlockSpec((1, 1, PAST, KV_W), lambda b, i: (b, l, 0, 0)),
                  pl.BlockSpec((N_HEADS, LANES), lambda b, i: (0, 0))],
        out_specs=pl.BlockSpec((WINDOW, ATTN_W), lambda b, i: (off + b * n_blk + i, 0)),
        out_shape=jax.ShapeDtypeStruct((ROWS, ATTN_W), BF16),
        input_output_aliases={0: 0},
        compiler_params=_cparams(("arbitrary", "arbitrary"), 40),
        name="attn_lat",
    )(q, k, v, cache_k, cache_v, sink_b)


def _rope_tables():
    rows = T_LAT // GRID_W
    row = jnp.repeat(jnp.arange(rows, dtype=F32), GRID_W)
    col = jnp.tile(jnp.arange(GRID_W, dtype=F32), rows)
    n_freq = HD // 4
    freqs = ROPE_BASE ** (-jnp.arange(n_freq, dtype=F32) / n_freq)
    ang = jnp.concatenate([row[:, None] * freqs, col[:, None] * freqs], axis=-1)
    cos = jnp.repeat(jnp.cos(ang), 2, axis=-1)
    sign = jnp.tile(jnp.array([-1.0, 1.0], F32), HD // 2)
    sin = jnp.repeat(jnp.sin(ang), 2, axis=-1) * sign
    return jnp.tile(cos, (1, N_KV)), jnp.tile(sin, (1, N_KV))


SEQ_PER_STEP_CTX = 4


def _four_kernel(u_ref, cs_ref, c64_ref, s64_ref, wf_ref, o_ref, *, n, n_sub):
    wf = wf_ref[0].astype(BF16)
    for j in range(n_sub):
        rows = slice(j * n, (j + 1) * n)
        ub = u_ref[rows, :]
        uc = _dot(ub, c64_ref[...]).astype(BF16)
        us = _dot(ub, s64_ref[...]).astype(BF16)
        f = _dot(cs_ref[...], jnp.concatenate([uc, us], axis=0))
        o_ref[rows, :] = _dot(f.astype(BF16), wf).astype(BF16)


def _dft_tables(n):
    j = np.arange(n)
    ang = 2.0 * np.pi * ((j[:, None] * j[None, :]) % n) / n
    cs = np.concatenate([np.cos(ang), -np.sin(ang)], axis=1) / np.sqrt(n)
    return jnp.asarray(cs.astype(np.float32)).astype(BF16)


def _dft_channel_tables():
    j = np.arange(HD)
    ang = 2.0 * np.pi * ((j[:, None] * j[None, :]) % HD) / HD
    eye = np.eye(FOUR_H)
    c = np.kron(eye, np.cos(ang)) / np.sqrt(HD)
    s = np.kron(eye, np.sin(ang)) / np.sqrt(HD)
    return jnp.asarray(c.astype(np.float32)).astype(BF16), jnp.asarray(s.astype(np.float32)).astype(BF16)


def _fourier(uf, cs, c64, s64, w_fourier, l, n, n_seq, n_sub, blk_off):
    rows = n_sub * n
    return pl.pallas_call(
        functools.partial(_four_kernel, n=n, n_sub=n_sub),
        grid=(n_seq // n_sub,),
        in_specs=[pl.BlockSpec((rows, FOUR_W), lambda s: (blk_off + s, 0)),
                  pl.BlockSpec((n, 2 * n), lambda s: (0, 0)),
                  pl.BlockSpec((FOUR_W, FOUR_W), lambda s: (0, 0)),
                  pl.BlockSpec((FOUR_W, FOUR_W), lambda s: (0, 0)),
                  pl.BlockSpec((1, FOUR_W, FOUR_W), lambda s: (l, 0, 0))],
        out_specs=pl.BlockSpec((rows, FOUR_W), lambda s: (blk_off + s, 0)),
        out_shape=jax.ShapeDtypeStruct((ROWS, FOUR_W), BF16),
        input_output_aliases={0: 0},
        compiler_params=_cparams(("arbitrary",), 40),
        name=f"fourier_{n}",
    )(uf, cs, c64, s64, w_fourier)


S5_TW = S5_T * S5_H
S5_R = 128
S5_TOK = S5_R * S5_T
S5_STEPS_CTX = ROWS_CTX // S5_TOK
S5_STEPS_LAT = (ROWS - ROWS_CTX) // S5_TOK
S5_STEPS = S5_STEPS_CTX + S5_STEPS_LAT
S5_SEQ = N_CTX // S5_STEPS_CTX


def _s5_tables(a_re, a_im, log_dt, b_re, b_im, c_re, c_im):
    f = lambda a: a.astype(F32)
    a_re, a_im, b_re, b_im, c_re, c_im = map(f, (a_re, a_im, b_re, b_im, c_re, c_im))
    dt = jnp.exp(f(log_dt))[..., None]
    x, y = a_re * dt, a_im * dt
    kk = jnp.arange(S5_T + 1, dtype=F32)[None, None, :, None, None]
    mag = jnp.exp(kk * x[:, :, None])
    pr, pi = mag * jnp.cos(kk * y[:, :, None]), mag * jnp.sin(kk * y[:, :, None])
    nr, ni = pr[:, :, 1] - 1.0, pi[:, :, 1]
    den = a_re * a_re + a_im * a_im
    qr, qi = (nr * a_re + ni * a_im) / den, (ni * a_re - nr * a_im) / den
    bb_r = qr[..., None] * b_re - qi[..., None] * b_im
    bb_i = qr[..., None] * b_im + qi[..., None] * b_re
    cat = lambda u, v: jnp.concatenate([u, v], axis=-1)
    prg, pig = pr.transpose(0, 1, 3, 2, 4), pi.transpose(0, 1, 3, 2, 4)
    pows = jnp.stack([cat(prg, prg), cat(-pig, pig), cat(prg, -pig), cat(-pig, -prg), cat(prg, pig), cat(-pig, prg)],
                     axis=3)
    bt_r, bt_i = bb_r.transpose(0, 1, 2, 4, 3), bb_i.transpose(0, 1, 2, 4, 3)
    wrows = jnp.stack([cat(bt_r, bt_i), cat(bt_i, bt_r), cat(bt_r, -bt_i), cat(c_re, c_re), cat(c_im, c_im)],
                      axis=3)
    ar, ai = pr[:, :, S5_T], pi[:, :, S5_T]
    decay = jnp.stack([cat(ar, ar), cat(-ai, ai), cat(ai, -ai), jnp.zeros_like(cat(ar, ar))], axis=3)
    return pows, wrows, decay


def _block_transpose(arrs):
    arrs = list(arrs)
    width = arrs[0].shape[1]
    blk = lax.broadcasted_iota(jnp.int32, arrs[0].shape, 1) >> 4
    s = len(arrs) // 2
    while s >= 1:
        keep = (blk & s) == 0
        for i in range(len(arrs)):
            if i & s:
                continue
            lo, hi = arrs[i], arrs[i + s]
            arrs[i] = jnp.where(keep, lo, pltpu.roll(hi, s * S5_H, 1))
            arrs[i + s] = jnp.where(keep, pltpu.roll(lo, width - s * S5_H, 1), hi)
        s //= 2
    return arrs


def _s5_build_operators(pw_ref, w_ref, e_scr, esw_scr, ft_scr, tp_scr):
    lane = lax.broadcasted_iota(jnp.int32, (S5_H, S5_TW), 1)
    for d in range(2):
        for g in range(S5_G):
            pw = lambda v, k: pw_ref[0, d, g, v, k:k + 1, :]
            b_ri, b_ir, b_conj, c_rr, c_ii = (w_ref[0, d, g, v] for v in range(5))
            lag_rows = []
            for t in range(S5_T):
                blk = slice(t * S5_H, (t + 1) * S5_H)
                ke = S5_T - 1 - t if d == 0 else t
                e_scr[d, g, blk, :] = (b_ri * pw(0, ke) + b_ir * pw(1, ke)).astype(BF16)
                esw_scr[d, g, blk, :] = (b_ir * pw(0, ke) - b_ri * pw(1, ke)).astype(BF16)
                kf = t + 1 if d == 0 else S5_T - t
                ft_scr[d, g, blk, :] = (c_rr * pw(2, kf) + c_ii * pw(3, kf)).astype(BF16)
                kl = t if d == 0 else S5_T - 1 - t
                lag_rows.append(c_rr * pw(4, kl) + c_ii * pw(5, kl))
            mh, ml = _split_bf16(jnp.concatenate(lag_rows, axis=0))
            bh, bl = _split_bf16(b_conj)
            kt = _dot_nt(bh, mh) + _dot_nt(bh, ml) + _dot_nt(bl, mh)
            for ti in range(S5_T):
                if d == 0:
                    sh = S5_H * ti
                    blk_rows = jnp.where(lane >= sh, pltpu.roll(kt, sh, 1) if sh else kt, 0.0)
                else:
                    sh = S5_H * (S5_T - 1 - ti)
                    blk_rows = jnp.where(lane < S5_TW - sh, pltpu.roll(kt, S5_TW - sh, 1) if sh else kt, 0.0)
                tp_scr[d, g, ti * S5_H:(ti + 1) * S5_H, :] = blk_rows.astype(BF16)


def _s5_kernel(us_ref, pw_ref, w_ref, a_ref, s0_ref, y_ref, fin_ref,
               e_scr, esw_scr, ft_scr, tp_scr, x_scr, y_scr, loc_scr, lsw_scr, prev_scr):
    step = pl.program_id(0)

    @pl.when(step == 0)
    def _():
        _s5_build_operators(pw_ref, w_ref, e_scr, esw_scr, ft_scr, tp_scr)

    rows = pl.ds
    x_by_step = [jnp.concatenate([us_ref[0, rows(tl, S5_R, stride=S5_T), :],
                                  us_ref[1, rows(tl, S5_R, stride=S5_T), :]], axis=1) for tl in range(S5_T)]
    for g, xg in enumerate(_block_transpose(x_by_step)):
        x_scr[g] = xg.astype(BF16)
    fin_ref[...] = jnp.zeros(fin_ref.shape, F32)

    def scan(d, nseq, nc):
        batch = S5_G
        for g0 in range(0, S5_G, batch):
            init = tuple((s0_ref[0, 0, d, g, 0:nseq, :2 * S5_P], s0_ref[0, 0, d, g, 0:nseq, 2 * S5_P:])
                         for g in range(g0, g0 + batch))

            def body(j, carry):
                ci = j if d == 0 else nc - 1 - j
                out = []
                for idx in range(batch):
                    g = g0 + idx
                    s, t = carry[idx]
                    ca, cb, cc = (a_ref[0, d, g, r:r + 1, :] for r in range(3))
                    prev_scr[g, rows(ci, nseq, stride=nc), :] = s
                    out.append((s * ca + t * cb + loc_scr[g, rows(ci, nseq, stride=nc), :],
                                t * ca + s * cc + lsw_scr[g, rows(ci, nseq, stride=nc), :]))
                return tuple(out)

            fin = lax.fori_loop(0, nc, body, init)
            for idx in range(batch):
                fin_ref[0, d, g0 + idx, 0:nseq, :] = fin[idx][0]

    for d in range(2):
        for g in range(S5_G):
            xg = x_scr[g]
            loc_scr[g] = _dot(xg, e_scr[d, g])
            lsw_scr[g] = _dot(xg, esw_scr[d, g])

        @pl.when(step < S5_STEPS_CTX)
        def _():
            scan(d, N_CTX // S5_STEPS_CTX, T_CTX // S5_T)

        @pl.when(step >= S5_STEPS_CTX)
        def _():
            scan(d, N_LAT // S5_STEPS_LAT, T_LAT // S5_T)

        for g in range(S5_G):
            yg = _dot(x_scr[g], tp_scr[d, g]) + _dot_nt(prev_scr[g].astype(BF16), ft_scr[d, g])
            if d == 0:
                y_scr[g] = yg
            else:
                y_scr[g] += yg
    for to, y_to in enumerate(_block_transpose([y_scr[g] for g in range(S5_G)])):
        y_ref[0, rows(to, S5_R, stride=S5_T), :] = y_to[:, :LANES]
        y_ref[1, rows(to, S5_R, stride=S5_T), :] = y_to[:, LANES:]


def _s5(us, tabs, s0, l):
    pows, wrows, decay = tabs
    lay = lambda *shape: pl.BlockSpec((1,) + shape, lambda h: (l,) + (0,) * len(shape))
    op_buf = pltpu.VMEM((2, S5_G, S5_TW, 2 * S5_P), BF16)
    scan_buf = pltpu.VMEM((S5_G, S5_R, 2 * S5_P), F32)
    return pl.pallas_call(
        _s5_kernel,
        grid=(S5_STEPS,),
        in_specs=[_halves_spec(S5_TOK, lambda h: h),
                  lay(2, S5_G, 6, S5_T + 1, 2 * S5_P), lay(2, S5_G, 5, S5_H, 2 * S5_P), lay(2, S5_G, 4, 2 * S5_P),
                  pl.BlockSpec((1, 1, 2, S5_G, S5_SEQ, 4 * S5_P), lambda h: (l, h, 0, 0, 0, 0))],
        out_specs=[_halves_spec(S5_TOK, lambda h: h),
                   pl.BlockSpec((1, 2, S5_G, S5_SEQ, 2 * S5_P), lambda h: (h, 0, 0, 0, 0))],
        out_shape=[_HALVES_SHAPE, jax.ShapeDtypeStruct((S5_STEPS, 2, S5_G, S5_SEQ, 2 * S5_P), F32)],
        scratch_shapes=[op_buf, op_buf, op_buf, pltpu.VMEM((2, S5_G, S5_TW, S5_TW), BF16),
                        pltpu.VMEM((S5_G, S5_R, S5_TW), BF16), pltpu.VMEM((S5_G, S5_R, S5_TW), F32),
                        scan_buf, scan_buf, scan_buf],
        compiler_params=_cparams(("arbitrary",), 56),
        name="s5",
    )(us, pows, wrows, decay, s0)


def _split_bf16(a):
    hi = a.astype(BF16)
    return hi, (a - hi.astype(F32)).astype(BF16)


def _outproj_kernel(*refs, n_x):
    x_refs = refs[:n_x]
    (at_ref, fo_ref, ys_ref, us_ref, d_ref, wg_ref, wo_ref, mod_ref, g_ref, b_ref, wr_ref,
     x1_ref, h2_ref, lg_ref, wob_ref) = refs[n_x:]

    @pl.when(pl.program_id(0) == 0)
    def _():
        wob_ref[...] = wo_ref[0].astype(BF16)

    ys = jnp.concatenate([ys_ref[0], ys_ref[1]], axis=1)
    us = jnp.concatenate([us_ref[0], us_ref[1]], axis=1)
    g = jax.nn.gelu(ys + us * d_ref[0])
    s5 = g * jax.nn.sigmoid(_dot(g.astype(BF16), wg_ref[0].astype(BF16)))
    mix = (_dot(at_ref[...], wob_ref[0:ATTN_W, :])
           + _dot(fo_ref[...], wob_ref[ATTN_W:ATTN_W + FOUR_W, :])
           + _dot(s5.astype(BF16), wob_ref[ATTN_W + FOUR_W:, :]))
    m = mod_ref[0, 0]
    r = ALPHA * _stream_block(x_refs, RB_OUT) + m[2:3] * mix
    rc = r - jnp.mean(r, axis=-1, keepdims=True)
    t = rc * lax.rsqrt(jnp.mean(rc * rc, axis=-1, keepdims=True) + LN_EPS)
    gain, bias = g_ref[0], b_ref[0]
    x1_ref[...] = t * gain + bias
    up = 1.0 + m[4:5]
    hb = (t * (gain * up) + (bias * up + m[3:4])).astype(BF16)
    h2_ref[...] = hb
    lg_ref[...] = _dot_nt(wr_ref[0].astype(BF16), hb)


def _outproj(attn, four, ys5, us, s5_d, w_glu, w_out, x, mod, ln_g, ln_b, w_router_t, l):
    row = lambda w: pl.BlockSpec((RB_OUT, w), lambda i: (i, 0))
    halves = _halves_spec(RB_OUT, lambda i: i)
    lay = lambda *shape: pl.BlockSpec((1,) + shape, lambda i: (l,) + (0,) * len(shape))
    x_specs = _stream_specs(x, RB_OUT)
    return pl.pallas_call(
        functools.partial(_outproj_kernel, n_x=len(x_specs)),
        grid=(ROWS // RB_OUT,),
        in_specs=x_specs + [row(ATTN_W), row(FOUR_W), halves, halves, lay(1, S5_W), lay(S5_W, S5_W), lay(D, D),
                            pl.BlockSpec((1, 1, 6, D), lambda i: (l, _mod_row(i, RB_OUT), 0, 0)),
                            lay(1, D), lay(1, D), lay(N_EXP, D)],
        out_specs=[row(D), row(D), pl.BlockSpec((N_EXP, RB_OUT), lambda i: (0, i))],
        out_shape=[jax.ShapeDtypeStruct((ROWS, D), F32), jax.ShapeDtypeStruct((ROWS, D), BF16),
                   jax.ShapeDtypeStruct((N_EXP, ROWS), F32)],
        scratch_shapes=[pltpu.VMEM((D, D), BF16)],
        compiler_params=_cparams(("arbitrary",), 40),
        name="outproj",
    )(*_stream_args(x), attn, four, ys5, us, s5_d, w_glu, w_out, mod, ln_g, ln_b, w_router_t)


SEQ_GROUP = LANES // N_EXP


def _router_kernel(lg_ref, crow_ref, ccol_ref, gcol_ref, *, n, cap, n_seq):
    rows = n_seq * N_EXP
    lg = jnp.concatenate([lg_ref[:, s * n:(s + 1) * n] for s in range(n_seq)], axis=0).reshape(n_seq, N_EXP, n)
    e = jnp.exp(lg - jnp.max(lg, axis=1, keepdims=True))
    aff = (e / jnp.sum(e, axis=1, keepdims=True)).reshape(rows, n)
    thr_bits = jnp.zeros((rows, 1), jnp.int32)
    for bit in range(30, -1, -1):
        cand = thr_bits | (1 << bit)
        cnt = jnp.sum(jnp.where(aff >= lax.bitcast_convert_type(cand, F32), 1.0, 0.0), axis=1, keepdims=True)
        thr_bits = jnp.where(cnt >= cap, cand, thr_bits)
    thr = lax.bitcast_convert_type(thr_bits, F32)
    above = aff > thr
    tied = aff == thr
    need = cap - jnp.sum(jnp.where(above, 1.0, 0.0), axis=1, keepdims=True)
    r0 = lax.broadcasted_iota(jnp.int32, (n, n), 0)
    r1 = lax.broadcasted_iota(jnp.int32, (n, n), 1)
    before = jnp.where(r0 < r1, 1.0, 0.0).astype(BF16)
    eye = jnp.where(r0 == r1, 1.0, 0.0).astype(BF16)
    tied_rank = _dot(jnp.where(tied, 1.0, 0.0).astype(BF16), before)
    sel = above | (tied & (tied_rank < need))
    slot = _dot(jnp.where(sel, 1.0, 0.0).astype(BF16), before)
    code = jnp.where(sel, slot + 1.0, 0.0)
    gate = jnp.where(sel, aff, 0.0)
    for s in range(n_seq):
        crow_ref[:, s * n:(s + 1) * n] = code[s * N_EXP:(s + 1) * N_EXP]
    for grp in range(pl.cdiv(n_seq, SEQ_GROUP)):
        lo, hi = grp * LANES, min((grp + 1) * LANES, rows)
        pad = [jnp.zeros((LANES - (hi - lo), n), F32)] if hi - lo < LANES else []
        ccol_ref[grp] = _dot_nt(eye, jnp.concatenate([code[lo:hi]] + pad, axis=0).astype(BF16))
        gcol_ref[grp] = _dot_nt(eye, jnp.concatenate([gate[lo:hi]] + pad, axis=0).astype(BF16))


def _router(lg, n, n_seq, blk, cap):
    groups = pl.cdiv(n_seq, SEQ_GROUP)
    whole = lambda *shape: pl.BlockSpec(shape, lambda i: (0,) * len(shape))
    return pl.pallas_call(
        functools.partial(_router_kernel, n=n, cap=cap, n_seq=n_seq),
        grid=(1,),
        in_specs=[pl.BlockSpec((N_EXP, n_seq * n), lambda i: (0, blk))],
        out_specs=[whole(N_EXP, n_seq * n), whole(groups, n, LANES), whole(groups, n, LANES)],
        out_shape=[jax.ShapeDtypeStruct((N_EXP, n_seq * n), F32),
                   jax.ShapeDtypeStruct((groups, n, LANES), F32),
                   jax.ShapeDtypeStruct((groups, n, LANES), F32)],
        compiler_params=_cparams(("arbitrary",), 48),
        name=f"router_{n}",
    )(lg)


def _gather_kernel(crow_ref, h_ref, o_ref, *, n, cap, n_sub):
    slot1 = (lax.broadcasted_iota(jnp.int32, (cap, n), 0) + 1).astype(F32)
    for j in range(n_sub):
        code = crow_ref[:, j * n:(j + 1) * n]
        onehot = jnp.concatenate(
            [jnp.where(code[e:e + 1, :] == slot1, 1.0, 0.0).astype(BF16) for e in range(N_EXP)], axis=0)
        xs = _dot(onehot, h_ref[j * n:(j + 1) * n, :])
        o_ref[:, j * cap:(j + 1) * cap, :] = xs.reshape(N_EXP, cap, D).astype(BF16)


def _gather(crow, h2, n, n_seq, n_sub, blk_off, cap):
    return pl.pallas_call(
        functools.partial(_gather_kernel, n=n, cap=cap, n_sub=n_sub),
        grid=(n_seq // n_sub,),
        in_specs=[pl.BlockSpec((N_EXP, n_sub * n), lambda s: (0, s)),
                  pl.BlockSpec((n_sub * n, D), lambda s: (blk_off + s, 0))],
        out_specs=pl.BlockSpec((N_EXP, n_sub * cap, D), lambda s: (0, s, 0)),
        out_shape=jax.ShapeDtypeStruct((N_EXP, n_seq * cap, D), BF16),
        compiler_params=_cparams(("arbitrary",), 48),
        name=f"gather_{n}",
    )(crow, h2)


SLOTS = N_CTX * CAP_CTX


def _ffn_kernel(xc_ref, xq_ref, wg_ref, wu0_ref, wu1_ref, wd_ref, yc_ref, yq_ref, wgb_ref, wub_ref, wdb_ref):
    stream = pl.program_id(1)

    @pl.when(stream == 0)
    def _():
        wgb_ref[...] = wg_ref[0, 0].astype(BF16)
        wub_ref[:, :FF // 2] = wu0_ref[0, 0].astype(BF16)
        wub_ref[:, FF // 2:] = wu1_ref[0, 0].astype(BF16)
        wdb_ref[...] = wd_ref[0, 0].astype(BF16)

    def swiglu(x_ref, y_ref):
        x = x_ref[0]
        g = _dot(x, wgb_ref[...])
        u = _dot(x, wub_ref[...])
        hid = (g * jax.nn.sigmoid(g) * u).astype(BF16)
        y_ref[0] = _dot(hid, wdb_ref[...]).astype(BF16)

    @pl.when(stream == 0)
    def _():
        swiglu(xc_ref, yc_ref)

    @pl.when(stream == 1)
    def _():
        swiglu(xq_ref, yq_ref)


def _ffn(xs_c, xs_q, w_gate, w_up, w_down, l):
    xspec = pl.BlockSpec((1, SLOTS, D), lambda e, s: (e, 0, 0))
    ahead = lambda e, s: jnp.minimum(e + s, N_EXP - 1)
    return pl.pallas_call(
        _ffn_kernel,
        grid=(N_EXP, 2),
        in_specs=[xspec, xspec,
                  pl.BlockSpec((1, 1, D, FF), lambda e, s: (l, ahead(e, s), 0, 0)),
                  pl.BlockSpec((1, 1, D, FF // 2), lambda e, s: (l, ahead(e, s), 0, 0)),
                  pl.BlockSpec((1, 1, D, FF // 2), lambda e, s: (l, e, 0, 1)),
                  pl.BlockSpec((1, 1, FF, D), lambda e, s: (l, e, 0, 0))],
        out_specs=[xspec, xspec],
        out_shape=[jax.ShapeDtypeStruct((N_EXP, SLOTS, D), BF16)] * 2,
        scratch_shapes=[pltpu.VMEM((D, FF), BF16), pltpu.VMEM((D, FF), BF16), pltpu.VMEM((FF, D), BF16)],
        compiler_params=_cparams(("arbitrary", "arbitrary"), 56),
        name="ffn",
    )(xs_c, xs_q, w_gate, w_up, w_up, w_down)


def _combine_kernel(ccol_ref, gcol_ref, ys_ref, x_ref, mod_ref, g_ref, b_ref, o_ref, *, cap, rb, n_sub):
    width = N_EXP * cap
    k = lax.broadcasted_iota(jnp.int32, (LANES, width), 0)
    col = lax.broadcasted_iota(jnp.int32, (LANES, width), 1)
    shift = cap.bit_length() - 1
    slot1 = ((lax.broadcasted_iota(jnp.int32, (1, width), 1) & (cap - 1)) + 1).astype(F32)
    code_t, gate_t = ccol_ref[0].astype(BF16), gcol_ref[0].astype(BF16)
    m = mod_ref[0, 0]
    for j in range(n_sub):
        seq = pl.program_id(0) * n_sub + j
        first = (seq % SEQ_GROUP) * N_EXP
        spread = jnp.where((col >> shift) + first == k, 1.0, 0.0).astype(BF16)
        code = _dot(code_t, spread)
        gate = _dot(gate_t, spread)
        weights = jnp.where(code == slot1, gate, 0.0).astype(BF16)
        moe = _dot(weights, ys_ref[:, j * cap:(j + 1) * cap, :].reshape(width, D))
        rows = slice(j * rb, (j + 1) * rb)
        o_ref[rows, :] = _layer_norm(ALPHA * x_ref[rows, :] + m[5:6] * moe, g_ref[0], b_ref[0])


def _combine(ccol, gcol, ys, x1, mod, ln_g, ln_b, l, cap, n_seq, n_sub, rb, rb_per_seq, rb_off, in_place):
    assert n_sub == 1 or rb_per_seq == 1
    rows = n_sub * rb
    lay = lambda *shape: pl.BlockSpec((1,) + shape, lambda s, r: (l,) + (0,) * len(shape))
    table = pl.BlockSpec((1, rb, LANES), lambda s, r: (s * n_sub // SEQ_GROUP, r, 0))
    block = lambda s, r: rb_off + s * rb_per_seq + r
    out_off = rb_off if in_place else 0
    return pl.pallas_call(
        functools.partial(_combine_kernel, cap=cap, rb=rb, n_sub=n_sub),
        grid=(n_seq // n_sub, rb_per_seq),
        in_specs=[table, table,
                  pl.BlockSpec((N_EXP, n_sub * cap, D), lambda s, r: (0, s, 0)),
                  pl.BlockSpec((rows, D), lambda s, r: (block(s, r), 0)),
                  pl.BlockSpec((1, 1, 6, D), lambda s, r: (l, _mod_row(block(s, r), rows), 0, 0)),
                  lay(1, D), lay(1, D)],
        out_specs=pl.BlockSpec((rows, D), lambda s, r: (block(s, r) - rb_off + out_off, 0)),
        out_shape=jax.ShapeDtypeStruct((ROWS if in_place else n_seq * rb_per_seq * rb, D), F32),
        input_output_aliases={3: 0} if in_place else {},
        compiler_params=_cparams(("arbitrary", "arbitrary"), 48),
        name=f"combine_{cap}",
    )(ccol, gcol, ys, x1, mod, ln_g, ln_b)


def kernel(x_prompt, x_sample, cache_k, cache_v, state_s5_re, state_s5_im, c, c_ctx, ln_in_g, ln_in_b, w_ada, b_ada,
           w_in, w_fourier, attn_sink, s5_a_re, s5_a_im, s5_log_dt, s5_b_re, s5_b_im, s5_c_re, s5_c_im, s5_d,
           s5_w_glu, w_out, ln1_g, ln1_b, w_router, w_gate, w_up, w_down, ln2_g, ln2_b):
    x = (x_prompt.reshape(ROWS_CTX, D), x_sample.reshape(ROWS - ROWS_CTX, D), ln_in_g.reshape(1, D), ln_in_b.reshape(1, D))
    cond8 = jnp.concatenate([c_ctx[None], c, jnp.zeros((8 - 1 - N_LAT, D), F32)], axis=0)
    mod = _ada(cond8, w_ada, b_ada).reshape(DEPTH, 8, 6, D)

    rope_tabs = _rope_tables()
    cs_ctx, cs_lat = _dft_tables(T_CTX), _dft_tables(T_LAT)
    c64, s64 = _dft_channel_tables()
    cache_k = cache_k.reshape(N_LAT, DEPTH, PAST, KV_W)
    cache_v = cache_v.reshape(N_LAT, DEPTH, PAST, KV_W)
    w_router_t = jnp.swapaxes(w_router, 1, 2)
    sink_b = jnp.broadcast_to(attn_sink[:, :, None], (DEPTH, N_HEADS, LANES))
    s5_tabs = _s5_tables(s5_a_re, s5_a_im, s5_log_dt, s5_b_re, s5_b_im, s5_c_re, s5_c_im)
    lat_per_step = N_LAT // S5_STEPS_LAT
    s0 = jnp.concatenate([state_s5_re, state_s5_im, state_s5_im, state_s5_re], axis=-1)
    s0 = s0.reshape(S5_STEPS_LAT, lat_per_step, DEPTH, 2, S5_G, 4 * S5_P).transpose(2, 0, 3, 4, 1, 5)
    s0 = jnp.pad(s0, ((0, 0), (S5_STEPS_CTX, 0), (0, 0), (0, 0), (0, S5_SEQ - lat_per_step), (0, 0)))
    ln1_g, ln1_b, ln2_g, ln2_b, s5_d = (a.reshape(DEPTH, 1, -1) for a in (ln1_g, ln1_b, ln2_g, ln2_b, s5_d))

    new_k, new_v, new_s = [], [], []
    for l in range(DEPTH):
        q, k, v, uf, us, k_t, v_t = _inproj(x, mod, w_in, rope_tabs, l)
        new_k.append(k_t)
        new_v.append(v_t)
        attn = _attn_lat(_attn_ctx(q, k, v, sink_b[l]), k, v, cache_k, cache_v, sink_b[l], l)
        four = _fourier(uf, cs_ctx, c64, s64, w_fourier, l, T_CTX, N_CTX, SEQ_PER_STEP_CTX, 0)
        four = _fourier(four, cs_lat, c64, s64, w_fourier, l, T_LAT, N_LAT, 1, ROWS_CTX // T_LAT)
        y5, fin = _s5(us, s5_tabs, s0, l)
        new_s.append(fin[:S5_STEPS_CTX].transpose(0, 3, 1, 2, 4).reshape(N_CTX, 2, S5_G, 2 * S5_P))
        x1, h2, lg = _outproj(attn, four, y5, us, s5_d, s5_w_glu, w_out, x, mod, ln1_g, ln1_b, w_router_t, l)
        crow_c, ccol_c, gcol_c = _router(lg, T_CTX, N_CTX, 0, CAP_CTX)
        crow_q, ccol_q, gcol_q = _router(lg, T_LAT, N_LAT, 1, CAP_LAT)
        xs_c = _gather(crow_c, h2, T_CTX, N_CTX, SEQ_PER_STEP_CTX, 0, CAP_CTX)
        xs_q = _gather(crow_q, h2, T_LAT, N_LAT, 1, ROWS_CTX // T_LAT, CAP_LAT)
        ys_c, ys_q = _ffn(xs_c, xs_q, w_gate, w_up, w_down, l)
        args_c = (ccol_c, gcol_c, ys_c), (mod, ln2_g, ln2_b, l, CAP_CTX, N_CTX, SEQ_PER_STEP_CTX, T_CTX, 1, 0)
        args_q = (ccol_q, gcol_q, ys_q), (mod, ln2_g, ln2_b, l, CAP_LAT, N_LAT, 1, RB_COMBINE_LAT,
                                          T_LAT // RB_COMBINE_LAT, ROWS_CTX // RB_COMBINE_LAT)
        if l < DEPTH - 1:
            x = _combine(*args_c[0], x1, *args_c[1], in_place=True)
            x = _combine(*args_q[0], x, *args_q[1], in_place=True)
        else:
            y_ctx = _combine(*args_c[0], x1, *args_c[1], in_place=False)
            y_lat = _combine(*args_q[0], x1, *args_q[1], in_place=False)

    new_s = jnp.stack(new_s, axis=1)

    def cache(feat_major):
        return jnp.stack(feat_major).reshape(DEPTH, N_KV, HD, N_CTX, T_CTX).transpose(3, 0, 4, 1, 2)

    return (y_ctx.reshape(N_CTX, T_CTX, D), y_lat.reshape(N_LAT, T_LAT, D),
            cache(new_k), cache(new_v), new_s[..., :S5_P], new_s[..., S5_P:])
```

```python
import functools

import jax
import jax.numpy as jnp
import numpy as np
from jax import lax
from jax.experimental import pallas as pl
from jax.experimental.pallas import tpu as pltpu

F32 = jnp.float32
BF16 = jnp.bfloat16

D = 1024
N_CTX, T_CTX = 16, 256
N_LAT, T_LAT = 4, 1024
ROWS_CTX = N_CTX * T_CTX
ROWS = ROWS_CTX + N_LAT * T_LAT
DEPTH = 4
PAST = 512
GRID_W = 64
N_HEADS, N_KV, HD = 8, 2, 64
GQA = N_HEADS // N_KV
ATTN_W, KV_W = N_HEADS * HD, N_KV * HD
LOG2E = 1.4426950408889634
Q_SCALE = HD ** -0.5 * LOG2E
WINDOW = 128
FOUR_H, FOUR_W = 4, 256
S5_G, S5_H, S5_P, S5_W = 16, 16, 64, 256
S5_T = 16
IN_W = ATTN_W + 2 * KV_W + FOUR_W + S5_W
N_EXP, FF = 16, 1024
CAP_CTX, CAP_LAT = 2 * T_CTX // N_EXP, 2 * T_LAT // N_EXP
LN_EPS = 1e-5
NEG_INF = -1e30
ALPHA = (2 * DEPTH) ** 0.25
ROPE_BASE = 10000.0
RB_IN = 1024
RB_OUT = 512
RB_COMBINE_LAT = 512
LANES = 128
MIB = 2 ** 20
HIGHEST = lax.Precision.HIGHEST


def _cparams(sem, vmem_mib):
    return pltpu.CompilerParams(dimension_semantics=sem, vmem_limit_bytes=vmem_mib * MIB)


def _dot(a, b):
    return jnp.dot(a, b, preferred_element_type=F32)


def _dot_nt(a, b):
    return lax.dot_general(a, b, (((1,), (1,)), ((), ())), preferred_element_type=F32)


def _dot_tn(a, b):
    return lax.dot_general(a, b, (((0,), (0,)), ((), ())), preferred_element_type=F32)


def _layer_norm(x, g, b):
    mu = jnp.mean(x, axis=-1, keepdims=True)
    xc = x - mu
    var = jnp.mean(xc * xc, axis=-1, keepdims=True)
    return xc * lax.rsqrt(var + LN_EPS) * g + b


def _mod_row(block, block_rows):
    first = block * block_rows
    return jnp.where(first < ROWS_CTX, 0, 1 + (first - ROWS_CTX) // T_LAT)


_HALVES_SHAPE = jax.ShapeDtypeStruct((2, ROWS, LANES), F32)


def _halves_spec(rows, row_block):
    return pl.BlockSpec((2, rows, LANES), lambda *ids: (0, row_block(*ids), 0))


def _stream_specs(x, rows):
    if not isinstance(x, tuple):
        return [pl.BlockSpec((rows, D), lambda i: (i, 0))]
    n_ctx = ROWS_CTX // rows
    vec = pl.BlockSpec((1, D), lambda i: (0, 0))
    return [pl.BlockSpec((rows, D), lambda i: (jnp.minimum(i, n_ctx - 1), 0)),
            pl.BlockSpec((rows, D), lambda i: (jnp.maximum(i - n_ctx, 0), 0)), vec, vec]


def _stream_args(x):
    return x if isinstance(x, tuple) else (x,)


def _stream_block(refs, rows):
    if len(refs) == 1:
        return refs[0][...]
    xc_ref, xq_ref, g_ref, b_ref = refs
    raw = jnp.where(pl.program_id(0) < ROWS_CTX // rows, xc_ref[...], xq_ref[...])
    return _layer_norm(raw, g_ref[...], b_ref[...])


def _ada_kernel(c_ref, w_ref, b_ref, o_ref):
    c = c_ref[...]
    s = (c * jax.nn.sigmoid(c)).astype(BF16)
    o_ref[0] = _dot(s, w_ref[0].astype(BF16)) + b_ref[0]


def _ada(cond8, w_ada, b_ada):
    tn = 1536
    return pl.pallas_call(
        _ada_kernel,
        grid=(DEPTH, 6 * D // tn),
        in_specs=[pl.BlockSpec((8, D), lambda l, j: (0, 0)),
                  pl.BlockSpec((1, D, tn), lambda l, j: (l, 0, j)),
                  pl.BlockSpec((1, 1, tn), lambda l, j: (l, 0, j))],
        out_specs=pl.BlockSpec((1, 8, tn), lambda l, j: (l, 0, j)),
        out_shape=jax.ShapeDtypeStruct((DEPTH, 8, 6 * D), F32),
        compiler_params=_cparams(("arbitrary", "arbitrary"), 40),
        name="adaln",
    )(cond8, w_ada, b_ada.reshape(DEPTH, 1, 6 * D))


def _inproj_kernel(*refs, n_x):
    x_refs = refs[:n_x]
    mod_ref, w_ref, cos_ref, sin_ref, q_ref, k_ref, v_ref, uf_ref, us_ref, kt_ref, vt_ref, wb_ref, wswb_ref = refs[n_x:]
    step = pl.program_id(0)
    n_ctx_steps = ROWS_CTX // RB_IN

    @pl.when(step == 0)
    def _():
        wb = w_ref[0].astype(BF16)
        wb_ref[...] = wb
        n_qk = ATTN_W + KV_W
        src = lax.broadcasted_iota(jnp.int32, (n_qk, n_qk), 0)
        dst = lax.broadcasted_iota(jnp.int32, (n_qk, n_qk), 1)
        swap = jnp.where(src == (dst ^ 1), 1.0, 0.0).astype(BF16)
        wswb_ref[...] = _dot(wb[:, :n_qk], swap).astype(BF16)

    m = mod_ref[0, 0]
    hb = (_stream_block(x_refs, RB_IN) * (1.0 + m[1:2]) + m[0:1]).astype(BF16)
    p = _dot(hb, wb_ref[...])
    q = p[:, :ATTN_W]
    k = p[:, ATTN_W:ATTN_W + KV_W]
    v = p[:, ATTN_W + KV_W:ATTN_W + 2 * KV_W]

    psw = _dot(hb, wswb_ref[...])
    cos, sin = cos_ref[...], sin_ref[...]
    wide = lambda t: jnp.concatenate([t] * (ATTN_W // KV_W), axis=1)
    latent = step >= n_ctx_steps
    q = jnp.where(latent, q * wide(cos) + psw[:, :ATTN_W] * wide(sin), q)
    k = jnp.where(latent, k * cos + psw[:, ATTN_W:] * sin, k)
    q_ref[...] = (q * Q_SCALE).astype(BF16)
    k_ref[...] = k
    v_ref[...] = v

    @pl.when(step < n_ctx_steps)
    def _():
        kt_ref[...] = k.T
        vt_ref[...] = v.T

    uf_ref[...] = p[:, ATTN_W + 2 * KV_W:ATTN_W + 2 * KV_W + FOUR_W].astype(BF16)
    for half in range(S5_W // LANES):
        lo = ATTN_W + 2 * KV_W + FOUR_W + half * LANES
        us_ref[half] = p[:, lo:lo + LANES]


def _inproj(x, mod, w_in, rope_tabs, l):
    assert RB_IN == T_LAT
    outs = ((ATTN_W, BF16), (KV_W, F32), (KV_W, F32), (FOUR_W, BF16))
    table = pl.BlockSpec((T_LAT, KV_W), lambda i: (0, 0))
    feat_major = pl.BlockSpec((KV_W, RB_IN), lambda i: (0, jnp.minimum(i, ROWS_CTX // RB_IN - 1)))
    x_specs = _stream_specs(x, RB_IN)
    return pl.pallas_call(
        functools.partial(_inproj_kernel, n_x=len(x_specs)),
        grid=(ROWS // RB_IN,),
        in_specs=x_specs + [pl.BlockSpec((1, 1, 6, D), lambda i: (l, _mod_row(i, RB_IN), 0, 0)),
                            pl.BlockSpec((1, D, IN_W), lambda i: (l, 0, 0)),
                            table, table],
        out_specs=([pl.BlockSpec((RB_IN, w), lambda i: (i, 0)) for w, _ in outs] + [_halves_spec(RB_IN, lambda i: i)]
                   + [feat_major, feat_major]),
        out_shape=([jax.ShapeDtypeStruct((ROWS, w), dt) for w, dt in outs] + [_HALVES_SHAPE]
                   + [jax.ShapeDtypeStruct((KV_W, ROWS_CTX), F32)] * 2),
        scratch_shapes=[pltpu.VMEM((D, IN_W), BF16), pltpu.VMEM((D, ATTN_W + KV_W), BF16)],
        compiler_params=_cparams(("arbitrary",), 56),
        name="inproj",
    )(*_stream_args(x), mod, w_in, *rope_tabs)


SINK_ROWS = 16


def _softmax_av(score_blocks, sink, value_blocks):
    m = sink
    for s in score_blocks:
        m = jnp.maximum(m, jnp.max(s, axis=0, keepdims=True))
    first = lax.broadcasted_iota(jnp.int32, (SINK_ROWS, 1), 0) == 0
    probs = [jnp.exp2(s - m).astype(BF16) for s in score_blocks]
    probs.append(jnp.where(first, jnp.exp2(sink - m), 0.0).astype(BF16))
    ext = []
    for v in value_blocks:
        ones = jnp.ones(v.shape, BF16)
        ext.append(jnp.concatenate([v, v, ones, ones], axis=1))
    row = lax.broadcasted_iota(jnp.int32, (SINK_ROWS, 4 * HD), 0)
    lane = lax.broadcasted_iota(jnp.int32, (SINK_ROWS, 4 * HD), 1)
    ext.append(jnp.where((row == 0) & (lane >= 2 * HD), 1.0, 0.0).astype(BF16))
    acc = _dot_tn(jnp.concatenate(probs, axis=0), jnp.concatenate(ext, axis=0))
    return acc[:, :2 * HD] / acc[:, 2 * HD:]


def _stack_group(q, kv):
    return jnp.concatenate([q[:, (kv * GQA + g) * HD:(kv * GQA + g + 1) * HD] for g in range(GQA)], axis=0)


def _sink_row(sink_ref, kv, rows):
    return LOG2E * jnp.concatenate(
        [jnp.broadcast_to(sink_ref[kv * GQA + g:kv * GQA + g + 1, 0:1], (1, rows)) for g in range(GQA)], axis=1)


def _unstack_group(o, rows):
    lane = lax.broadcasted_iota(jnp.int32, (rows, 2 * HD), 1)
    return [jnp.where(lane < HD, o[2 * j * rows:(2 * j + 1) * rows], o[(2 * j + 1) * rows:(2 * j + 2) * rows])
            for j in range(GQA // 2)]


def _attn_ctx_kernel(q_ref, k_ref, v_ref, sink_ref, o_ref):
    q = q_ref[...]
    k = k_ref[...].astype(BF16)
    v = v_ref[...].astype(BF16)
    outs = []
    for kv in range(N_KV):
        head = slice(kv * HD, (kv + 1) * HD)
        s = _dot_nt(k[:, head], _stack_group(q, kv))
        o = _softmax_av([s], _sink_row(sink_ref, kv, T_CTX), [v[:, head]])
        outs += _unstack_group(o, T_CTX)
    o_ref[...] = jnp.concatenate(outs, axis=1).astype(BF16)


def _attn_ctx(q, k, v, sink_b):
    return pl.pallas_call(
        _attn_ctx_kernel,
        grid=(N_CTX,),
        in_specs=[pl.BlockSpec((T_CTX, ATTN_W), lambda s: (s, 0)),
                  pl.BlockSpec((T_CTX, KV_W), lambda s: (s, 0)),
                  pl.BlockSpec((T_CTX, KV_W), lambda s: (s, 0)),
                  pl.BlockSpec((N_HEADS, LANES), lambda s: (0, 0))],
        out_specs=pl.BlockSpec((T_CTX, ATTN_W), lambda s: (s, 0)),
        out_shape=jax.ShapeDtypeStruct((ROWS, ATTN_W), BF16),
        input_output_aliases={0: 0},
        compiler_params=_cparams(("arbitrary",), 32),
        name="attn_ctx",
    )(q, k, v, sink_b)


def _attn_lat_kernel(q_ref, k_ref, v_ref, ck_ref, cv_ref, sink_ref, o_ref):
    i = pl.program_id(1)
    n_blk = T_LAT // WINDOW
    q = q_ref[...]

    def kv_block(j):
        st = pl.multiple_of(j * WINDOW, WINDOW)
        return k_ref[pl.ds(st, WINDOW), :].astype(BF16), v_ref[pl.ds(st, WINDOW), :].astype(BF16)

    k0, v0 = kv_block(jnp.maximum(i - 1, 0))
    k1, v1 = kv_block(i)
    k2, v2 = kv_block(jnp.minimum(i + 1, n_blk - 1))
    k_win = jnp.concatenate([k0, k1, k2], axis=0)
    k_ctx = ck_ref[0, 0].astype(BF16)
    v_ctx = cv_ref[0, 0].astype(BF16)
    rows = GQA * WINDOW
    c = lax.broadcasted_iota(jnp.int32, (WINDOW, rows), 0)
    r = lax.broadcasted_iota(jnp.int32, (WINDOW, rows), 1) & (WINDOW - 1)
    far = 2 * WINDOW
    prev_ok = c >= r + jnp.where(i > 0, 0, far)
    next_ok = c + jnp.where(i < n_blk - 1, 0, far) <= r
    outs = []
    for kv in range(N_KV):
        head = slice(kv * HD, (kv + 1) * HD)
        qs = _stack_group(q, kv)
        s_win = _dot_nt(k_win[:, head], qs)
        scores = [jnp.where(prev_ok, s_win[:WINDOW], NEG_INF), s_win[WINDOW:2 * WINDOW],
                  jnp.where(next_ok, s_win[2 * WINDOW:], NEG_INF), _dot_nt(k_ctx[:, head], qs)]
        values = [v0[:, head], v1[:, head], v2[:, head], v_ctx[:, head]]
        outs += _unstack_group(_softmax_av(scores, _sink_row(sink_ref, kv, WINDOW), values), WINDOW)
    o_ref[...] = jnp.concatenate(outs, axis=1).astype(BF16)


def _attn_lat(q, k, v, cache_k, cache_v, sink_b, l):
    n_blk = T_LAT // WINDOW
    off = ROWS_CTX // WINDOW
    return pl.pallas_call(
        _attn_lat_kernel,
        grid=(N_LAT, n_blk),
        in_specs=[pl.BlockSpec((WINDOW, ATTN_W), lambda b, i: (off + b * n_blk + i, 0)),
                  pl.BlockSpec((T_LAT, KV_W), lambda b, i: (ROWS_CTX // T_LAT + b, 0)),
                  pl.BlockSpec((T_LAT, KV_W), lambda b, i: (ROWS_CTX // T_LAT + b, 0)),
                  pl.BlockSpec((1, 1, PAST, KV_W), lambda b, i: (b, l, 0, 0)),
                  pl.BlockSpec((1, 1, PAST, KV_W), lambda b, i: (b, l, 0, 0)),
                  pl.BlockSpec((N_HEADS, LANES), lambda b, i: (0, 0))],
        out_specs=pl.BlockSpec((WINDOW, ATTN_W), lambda b, i: (off + b * n_blk + i, 0)),
        out_shape=jax.ShapeDtypeStruct((ROWS, ATTN_W), BF16),
        input_output_aliases={0: 0},
        compiler_params=_cparams(("arbitrary", "arbitrary"), 40),
        name="attn_lat",
    )(q, k, v, cache_k, cache_v, sink_b)


def _rope_tables():
    rows = T_LAT // GRID_W
    row = jnp.repeat(jnp.arange(rows, dtype=F32), GRID_W)
    col = jnp.tile(jnp.arange(GRID_W, dtype=F32), rows)
    n_freq = HD // 4
    freqs = ROPE_BASE ** (-jnp.arange(n_freq, dtype=F32) / n_freq)
    ang = jnp.concatenate([row[:, None] * freqs, col[:, None] * freqs], axis=-1)
    cos = jnp.repeat(jnp.cos(ang), 2, axis=-1)
    sign = jnp.tile(jnp.array([-1.0, 1.0], F32), HD // 2)
    sin = jnp.repeat(jnp.sin(ang), 2, axis=-1) * sign
    return jnp.tile(cos, (1, N_KV)), jnp.tile(sin, (1, N_KV))


SEQ_PER_STEP_CTX = 4


def _four_kernel(u_ref, cs_ref, c64_ref, s64_ref, wf_ref, o_ref, *, n, n_sub):
    wf = wf_ref[0].astype(BF16)
    for j in range(n_sub):
        rows = slice(j * n, (j + 1) * n)
        ub = u_ref[rows, :]
        uc = _dot(ub, c64_ref[...]).astype(BF16)
        us = _dot(ub, s64_ref[...]).astype(BF16)
        f = _dot(cs_ref[...], jnp.concatenate([uc, us], axis=0))
        o_ref[rows, :] = _dot(f.astype(BF16), wf).astype(BF16)


def _dft_tables(n):
    j = np.arange(n)
    ang = 2.0 * np.pi * ((j[:, None] * j[None, :]) % n) / n
    cs = np.concatenate([np.cos(ang), -np.sin(ang)], axis=1) / np.sqrt(n)
    return jnp.asarray(cs.astype(np.float32)).astype(BF16)


def _dft_channel_tables():
    j = np.arange(HD)
    ang = 2.0 * np.pi * ((j[:, None] * j[None, :]) % HD) / HD
    eye = np.eye(FOUR_H)
    c = np.kron(eye, np.cos(ang)) / np.sqrt(HD)
    s = np.kron(eye, np.sin(ang)) / np.sqrt(HD)
    return jnp.asarray(c.astype(np.float32)).astype(BF16), jnp.asarray(s.astype(np.float32)).astype(BF16)


def _fourier(uf, cs, c64, s64, w_fourier, l, n, n_seq, n_sub, blk_off):
    rows = n_sub * n
    return pl.pallas_call(
        functools.partial(_four_kernel, n=n, n_sub=n_sub),
        grid=(n_seq // n_sub,),
        in_specs=[pl.BlockSpec((rows, FOUR_W), lambda s: (blk_off + s, 0)),
                  pl.BlockSpec((n, 2 * n), lambda s: (0, 0)),
                  pl.BlockSpec((FOUR_W, FOUR_W), lambda s: (0, 0)),
                  pl.BlockSpec((FOUR_W, FOUR_W), lambda s: (0, 0)),
                  pl.BlockSpec((1, FOUR_W, FOUR_W), lambda s: (l, 0, 0))],
        out_specs=pl.BlockSpec((rows, FOUR_W), lambda s: (blk_off + s, 0)),
        out_shape=jax.ShapeDtypeStruct((ROWS, FOUR_W), BF16),
        input_output_aliases={0: 0},
        compiler_params=_cparams(("arbitrary",), 40),
        name=f"fourier_{n}",
    )(uf, cs, c64, s64, w_fourier)


S5_TW = S5_T * S5_H
S5_R = 128
S5_TOK = S5_R * S5_T
S5_STEPS_CTX = ROWS_CTX // S5_TOK
S5_STEPS_LAT = (ROWS - ROWS_CTX) // S5_TOK
S5_STEPS = S5_STEPS_CTX + S5_STEPS_LAT
S5_SEQ = N_CTX // S5_STEPS_CTX


def _s5_tables(a_re, a_im, log_dt, b_re, b_im, c_re, c_im):
    f = lambda a: a.astype(F32)
    a_re, a_im, b_re, b_im, c_re, c_im = map(f, (a_re, a_im, b_re, b_im, c_re, c_im))
    dt = jnp.exp(f(log_dt))[..., None]
    x, y = a_re * dt, a_im * dt
    kk = jnp.arange(S5_T + 1, dtype=F32)[None, None, :, None, None]
    mag = jnp.exp(kk * x[:, :, None])
    pr, pi = mag * jnp.cos(kk * y[:, :, None]), mag * jnp.sin(kk * y[:, :, None])
    nr, ni = pr[:, :, 1] - 1.0, pi[:, :, 1]
    den = a_re * a_re + a_im * a_im
    qr, qi = (nr * a_re + ni * a_im) / den, (ni * a_re - nr * a_im) / den
    bb_r = qr[..., None] * b_re - qi[..., None] * b_im
    bb_i = qr[..., None] * b_im + qi[..., None] * b_re
    cat = lambda u, v: jnp.concatenate([u, v], axis=-1)
    prg, pig = pr.transpose(0, 1, 3, 2, 4), pi.transpose(0, 1, 3, 2, 4)
    pows = jnp.stack([cat(prg, prg), cat(-pig, pig), cat(prg, -pig), cat(-pig, -prg), cat(prg, pig), cat(-pig, prg)],
                     axis=3)
    bt_r, bt_i = bb_r.transpose(0, 1, 2, 4, 3), bb_i.transpose(0, 1, 2, 4, 3)
    wrows = jnp.stack([cat(bt_r, bt_i), cat(bt_i, bt_r), cat(bt_r, -bt_i), cat(c_re, c_re), cat(c_im, c_im)],
                      axis=3)
    ar, ai = pr[:, :, S5_T], pi[:, :, S5_T]
    decay = jnp.stack([cat(ar, ar), cat(-ai, ai), cat(ai, -ai), jnp.zeros_like(cat(ar, ar))], axis=3)
    return pows, wrows, decay


def _block_transpose(arrs):
    arrs = list(arrs)
    width = arrs[0].shape[1]
    blk = lax.broadcasted_iota(jnp.int32, arrs[0].shape, 1) >> 4
    s = len(arrs) // 2
    while s >= 1:
        keep = (blk & s) == 0
        for i in range(len(arrs)):
            if i & s:
                continue
            lo, hi = arrs[i], arrs[i + s]
            arrs[i] = jnp.where(keep, lo, pltpu.roll(hi, s * S5_H, 1))
            arrs[i + s] = jnp.where(keep, pltpu.roll(lo, width - s * S5_H, 1), hi)
        s //= 2
    return arrs


def _s5_build_operators(pw_ref, w_ref, e_scr, esw_scr, ft_scr, tp_scr):
    lane = lax.broadcasted_iota(jnp.int32, (S5_H, S5_TW), 1)
    for d in range(2):
        for g in range(S5_G):
            pw = lambda v, k: pw_ref[0, d, g, v, k:k + 1, :]
            b_ri, b_ir, b_conj, c_rr, c_ii = (w_ref[0, d, g, v] for v in range(5))
            lag_rows = []
            for t in range(S5_T):
                blk = slice(t * S5_H, (t + 1) * S5_H)
                ke = S5_T - 1 - t if d == 0 else t
                e_scr[d, g, blk, :] = (b_ri * pw(0, ke) + b_ir * pw(1, ke)).astype(BF16)
                esw_scr[d, g, blk, :] = (b_ir * pw(0, ke) - b_ri * pw(1, ke)).astype(BF16)
                kf = t + 1 if d == 0 else S5_T - t
                ft_scr[d, g, blk, :] = (c_rr * pw(2, kf) + c_ii * pw(3, kf)).astype(BF16)
                kl = t if d == 0 else S5_T - 1 - t
                lag_rows.append(c_rr * pw(4, kl) + c_ii * pw(5, kl))
            mh, ml = _split_bf16(jnp.concatenate(lag_rows, axis=0))
            bh, bl = _split_bf16(b_conj)
            kt = _dot_nt(bh, mh) + _dot_nt(bh, ml) + _dot_nt(bl, mh)
            for ti in range(S5_T):
                if d == 0:
                    sh = S5_H * ti
                    blk_rows = jnp.where(lane >= sh, pltpu.roll(kt, sh, 1) if sh else kt, 0.0)
                else:
                    sh = S5_H * (S5_T - 1 - ti)
                    blk_rows = jnp.where(lane < S5_TW - sh, pltpu.roll(kt, S5_TW - sh, 1) if sh else kt, 0.0)
                tp_scr[d, g, ti * S5_H:(ti + 1) * S5_H, :] = blk_rows.astype(BF16)


def _s5_kernel(us_ref, pw_ref, w_ref, a_ref, s0_ref, y_ref, fin_ref,
               e_scr, esw_scr, ft_scr, tp_scr, x_scr, y_scr, loc_scr, lsw_scr, prev_scr):
    step = pl.program_id(0)

    @pl.when(step == 0)
    def _():
        _s5_build_operators(pw_ref, w_ref, e_scr, esw_scr, ft_scr, tp_scr)

    rows = pl.ds
    x_by_step = [jnp.concatenate([us_ref[0, rows(tl, S5_R, stride=S5_T), :],
                                  us_ref[1, rows(tl, S5_R, stride=S5_T), :]], axis=1) for tl in range(S5_T)]
    for g, xg in enumerate(_block_transpose(x_by_step)):
        x_scr[g] = xg.astype(BF16)
    fin_ref[...] = jnp.zeros(fin_ref.shape, F32)

    def scan(d, nseq, nc):
        batch = S5_G
        for g0 in range(0, S5_G, batch):
            init = tuple((s0_ref[0, 0, d, g, 0:nseq, :2 * S5_P], s0_ref[0, 0, d, g, 0:nseq, 2 * S5_P:])
                         for g in range(g0, g0 + batch))

            def body(j, carry):
                ci = j if d == 0 else nc - 1 - j
                out = []
                for idx in range(batch):
                    g = g0 + idx
                    s, t = carry[idx]
                    ca, cb, cc = (a_ref[0, d, g, r:r + 1, :] for r in range(3))
                    prev_scr[g, rows(ci, nseq, stride=nc), :] = s
                    out.append((s * ca + t * cb + loc_scr[g, rows(ci, nseq, stride=nc), :],
                                t * ca + s * cc + lsw_scr[g, rows(ci, nseq, stride=nc), :]))
                return tuple(out)

            fin = lax.fori_loop(0, nc, body, init)
            for idx in range(batch):
                fin_ref[0, d, g0 + idx, 0:nseq, :] = fin[idx][0]

    for d in range(2):
        for g in range(S5_G):
            xg = x_scr[g]
            loc_scr[g] = _dot(xg, e_scr[d, g])
            lsw_scr[g] = _dot(xg, esw_scr[d, g])

        @pl.when(step < S5_STEPS_CTX)
        def _():
            scan(d, N_CTX // S5_STEPS_CTX, T_CTX // S5_T)

        @pl.when(step >= S5_STEPS_CTX)
        def _():
            scan(d, N_LAT // S5_STEPS_LAT, T_LAT // S5_T)

        for g in range(S5_G):
            yg = _dot(x_scr[g], tp_scr[d, g]) + _dot_nt(prev_scr[g].astype(BF16), ft_scr[d, g])
            if d == 0:
                y_scr[g] = yg
            else:
                y_scr[g] += yg
    for to, y_to in enumerate(_block_transpose([y_scr[g] for g in range(S5_G)])):
        y_ref[0, rows(to, S5_R, stride=S5_T), :] = y_to[:, :LANES]
        y_ref[1, rows(to, S5_R, stride=S5_T), :] = y_to[:, LANES:]


def _s5(us, tabs, s0, l):
    pows, wrows, decay = tabs
    lay = lambda *shape: pl.BlockSpec((1,) + shape, lambda h: (l,) + (0,) * len(shape))
    op_buf = pltpu.VMEM((2, S5_G, S5_TW, 2 * S5_P), BF16)
    scan_buf = pltpu.VMEM((S5_G, S5_R, 2 * S5_P), F32)
    return pl.pallas_call(
        _s5_kernel,
        grid=(S5_STEPS,),
        in_specs=[_halves_spec(S5_TOK, lambda h: h),
                  lay(2, S5_G, 6, S5_T + 1, 2 * S5_P), lay(2, S5_G, 5, S5_H, 2 * S5_P), lay(2, S5_G, 4, 2 * S5_P),
                  pl.BlockSpec((1, 1, 2, S5_G, S5_SEQ, 4 * S5_P), lambda h: (l, h, 0, 0, 0, 0))],
        out_specs=[_halves_spec(S5_TOK, lambda h: h),
                   pl.BlockSpec((1, 2, S5_G, S5_SEQ, 2 * S5_P), lambda h: (h, 0, 0, 0, 0))],
        out_shape=[_HALVES_SHAPE, jax.ShapeDtypeStruct((S5_STEPS, 2, S5_G, S5_SEQ, 2 * S5_P), F32)],
        scratch_shapes=[op_buf, op_buf, op_buf, pltpu.VMEM((2, S5_G, S5_TW, S5_TW), BF16),
                        pltpu.VMEM((S5_G, S5_R, S5_TW), BF16), pltpu.VMEM((S5_G, S5_R, S5_TW), F32),
                        scan_buf, scan_buf, scan_buf],
        compiler_params=_cparams(("arbitrary",), 56),
        name="s5",
    )(us, pows, wrows, decay, s0)


def _split_bf16(a):
    hi = a.astype(BF16)
    return hi, (a - hi.astype(F32)).astype(BF16)


def _outproj_kernel(*refs, n_x):
    x_refs = refs[:n_x]
    (at_ref, fo_ref, ys_ref, us_ref, d_ref, wg_ref, wo_ref, mod_ref, g_ref, b_ref, wr_ref,
     x1_ref, h2_ref, lg_ref, wob_ref) = refs[n_x:]

    @pl.when(pl.program_id(0) == 0)
    def _():
        wob_ref[...] = wo_ref[0].astype(BF16)

    ys = jnp.concatenate([ys_ref[0], ys_ref[1]], axis=1)
    us = jnp.concatenate([us_ref[0], us_ref[1]], axis=1)
    g = jax.nn.gelu(ys + us * d_ref[0])
    s5 = g * jax.nn.sigmoid(_dot(g.astype(BF16), wg_ref[0].astype(BF16)))
    mix = (_dot(at_ref[...], wob_ref[0:ATTN_W, :])
           + _dot(fo_ref[...], wob_ref[ATTN_W:ATTN_W + FOUR_W, :])
           + _dot(s5.astype(BF16), wob_ref[ATTN_W + FOUR_W:, :]))
    m = mod_ref[0, 0]
    r = ALPHA * _stream_block(x_refs, RB_OUT) + m[2:3] * mix
    rc = r - jnp.mean(r, axis=-1, keepdims=True)
    t = rc * lax.rsqrt(jnp.mean(rc * rc, axis=-1, keepdims=True) + LN_EPS)
    gain, bias = g_ref[0], b_ref[0]
    x1_ref[...] = t * gain + bias
    up = 1.0 + m[4:5]
    hb = (t * (gain * up) + (bias * up + m[3:4])).astype(BF16)
    h2_ref[...] = hb
    lg_ref[...] = _dot_nt(wr_ref[0].astype(BF16), hb)


def _outproj(attn, four, ys5, us, s5_d, w_glu, w_out, x, mod, ln_g, ln_b, w_router_t, l):
    row = lambda w: pl.BlockSpec((RB_OUT, w), lambda i: (i, 0))
    halves = _halves_spec(RB_OUT, lambda i: i)
    lay = lambda *shape: pl.BlockSpec((1,) + shape, lambda i: (l,) + (0,) * len(shape))
    x_specs = _stream_specs(x, RB_OUT)
    return pl.pallas_call(
        functools.partial(_outproj_kernel, n_x=len(x_specs)),
        grid=(ROWS // RB_OUT,),
        in_specs=x_specs + [row(ATTN_W), row(FOUR_W), halves, halves, lay(1, S5_W), lay(S5_W, S5_W), lay(D, D),
                            pl.BlockSpec((1, 1, 6, D), lambda i: (l, _mod_row(i, RB_OUT), 0, 0)),
                            lay(1, D), lay(1, D), lay(N_EXP, D)],
        out_specs=[row(D), row(D), pl.BlockSpec((N_EXP, RB_OUT), lambda i: (0, i))],
        out_shape=[jax.ShapeDtypeStruct((ROWS, D), F32), jax.ShapeDtypeStruct((ROWS, D), BF16),
                   jax.ShapeDtypeStruct((N_EXP, ROWS), F32)],
        scratch_shapes=[pltpu.VMEM((D, D), BF16)],
        compiler_params=_cparams(("arbitrary",), 40),
        name="outproj",
    )(*_stream_args(x), attn, four, ys5, us, s5_d, w_glu, w_out, mod, ln_g, ln_b, w_router_t)


def _router_kernel(lg_ref, crow_ref, grow_ref, *, n, cap, n_seq):
    rows = n_seq * N_EXP
    lg = jnp.concatenate([lg_ref[:, s * n:(s + 1) * n] for s in range(n_seq)], axis=0).reshape(n_seq, N_EXP, n)
    e = jnp.exp(lg - jnp.max(lg, axis=1, keepdims=True))
    aff = (e / jnp.sum(e, axis=1, keepdims=True)).reshape(rows, n)
    thr_bits = jnp.zeros((rows, 1), jnp.int32)
    for bit in range(30, -1, -1):
        cand = thr_bits | (1 << bit)
        cnt = jnp.sum(jnp.where(aff >= lax.bitcast_convert_type(cand, F32), 1.0, 0.0), axis=1, keepdims=True)
        thr_bits = jnp.where(cnt >= cap, cand, thr_bits)
    thr = lax.bitcast_convert_type(thr_bits, F32)
    above = aff > thr
    tied = aff == thr
    need = cap - jnp.sum(jnp.where(above, 1.0, 0.0), axis=1, keepdims=True)
    r0 = lax.broadcasted_iota(jnp.int32, (n, n), 0)
    r1 = lax.broadcasted_iota(jnp.int32, (n, n), 1)
    before = jnp.where(r0 < r1, 1.0, 0.0).astype(BF16)
    tied_rank = _dot(jnp.where(tied, 1.0, 0.0).astype(BF16), before)
    sel = above | (tied & (tied_rank < need))
    slot = _dot(jnp.where(sel, 1.0, 0.0).astype(BF16), before)
    code = jnp.where(sel, slot + 1.0, 0.0)
    gate = jnp.where(sel, aff, 0.0)
    for s in range(n_seq):
        crow_ref[:, s * n:(s + 1) * n] = code[s * N_EXP:(s + 1) * N_EXP]
        grow_ref[:, s * n:(s + 1) * n] = gate[s * N_EXP:(s + 1) * N_EXP]


def _router(lg, n, n_seq, blk, cap):
    table = pl.BlockSpec((N_EXP, n_seq * n), lambda i: (0, 0))
    return pl.pallas_call(
        functools.partial(_router_kernel, n=n, cap=cap, n_seq=n_seq),
        grid=(1,),
        in_specs=[pl.BlockSpec((N_EXP, n_seq * n), lambda i: (0, blk))],
        out_specs=[table, table],
        out_shape=[jax.ShapeDtypeStruct((N_EXP, n_seq * n), F32)] * 2,
        compiler_params=_cparams(("arbitrary",), 48),
        name=f"router_{n}",
    )(lg)


def _gather_kernel(crow_ref, h_ref, o_ref, *, n, cap, n_sub):
    slot1 = (lax.broadcasted_iota(jnp.int32, (cap, n), 0) + 1).astype(F32)
    for j in range(n_sub):
        code = crow_ref[:, j * n:(j + 1) * n]
        onehot = jnp.concatenate(
            [jnp.where(code[e:e + 1, :] == slot1, 1.0, 0.0).astype(BF16) for e in range(N_EXP)], axis=0)
        xs = _dot(onehot, h_ref[j * n:(j + 1) * n, :])
        o_ref[:, j * cap:(j + 1) * cap, :] = xs.reshape(N_EXP, cap, D).astype(BF16)


def _gather(crow, h2, n, n_seq, n_sub, blk_off, cap):
    return pl.pallas_call(
        functools.partial(_gather_kernel, n=n, cap=cap, n_sub=n_sub),
        grid=(n_seq // n_sub,),
        in_specs=[pl.BlockSpec((N_EXP, n_sub * n), lambda s: (0, s)),
                  pl.BlockSpec((n_sub * n, D), lambda s: (blk_off + s, 0))],
        out_specs=pl.BlockSpec((N_EXP, n_sub * cap, D), lambda s: (0, s, 0)),
        out_shape=jax.ShapeDtypeStruct((N_EXP, n_seq * cap, D), BF16),
        compiler_params=_cparams(("arbitrary",), 48),
        name=f"gather_{n}",
    )(crow, h2)


SLOTS = N_CTX * CAP_CTX


def _ffn_kernel(xc_ref, xq_ref, wg_ref, wu0_ref, wu1_ref, wd_ref, yc_ref, yq_ref, wgb_ref, wub_ref, wdb_ref):
    stream = pl.program_id(1)

    @pl.when(stream == 0)
    def _():
        wgb_ref[...] = wg_ref[0, 0].astype(BF16)
        wub_ref[:, :FF // 2] = wu0_ref[0, 0].astype(BF16)
        wub_ref[:, FF // 2:] = wu1_ref[0, 0].astype(BF16)
        wdb_ref[...] = wd_ref[0, 0].astype(BF16)

    def swiglu(x_ref, y_ref):
        x = x_ref[0]
        g = _dot(x, wgb_ref[...])
        u = _dot(x, wub_ref[...])
        hid = (g * jax.nn.sigmoid(g) * u).astype(BF16)
        y_ref[0] = _dot(hid, wdb_ref[...]).astype(BF16)

    @pl.when(stream == 0)
    def _():
        swiglu(xc_ref, yc_ref)

    @pl.when(stream == 1)
    def _():
        swiglu(xq_ref, yq_ref)


def _ffn(xs_c, xs_q, w_gate, w_up, w_down, l):
    xspec = pl.BlockSpec((1, SLOTS, D), lambda e, s: (e, 0, 0))
    ahead = lambda e, s: jnp.minimum(e + s, N_EXP - 1)
    return pl.pallas_call(
        _ffn_kernel,
        grid=(N_EXP, 2),
        in_specs=[xspec, xspec,
                  pl.BlockSpec((1, 1, D, FF), lambda e, s: (l, ahead(e, s), 0, 0)),
                  pl.BlockSpec((1, 1, D, FF // 2), lambda e, s: (l, ahead(e, s), 0, 0)),
                  pl.BlockSpec((1, 1, D, FF // 2), lambda e, s: (l, e, 0, 1)),
                  pl.BlockSpec((1, 1, FF, D), lambda e, s: (l, e, 0, 0))],
        out_specs=[xspec, xspec],
        out_shape=[jax.ShapeDtypeStruct((N_EXP, SLOTS, D), BF16)] * 2,
        scratch_shapes=[pltpu.VMEM((D, FF), BF16), pltpu.VMEM((D, FF), BF16), pltpu.VMEM((FF, D), BF16)],
        compiler_params=_cparams(("arbitrary", "arbitrary"), 56),
        name="ffn",
    )(xs_c, xs_q, w_gate, w_up, w_up, w_down)


def _combine_kernel(crow_ref, grow_ref, ys_ref, x_ref, mod_ref, g_ref, b_ref, o_ref, *, cap, rb, n_sub):
    width = N_EXP * cap
    slot1 = (lax.broadcasted_iota(jnp.int32, (cap, rb), 0) + 1).astype(F32)
    m = mod_ref[0, 0]
    for j in range(n_sub):
        rows = slice(j * rb, (j + 1) * rb)
        code, gate = crow_ref[:, rows], grow_ref[:, rows]
        weights_t = jnp.concatenate(
            [jnp.where(code[e:e + 1, :] == slot1, gate[e:e + 1, :], 0.0).astype(BF16) for e in range(N_EXP)], axis=0)
        moe = _dot_tn(weights_t, ys_ref[:, j * cap:(j + 1) * cap, :].reshape(width, D))
        o_ref[rows, :] = _layer_norm(ALPHA * x_ref[rows, :] + m[5:6] * moe, g_ref[0], b_ref[0])


def _combine(crow, grow, ys, x1, mod, ln_g, ln_b, l, cap, n_seq, n_sub, rb, rb_per_seq, rb_off, in_place):
    assert n_sub == 1 or rb_per_seq == 1
    rows = n_sub * rb
    lay = lambda *shape: pl.BlockSpec((1,) + shape, lambda s, r: (l,) + (0,) * len(shape))
    table = pl.BlockSpec((N_EXP, rows), lambda s, r: (0, s * rb_per_seq + r))
    block = lambda s, r: rb_off + s * rb_per_seq + r
    out_off = rb_off if in_place else 0
    return pl.pallas_call(
        functools.partial(_combine_kernel, cap=cap, rb=rb, n_sub=n_sub),
        grid=(n_seq // n_sub, rb_per_seq),
        in_specs=[table, table,
                  pl.BlockSpec((N_EXP, n_sub * cap, D), lambda s, r: (0, s, 0)),
                  pl.BlockSpec((rows, D), lambda s, r: (block(s, r), 0)),
                  pl.BlockSpec((1, 1, 6, D), lambda s, r: (l, _mod_row(block(s, r), rows), 0, 0)),
                  lay(1, D), lay(1, D)],
        out_specs=pl.BlockSpec((rows, D), lambda s, r: (block(s, r) - rb_off + out_off, 0)),
        out_shape=jax.ShapeDtypeStruct((ROWS if in_place else n_seq * rb_per_seq * rb, D), F32),
        input_output_aliases={3: 0} if in_place else {},
        compiler_params=_cparams(("arbitrary", "arbitrary"), 48),
        name=f"combine_{cap}",
    )(crow, grow, ys, x1, mod, ln_g, ln_b)


def kernel(x_prompt, x_sample, cache_k, cache_v, state_s5_re, state_s5_im, c, c_ctx, ln_in_g, ln_in_b, w_ada, b_ada,
           w_in, w_fourier, attn_sink, s5_a_re, s5_a_im, s5_log_dt, s5_b_re, s5_b_im, s5_c_re, s5_c_im, s5_d,
           s5_w_glu, w_out, ln1_g, ln1_b, w_router, w_gate, w_up, w_down, ln2_g, ln2_b):
    x = (x_prompt.reshape(ROWS_CTX, D), x_sample.reshape(ROWS - ROWS_CTX, D), ln_in_g.reshape(1, D), ln_in_b.reshape(1, D))
    cond8 = jnp.concatenate([c_ctx[None], c, jnp.zeros((8 - 1 - N_LAT, D), F32)], axis=0)
    mod = _ada(cond8, w_ada, b_ada).reshape(DEPTH, 8, 6, D)

    rope_tabs = _rope_tables()
    cs_ctx, cs_lat = _dft_tables(T_CTX), _dft_tables(T_LAT)
    c64, s64 = _dft_channel_tables()
    cache_k = cache_k.reshape(N_LAT, DEPTH, PAST, KV_W)
    cache_v = cache_v.reshape(N_LAT, DEPTH, PAST, KV_W)
    w_router_t = jnp.swapaxes(w_router, 1, 2)
    sink_b = jnp.broadcast_to(attn_sink[:, :, None], (DEPTH, N_HEADS, LANES))
    s5_tabs = _s5_tables(s5_a_re, s5_a_im, s5_log_dt, s5_b_re, s5_b_im, s5_c_re, s5_c_im)
    lat_per_step = N_LAT // S5_STEPS_LAT
    s0 = jnp.concatenate([state_s5_re, state_s5_im, state_s5_im, state_s5_re], axis=-1)
    s0 = s0.reshape(S5_STEPS_LAT, lat_per_step, DEPTH, 2, S5_G, 4 * S5_P).transpose(2, 0, 3, 4, 1, 5)
    s0 = jnp.pad(s0, ((0, 0), (S5_STEPS_CTX, 0), (0, 0), (0, 0), (0, S5_SEQ - lat_per_step), (0, 0)))
    ln1_g, ln1_b, ln2_g, ln2_b, s5_d = (a.reshape(DEPTH, 1, -1) for a in (ln1_g, ln1_b, ln2_g, ln2_b, s5_d))

    new_k, new_v, new_s = [], [], []
    for l in range(DEPTH):
        q, k, v, uf, us, k_t, v_t = _inproj(x, mod, w_in, rope_tabs, l)
        new_k.append(k_t)
        new_v.append(v_t)
        attn = _attn_lat(_attn_ctx(q, k, v, sink_b[l]), k, v, cache_k, cache_v, sink_b[l], l)
        four = _fourier(uf, cs_ctx, c64, s64, w_fourier, l, T_CTX, N_CTX, SEQ_PER_STEP_CTX, 0)
        four = _fourier(four, cs_lat, c64, s64, w_fourier, l, T_LAT, N_LAT, 1, ROWS_CTX // T_LAT)
        y5, fin = _s5(us, s5_tabs, s0, l)
        new_s.append(fin[:S5_STEPS_CTX].transpose(0, 3, 1, 2, 4).reshape(N_CTX, 2, S5_G, 2 * S5_P))
        x1, h2, lg = _outproj(attn, four, y5, us, s5_d, s5_w_glu, w_out, x, mod, ln1_g, ln1_b, w_router_t, l)
        crow_c, grow_c = _router(lg, T_CTX, N_CTX, 0, CAP_CTX)
        crow_q, grow_q = _router(lg, T_LAT, N_LAT, 1, CAP_LAT)
        xs_c = _gather(crow_c, h2, T_CTX, N_CTX, SEQ_PER_STEP_CTX, 0, CAP_CTX)
        xs_q = _gather(crow_q, h2, T_LAT, N_LAT, 1, ROWS_CTX // T_LAT, CAP_LAT)
        ys_c, ys_q = _ffn(xs_c, xs_q, w_gate, w_up, w_down, l)
        args_c = (crow_c, grow_c, ys_c), (mod, ln2_g, ln2_b, l, CAP_CTX, N_CTX, SEQ_PER_STEP_CTX, T_CTX, 1, 0)
        args_q = (crow_q, grow_q, ys_q), (mod, ln2_g, ln2_b, l, CAP_LAT, N_LAT, 1, RB_COMBINE_LAT,
                                          T_LAT // RB_COMBINE_LAT, ROWS_CTX // RB_COMBINE_LAT)
        if l < DEPTH - 1:
            x = _combine(*args_c[0], x1, *args_c[1], in_place=True)
            x = _combine(*args_q[0], x, *args_q[1], in_place=True)
        else:
            y_ctx = _combine(*args_c[0], x1, *args_c[1], in_place=False)
            y_lat = _combine(*args_q[0], x1, *args_q[1], in_place=False)

    new_s = jnp.stack(new_s, axis=1)

    def cache(feat_major):
        return jnp.stack(feat_major).reshape(DEPTH, N_KV, HD, N_CTX, T_CTX).transpose(3, 0, 4, 1, 2)

    return (y_ctx.reshape(N_CTX, T_CTX, D), y_lat.reshape(N_LAT, T_LAT, D),
            cache(new_k), cache(new_v), new_s[..., :S5_P], new_s[..., S5_P:])
```

```python
import functools

import jax
import jax.numpy as jnp
import numpy as np
from jax import lax
from jax.experimental import pallas as pl
from jax.experimental.pallas import tpu as pltpu

F32 = jnp.float32
BF16 = jnp.bfloat16

D = 1024
N_CTX, T_CTX = 16, 256
N_LAT, T_LAT = 4, 1024
ROWS_CTX = N_CTX * T_CTX
ROWS = ROWS_CTX + N_LAT * T_LAT
DEPTH = 4
PAST = 512
GRID_W = 64
N_HEADS, N_KV, HD = 8, 2, 64
GQA = N_HEADS // N_KV
ATTN_W, KV_W = N_HEADS * HD, N_KV * HD
LOG2E = 1.4426950408889634
Q_SCALE = HD ** -0.5 * LOG2E
WINDOW = 128
FOUR_H, FOUR_W = 4, 256
S5_G, S5_H, S5_P, S5_W = 16, 16, 64, 256
S5_T = 16
IN_W = ATTN_W + 2 * KV_W + FOUR_W + S5_W
N_EXP, FF = 16, 1024
CAP_CTX, CAP_LAT = 2 * T_CTX // N_EXP, 2 * T_LAT // N_EXP
LN_EPS = 1e-5
NEG_INF = -1e30
ALPHA = (2 * DEPTH) ** 0.25
ROPE_BASE = 10000.0
RB_IN = 1024
RB_OUT = 512
RB_COMBINE_LAT = 512
LANES = 128
MIB = 2 ** 20
HIGHEST = lax.Precision.HIGHEST


def _cparams(sem, vmem_mib):
    return pltpu.CompilerParams(dimension_semantics=sem, vmem_limit_bytes=vmem_mib * MIB)


def _dot(a, b):
    return jnp.dot(a, b, preferred_element_type=F32)


def _dot_nt(a, b):
    return lax.dot_general(a, b, (((1,), (1,)), ((), ())), preferred_element_type=F32)


def _dot_tn(a, b):
    return lax.dot_general(a, b, (((0,), (0,)), ((), ())), preferred_element_type=F32)


def _layer_norm(x, g, b):
    mu = jnp.mean(x, axis=-1, keepdims=True)
    xc = x - mu
    var = jnp.mean(xc * xc, axis=-1, keepdims=True)
    return xc * lax.rsqrt(var + LN_EPS) * g + b


def _mod_row(block, block_rows):
    first = block * block_rows
    return jnp.where(first < ROWS_CTX, 0, 1 + (first - ROWS_CTX) // T_LAT)


_HALVES_SHAPE = jax.ShapeDtypeStruct((2, ROWS, LANES), F32)


def _halves_spec(rows, row_block):
    return pl.BlockSpec((2, rows, LANES), lambda *ids: (0, row_block(*ids), 0))


def _stream_specs(x, rows):
    if not isinstance(x, tuple):
        return [pl.BlockSpec((rows, D), lambda i: (i, 0))]
    n_ctx = ROWS_CTX // rows
    vec = pl.BlockSpec((1, D), lambda i: (0, 0))
    return [pl.BlockSpec((rows, D), lambda i: (jnp.minimum(i, n_ctx - 1), 0)),
            pl.BlockSpec((rows, D), lambda i: (jnp.maximum(i - n_ctx, 0), 0)), vec, vec]


def _stream_args(x):
    return x if isinstance(x, tuple) else (x,)


def _stream_block(refs, rows):
    if len(refs) == 1:
        return refs[0][...]
    xc_ref, xq_ref, g_ref, b_ref = refs
    raw = jnp.where(pl.program_id(0) < ROWS_CTX // rows, xc_ref[...], xq_ref[...])
    return _layer_norm(raw, g_ref[...], b_ref[...])


def _ada_kernel(c_ref, w_ref, b_ref, o_ref):
    c = c_ref[...]
    s = (c * jax.nn.sigmoid(c)).astype(BF16)
    o_ref[0] = _dot(s, w_ref[0].astype(BF16)) + b_ref[0]


def _ada(cond8, w_ada, b_ada):
    tn = 1536
    return pl.pallas_call(
        _ada_kernel,
        grid=(DEPTH, 6 * D // tn),
        in_specs=[pl.BlockSpec((8, D), lambda l, j: (0, 0)),
                  pl.BlockSpec((1, D, tn), lambda l, j: (l, 0, j)),
                  pl.BlockSpec((1, 1, tn), lambda l, j: (l, 0, j))],
        out_specs=pl.BlockSpec((1, 8, tn), lambda l, j: (l, 0, j)),
        out_shape=jax.ShapeDtypeStruct((DEPTH, 8, 6 * D), F32),
        compiler_params=_cparams(("arbitrary", "arbitrary"), 40),
        name="adaln",
    )(cond8, w_ada, b_ada.reshape(DEPTH, 1, 6 * D))


def _inproj_kernel(*refs, n_x):
    x_refs = refs[:n_x]
    (mod_ref, w_ref, cos_ref, sin_ref, _, _, q_ref, k_ref, v_ref, uf_ref, us_ref, kc_ref, vc_ref,
     wb_ref, wswb_ref) = refs[n_x:]
    step = pl.program_id(0)
    n_ctx_steps = ROWS_CTX // RB_IN

    @pl.when(step == 0)
    def _():
        wb = w_ref[0].astype(BF16)
        wb_ref[...] = wb
        n_qk = ATTN_W + KV_W
        src = lax.broadcasted_iota(jnp.int32, (n_qk, n_qk), 0)
        dst = lax.broadcasted_iota(jnp.int32, (n_qk, n_qk), 1)
        swap = jnp.where(src == (dst ^ 1), 1.0, 0.0).astype(BF16)
        wswb_ref[...] = _dot(wb[:, :n_qk], swap).astype(BF16)

    m = mod_ref[0, 0]
    hb = (_stream_block(x_refs, RB_IN) * (1.0 + m[1:2]) + m[0:1]).astype(BF16)
    p = _dot(hb, wb_ref[...])
    q = p[:, :ATTN_W]
    k = p[:, ATTN_W:ATTN_W + KV_W]
    v = p[:, ATTN_W + KV_W:ATTN_W + 2 * KV_W]

    psw = _dot(hb, wswb_ref[...])
    cos, sin = cos_ref[...], sin_ref[...]
    wide = lambda t: jnp.concatenate([t] * (ATTN_W // KV_W), axis=1)
    latent = step >= n_ctx_steps
    q = jnp.where(latent, q * wide(cos) + psw[:, :ATTN_W] * wide(sin), q)
    k = jnp.where(latent, k * cos + psw[:, ATTN_W:] * sin, k)
    q_ref[...] = (q * Q_SCALE).astype(BF16)
    k_ref[...] = k
    v_ref[...] = v

    @pl.when(step < n_ctx_steps)
    def _():
        for j in range(RB_IN // T_CTX):
            rows = slice(j * T_CTX, (j + 1) * T_CTX)
            kc_ref[j, 0] = k[rows].T
            vc_ref[j, 0] = v[rows].T

    uf_ref[...] = p[:, ATTN_W + 2 * KV_W:ATTN_W + 2 * KV_W + FOUR_W].astype(BF16)
    for half in range(S5_W // LANES):
        lo = ATTN_W + 2 * KV_W + FOUR_W + half * LANES
        us_ref[half] = p[:, lo:lo + LANES]


CACHE_SHAPE = jax.ShapeDtypeStruct((N_CTX, DEPTH, KV_W, T_CTX), F32)


def _inproj(x, mod, w_in, rope_tabs, caches, l):
    assert RB_IN == T_LAT
    outs = ((ATTN_W, BF16), (KV_W, F32), (KV_W, F32), (FOUR_W, BF16))
    table = pl.BlockSpec((T_LAT, KV_W), lambda i: (0, 0))
    any_space = pl.BlockSpec(memory_space=pl.ANY)
    slab = pl.BlockSpec((RB_IN // T_CTX, 1, KV_W, T_CTX), lambda i: (jnp.minimum(i, ROWS_CTX // RB_IN - 1), l, 0, 0))
    x_specs = _stream_specs(x, RB_IN)
    n_in = len(x_specs) + 4
    return pl.pallas_call(
        functools.partial(_inproj_kernel, n_x=len(x_specs)),
        grid=(ROWS // RB_IN,),
        in_specs=x_specs + [pl.BlockSpec((1, 1, 6, D), lambda i: (l, _mod_row(i, RB_IN), 0, 0)),
                            pl.BlockSpec((1, D, IN_W), lambda i: (l, 0, 0)),
                            table, table, any_space, any_space],
        out_specs=([pl.BlockSpec((RB_IN, w), lambda i: (i, 0)) for w, _ in outs] + [_halves_spec(RB_IN, lambda i: i)]
                   + [slab, slab]),
        out_shape=([jax.ShapeDtypeStruct((ROWS, w), dt) for w, dt in outs] + [_HALVES_SHAPE] + [CACHE_SHAPE] * 2),
        input_output_aliases={n_in: 5, n_in + 1: 6},
        scratch_shapes=[pltpu.VMEM((D, IN_W), BF16), pltpu.VMEM((D, ATTN_W + KV_W), BF16)],
        compiler_params=_cparams(("arbitrary",), 56),
        name="inproj",
    )(*_stream_args(x), mod, w_in, *rope_tabs, *caches)


SINK_ROWS = 16


def _softmax_av(score_blocks, sink, value_blocks):
    m = sink
    for s in score_blocks:
        m = jnp.maximum(m, jnp.max(s, axis=0, keepdims=True))
    first = lax.broadcasted_iota(jnp.int32, (SINK_ROWS, 1), 0) == 0
    probs = [jnp.exp2(s - m).astype(BF16) for s in score_blocks]
    probs.append(jnp.where(first, jnp.exp2(sink - m), 0.0).astype(BF16))
    ext = []
    for v in value_blocks:
        ones = jnp.ones(v.shape, BF16)
        ext.append(jnp.concatenate([v, v, ones, ones], axis=1))
    row = lax.broadcasted_iota(jnp.int32, (SINK_ROWS, 4 * HD), 0)
    lane = lax.broadcasted_iota(jnp.int32, (SINK_ROWS, 4 * HD), 1)
    ext.append(jnp.where((row == 0) & (lane >= 2 * HD), 1.0, 0.0).astype(BF16))
    acc = _dot_tn(jnp.concatenate(probs, axis=0), jnp.concatenate(ext, axis=0))
    return acc[:, :2 * HD] / acc[:, 2 * HD:]


def _stack_group(q, kv):
    return jnp.concatenate([q[:, (kv * GQA + g) * HD:(kv * GQA + g + 1) * HD] for g in range(GQA)], axis=0)


def _sink_row(sink_ref, kv, rows):
    return LOG2E * jnp.concatenate(
        [jnp.broadcast_to(sink_ref[kv * GQA + g:kv * GQA + g + 1, 0:1], (1, rows)) for g in range(GQA)], axis=1)


def _unstack_group(o, rows):
    lane = lax.broadcasted_iota(jnp.int32, (rows, 2 * HD), 1)
    return [jnp.where(lane < HD, o[2 * j * rows:(2 * j + 1) * rows], o[(2 * j + 1) * rows:(2 * j + 2) * rows])
            for j in range(GQA // 2)]


def _attn_ctx_kernel(q_ref, k_ref, v_ref, sink_ref, o_ref):
    q = q_ref[...]
    k = k_ref[...].astype(BF16)
    v = v_ref[...].astype(BF16)
    outs = []
    for kv in range(N_KV):
        head = slice(kv * HD, (kv + 1) * HD)
        s = _dot_nt(k[:, head], _stack_group(q, kv))
        o = _softmax_av([s], _sink_row(sink_ref, kv, T_CTX), [v[:, head]])
        outs += _unstack_group(o, T_CTX)
    o_ref[...] = jnp.concatenate(outs, axis=1).astype(BF16)


def _attn_ctx(q, k, v, sink_b):
    return pl.pallas_call(
        _attn_ctx_kernel,
        grid=(N_CTX,),
        in_specs=[pl.BlockSpec((T_CTX, ATTN_W), lambda s: (s, 0)),
                  pl.BlockSpec((T_CTX, KV_W), lambda s: (s, 0)),
                  pl.BlockSpec((T_CTX, KV_W), lambda s: (s, 0)),
                  pl.BlockSpec((N_HEADS, LANES), lambda s: (0, 0))],
        out_specs=pl.BlockSpec((T_CTX, ATTN_W), lambda s: (s, 0)),
        out_shape=jax.ShapeDtypeStruct((ROWS, ATTN_W), BF16),
        input_output_aliases={0: 0},
        compiler_params=_cparams(("arbitrary",), 32),
        name="attn_ctx",
    )(q, k, v, sink_b)


def _attn_lat_kernel(q_ref, k_ref, v_ref, ck_ref, cv_ref, sink_ref, o_ref):
    i = pl.program_id(1)
    n_blk = T_LAT // WINDOW
    q = q_ref[...]

    def kv_block(j):
        st = pl.multiple_of(j * WINDOW, WINDOW)
        return k_ref[pl.ds(st, WINDOW), :].astype(BF16), v_ref[pl.ds(st, WINDOW), :].astype(BF16)

    k0, v0 = kv_block(jnp.maximum(i - 1, 0))
    k1, v1 = kv_block(i)
    k2, v2 = kv_block(jnp.minimum(i + 1, n_blk - 1))
    k_win = jnp.concatenate([k0, k1, k2], axis=0)
    k_ctx = ck_ref[0, 0].astype(BF16)
    v_ctx = cv_ref[0, 0].astype(BF16)
    rows = GQA * WINDOW
    c = lax.broadcasted_iota(jnp.int32, (WINDOW, rows), 0)
    r = lax.broadcasted_iota(jnp.int32, (WINDOW, rows), 1) & (WINDOW - 1)
    far = 2 * WINDOW
    prev_ok = c >= r + jnp.where(i > 0, 0, far)
    next_ok = c + jnp.where(i < n_blk - 1, 0, far) <= r
    outs = []
    for kv in range(N_KV):
        head = slice(kv * HD, (kv + 1) * HD)
        qs = _stack_group(q, kv)
        s_win = _dot_nt(k_win[:, head], qs)
        scores = [jnp.where(prev_ok, s_win[:WINDOW], NEG_INF), s_win[WINDOW:2 * WINDOW],
                  jnp.where(next_ok, s_win[2 * WINDOW:], NEG_INF), _dot_nt(k_ctx[:, head], qs)]
        values = [v0[:, head], v1[:, head], v2[:, head], v_ctx[:, head]]
        outs += _unstack_group(_softmax_av(scores, _sink_row(sink_ref, kv, WINDOW), values), WINDOW)
    o_ref[...] = jnp.concatenate(outs, axis=1).astype(BF16)


def _attn_lat(q, k, v, cache_k, cache_v, sink_b, l):
    n_blk = T_LAT // WINDOW
    off = ROWS_CTX // WINDOW
    return pl.pallas_call(
        _attn_lat_kernel,
        grid=(N_LAT, n_blk),
        in_specs=[pl.BlockSpec((WINDOW, ATTN_W), lambda b, i: (off + b * n_blk + i, 0)),
                  pl.BlockSpec((T_LAT, KV_W), lambda b, i: (ROWS_CTX // T_LAT + b, 0)),
                  pl.BlockSpec((T_LAT, KV_W), lambda b, i: (ROWS_CTX // T_LAT + b, 0)),
                  pl.BlockSpec((1, 1, PAST, KV_W), lambda b, i: (b, l, 0, 0)),
                  pl.BlockSpec((1, 1, PAST, KV_W), lambda b, i: (b, l, 0, 0)),
                  pl.BlockSpec((N_HEADS, LANES), lambda b, i: (0, 0))],
        out_specs=pl.BlockSpec((WINDOW, ATTN_W), lambda b, i: (off + b * n_blk + i, 0)),
        out_shape=jax.ShapeDtypeStruct((ROWS, ATTN_W), BF16),
        input_output_aliases={0: 0},
        compiler_params=_cparams(("arbitrary", "arbitrary"), 40),
        name="attn_lat",
    )(q, k, v, cache_k, cache_v, sink_b)


def _rope_tables():
    rows = T_LAT // GRID_W
    row = jnp.repeat(jnp.arange(rows, dtype=F32), GRID_W)
    col = jnp.tile(jnp.arange(GRID_W, dtype=F32), rows)
    n_freq = HD // 4
    freqs = ROPE_BASE ** (-jnp.arange(n_freq, dtype=F32) / n_freq)
    ang = jnp.concatenate([row[:, None] * freqs, col[:, None] * freqs], axis=-1)
    cos = jnp.repeat(jnp.cos(ang), 2, axis=-1)
    sign = jnp.tile(jnp.array([-1.0, 1.0], F32), HD // 2)
    sin = jnp.repeat(jnp.sin(ang), 2, axis=-1) * sign
    return jnp.tile(cos, (1, N_KV)), jnp.tile(sin, (1, N_KV))


SEQ_PER_STEP_CTX = 4


def _four_kernel(u_ref, cs_ref, c64_ref, s64_ref, wf_ref, o_ref, *, n, n_sub):
    wf = wf_ref[0].astype(BF16)
    for j in range(n_sub):
        rows = slice(j * n, (j + 1) * n)
        ub = u_ref[rows, :]
        uc = _dot(ub, c64_ref[...]).astype(BF16)
        us = _dot(ub, s64_ref[...]).astype(BF16)
        f = _dot(cs_ref[...], jnp.concatenate([uc, us], axis=0))
        o_ref[rows, :] = _dot(f.astype(BF16), wf).astype(BF16)


def _dft_tables(n):
    j = np.arange(n)
    ang = 2.0 * np.pi * ((j[:, None] * j[None, :]) % n) / n
    cs = np.concatenate([np.cos(ang), -np.sin(ang)], axis=1) / np.sqrt(n)
    return jnp.asarray(cs.astype(np.float32)).astype(BF16)


def _dft_channel_tables():
    j = np.arange(HD)
    ang = 2.0 * np.pi * ((j[:, None] * j[None, :]) % HD) / HD
    eye = np.eye(FOUR_H)
    c = np.kron(eye, np.cos(ang)) / np.sqrt(HD)
    s = np.kron(eye, np.sin(ang)) / np.sqrt(HD)
    return jnp.asarray(c.astype(np.float32)).astype(BF16), jnp.asarray(s.astype(np.float32)).astype(BF16)


def _fourier(uf, cs, c64, s64, w_fourier, l, n, n_seq, n_sub, blk_off):
    rows = n_sub * n
    return pl.pallas_call(
        functools.partial(_four_kernel, n=n, n_sub=n_sub),
        grid=(n_seq // n_sub,),
        in_specs=[pl.BlockSpec((rows, FOUR_W), lambda s: (blk_off + s, 0)),
                  pl.BlockSpec((n, 2 * n), lambda s: (0, 0)),
                  pl.BlockSpec((FOUR_W, FOUR_W), lambda s: (0, 0)),
                  pl.BlockSpec((FOUR_W, FOUR_W), lambda s: (0, 0)),
                  pl.BlockSpec((1, FOUR_W, FOUR_W), lambda s: (l, 0, 0))],
        out_specs=pl.BlockSpec((rows, FOUR_W), lambda s: (blk_off + s, 0)),
        out_shape=jax.ShapeDtypeStruct((ROWS, FOUR_W), BF16),
        input_output_aliases={0: 0},
        compiler_params=_cparams(("arbitrary",), 40),
        name=f"fourier_{n}",
    )(uf, cs, c64, s64, w_fourier)


S5_TW = S5_T * S5_H
S5_R = 128
S5_TOK = S5_R * S5_T
S5_STEPS_CTX = ROWS_CTX // S5_TOK
S5_STEPS_LAT = (ROWS - ROWS_CTX) // S5_TOK
S5_STEPS = S5_STEPS_CTX + S5_STEPS_LAT
S5_SEQ = N_CTX // S5_STEPS_CTX


def _s5_tables(a_re, a_im, log_dt, b_re, b_im, c_re, c_im):
    f = lambda a: a.astype(F32)
    a_re, a_im, b_re, b_im, c_re, c_im = map(f, (a_re, a_im, b_re, b_im, c_re, c_im))
    dt = jnp.exp(f(log_dt))[..., None]
    x, y = a_re * dt, a_im * dt
    kk = jnp.arange(S5_T + 1, dtype=F32)[None, None, :, None, None]
    mag = jnp.exp(kk * x[:, :, None])
    pr, pi = mag * jnp.cos(kk * y[:, :, None]), mag * jnp.sin(kk * y[:, :, None])
    nr, ni = pr[:, :, 1] - 1.0, pi[:, :, 1]
    den = a_re * a_re + a_im * a_im
    qr, qi = (nr * a_re + ni * a_im) / den, (ni * a_re - nr * a_im) / den
    bb_r = qr[..., None] * b_re - qi[..., None] * b_im
    bb_i = qr[..., None] * b_im + qi[..., None] * b_re
    cat = lambda u, v: jnp.concatenate([u, v], axis=-1)
    prg, pig = pr.transpose(0, 1, 3, 2, 4), pi.transpose(0, 1, 3, 2, 4)
    def lane_pairs(re, im, patterns):
        coef = lambda k: jnp.asarray(np.repeat(np.array([[p[0][k], p[1][k]] for p in patterns], np.float32),
                                               re.shape[-1], axis=-1))[:, None, :]
        return cat(re, re)[..., None, :, :] * coef(0) + cat(im, im)[..., None, :, :] * coef(1)

    keep_re, keep_im, neg_re, neg_im = (1, 0), (0, 1), (-1, 0), (0, -1)
    pows = lane_pairs(prg, pig, [(keep_re, keep_re), (neg_im, keep_im), (keep_re, neg_im), (neg_im, neg_re),
                                 (keep_re, keep_im), (neg_im, keep_re)])
    bt_r, bt_i = bb_r.transpose(0, 1, 2, 4, 3), bb_i.transpose(0, 1, 2, 4, 3)
    wrows = jnp.concatenate([lane_pairs(bt_r, bt_i, [(keep_re, keep_im), (keep_im, keep_re), (keep_re, neg_im)]),
                             lane_pairs(c_re, c_im, [(keep_re, keep_re), (keep_im, keep_im)])],
                            axis=3)
    ar, ai = pr[:, :, S5_T], pi[:, :, S5_T]
    decay = jnp.stack([cat(ar, ar), cat(-ai, ai), cat(ai, -ai), jnp.zeros_like(cat(ar, ar))], axis=3)
    return pows, wrows, decay


def _block_transpose(arrs):
    arrs = list(arrs)
    width = arrs[0].shape[1]
    blk = lax.broadcasted_iota(jnp.int32, arrs[0].shape, 1) >> 4
    s = len(arrs) // 2
    while s >= 1:
        keep = (blk & s) == 0
        for i in range(len(arrs)):
            if i & s:
                continue
            lo, hi = arrs[i], arrs[i + s]
            arrs[i] = jnp.where(keep, lo, pltpu.roll(hi, s * S5_H, 1))
            arrs[i + s] = jnp.where(keep, pltpu.roll(lo, width - s * S5_H, 1), hi)
        s //= 2
    return arrs


def _s5_build_operators(pw_ref, w_ref, e_scr, esw_scr, ft_scr, tp_scr):
    lane = lax.broadcasted_iota(jnp.int32, (S5_H, S5_TW), 1)
    for d in range(2):
        for g in range(S5_G):
            pw = lambda v, k: pw_ref[0, d, g, v, k:k + 1, :]
            b_ri, b_ir, b_conj, c_rr, c_ii = (w_ref[0, d, g, v] for v in range(5))
            lag_rows = []
            for t in range(S5_T):
                blk = slice(t * S5_H, (t + 1) * S5_H)
                ke = S5_T - 1 - t if d == 0 else t
                e_scr[d, g, blk, :] = (b_ri * pw(0, ke) + b_ir * pw(1, ke)).astype(BF16)
                esw_scr[d, g, blk, :] = (b_ir * pw(0, ke) - b_ri * pw(1, ke)).astype(BF16)
                kf = t + 1 if d == 0 else S5_T - t
                ft_scr[d, g, blk, :] = (c_rr * pw(2, kf) + c_ii * pw(3, kf)).astype(BF16)
                kl = t if d == 0 else S5_T - 1 - t
                lag_rows.append(c_rr * pw(4, kl) + c_ii * pw(5, kl))
            mh, ml = _split_bf16(jnp.concatenate(lag_rows, axis=0))
            bh, bl = _split_bf16(b_conj)
            kt = _dot_nt(bh, mh) + _dot_nt(bh, ml) + _dot_nt(bl, mh)
            for ti in range(S5_T):
                if d == 0:
                    sh = S5_H * ti
                    blk_rows = jnp.where(lane >= sh, pltpu.roll(kt, sh, 1) if sh else kt, 0.0)
                else:
                    sh = S5_H * (S5_T - 1 - ti)
                    blk_rows = jnp.where(lane < S5_TW - sh, pltpu.roll(kt, S5_TW - sh, 1) if sh else kt, 0.0)
                tp_scr[d, g, ti * S5_H:(ti + 1) * S5_H, :] = blk_rows.astype(BF16)


def _s5_kernel(us_ref, pw_ref, w_ref, a_ref, s0_ref, y_ref, fin_ref,
               e_scr, esw_scr, ft_scr, tp_scr, x_scr, y_scr, loc_scr, lsw_scr, prev_scr):
    step = pl.program_id(0)

    @pl.when(step == 0)
    def _():
        _s5_build_operators(pw_ref, w_ref, e_scr, esw_scr, ft_scr, tp_scr)

    rows = pl.ds
    x_by_step = [jnp.concatenate([us_ref[0, rows(tl, S5_R, stride=S5_T), :],
                                  us_ref[1, rows(tl, S5_R, stride=S5_T), :]], axis=1) for tl in range(S5_T)]
    for g, xg in enumerate(_block_transpose(x_by_step)):
        x_scr[g] = xg.astype(BF16)
    fin_ref[...] = jnp.zeros(fin_ref.shape, F32)

    def scan(d, nseq, nc):
        batch = S5_G
        for g0 in range(0, S5_G, batch):
            init = tuple((s0_ref[0, 0, d, g, 0:nseq, :2 * S5_P], s0_ref[0, 0, d, g, 0:nseq, 2 * S5_P:])
                         for g in range(g0, g0 + batch))

            def body(j, carry):
                ci = j if d == 0 else nc - 1 - j
                out = []
                for idx in range(batch):
                    g = g0 + idx
                    s, t = carry[idx]
                    ca, cb, cc = (a_ref[0, d, g, r:r + 1, :] for r in range(3))
                    prev_scr[g, rows(ci, nseq, stride=nc), :] = s
                    out.append((s * ca + t * cb + loc_scr[g, rows(ci, nseq, stride=nc), :],
                                t * ca + s * cc + lsw_scr[g, rows(ci, nseq, stride=nc), :]))
                return tuple(out)

            fin = lax.fori_loop(0, nc, body, init)
            for idx in range(batch):
                fin_ref[0, d, g0 + idx, 0:nseq, :] = fin[idx][0]

    for d in range(2):
        for g in range(S5_G):
            xg = x_scr[g]
            loc_scr[g] = _dot(xg, e_scr[d, g])
            lsw_scr[g] = _dot(xg, esw_scr[d, g])

        @pl.when(step < S5_STEPS_CTX)
        def _():
            scan(d, N_CTX // S5_STEPS_CTX, T_CTX // S5_T)

        @pl.when(step >= S5_STEPS_CTX)
        def _():
            scan(d, N_LAT // S5_STEPS_LAT, T_LAT // S5_T)

        for g in range(S5_G):
            yg = _dot(x_scr[g], tp_scr[d, g]) + _dot_nt(prev_scr[g].astype(BF16), ft_scr[d, g])
            if d == 0:
                y_scr[g] = yg
            else:
                y_scr[g] += yg
    for to, y_to in enumerate(_block_transpose([y_scr[g] for g in range(S5_G)])):
        y_ref[0, rows(to, S5_R, stride=S5_T), :] = y_to[:, :LANES]
        y_ref[1, rows(to, S5_R, stride=S5_T), :] = y_to[:, LANES:]


def _s5(us, tabs, s0, l):
    pows, wrows, decay = tabs
    lay = lambda *shape: pl.BlockSpec((1,) + shape, lambda h: (l,) + (0,) * len(shape))
    op_buf = pltpu.VMEM((2, S5_G, S5_TW, 2 * S5_P), BF16)
    scan_buf = pltpu.VMEM((S5_G, S5_R, 2 * S5_P), F32)
    return pl.pallas_call(
        _s5_kernel,
        grid=(S5_STEPS,),
        in_specs=[_halves_spec(S5_TOK, lambda h: h),
                  lay(2, S5_G, 6, S5_T + 1, 2 * S5_P), lay(2, S5_G, 5, S5_H, 2 * S5_P), lay(2, S5_G, 4, 2 * S5_P),
                  pl.BlockSpec((1, 1, 2, S5_G, S5_SEQ, 4 * S5_P), lambda h: (l, h, 0, 0, 0, 0))],
        out_specs=[_halves_spec(S5_TOK, lambda h: h),
                   pl.BlockSpec((1, 2, S5_G, S5_SEQ, 2 * S5_P), lambda h: (h, 0, 0, 0, 0))],
        out_shape=[_HALVES_SHAPE, jax.ShapeDtypeStruct((S5_STEPS, 2, S5_G, S5_SEQ, 2 * S5_P), F32)],
        scratch_shapes=[op_buf, op_buf, op_buf, pltpu.VMEM((2, S5_G, S5_TW, S5_TW), BF16),
                        pltpu.VMEM((S5_G, S5_R, S5_TW), BF16), pltpu.VMEM((S5_G, S5_R, S5_TW), F32),
                        scan_buf, scan_buf, scan_buf],
        compiler_params=_cparams(("arbitrary",), 56),
        name="s5",
    )(us, pows, wrows, decay, s0)


def _split_bf16(a):
    hi = a.astype(BF16)
    return hi, (a - hi.astype(F32)).astype(BF16)


def _outproj_kernel(*refs, n_x):
    x_refs = refs[:n_x]
    (at_ref, fo_ref, ys_ref, us_ref, d_ref, wg_ref, wo_ref, mod_ref, g_ref, b_ref, wr_ref,
     x1_ref, h2_ref, lg_ref, wob_ref) = refs[n_x:]

    @pl.when(pl.program_id(0) == 0)
    def _():
        wob_ref[...] = wo_ref[0].astype(BF16)

    ys = jnp.concatenate([ys_ref[0], ys_ref[1]], axis=1)
    us = jnp.concatenate([us_ref[0], us_ref[1]], axis=1)
    g = jax.nn.gelu(ys + us * d_ref[0])
    s5 = g * jax.nn.sigmoid(_dot(g.astype(BF16), wg_ref[0].astype(BF16)))
    mix = (_dot(at_ref[...], wob_ref[0:ATTN_W, :])
           + _dot(fo_ref[...], wob_ref[ATTN_W:ATTN_W + FOUR_W, :])
           + _dot(s5.astype(BF16), wob_ref[ATTN_W + FOUR_W:, :]))
    m = mod_ref[0, 0]
    r = ALPHA * _stream_block(x_refs, RB_OUT) + m[2:3] * mix
    rc = r - jnp.mean(r, axis=-1, keepdims=True)
    t = rc * lax.rsqrt(jnp.mean(rc * rc, axis=-1, keepdims=True) + LN_EPS)
    gain, bias = g_ref[0], b_ref[0]
    x1_ref[...] = t * gain + bias
    up = 1.0 + m[4:5]
    hb = (t * (gain * up) + (bias * up + m[3:4])).astype(BF16)
    h2_ref[...] = hb
    lg_ref[...] = _dot_nt(wr_ref[0].astype(BF16), hb)


def _outproj(attn, four, ys5, us, s5_d, w_glu, w_out, x, mod, ln_g, ln_b, w_router_t, l):
    row = lambda w: pl.BlockSpec((RB_OUT, w), lambda i: (i, 0))
    halves = _halves_spec(RB_OUT, lambda i: i)
    lay = lambda *shape: pl.BlockSpec((1,) + shape, lambda i: (l,) + (0,) * len(shape))
    x_specs = _stream_specs(x, RB_OUT)
    return pl.pallas_call(
        functools.partial(_outproj_kernel, n_x=len(x_specs)),
        grid=(ROWS // RB_OUT,),
        in_specs=x_specs + [row(ATTN_W), row(FOUR_W), halves, halves, lay(1, S5_W), lay(S5_W, S5_W), lay(D, D),
                            pl.BlockSpec((1, 1, 6, D), lambda i: (l, _mod_row(i, RB_OUT), 0, 0)),
                            lay(1, D), lay(1, D), lay(N_EXP, D)],
        out_specs=[row(D), row(D), pl.BlockSpec((N_EXP, RB_OUT), lambda i: (0, i))],
        out_shape=[jax.ShapeDtypeStruct((ROWS, D), F32), jax.ShapeDtypeStruct((ROWS, D), BF16),
                   jax.ShapeDtypeStruct((N_EXP, ROWS), F32)],
        scratch_shapes=[pltpu.VMEM((D, D), BF16)],
        compiler_params=_cparams(("arbitrary",), 40),
        name="outproj",
    )(*_stream_args(x), attn, four, ys5, us, s5_d, w_glu, w_out, mod, ln_g, ln_b, w_router_t)


def _router_kernel(lg_ref, crow_ref, grow_ref, *, n, cap, n_seq):
    rows = n_seq * N_EXP
    lg = jnp.concatenate([lg_ref[:, s * n:(s + 1) * n] for s in range(n_seq)], axis=0).reshape(n_seq, N_EXP, n)
    e = jnp.exp(lg - jnp.max(lg, axis=1, keepdims=True))
    aff = (e / jnp.sum(e, axis=1, keepdims=True)).reshape(rows, n)
    thr_bits = jnp.zeros((rows, 1), jnp.int32)
    for bit in range(30, -1, -1):
        cand = thr_bits | (1 << bit)
        cnt = jnp.sum(jnp.where(aff >= lax.bitcast_convert_type(cand, F32), 1.0, 0.0), axis=1, keepdims=True)
        thr_bits = jnp.where(cnt >= cap, cand, thr_bits)
    thr = lax.bitcast_convert_type(thr_bits, F32)
    above = aff > thr
    tied = aff == thr
    need = cap - jnp.sum(jnp.where(above, 1.0, 0.0), axis=1, keepdims=True)
    r0 = lax.broadcasted_iota(jnp.int32, (n, n), 0)
    r1 = lax.broadcasted_iota(jnp.int32, (n, n), 1)
    before = jnp.where(r0 < r1, 1.0, 0.0).astype(BF16)
    tied_rank = _dot(jnp.where(tied, 1.0, 0.0).astype(BF16), before)
    sel = above | (tied & (tied_rank < need))
    slot = _dot(jnp.where(sel, 1.0, 0.0).astype(BF16), before)
    code = jnp.where(sel, slot + 1.0, 0.0)
    gate = jnp.where(sel, aff, 0.0)
    for s in range(n_seq):
        crow_ref[:, s * n:(s + 1) * n] = code[s * N_EXP:(s + 1) * N_EXP]
        grow_ref[:, s * n:(s + 1) * n] = gate[s * N_EXP:(s + 1) * N_EXP]


def _router(lg, n, n_seq, blk, cap):
    table = pl.BlockSpec((N_EXP, n_seq * n), lambda i: (0, 0))
    return pl.pallas_call(
        functools.partial(_router_kernel, n=n, cap=cap, n_seq=n_seq),
        grid=(1,),
        in_specs=[pl.BlockSpec((N_EXP, n_seq * n), lambda i: (0, blk))],
        out_specs=[table, table],
        out_shape=[jax.ShapeDtypeStruct((N_EXP, n_seq * n), F32)] * 2,
        compiler_params=_cparams(("arbitrary",), 48),
        name=f"router_{n}",
    )(lg)


def _gather_kernel(crow_ref, h_ref, o_ref, *, n, cap, n_sub):
    slot1 = (lax.broadcasted_iota(jnp.int32, (cap, n), 0) + 1).astype(F32)
    for j in range(n_sub):
        code = crow_ref[:, j * n:(j + 1) * n]
        onehot = jnp.concatenate(
            [jnp.where(code[e:e + 1, :] == slot1, 1.0, 0.0).astype(BF16) for e in range(N_EXP)], axis=0)
        xs = _dot(onehot, h_ref[j * n:(j + 1) * n, :])
        o_ref[:, j * cap:(j + 1) * cap, :] = xs.reshape(N_EXP, cap, D).astype(BF16)


def _gather(crow, h2, n, n_seq, n_sub, blk_off, cap):
    return pl.pallas_call(
        functools.partial(_gather_kernel, n=n, cap=cap, n_sub=n_sub),
        grid=(n_seq // n_sub,),
        in_specs=[pl.BlockSpec((N_EXP, n_sub * n), lambda s: (0, s)),
                  pl.BlockSpec((n_sub * n, D), lambda s: (blk_off + s, 0))],
        out_specs=pl.BlockSpec((N_EXP, n_sub * cap, D), lambda s: (0, s, 0)),
        out_shape=jax.ShapeDtypeStruct((N_EXP, n_seq * cap, D), BF16),
        compiler_params=_cparams(("arbitrary",), 48),
        name=f"gather_{n}",
    )(crow, h2)


SLOTS = N_CTX * CAP_CTX


def _ffn_kernel(xc_ref, xq_ref, wg_ref, wu0_ref, wu1_ref, wd_ref, yc_ref, yq_ref, wgb_ref, wub_ref, wdb_ref):
    stream = pl.program_id(1)

    @pl.when(stream == 0)
    def _():
        wgb_ref[...] = wg_ref[0, 0].astype(BF16)
        wub_ref[:, :FF // 2] = wu0_ref[0, 0].astype(BF16)
        wub_ref[:, FF // 2:] = wu1_ref[0, 0].astype(BF16)
        wdb_ref[...] = wd_ref[0, 0].astype(BF16)

    def swiglu(x_ref, y_ref):
        x = x_ref[0]
        g = _dot(x, wgb_ref[...])
        u = _dot(x, wub_ref[...])
        hid = (g * jax.nn.sigmoid(g) * u).astype(BF16)
        y_ref[0] = _dot(hid, wdb_ref[...]).astype(BF16)

    @pl.when(stream == 0)
    def _():
        swiglu(xc_ref, yc_ref)

    @pl.when(stream == 1)
    def _():
        swiglu(xq_ref, yq_ref)


def _ffn(xs_c, xs_q, w_gate, w_up, w_down, l):
    xspec = pl.BlockSpec((1, SLOTS, D), lambda e, s: (e, 0, 0))
    ahead = lambda e, s: jnp.minimum(e + s, N_EXP - 1)
    return pl.pallas_call(
        _ffn_kernel,
        grid=(N_EXP, 2),
        in_specs=[xspec, xspec,
                  pl.BlockSpec((1, 1, D, FF), lambda e, s: (l, ahead(e, s), 0, 0)),
                  pl.BlockSpec((1, 1, D, FF // 2), lambda e, s: (l, ahead(e, s), 0, 0)),
                  pl.BlockSpec((1, 1, D, FF // 2), lambda e, s: (l, e, 0, 1)),
                  pl.BlockSpec((1, 1, FF, D), lambda e, s: (l, e, 0, 0))],
        out_specs=[xspec, xspec],
        out_shape=[jax.ShapeDtypeStruct((N_EXP, SLOTS, D), BF16)] * 2,
        scratch_shapes=[pltpu.VMEM((D, FF), BF16), pltpu.VMEM((D, FF), BF16), pltpu.VMEM((FF, D), BF16)],
        compiler_params=_cparams(("arbitrary", "arbitrary"), 56),
        name="ffn",
    )(xs_c, xs_q, w_gate, w_up, w_up, w_down)


def _combine_kernel(crow_ref, grow_ref, ys_ref, x_ref, mod_ref, g_ref, b_ref, o_ref, *, cap, rb, n_sub):
    width = N_EXP * cap
    slot1 = (lax.broadcasted_iota(jnp.int32, (cap, rb), 0) + 1).astype(F32)
    m = mod_ref[0, 0]
    for j in range(n_sub):
        rows = slice(j * rb, (j + 1) * rb)
        code, gate = crow_ref[:, rows], grow_ref[:, rows]
        weights_t = jnp.concatenate(
            [jnp.where(code[e:e + 1, :] == slot1, gate[e:e + 1, :], 0.0).astype(BF16) for e in range(N_EXP)], axis=0)
        moe = _dot_tn(weights_t, ys_ref[:, j * cap:(j + 1) * cap, :].reshape(width, D))
        o_ref[rows, :] = _layer_norm(ALPHA * x_ref[rows, :] + m[5:6] * moe, g_ref[0], b_ref[0])


def _combine(crow, grow, ys, x1, mod, ln_g, ln_b, l, cap, n_seq, n_sub, rb, rb_per_seq, rb_off, in_place):
    assert n_sub == 1 or rb_per_seq == 1
    rows = n_sub * rb
    lay = lambda *shape: pl.BlockSpec((1,) + shape, lambda s, r: (l,) + (0,) * len(shape))
    table = pl.BlockSpec((N_EXP, rows), lambda s, r: (0, s * rb_per_seq + r))
    block = lambda s, r: rb_off + s * rb_per_seq + r
    out_off = rb_off if in_place else 0
    return pl.pallas_call(
        functools.partial(_combine_kernel, cap=cap, rb=rb, n_sub=n_sub),
        grid=(n_seq // n_sub, rb_per_seq),
        in_specs=[table, table,
                  pl.BlockSpec((N_EXP, n_sub * cap, D), lambda s, r: (0, s, 0)),
                  pl.BlockSpec((rows, D), lambda s, r: (block(s, r), 0)),
                  pl.BlockSpec((1, 1, 6, D), lambda s, r: (l, _mod_row(block(s, r), rows), 0, 0)),
                  lay(1, D), lay(1, D)],
        out_specs=pl.BlockSpec((rows, D), lambda s, r: (block(s, r) - rb_off + out_off, 0)),
        out_shape=jax.ShapeDtypeStruct((ROWS if in_place else n_seq * rb_per_seq * rb, D), F32),
        input_output_aliases={3: 0} if in_place else {},
        compiler_params=_cparams(("arbitrary", "arbitrary"), 48),
        name=f"combine_{cap}",
    )(crow, grow, ys, x1, mod, ln_g, ln_b)


def kernel(x_prompt, x_sample, cache_k, cache_v, state_s5_re, state_s5_im, c, c_ctx, ln_in_g, ln_in_b, w_ada, b_ada,
           w_in, w_fourier, attn_sink, s5_a_re, s5_a_im, s5_log_dt, s5_b_re, s5_b_im, s5_c_re, s5_c_im, s5_d,
           s5_w_glu, w_out, ln1_g, ln1_b, w_router, w_gate, w_up, w_down, ln2_g, ln2_b):
    x = (x_prompt.reshape(ROWS_CTX, D), x_sample.reshape(ROWS - ROWS_CTX, D), ln_in_g.reshape(1, D), ln_in_b.reshape(1, D))
    cond8 = jnp.concatenate([c_ctx[None], c, jnp.zeros((8 - 1 - N_LAT, D), F32)], axis=0)
    mod = _ada(cond8, w_ada, b_ada).reshape(DEPTH, 8, 6, D)

    rope_tabs = _rope_tables()
    cs_ctx, cs_lat = _dft_tables(T_CTX), _dft_tables(T_LAT)
    c64, s64 = _dft_channel_tables()
    cache_k = cache_k.reshape(N_LAT, DEPTH, PAST, KV_W)
    cache_v = cache_v.reshape(N_LAT, DEPTH, PAST, KV_W)
    w_router_t = jnp.swapaxes(w_router, 1, 2)
    sink_b = jnp.broadcast_to(attn_sink[:, :, None], (DEPTH, N_HEADS, LANES))
    s5_tabs = _s5_tables(s5_a_re, s5_a_im, s5_log_dt, s5_b_re, s5_b_im, s5_c_re, s5_c_im)
    lat_per_step = N_LAT // S5_STEPS_LAT
    s0 = jnp.concatenate([state_s5_re, state_s5_im, state_s5_im, state_s5_re], axis=-1)
    s0 = s0.reshape(S5_STEPS_LAT, lat_per_step, DEPTH, 2, S5_G, 4 * S5_P).transpose(2, 0, 3, 4, 1, 5)
    s0 = jnp.pad(s0, ((0, 0), (S5_STEPS_CTX, 0), (0, 0), (0, 0), (0, S5_SEQ - lat_per_step), (0, 0)))
    ln1_g, ln1_b, ln2_g, ln2_b, s5_d = (a.reshape(DEPTH, 1, -1) for a in (ln1_g, ln1_b, ln2_g, ln2_b, s5_d))

    new_s = []
    caches = (jnp.zeros(CACHE_SHAPE.shape, F32), jnp.zeros(CACHE_SHAPE.shape, F32))
    for l in range(DEPTH):
        q, k, v, uf, us, *caches = _inproj(x, mod, w_in, rope_tabs, caches, l)
        attn = _attn_lat(_attn_ctx(q, k, v, sink_b[l]), k, v, cache_k, cache_v, sink_b[l], l)
        four = _fourier(uf, cs_ctx, c64, s64, w_fourier, l, T_CTX, N_CTX, SEQ_PER_STEP_CTX, 0)
        four = _fourier(four, cs_lat, c64, s64, w_fourier, l, T_LAT, N_LAT, 1, ROWS_CTX // T_LAT)
        y5, fin = _s5(us, s5_tabs, s0, l)
        new_s.append(fin[:S5_STEPS_CTX].transpose(0, 3, 1, 2, 4).reshape(N_CTX, 2, S5_G, 2 * S5_P))
        x1, h2, lg = _outproj(attn, four, y5, us, s5_d, s5_w_glu, w_out, x, mod, ln1_g, ln1_b, w_router_t, l)
        crow_c, grow_c = _router(lg, T_CTX, N_CTX, 0, CAP_CTX)
        crow_q, grow_q = _router(lg, T_LAT, N_LAT, 1, CAP_LAT)
        xs_c = _gather(crow_c, h2, T_CTX, N_CTX, SEQ_PER_STEP_CTX, 0, CAP_CTX)
        xs_q = _gather(crow_q, h2, T_LAT, N_LAT, 1, ROWS_CTX // T_LAT, CAP_LAT)
        ys_c, ys_q = _ffn(xs_c, xs_q, w_gate, w_up, w_down, l)
        args_c = (crow_c, grow_c, ys_c), (mod, ln2_g, ln2_b, l, CAP_CTX, N_CTX, SEQ_PER_STEP_CTX, T_CTX, 1, 0)
        args_q = (crow_q, grow_q, ys_q), (mod, ln2_g, ln2_b, l, CAP_LAT, N_LAT, 1, RB_COMBINE_LAT,
                                          T_LAT // RB_COMBINE_LAT, ROWS_CTX // RB_COMBINE_LAT)
        if l < DEPTH - 1:
            x = _combine(*args_c[0], x1, *args_c[1], in_place=True)
            x = _combine(*args_q[0], x, *args_q[1], in_place=True)
        else:
            y_ctx = _combine(*args_c[0], x1, *args_c[1], in_place=False)
            y_lat = _combine(*args_q[0], x1, *args_q[1], in_place=False)

    new_s = jnp.stack(new_s, axis=1)

    def cache(feat_major):
        return feat_major.reshape(N_CTX, DEPTH, N_KV, HD, T_CTX).transpose(0, 1, 4, 2, 3)

    return (y_ctx.reshape(N_CTX, T_CTX, D), y_lat.reshape(N_LAT, T_LAT, D),
            cache(caches[0]), cache(caches[1]), new_s[..., :S5_P], new_s[..., S5_P:])
```

```python
import functools

import jax
import jax.numpy as jnp
import numpy as np
from jax import lax
from jax.experimental import pallas as pl
from jax.experimental.pallas import tpu as pltpu

F32 = jnp.float32
BF16 = jnp.bfloat16

D = 1024
N_CTX, T_CTX = 16, 256
N_LAT, T_LAT = 4, 1024
ROWS_CTX = N_CTX * T_CTX
ROWS = ROWS_CTX + N_LAT * T_LAT
DEPTH = 4
PAST = 512
GRID_W = 64
N_HEADS, N_KV, HD = 8, 2, 64
GQA = N_HEADS // N_KV
ATTN_W, KV_W = N_HEADS * HD, N_KV * HD
LOG2E = 1.4426950408889634
Q_SCALE = HD ** -0.5 * LOG2E
WINDOW = 128
FOUR_H, FOUR_W = 4, 256
S5_G, S5_H, S5_P, S5_W = 16, 16, 64, 256
S5_T = 16
IN_W = ATTN_W + 2 * KV_W + FOUR_W + S5_W
N_EXP, FF = 16, 1024
CAP_CTX, CAP_LAT = 2 * T_CTX // N_EXP, 2 * T_LAT // N_EXP
LN_EPS = 1e-5
NEG_INF = -1e30
ALPHA = (2 * DEPTH) ** 0.25
ROPE_BASE = 10000.0
RB_IN = 1024
RB_OUT = 512
RB_COMBINE_LAT = 512
LANES = 128
MIB = 2 ** 20
HIGHEST = lax.Precision.HIGHEST


def _cparams(sem, vmem_mib):
    return pltpu.CompilerParams(dimension_semantics=sem, vmem_limit_bytes=vmem_mib * MIB)


def _dot(a, b):
    return jnp.dot(a, b, preferred_element_type=F32)


def _dot_nt(a, b):
    return lax.dot_general(a, b, (((1,), (1,)), ((), ())), preferred_element_type=F32)


def _dot_tn(a, b):
    return lax.dot_general(a, b, (((0,), (0,)), ((), ())), preferred_element_type=F32)


def _layer_norm(x, g, b):
    mu = jnp.mean(x, axis=-1, keepdims=True)
    xc = x - mu
    var = jnp.mean(xc * xc, axis=-1, keepdims=True)
    return xc * lax.rsqrt(var + LN_EPS) * g + b


def _mod_row(block, block_rows):
    first = block * block_rows
    return jnp.where(first < ROWS_CTX, 0, 1 + (first - ROWS_CTX) // T_LAT)


_HALVES_SHAPE = jax.ShapeDtypeStruct((2, ROWS, LANES), F32)


def _halves_spec(rows, row_block):
    return pl.BlockSpec((2, rows, LANES), lambda *ids: (0, row_block(*ids), 0))


def _stream_specs(x, rows):
    if not isinstance(x, tuple):
        return [pl.BlockSpec((rows, D), lambda i: (i, 0))]
    n_ctx = ROWS_CTX // rows
    vec = pl.BlockSpec((1, D), lambda i: (0, 0))
    return [pl.BlockSpec((rows, D), lambda i: (jnp.minimum(i, n_ctx - 1), 0)),
            pl.BlockSpec((rows, D), lambda i: (jnp.maximum(i - n_ctx, 0), 0)), vec, vec]


def _stream_args(x):
    return x if isinstance(x, tuple) else (x,)


def _stream_block(refs, rows):
    if len(refs) == 1:
        return refs[0][...]
    xc_ref, xq_ref, g_ref, b_ref = refs
    raw = jnp.where(pl.program_id(0) < ROWS_CTX // rows, xc_ref[...], xq_ref[...])
    return _layer_norm(raw, g_ref[...], b_ref[...])


def _ada_kernel(c_ref, w_ref, b_ref, o_ref):
    c = c_ref[...]
    s = (c * jax.nn.sigmoid(c)).astype(BF16)
    o_ref[0] = _dot(s, w_ref[0].astype(BF16)) + b_ref[0]


def _ada(cond8, w_ada, b_ada):
    tn = 1536
    return pl.pallas_call(
        _ada_kernel,
        grid=(DEPTH, 6 * D // tn),
        in_specs=[pl.BlockSpec((8, D), lambda l, j: (0, 0)),
                  pl.BlockSpec((1, D, tn), lambda l, j: (l, 0, j)),
                  pl.BlockSpec((1, 1, tn), lambda l, j: (l, 0, j))],
        out_specs=pl.BlockSpec((1, 8, tn), lambda l, j: (l, 0, j)),
        out_shape=jax.ShapeDtypeStruct((DEPTH, 8, 6 * D), F32),
        compiler_params=_cparams(("arbitrary", "arbitrary"), 40),
        name="adaln",
    )(cond8, w_ada, b_ada.reshape(DEPTH, 1, 6 * D))


def _inproj_kernel(*refs, n_x):
    x_refs = refs[:n_x]
    (mod_ref, w_ref, cos_ref, sin_ref, _, _, q_ref, k_ref, v_ref, uf_ref, us_ref, kc_ref, vc_ref,
     wb_ref, wswb_ref) = refs[n_x:]
    step = pl.program_id(0)
    n_ctx_steps = ROWS_CTX // RB_IN

    @pl.when(step == 0)
    def _():
        wb = w_ref[0].astype(BF16)
        wb_ref[...] = wb
        n_qk = ATTN_W + KV_W
        src = lax.broadcasted_iota(jnp.int32, (n_qk, n_qk), 0)
        dst = lax.broadcasted_iota(jnp.int32, (n_qk, n_qk), 1)
        swap = jnp.where(src == (dst ^ 1), 1.0, 0.0).astype(BF16)
        wswb_ref[...] = _dot(wb[:, :n_qk], swap).astype(BF16)

    m = mod_ref[0, 0]
    hb = (_stream_block(x_refs, RB_IN) * (1.0 + m[1:2]) + m[0:1]).astype(BF16)
    p = _dot(hb, wb_ref[...])
    q = p[:, :ATTN_W]
    k = p[:, ATTN_W:ATTN_W + KV_W]
    v = p[:, ATTN_W + KV_W:ATTN_W + 2 * KV_W]

    psw = _dot(hb, wswb_ref[...])
    cos, sin = cos_ref[...], sin_ref[...]
    wide = lambda t: jnp.concatenate([t] * (ATTN_W // KV_W), axis=1)
    latent = step >= n_ctx_steps
    q = jnp.where(latent, q * wide(cos) + psw[:, :ATTN_W] * wide(sin), q)
    k = jnp.where(latent, k * cos + psw[:, ATTN_W:] * sin, k)
    q_ref[...] = (q * Q_SCALE).astype(BF16)
    k_ref[...] = k.astype(BF16)
    v_ref[...] = v.astype(BF16)

    @pl.when(step < n_ctx_steps)
    def _():
        for j in range(RB_IN // T_CTX):
            rows = slice(j * T_CTX, (j + 1) * T_CTX)
            kc_ref[j, 0] = k[rows].T
            vc_ref[j, 0] = v[rows].T

    uf_ref[...] = p[:, ATTN_W + 2 * KV_W:ATTN_W + 2 * KV_W + FOUR_W].astype(BF16)
    for half in range(S5_W // LANES):
        lo = ATTN_W + 2 * KV_W + FOUR_W + half * LANES
        us_ref[half] = p[:, lo:lo + LANES]


CACHE_SHAPE = jax.ShapeDtypeStruct((N_CTX, DEPTH, KV_W, T_CTX), F32)


def _inproj(x, mod, w_in, rope_tabs, caches, l):
    assert RB_IN == T_LAT
    outs = ((ATTN_W, BF16), (KV_W, BF16), (KV_W, BF16), (FOUR_W, BF16))
    table = pl.BlockSpec((T_LAT, KV_W), lambda i: (0, 0))
    any_space = pl.BlockSpec(memory_space=pl.ANY)
    slab = pl.BlockSpec((RB_IN // T_CTX, 1, KV_W, T_CTX), lambda i: (jnp.minimum(i, ROWS_CTX // RB_IN - 1), l, 0, 0))
    x_specs = _stream_specs(x, RB_IN)
    n_in = len(x_specs) + 4
    return pl.pallas_call(
        functools.partial(_inproj_kernel, n_x=len(x_specs)),
        grid=(ROWS // RB_IN,),
        in_specs=x_specs + [pl.BlockSpec((1, 1, 6, D), lambda i: (l, _mod_row(i, RB_IN), 0, 0)),
                            pl.BlockSpec((1, D, IN_W), lambda i: (l, 0, 0)),
                            table, table, any_space, any_space],
        out_specs=([pl.BlockSpec((RB_IN, w), lambda i: (i, 0)) for w, _ in outs] + [_halves_spec(RB_IN, lambda i: i)]
                   + [slab, slab]),
        out_shape=([jax.ShapeDtypeStruct((ROWS, w), dt) for w, dt in outs] + [_HALVES_SHAPE] + [CACHE_SHAPE] * 2),
        input_output_aliases={n_in: 5, n_in + 1: 6},
        scratch_shapes=[pltpu.VMEM((D, IN_W), BF16), pltpu.VMEM((D, ATTN_W + KV_W), BF16)],
        compiler_params=_cparams(("arbitrary",), 56),
        name="inproj",
    )(*_stream_args(x), mod, w_in, *rope_tabs, *caches)


SINK_ROWS = 16


def _softmax_av(score_blocks, sink, value_blocks):
    m = sink
    for s in score_blocks:
        m = jnp.maximum(m, jnp.max(s, axis=0, keepdims=True))
    first = lax.broadcasted_iota(jnp.int32, (SINK_ROWS, 1), 0) == 0
    probs = [jnp.exp2(s - m).astype(BF16) for s in score_blocks]
    probs.append(jnp.where(first, jnp.exp2(sink - m), 0.0).astype(BF16))
    ext = []
    for v in value_blocks:
        ones = jnp.ones(v.shape, BF16)
        ext.append(jnp.concatenate([v, v, ones, ones], axis=1))
    row = lax.broadcasted_iota(jnp.int32, (SINK_ROWS, 4 * HD), 0)
    lane = lax.broadcasted_iota(jnp.int32, (SINK_ROWS, 4 * HD), 1)
    ext.append(jnp.where((row == 0) & (lane >= 2 * HD), 1.0, 0.0).astype(BF16))
    acc = _dot_tn(jnp.concatenate(probs, axis=0), jnp.concatenate(ext, axis=0))
    return acc[:, :2 * HD] / acc[:, 2 * HD:]


def _stack_group(q, kv):
    return jnp.concatenate([q[:, (kv * GQA + g) * HD:(kv * GQA + g + 1) * HD] for g in range(GQA)], axis=0)


def _sink_row(sink_ref, kv, rows):
    return LOG2E * jnp.concatenate(
        [jnp.broadcast_to(sink_ref[kv * GQA + g:kv * GQA + g + 1, 0:1], (1, rows)) for g in range(GQA)], axis=1)


def _unstack_group(o, rows):
    lane = lax.broadcasted_iota(jnp.int32, (rows, 2 * HD), 1)
    return [jnp.where(lane < HD, o[2 * j * rows:(2 * j + 1) * rows], o[(2 * j + 1) * rows:(2 * j + 2) * rows])
            for j in range(GQA // 2)]


def _attn_ctx_kernel(q_ref, k_ref, v_ref, sink_ref, o_ref):
    q = q_ref[...]
    k = k_ref[...]
    v = v_ref[...]
    outs = []
    for kv in range(N_KV):
        head = slice(kv * HD, (kv + 1) * HD)
        s = _dot_nt(k[:, head], _stack_group(q, kv))
        o = _softmax_av([s], _sink_row(sink_ref, kv, T_CTX), [v[:, head]])
        outs += _unstack_group(o, T_CTX)
    o_ref[...] = jnp.concatenate(outs, axis=1).astype(BF16)


def _attn_ctx(q, k, v, sink_b):
    return pl.pallas_call(
        _attn_ctx_kernel,
        grid=(N_CTX,),
        in_specs=[pl.BlockSpec((T_CTX, ATTN_W), lambda s: (s, 0)),
                  pl.BlockSpec((T_CTX, KV_W), lambda s: (s, 0)),
                  pl.BlockSpec((T_CTX, KV_W), lambda s: (s, 0)),
                  pl.BlockSpec((N_HEADS, LANES), lambda s: (0, 0))],
        out_specs=pl.BlockSpec((T_CTX, ATTN_W), lambda s: (s, 0)),
        out_shape=jax.ShapeDtypeStruct((ROWS, ATTN_W), BF16),
        input_output_aliases={0: 0},
        compiler_params=_cparams(("arbitrary",), 32),
        name="attn_ctx",
    )(q, k, v, sink_b)


def _attn_lat_kernel(q_ref, k_ref, v_ref, ck_ref, cv_ref, sink_ref, o_ref):
    i = pl.program_id(1)
    n_blk = T_LAT // WINDOW
    q = q_ref[...]

    def kv_block(j):
        st = pl.multiple_of(j * WINDOW, WINDOW)
        return k_ref[pl.ds(st, WINDOW), :], v_ref[pl.ds(st, WINDOW), :]

    k0, v0 = kv_block(jnp.maximum(i - 1, 0))
    k1, v1 = kv_block(i)
    k2, v2 = kv_block(jnp.minimum(i + 1, n_blk - 1))
    k_win = jnp.concatenate([k0, k1, k2], axis=0)
    k_ctx = ck_ref[0, 0].astype(BF16)
    v_ctx = cv_ref[0, 0].astype(BF16)
    rows = GQA * WINDOW
    c = lax.broadcasted_iota(jnp.int32, (WINDOW, rows), 0)
    r = lax.broadcasted_iota(jnp.int32, (WINDOW, rows), 1) & (WINDOW - 1)
    far = 2 * WINDOW
    prev_ok = c >= r + jnp.where(i > 0, 0, far)
    next_ok = c + jnp.where(i < n_blk - 1, 0, far) <= r
    outs = []
    for kv in range(N_KV):
        head = slice(kv * HD, (kv + 1) * HD)
        qs = _stack_group(q, kv)
        s_win = _dot_nt(k_win[:, head], qs)
        scores = [jnp.where(prev_ok, s_win[:WINDOW], NEG_INF), s_win[WINDOW:2 * WINDOW],
                  jnp.where(next_ok, s_win[2 * WINDOW:], NEG_INF), _dot_nt(k_ctx[:, head], qs)]
        values = [v0[:, head], v1[:, head], v2[:, head], v_ctx[:, head]]
        outs += _unstack_group(_softmax_av(scores, _sink_row(sink_ref, kv, WINDOW), values), WINDOW)
    o_ref[...] = jnp.concatenate(outs, axis=1).astype(BF16)


def _attn_lat(q, k, v, cache_k, cache_v, sink_b, l):
    n_blk = T_LAT // WINDOW
    off = ROWS_CTX // WINDOW
    return pl.pallas_call(
        _attn_lat_kernel,
        grid=(N_LAT, n_blk),
        in_specs=[pl.BlockSpec((WINDOW, ATTN_W), lambda b, i: (off + b * n_blk + i, 0)),
                  pl.BlockSpec((T_LAT, KV_W), lambda b, i: (ROWS_CTX // T_LAT + b, 0)),
                  pl.BlockSpec((T_LAT, KV_W), lambda b, i: (ROWS_CTX // T_LAT + b, 0)),
                  pl.BlockSpec((1, 1, PAST, KV_W), lambda b, i: (b, l, 0, 0)),
                  pl.BlockSpec((1, 1, PAST, KV_W), lambda b, i: (b, l, 0, 0)),
                  pl.BlockSpec((N_HEADS, LANES), lambda b, i: (0, 0))],
        out_specs=pl.BlockSpec((WINDOW, ATTN_W), lambda b, i: (off + b * n_blk + i, 0)),
        out_shape=jax.ShapeDtypeStruct((ROWS, ATTN_W), BF16),
        input_output_aliases={0: 0},
        compiler_params=_cparams(("arbitrary", "arbitrary"), 40),
        name="attn_lat",
    )(q, k, v, cache_k, cache_v, sink_b)


def _rope_tables():
    rows = T_LAT // GRID_W
    row = jnp.repeat(jnp.arange(rows, dtype=F32), GRID_W)
    col = jnp.tile(jnp.arange(GRID_W, dtype=F32), rows)
    n_freq = HD // 4
    freqs = ROPE_BASE ** (-jnp.arange(n_freq, dtype=F32) / n_freq)
    ang = jnp.concatenate([row[:, None] * freqs, col[:, None] * freqs], axis=-1)
    cos = jnp.repeat(jnp.cos(ang), 2, axis=-1)
    sign = jnp.tile(jnp.array([-1.0, 1.0], F32), HD // 2)
    sin = jnp.repeat(jnp.sin(ang), 2, axis=-1) * sign
    return jnp.tile(cos, (1, N_KV)), jnp.tile(sin, (1, N_KV))


SEQ_PER_STEP_CTX = 4


def _four_kernel(u_ref, cs_ref, c64_ref, s64_ref, wf_ref, o_ref, *, n, n_sub):
    ub = u_ref[...]
    uc = _dot(ub, c64_ref[...]).astype(BF16)
    us = _dot(ub, s64_ref[...]).astype(BF16)
    f = jnp.concatenate([_dot(cs_ref[...], jnp.concatenate([uc[j * n:(j + 1) * n], us[j * n:(j + 1) * n]], axis=0))
                         for j in range(n_sub)], axis=0)
    o_ref[...] = _dot(f.astype(BF16), wf_ref[0].astype(BF16)).astype(BF16)


def _dft_tables(n):
    j = np.arange(n)
    ang = 2.0 * np.pi * ((j[:, None] * j[None, :]) % n) / n
    cs = np.concatenate([np.cos(ang), -np.sin(ang)], axis=1) / np.sqrt(n)
    return jnp.asarray(cs.astype(np.float32)).astype(BF16)


def _dft_channel_tables():
    j = np.arange(HD)
    ang = 2.0 * np.pi * ((j[:, None] * j[None, :]) % HD) / HD
    eye = np.eye(FOUR_H)
    c = np.kron(eye, np.cos(ang)) / np.sqrt(HD)
    s = np.kron(eye, np.sin(ang)) / np.sqrt(HD)
    return jnp.asarray(c.astype(np.float32)).astype(BF16), jnp.asarray(s.astype(np.float32)).astype(BF16)


def _fourier(uf, cs, c64, s64, w_fourier, l, n, n_seq, n_sub, blk_off):
    rows = n_sub * n
    return pl.pallas_call(
        functools.partial(_four_kernel, n=n, n_sub=n_sub),
        grid=(n_seq // n_sub,),
        in_specs=[pl.BlockSpec((rows, FOUR_W), lambda s: (blk_off + s, 0)),
                  pl.BlockSpec((n, 2 * n), lambda s: (0, 0)),
                  pl.BlockSpec((FOUR_W, FOUR_W), lambda s: (0, 0)),
                  pl.BlockSpec((FOUR_W, FOUR_W), lambda s: (0, 0)),
                  pl.BlockSpec((1, FOUR_W, FOUR_W), lambda s: (l, 0, 0))],
        out_specs=pl.BlockSpec((rows, FOUR_W), lambda s: (blk_off + s, 0)),
        out_shape=jax.ShapeDtypeStruct((ROWS, FOUR_W), BF16),
        input_output_aliases={0: 0},
        compiler_params=_cparams(("arbitrary",), 40),
        name=f"fourier_{n}",
    )(uf, cs, c64, s64, w_fourier)


S5_TW = S5_T * S5_H
S5_R = 128
S5_TOK = S5_R * S5_T
S5_STEPS_CTX = ROWS_CTX // S5_TOK
S5_STEPS_LAT = (ROWS - ROWS_CTX) // S5_TOK
S5_STEPS = S5_STEPS_CTX + S5_STEPS_LAT
S5_SEQ = N_CTX // S5_STEPS_CTX


def _s5_tables(a_re, a_im, log_dt, b_re, b_im, c_re, c_im):
    f = lambda a: a.astype(F32)
    a_re, a_im, b_re, b_im, c_re, c_im = map(f, (a_re, a_im, b_re, b_im, c_re, c_im))
    dt = jnp.exp(f(log_dt))[..., None]
    x, y = a_re * dt, a_im * dt
    kk = jnp.arange(S5_T + 1, dtype=F32)[None, None, :, None, None]
    mag = jnp.exp(kk * x[:, :, None])
    pr, pi = mag * jnp.cos(kk * y[:, :, None]), mag * jnp.sin(kk * y[:, :, None])
    nr, ni = pr[:, :, 1] - 1.0, pi[:, :, 1]
    den = a_re * a_re + a_im * a_im
    qr, qi = (nr * a_re + ni * a_im) / den, (ni * a_re - nr * a_im) / den
    bb_r = qr[..., None] * b_re - qi[..., None] * b_im
    bb_i = qr[..., None] * b_im + qi[..., None] * b_re
    cat = lambda u, v: jnp.concatenate([u, v], axis=-1)
    prg, pig = pr.transpose(0, 1, 3, 2, 4), pi.transpose(0, 1, 3, 2, 4)
    def lane_pairs(re, im, patterns):
        coef = lambda k: jnp.asarray(np.repeat(np.array([[p[0][k], p[1][k]] for p in patterns], np.float32),
                                               re.shape[-1], axis=-1))[:, None, :]
        return cat(re, re)[..., None, :, :] * coef(0) + cat(im, im)[..., None, :, :] * coef(1)

    keep_re, keep_im, neg_re, neg_im = (1, 0), (0, 1), (-1, 0), (0, -1)
    pows = lane_pairs(prg, pig, [(keep_re, keep_re), (neg_im, keep_im), (keep_re, neg_im), (neg_im, neg_re),
                                 (keep_re, keep_im), (neg_im, keep_re)])
    bt_r, bt_i = bb_r.transpose(0, 1, 2, 4, 3), bb_i.transpose(0, 1, 2, 4, 3)
    wrows = jnp.concatenate([lane_pairs(bt_r, bt_i, [(keep_re, keep_im), (keep_im, keep_re), (keep_re, neg_im)]),
                             lane_pairs(c_re, c_im, [(keep_re, keep_re), (keep_im, keep_im)])],
                            axis=3)
    ar, ai = pr[:, :, S5_T], pi[:, :, S5_T]
    decay = jnp.stack([cat(ar, ar), cat(-ai, ai), cat(ai, -ai), jnp.zeros_like(cat(ar, ar))], axis=3)
    return pows, wrows, decay


def _block_transpose(arrs):
    arrs = list(arrs)
    width = arrs[0].shape[1]
    blk = lax.broadcasted_iota(jnp.int32, arrs[0].shape, 1) >> 4
    s = len(arrs) // 2
    while s >= 1:
        keep = (blk & s) == 0
        for i in range(len(arrs)):
            if i & s:
                continue
            lo, hi = arrs[i], arrs[i + s]
            arrs[i] = jnp.where(keep, lo, pltpu.roll(hi, s * S5_H, 1))
            arrs[i + s] = jnp.where(keep, pltpu.roll(lo, width - s * S5_H, 1), hi)
        s //= 2
    return arrs


def _s5_build_operators(pw_ref, w_ref, e_scr, esw_scr, ft_scr, tp_scr):
    lane = lax.broadcasted_iota(jnp.int32, (S5_H, S5_TW), 1)
    for d in range(2):
        for g in range(S5_G):
            pw = lambda v, k: pw_ref[0, d, g, v, k:k + 1, :]
            b_ri, b_ir, b_conj, c_rr, c_ii = (w_ref[0, d, g, v] for v in range(5))
            lag_rows = []
            for t in range(S5_T):
                blk = slice(t * S5_H, (t + 1) * S5_H)
                ke = S5_T - 1 - t if d == 0 else t
                e_scr[d, g, blk, :] = (b_ri * pw(0, ke) + b_ir * pw(1, ke)).astype(BF16)
                esw_scr[d, g, blk, :] = (b_ir * pw(0, ke) - b_ri * pw(1, ke)).astype(BF16)
                kf = t + 1 if d == 0 else S5_T - t
                ft_scr[d, g, blk, :] = (c_rr * pw(2, kf) + c_ii * pw(3, kf)).astype(BF16)
                kl = t if d == 0 else S5_T - 1 - t
                lag_rows.append(c_rr * pw(4, kl) + c_ii * pw(5, kl))
            mh, ml = _split_bf16(jnp.concatenate(lag_rows, axis=0))
            bh, bl = _split_bf16(b_conj)
            kt = _dot_nt(bh, mh) + _dot_nt(bh, ml) + _dot_nt(bl, mh)
            for ti in range(S5_T):
                if d == 0:
                    sh = S5_H * ti
                    blk_rows = jnp.where(lane >= sh, pltpu.roll(kt, sh, 1) if sh else kt, 0.0)
                else:
                    sh = S5_H * (S5_T - 1 - ti)
                    blk_rows = jnp.where(lane < S5_TW - sh, pltpu.roll(kt, S5_TW - sh, 1) if sh else kt, 0.0)
                tp_scr[d, g, ti * S5_H:(ti + 1) * S5_H, :] = blk_rows.astype(BF16)


def _s5_kernel(us_ref, pw_ref, w_ref, a_ref, s0_ref, y_ref, fin_ref,
               e_scr, esw_scr, ft_scr, tp_scr, x_scr, y_scr, loc_scr, lsw_scr, prev_scr):
    step = pl.program_id(0)

    @pl.when(step == 0)
    def _():
        _s5_build_operators(pw_ref, w_ref, e_scr, esw_scr, ft_scr, tp_scr)

    rows = pl.ds
    x_by_step = [jnp.concatenate([us_ref[0, rows(tl, S5_R, stride=S5_T), :],
                                  us_ref[1, rows(tl, S5_R, stride=S5_T), :]], axis=1) for tl in range(S5_T)]
    for g, xg in enumerate(_block_transpose(x_by_step)):
        x_scr[g] = xg.astype(BF16)
    fin_ref[...] = jnp.zeros(fin_ref.shape, F32)

    def scan(d, nseq, nc):
        batch = S5_G
        for g0 in range(0, S5_G, batch):
            init = tuple((s0_ref[0, 0, d, g, 0:nseq, :2 * S5_P], s0_ref[0, 0, d, g, 0:nseq, 2 * S5_P:])
                         for g in range(g0, g0 + batch))

            def body(j, carry):
                ci = j if d == 0 else nc - 1 - j
                out = []
                for idx in range(batch):
                    g = g0 + idx
                    s, t = carry[idx]
                    ca, cb, cc = (a_ref[0, d, g, r:r + 1, :] for r in range(3))
                    prev_scr[g, rows(ci, nseq, stride=nc), :] = s
                    out.append((s * ca + t * cb + loc_scr[g, rows(ci, nseq, stride=nc), :],
                                t * ca + s * cc + lsw_scr[g, rows(ci, nseq, stride=nc), :]))
                return tuple(out)

            fin = lax.fori_loop(0, nc, body, init)
            for idx in range(batch):
                fin_ref[0, d, g0 + idx, 0:nseq, :] = fin[idx][0]

    for d in range(2):
        for g in range(S5_G):
            xg = x_scr[g]
            loc_scr[g] = _dot(xg, e_scr[d, g])
            lsw_scr[g] = _dot(xg, esw_scr[d, g])

        @pl.when(step < S5_STEPS_CTX)
        def _():
            scan(d, N_CTX // S5_STEPS_CTX, T_CTX // S5_T)

        @pl.when(step >= S5_STEPS_CTX)
        def _():
            scan(d, N_LAT // S5_STEPS_LAT, T_LAT // S5_T)

        for g in range(S5_G):
            yg = _dot(x_scr[g], tp_scr[d, g]) + _dot_nt(prev_scr[g].astype(BF16), ft_scr[d, g])
            if d == 0:
                y_scr[g] = yg
            else:
                y_scr[g] += yg
    for to, y_to in enumerate(_block_transpose([y_scr[g] for g in range(S5_G)])):
        y_ref[0, rows(to, S5_R, stride=S5_T), :] = y_to[:, :LANES]
        y_ref[1, rows(to, S5_R, stride=S5_T), :] = y_to[:, LANES:]


def _s5(us, tabs, s0, l):
    pows, wrows, decay = tabs
    lay = lambda *shape: pl.BlockSpec((1,) + shape, lambda h: (l,) + (0,) * len(shape))
    op_buf = pltpu.VMEM((2, S5_G, S5_TW, 2 * S5_P), BF16)
    scan_buf = pltpu.VMEM((S5_G, S5_R, 2 * S5_P), F32)
    return pl.pallas_call(
        _s5_kernel,
        grid=(S5_STEPS,),
        in_specs=[_halves_spec(S5_TOK, lambda h: h),
                  lay(2, S5_G, 6, S5_T + 1, 2 * S5_P), lay(2, S5_G, 5, S5_H, 2 * S5_P), lay(2, S5_G, 4, 2 * S5_P),
                  pl.BlockSpec((1, 1, 2, S5_G, S5_SEQ, 4 * S5_P), lambda h: (l, h, 0, 0, 0, 0))],
        out_specs=[_halves_spec(S5_TOK, lambda h: h),
                   pl.BlockSpec((1, 2, S5_G, S5_SEQ, 2 * S5_P), lambda h: (h, 0, 0, 0, 0))],
        out_shape=[_HALVES_SHAPE, jax.ShapeDtypeStruct((S5_STEPS, 2, S5_G, S5_SEQ, 2 * S5_P), F32)],
        scratch_shapes=[op_buf, op_buf, op_buf, pltpu.VMEM((2, S5_G, S5_TW, S5_TW), BF16),
                        pltpu.VMEM((S5_G, S5_R, S5_TW), BF16), pltpu.VMEM((S5_G, S5_R, S5_TW), F32),
                        scan_buf, scan_buf, scan_buf],
        compiler_params=_cparams(("arbitrary",), 56),
        name="s5",
    )(us, pows, wrows, decay, s0)


def _split_bf16(a):
    hi = a.astype(BF16)
    return hi, (a - hi.astype(F32)).astype(BF16)


def _outproj_kernel(*refs, n_x):
    x_refs = refs[:n_x]
    (at_ref, fo_ref, ys_ref, us_ref, d_ref, wg_ref, wo_ref, mod_ref, g_ref, b_ref, wr_ref,
     x1_ref, h2_ref, lg_ref, wob_ref) = refs[n_x:]

    @pl.when(pl.program_id(0) == 0)
    def _():
        wob_ref[...] = wo_ref[0].astype(BF16)

    ys = jnp.concatenate([ys_ref[0], ys_ref[1]], axis=1)
    us = jnp.concatenate([us_ref[0], us_ref[1]], axis=1)
    g = jax.nn.gelu(ys + us * d_ref[0])
    s5 = g * jax.nn.sigmoid(_dot(g.astype(BF16), wg_ref[0].astype(BF16)))
    mix = (_dot(at_ref[...], wob_ref[0:ATTN_W, :])
           + _dot(fo_ref[...], wob_ref[ATTN_W:ATTN_W + FOUR_W, :])
           + _dot(s5.astype(BF16), wob_ref[ATTN_W + FOUR_W:, :]))
    m = mod_ref[0, 0]
    r = ALPHA * _stream_block(x_refs, RB_OUT) + m[2:3] * mix
    rc = r - jnp.mean(r, axis=-1, keepdims=True)
    t = rc * lax.rsqrt(jnp.mean(rc * rc, axis=-1, keepdims=True) + LN_EPS)
    gain, bias = g_ref[0], b_ref[0]
    x1_ref[...] = t * gain + bias
    up = 1.0 + m[4:5]
    hb = (t * (gain * up) + (bias * up + m[3:4])).astype(BF16)
    h2_ref[...] = hb
    lg_ref[...] = _dot_nt(wr_ref[0].astype(BF16), hb)


def _outproj(attn, four, ys5, us, s5_d, w_glu, w_out, x, mod, ln_g, ln_b, w_router_t, l):
    row = lambda w: pl.BlockSpec((RB_OUT, w), lambda i: (i, 0))
    halves = _halves_spec(RB_OUT, lambda i: i)
    lay = lambda *shape: pl.BlockSpec((1,) + shape, lambda i: (l,) + (0,) * len(shape))
    x_specs = _stream_specs(x, RB_OUT)
    return pl.pallas_call(
        functools.partial(_outproj_kernel, n_x=len(x_specs)),
        grid=(ROWS // RB_OUT,),
        in_specs=x_specs + [row(ATTN_W), row(FOUR_W), halves, halves, lay(1, S5_W), lay(S5_W, S5_W), lay(D, D),
                            pl.BlockSpec((1, 1, 6, D), lambda i: (l, _mod_row(i, RB_OUT), 0, 0)),
                            lay(1, D), lay(1, D), lay(N_EXP, D)],
        out_specs=[row(D), row(D), pl.BlockSpec((N_EXP, RB_OUT), lambda i: (0, i))],
        out_shape=[jax.ShapeDtypeStruct((ROWS, D), F32), jax.ShapeDtypeStruct((ROWS, D), BF16),
                   jax.ShapeDtypeStruct((N_EXP, ROWS), F32)],
        scratch_shapes=[pltpu.VMEM((D, D), BF16)],
        compiler_params=_cparams(("arbitrary",), 40),
        name="outproj",
    )(*_stream_args(x), attn, four, ys5, us, s5_d, w_glu, w_out, mod, ln_g, ln_b, w_router_t)


def _router_kernel(lg_ref, crow_ref, grow_ref, *, n, cap, n_seq):
    rows = n_seq * N_EXP
    lg = jnp.concatenate([lg_ref[:, s * n:(s + 1) * n] for s in range(n_seq)], axis=0).reshape(n_seq, N_EXP, n)
    e = jnp.exp(lg - jnp.max(lg, axis=1, keepdims=True))
    aff = (e / jnp.sum(e, axis=1, keepdims=True)).reshape(rows, n)
    thr_bits = jnp.zeros((rows, 1), jnp.int32)
    for bit in range(30, -1, -1):
        cand = thr_bits | (1 << bit)
        cnt = jnp.sum(jnp.where(aff >= lax.bitcast_convert_type(cand, F32), 1.0, 0.0), axis=1, keepdims=True)
        thr_bits = jnp.where(cnt >= cap, cand, thr_bits)
    thr = lax.bitcast_convert_type(thr_bits, F32)
    above = aff > thr
    tied = aff == thr
    need = cap - jnp.sum(jnp.where(above, 1.0, 0.0), axis=1, keepdims=True)
    r0 = lax.broadcasted_iota(jnp.int32, (n, n), 0)
    r1 = lax.broadcasted_iota(jnp.int32, (n, n), 1)
    before = jnp.where(r0 < r1, 1.0, 0.0).astype(BF16)
    tied_rank = _dot(jnp.where(tied, 1.0, 0.0).astype(BF16), before)
    sel = above | (tied & (tied_rank < need))
    slot = _dot(jnp.where(sel, 1.0, 0.0).astype(BF16), before)
    code = jnp.where(sel, slot + 1.0, 0.0)
    gate = jnp.where(sel, aff, 0.0)
    for s in range(n_seq):
        crow_ref[:, s * n:(s + 1) * n] = code[s * N_EXP:(s + 1) * N_EXP]
        grow_ref[:, s * n:(s + 1) * n] = gate[s * N_EXP:(s + 1) * N_EXP]


def _router(lg, n, n_seq, blk, cap):
    table = pl.BlockSpec((N_EXP, n_seq * n), lambda i: (0, 0))
    return pl.pallas_call(
        functools.partial(_router_kernel, n=n, cap=cap, n_seq=n_seq),
        grid=(1,),
        in_specs=[pl.BlockSpec((N_EXP, n_seq * n), lambda i: (0, blk))],
        out_specs=[table, table],
        out_shape=[jax.ShapeDtypeStruct((N_EXP, n_seq * n), F32)] * 2,
        compiler_params=_cparams(("arbitrary",), 48),
        name=f"router_{n}",
    )(lg)


def _gather_kernel(crow_ref, h_ref, o_ref, *, n, cap, n_sub):
    slot1 = (lax.broadcasted_iota(jnp.int32, (cap, n), 0) + 1).astype(F32)
    for j in range(n_sub):
        code = crow_ref[:, j * n:(j + 1) * n]
        onehot = jnp.concatenate(
            [jnp.where(code[e:e + 1, :] == slot1, 1.0, 0.0).astype(BF16) for e in range(N_EXP)], axis=0)
        xs = _dot(onehot, h_ref[j * n:(j + 1) * n, :])
        o_ref[:, j * cap:(j + 1) * cap, :] = xs.reshape(N_EXP, cap, D).astype(BF16)


def _gather(crow, h2, n, n_seq, n_sub, blk_off, cap):
    return pl.pallas_call(
        functools.partial(_gather_kernel, n=n, cap=cap, n_sub=n_sub),
        grid=(n_seq // n_sub,),
        in_specs=[pl.BlockSpec((N_EXP, n_sub * n), lambda s: (0, s)),
                  pl.BlockSpec((n_sub * n, D), lambda s: (blk_off + s, 0))],
        out_specs=pl.BlockSpec((N_EXP, n_sub * cap, D), lambda s: (0, s, 0)),
        out_shape=jax.ShapeDtypeStruct((N_EXP, n_seq * cap, D), BF16),
        compiler_params=_cparams(("arbitrary",), 48),
        name=f"gather_{n}",
    )(crow, h2)


SLOTS = N_CTX * CAP_CTX


def _ffn_kernel(xc_ref, xq_ref, wg_ref, wu0_ref, wu1_ref, wd_ref, yc_ref, yq_ref, wgb_ref, wub_ref, wdb_ref):
    stream = pl.program_id(1)

    @pl.when(stream == 0)
    def _():
        wgb_ref[...] = wg_ref[0, 0].astype(BF16)
        wub_ref[:, :FF // 2] = wu0_ref[0, 0].astype(BF16)
        wub_ref[:, FF // 2:] = wu1_ref[0, 0].astype(BF16)
        wdb_ref[...] = wd_ref[0, 0].astype(BF16)

    def swiglu(x_ref, y_ref):
        x = x_ref[0]
        g = _dot(x, wgb_ref[...])
        u = _dot(x, wub_ref[...])
        hid = (g * jax.nn.sigmoid(g) * u).astype(BF16)
        y_ref[0] = _dot(hid, wdb_ref[...]).astype(BF16)

    @pl.when(stream == 0)
    def _():
        swiglu(xc_ref, yc_ref)

    @pl.when(stream == 1)
    def _():
        swiglu(xq_ref, yq_ref)


def _ffn(xs_c, xs_q, w_gate, w_up, w_down, l):
    xspec = pl.BlockSpec((1, SLOTS, D), lambda e, s: (e, 0, 0))
    ahead = lambda e, s: jnp.minimum(e + s, N_EXP - 1)
    return pl.pallas_call(
        _ffn_kernel,
        grid=(N_EXP, 2),
        in_specs=[xspec, xspec,
                  pl.BlockSpec((1, 1, D, FF), lambda e, s: (l, ahead(e, s), 0, 0)),
                  pl.BlockSpec((1, 1, D, FF // 2), lambda e, s: (l, ahead(e, s), 0, 0)),
                  pl.BlockSpec((1, 1, D, FF // 2), lambda e, s: (l, e, 0, 1)),
                  pl.BlockSpec((1, 1, FF, D), lambda e, s: (l, e, 0, 0))],
        out_specs=[xspec, xspec],
        out_shape=[jax.ShapeDtypeStruct((N_EXP, SLOTS, D), BF16)] * 2,
        scratch_shapes=[pltpu.VMEM((D, FF), BF16), pltpu.VMEM((D, FF), BF16), pltpu.VMEM((FF, D), BF16)],
        compiler_params=_cparams(("arbitrary", "arbitrary"), 56),
        name="ffn",
    )(xs_c, xs_q, w_gate, w_up, w_up, w_down)


def _combine_kernel(crow_ref, grow_ref, ys_ref, x_ref, mod_ref, g_ref, b_ref, o_ref, *, cap, rb, n_sub):
    width = N_EXP * cap
    slot1 = (lax.broadcasted_iota(jnp.int32, (cap, rb), 0) + 1).astype(F32)
    m = mod_ref[0, 0]
    for j in range(n_sub):
        rows = slice(j * rb, (j + 1) * rb)
        code, gate = crow_ref[:, rows], grow_ref[:, rows]
        weights_t = jnp.concatenate(
            [jnp.where(code[e:e + 1, :] == slot1, gate[e:e + 1, :], 0.0).astype(BF16) for e in range(N_EXP)], axis=0)
        moe = _dot_tn(weights_t, ys_ref[:, j * cap:(j + 1) * cap, :].reshape(width, D))
        o_ref[rows, :] = _layer_norm(ALPHA * x_ref[rows, :] + m[5:6] * moe, g_ref[0], b_ref[0])


def _combine(crow, grow, ys, x1, mod, ln_g, ln_b, l, cap, n_seq, n_sub, rb, rb_per_seq, rb_off, in_place):
    assert n_sub == 1 or rb_per_seq == 1
    rows = n_sub * rb
    lay = lambda *shape: pl.BlockSpec((1,) + shape, lambda s, r: (l,) + (0,) * len(shape))
    table = pl.BlockSpec((N_EXP, rows), lambda s, r: (0, s * rb_per_seq + r))
    block = lambda s, r: rb_off + s * rb_per_seq + r
    out_off = rb_off if in_place else 0
    return pl.pallas_call(
        functools.partial(_combine_kernel, cap=cap, rb=rb, n_sub=n_sub),
        grid=(n_seq // n_sub, rb_per_seq),
        in_specs=[table, table,
                  pl.BlockSpec((N_EXP, n_sub * cap, D), lambda s, r: (0, s, 0)),
                  pl.BlockSpec((rows, D), lambda s, r: (block(s, r), 0)),
                  pl.BlockSpec((1, 1, 6, D), lambda s, r: (l, _mod_row(block(s, r), rows), 0, 0)),
                  lay(1, D), lay(1, D)],
        out_specs=pl.BlockSpec((rows, D), lambda s, r: (block(s, r) - rb_off + out_off, 0)),
        out_shape=jax.ShapeDtypeStruct((ROWS if in_place else n_seq * rb_per_seq * rb, D), F32),
        input_output_aliases={3: 0} if in_place else {},
        compiler_params=_cparams(("arbitrary", "arbitrary"), 48),
        name=f"combine_{cap}",
    )(crow, grow, ys, x1, mod, ln_g, ln_b)


def kernel(x_prompt, x_sample, cache_k, cache_v, state_s5_re, state_s5_im, c, c_ctx, ln_in_g, ln_in_b, w_ada, b_ada,
           w_in, w_fourier, attn_sink, s5_a_re, s5_a_im, s5_log_dt, s5_b_re, s5_b_im, s5_c_re, s5_c_im, s5_d,
           s5_w_glu, w_out, ln1_g, ln1_b, w_router, w_gate, w_up, w_down, ln2_g, ln2_b):
    x = (x_prompt.reshape(ROWS_CTX, D), x_sample.reshape(ROWS - ROWS_CTX, D), ln_in_g.reshape(1, D), ln_in_b.reshape(1, D))
    cond8 = jnp.concatenate([c_ctx[None], c, jnp.zeros((8 - 1 - N_LAT, D), F32)], axis=0)
    mod = _ada(cond8, w_ada, b_ada).reshape(DEPTH, 8, 6, D)

    rope_tabs = _rope_tables()
    cs_ctx, cs_lat = _dft_tables(T_CTX), _dft_tables(T_LAT)
    c64, s64 = _dft_channel_tables()
    cache_k = cache_k.reshape(N_LAT, DEPTH, PAST, KV_W)
    cache_v = cache_v.reshape(N_LAT, DEPTH, PAST, KV_W)
    w_router_t = jnp.swapaxes(w_router, 1, 2)
    sink_b = jnp.broadcast_to(attn_sink[:, :, None], (DEPTH, N_HEADS, LANES))
    s5_tabs = _s5_tables(s5_a_re, s5_a_im, s5_log_dt, s5_b_re, s5_b_im, s5_c_re, s5_c_im)
    lat_per_step = N_LAT // S5_STEPS_LAT
    s0 = jnp.concatenate([state_s5_re, state_s5_im, state_s5_im, state_s5_re], axis=-1)
    s0 = s0.reshape(S5_STEPS_LAT, lat_per_step, DEPTH, 2, S5_G, 4 * S5_P).transpose(2, 0, 3, 4, 1, 5)
    s0 = jnp.pad(s0, ((0, 0), (S5_STEPS_CTX, 0), (0, 0), (0, 0), (0, S5_SEQ - lat_per_step), (0, 0)))
    ln1_g, ln1_b, ln2_g, ln2_b, s5_d = (a.reshape(DEPTH, 1, -1) for a in (ln1_g, ln1_b, ln2_g, ln2_b, s5_d))

    new_s = []
    caches = (jnp.zeros(CACHE_SHAPE.shape, F32), jnp.zeros(CACHE_SHAPE.shape, F32))
    for l in range(DEPTH):
        q, k, v, uf, us, *caches = _inproj(x, mod, w_in, rope_tabs, caches, l)
        attn = _attn_lat(_attn_ctx(q, k, v, sink_b[l]), k, v, cache_k, cache_v, sink_b[l], l)
        four = _fourier(uf, cs_ctx, c64, s64, w_fourier, l, T_CTX, N_CTX, SEQ_PER_STEP_CTX, 0)
        four = _fourier(four, cs_lat, c64, s64, w_fourier, l, T_LAT, N_LAT, 1, ROWS_CTX // T_LAT)
        y5, fin = _s5(us, s5_tabs, s0, l)
        new_s.append(fin[:S5_STEPS_CTX].transpose(0, 3, 1, 2, 4).reshape(N_CTX, 2, S5_G, 2 * S5_P))
        x1, h2, lg = _outproj(attn, four, y5, us, s5_d, s5_w_glu, w_out, x, mod, ln1_g, ln1_b, w_router_t, l)
        crow_c, grow_c = _router(lg, T_CTX, N_CTX, 0, CAP_CTX)
        crow_q, grow_q = _router(lg, T_LAT, N_LAT, 1, CAP_LAT)
        xs_c = _gather(crow_c, h2, T_CTX, N_CTX, SEQ_PER_STEP_CTX, 0, CAP_CTX)
        xs_q = _gather(crow_q, h2, T_LAT, N_LAT, 1, ROWS_CTX // T_LAT, CAP_LAT)
        ys_c, ys_q = _ffn(xs_c, xs_q, w_gate, w_up, w_down, l)
        args_c = (crow_c, grow_c, ys_c), (mod, ln2_g, ln2_b, l, CAP_CTX, N_CTX, SEQ_PER_STEP_CTX, T_CTX, 1, 0)
        args_q = (crow_q, grow_q, ys_q), (mod, ln2_g, ln2_b, l, CAP_LAT, N_LAT, 1, RB_COMBINE_LAT,
                                          T_LAT // RB_COMBINE_LAT, ROWS_CTX // RB_COMBINE_LAT)
        if l < DEPTH - 1:
            x = _combine(*args_c[0], x1, *args_c[1], in_place=True)
            x = _combine(*args_q[0], x, *args_q[1], in_place=True)
        else:
            y_ctx = _combine(*args_c[0], x1, *args_c[1], in_place=False)
            y_lat = _combine(*args_q[0], x1, *args_q[1], in_place=False)

    new_s = jnp.stack(new_s, axis=1)

    def cache(feat_major):
        return feat_major.reshape(N_CTX, DEPTH, N_KV, HD, T_CTX).transpose(0, 1, 4, 2, 3)

    return (y_ctx.reshape(N_CTX, T_CTX, D), y_lat.reshape(N_LAT, T_LAT, D),
            cache(caches[0]), cache(caches[1]), new_s[..., :S5_P], new_s[..., S5_P:])
```

```python
import functools

import jax
import jax.numpy as jnp
import numpy as np
from jax import lax
from jax.experimental import pallas as pl
from jax.experimental.pallas import tpu as pltpu

F32 = jnp.float32
BF16 = jnp.bfloat16

D = 1024
N_CTX, T_CTX = 16, 256
N_LAT, T_LAT = 4, 1024
ROWS_CTX = N_CTX * T_CTX
ROWS = ROWS_CTX + N_LAT * T_LAT
DEPTH = 4
PAST = 512
GRID_W = 64
N_HEADS, N_KV, HD = 8, 2, 64
GQA = N_HEADS // N_KV
ATTN_W, KV_W = N_HEADS * HD, N_KV * HD
LOG2E = 1.4426950408889634
Q_SCALE = HD ** -0.5 * LOG2E
WINDOW = 128
FOUR_H, FOUR_W = 4, 256
S5_G, S5_H, S5_P, S5_W = 16, 16, 64, 256
S5_T = 16
IN_W = ATTN_W + 2 * KV_W + FOUR_W + S5_W
N_EXP, FF = 16, 1024
CAP_CTX, CAP_LAT = 2 * T_CTX // N_EXP, 2 * T_LAT // N_EXP
LN_EPS = 1e-5
NEG_INF = -1e30
ALPHA = (2 * DEPTH) ** 0.25
ROPE_BASE = 10000.0
RB_IN = 1024
RB_OUT = 512
RB_COMBINE_LAT = 512
LANES = 128
MIB = 2 ** 20


def _cparams(sem, vmem_mib):
    return pltpu.CompilerParams(dimension_semantics=sem, vmem_limit_bytes=vmem_mib * MIB)


def _dot(a, b):
    return jnp.dot(a, b, preferred_element_type=F32)


def _dot_nt(a, b):
    return lax.dot_general(a, b, (((1,), (1,)), ((), ())), preferred_element_type=F32)


def _dot_tn(a, b):
    return lax.dot_general(a, b, (((0,), (0,)), ((), ())), preferred_element_type=F32)


def _layer_norm(x, g, b):
    mu = jnp.mean(x, axis=-1, keepdims=True)
    xc = x - mu
    var = jnp.mean(xc * xc, axis=-1, keepdims=True)
    return xc * lax.rsqrt(var + LN_EPS) * g + b


def _mod_row(block, block_rows):
    first = block * block_rows
    return jnp.where(first < ROWS_CTX, 0, 1 + (first - ROWS_CTX) // T_LAT)


_HALVES_SHAPE = jax.ShapeDtypeStruct((2, ROWS, LANES), F32)


def _halves_spec(rows, row_block):
    return pl.BlockSpec((2, rows, LANES), lambda *ids: (0, row_block(*ids), 0))


def _stream_specs(x, rows):
    if not isinstance(x, tuple):
        return [pl.BlockSpec((rows, D), lambda i: (i, 0))]
    n_ctx = ROWS_CTX // rows
    vec = pl.BlockSpec((1, D), lambda i: (0, 0))
    return [pl.BlockSpec((rows, D), lambda i: (jnp.minimum(i, n_ctx - 1), 0)),
            pl.BlockSpec((rows, D), lambda i: (jnp.maximum(i - n_ctx, 0), 0)), vec, vec]


def _stream_args(x):
    return x if isinstance(x, tuple) else (x,)


def _stream_block(refs, rows):
    if len(refs) == 1:
        return refs[0][...]
    xc_ref, xq_ref, g_ref, b_ref = refs
    raw = jnp.where(pl.program_id(0) < ROWS_CTX // rows, xc_ref[...], xq_ref[...])
    return _layer_norm(raw, g_ref[...], b_ref[...])


def _ada_kernel(c_ref, w_ref, b_ref, o_ref):
    c = c_ref[...]
    s = (c * jax.nn.sigmoid(c)).astype(BF16)
    o_ref[0] = _dot(s, w_ref[0].astype(BF16)) + b_ref[0]


def _ada(cond8, w_ada, b_ada):
    tn = 1536
    return pl.pallas_call(
        _ada_kernel,
        grid=(DEPTH, 6 * D // tn),
        in_specs=[pl.BlockSpec((8, D), lambda l, j: (0, 0)),
                  pl.BlockSpec((1, D, tn), lambda l, j: (l, 0, j)),
                  pl.BlockSpec((1, 1, tn), lambda l, j: (l, 0, j))],
        out_specs=pl.BlockSpec((1, 8, tn), lambda l, j: (l, 0, j)),
        out_shape=jax.ShapeDtypeStruct((DEPTH, 8, 6 * D), F32),
        compiler_params=_cparams(("arbitrary", "arbitrary"), 40),
        name="adaln",
    )(cond8, w_ada, b_ada.reshape(DEPTH, 1, 6 * D))


def _inproj_kernel(*refs, n_x):
    x_refs = refs[:n_x]
    (mod_ref, w_ref, cos_ref, sin_ref, _, _, q_ref, k_ref, v_ref, uf_ref, us_ref, kc_ref, vc_ref,
     wb_ref, wswb_ref) = refs[n_x:]
    step = pl.program_id(0)
    n_ctx_steps = ROWS_CTX // RB_IN

    @pl.when(step == 0)
    def _():
        wb = w_ref[0].astype(BF16)
        wb_ref[...] = wb
        n_qk = ATTN_W + KV_W
        src = lax.broadcasted_iota(jnp.int32, (n_qk, n_qk), 0)
        dst = lax.broadcasted_iota(jnp.int32, (n_qk, n_qk), 1)
        swap = jnp.where(src == (dst ^ 1), 1.0, 0.0).astype(BF16)
        wswb_ref[...] = _dot(wb[:, :n_qk], swap).astype(BF16)

    m = mod_ref[0, 0]
    hb = (_stream_block(x_refs, RB_IN) * (1.0 + m[1:2]) + m[0:1]).astype(BF16)
    p = _dot(hb, wb_ref[...])
    q = p[:, :ATTN_W]
    k = p[:, ATTN_W:ATTN_W + KV_W]
    v = p[:, ATTN_W + KV_W:ATTN_W + 2 * KV_W]

    psw = _dot(hb, wswb_ref[...])
    cos, sin = cos_ref[...], sin_ref[...]
    wide = lambda t: jnp.concatenate([t] * (ATTN_W // KV_W), axis=1)
    latent = step >= n_ctx_steps
    q = jnp.where(latent, q * wide(cos) + psw[:, :ATTN_W] * wide(sin), q)
    k = jnp.where(latent, k * cos + psw[:, ATTN_W:] * sin, k)
    q_ref[...] = (q * Q_SCALE).astype(BF16)
    k_ref[...] = k.astype(BF16)
    v_ref[...] = v.astype(BF16)

    @pl.when(step < n_ctx_steps)
    def _():
        for j in range(RB_IN // T_CTX):
            rows = slice(j * T_CTX, (j + 1) * T_CTX)
            kc_ref[j, 0] = k[rows].T
            vc_ref[j, 0] = v[rows].T

    uf_ref[...] = p[:, ATTN_W + 2 * KV_W:ATTN_W + 2 * KV_W + FOUR_W].astype(BF16)
    for half in range(S5_W // LANES):
        lo = ATTN_W + 2 * KV_W + FOUR_W + half * LANES
        us_ref[half] = p[:, lo:lo + LANES]


CACHE_SHAPE = jax.ShapeDtypeStruct((N_CTX, DEPTH, KV_W, T_CTX), F32)


def _inproj(x, mod, w_in, rope_tabs, caches, l):
    assert RB_IN == T_LAT
    outs = ((ATTN_W, BF16), (KV_W, BF16), (KV_W, BF16), (FOUR_W, BF16))
    table = pl.BlockSpec((T_LAT, KV_W), lambda i: (0, 0))
    any_space = pl.BlockSpec(memory_space=pl.ANY)
    slab = pl.BlockSpec((RB_IN // T_CTX, 1, KV_W, T_CTX), lambda i: (jnp.minimum(i, ROWS_CTX // RB_IN - 1), l, 0, 0))
    x_specs = _stream_specs(x, RB_IN)
    n_in = len(x_specs) + 4
    return pl.pallas_call(
        functools.partial(_inproj_kernel, n_x=len(x_specs)),
        grid=(ROWS // RB_IN,),
        in_specs=x_specs + [pl.BlockSpec((1, 1, 6, D), lambda i: (l, _mod_row(i, RB_IN), 0, 0)),
                            pl.BlockSpec((1, D, IN_W), lambda i: (l, 0, 0)),
                            table, table, any_space, any_space],
        out_specs=([pl.BlockSpec((RB_IN, w), lambda i: (i, 0)) for w, _ in outs] + [_halves_spec(RB_IN, lambda i: i)]
                   + [slab, slab]),
        out_shape=([jax.ShapeDtypeStruct((ROWS, w), dt) for w, dt in outs] + [_HALVES_SHAPE] + [CACHE_SHAPE] * 2),
        input_output_aliases={n_in: 5, n_in + 1: 6},
        scratch_shapes=[pltpu.VMEM((D, IN_W), BF16), pltpu.VMEM((D, ATTN_W + KV_W), BF16)],
        compiler_params=_cparams(("arbitrary",), 56),
        name="inproj",
    )(*_stream_args(x), mod, w_in, *rope_tabs, *caches)


SINK_ROWS = 16


def _softmax_av(score_blocks, sink, value_blocks):
    m = sink
    for s in score_blocks:
        m = jnp.maximum(m, jnp.max(s, axis=0, keepdims=True))
    first = lax.broadcasted_iota(jnp.int32, (SINK_ROWS, 1), 0) == 0
    probs = [jnp.exp2(s - m).astype(BF16) for s in score_blocks]
    probs.append(jnp.where(first, jnp.exp2(sink - m), 0.0).astype(BF16))
    ext = []
    for v in value_blocks:
        ones = jnp.ones(v.shape, BF16)
        ext.append(jnp.concatenate([v, v, ones, ones], axis=1))
    row = lax.broadcasted_iota(jnp.int32, (SINK_ROWS, 4 * HD), 0)
    lane = lax.broadcasted_iota(jnp.int32, (SINK_ROWS, 4 * HD), 1)
    ext.append(jnp.where((row == 0) & (lane >= 2 * HD), 1.0, 0.0).astype(BF16))
    acc = _dot_tn(jnp.concatenate(probs, axis=0), jnp.concatenate(ext, axis=0))
    return acc[:, :2 * HD] / acc[:, 2 * HD:]


def _stack_group(q, kv):
    return jnp.concatenate([q[:, (kv * GQA + g) * HD:(kv * GQA + g + 1) * HD] for g in range(GQA)], axis=0)


def _sink_row(sink_ref, kv, rows):
    return LOG2E * jnp.concatenate(
        [jnp.broadcast_to(sink_ref[kv * GQA + g:kv * GQA + g + 1, 0:1], (1, rows)) for g in range(GQA)], axis=1)


def _unstack_group(o, rows):
    lane = lax.broadcasted_iota(jnp.int32, (rows, 2 * HD), 1)
    return [jnp.where(lane < HD, o[2 * j * rows:(2 * j + 1) * rows], o[(2 * j + 1) * rows:(2 * j + 2) * rows])
            for j in range(GQA // 2)]


def _attn_ctx_kernel(q_ref, k_ref, v_ref, sink_ref, o_ref):
    q = q_ref[...]
    k = k_ref[...]
    v = v_ref[...]
    outs = []
    for kv in range(N_KV):
        head = slice(kv * HD, (kv + 1) * HD)
        s = _dot_nt(k[:, head], _stack_group(q, kv))
        o = _softmax_av([s], _sink_row(sink_ref, kv, T_CTX), [v[:, head]])
        outs += _unstack_group(o, T_CTX)
    o_ref[...] = jnp.concatenate(outs, axis=1).astype(BF16)


def _attn_ctx(q, k, v, sink_b):
    return pl.pallas_call(
        _attn_ctx_kernel,
        grid=(N_CTX,),
        in_specs=[pl.BlockSpec((T_CTX, ATTN_W), lambda s: (s, 0)),
                  pl.BlockSpec((T_CTX, KV_W), lambda s: (s, 0)),
                  pl.BlockSpec((T_CTX, KV_W), lambda s: (s, 0)),
                  pl.BlockSpec((N_HEADS, LANES), lambda s: (0, 0))],
        out_specs=pl.BlockSpec((T_CTX, ATTN_W), lambda s: (s, 0)),
        out_shape=jax.ShapeDtypeStruct((ROWS, ATTN_W), BF16),
        input_output_aliases={0: 0},
        compiler_params=_cparams(("arbitrary",), 32),
        name="attn_ctx",
    )(q, k, v, sink_b)


def _attn_lat_kernel(q_ref, k_ref, v_ref, ck_ref, cv_ref, sink_ref, o_ref):
    i = pl.program_id(1)
    n_blk = T_LAT // WINDOW
    q = q_ref[...]

    def kv_block(j):
        st = pl.multiple_of(j * WINDOW, WINDOW)
        return k_ref[pl.ds(st, WINDOW), :], v_ref[pl.ds(st, WINDOW), :]

    k0, v0 = kv_block(jnp.maximum(i - 1, 0))
    k1, v1 = kv_block(i)
    k2, v2 = kv_block(jnp.minimum(i + 1, n_blk - 1))
    k_win = jnp.concatenate([k0, k1, k2], axis=0)
    k_ctx = ck_ref[0, 0].astype(BF16)
    v_ctx = cv_ref[0, 0].astype(BF16)
    rows = GQA * WINDOW
    c = lax.broadcasted_iota(jnp.int32, (WINDOW, rows), 0)
    r = lax.broadcasted_iota(jnp.int32, (WINDOW, rows), 1) & (WINDOW - 1)
    far = 2 * WINDOW
    prev_ok = c >= r + jnp.where(i > 0, 0, far)
    next_ok = c + jnp.where(i < n_blk - 1, 0, far) <= r
    outs = []
    for kv in range(N_KV):
        head = slice(kv * HD, (kv + 1) * HD)
        qs = _stack_group(q, kv)
        s_win = _dot_nt(k_win[:, head], qs)
        scores = [jnp.where(prev_ok, s_win[:WINDOW], NEG_INF), s_win[WINDOW:2 * WINDOW],
                  jnp.where(next_ok, s_win[2 * WINDOW:], NEG_INF), _dot_nt(k_ctx[:, head], qs)]
        values = [v0[:, head], v1[:, head], v2[:, head], v_ctx[:, head]]
        outs += _unstack_group(_softmax_av(scores, _sink_row(sink_ref, kv, WINDOW), values), WINDOW)
    o_ref[...] = jnp.concatenate(outs, axis=1).astype(BF16)


def _attn_lat(q, k, v, cache_k, cache_v, sink_b, l):
    n_blk = T_LAT // WINDOW
    off = ROWS_CTX // WINDOW
    return pl.pallas_call(
        _attn_lat_kernel,
        grid=(N_LAT, n_blk),
        in_specs=[pl.BlockSpec((WINDOW, ATTN_W), lambda b, i: (off + b * n_blk + i, 0)),
                  pl.BlockSpec((T_LAT, KV_W), lambda b, i: (ROWS_CTX // T_LAT + b, 0)),
                  pl.BlockSpec((T_LAT, KV_W), lambda b, i: (ROWS_CTX // T_LAT + b, 0)),
                  pl.BlockSpec((1, 1, PAST, KV_W), lambda b, i: (b, l, 0, 0)),
                  pl.BlockSpec((1, 1, PAST, KV_W), lambda b, i: (b, l, 0, 0)),
                  pl.BlockSpec((N_HEADS, LANES), lambda b, i: (0, 0))],
        out_specs=pl.BlockSpec((WINDOW, ATTN_W), lambda b, i: (off + b * n_blk + i, 0)),
        out_shape=jax.ShapeDtypeStruct((ROWS, ATTN_W), BF16),
        input_output_aliases={0: 0},
        compiler_params=_cparams(("arbitrary", "arbitrary"), 40),
        name="attn_lat",
    )(q, k, v, cache_k, cache_v, sink_b)


def _rope_tables():
    rows = T_LAT // GRID_W
    row = jnp.repeat(jnp.arange(rows, dtype=F32), GRID_W)
    col = jnp.tile(jnp.arange(GRID_W, dtype=F32), rows)
    n_freq = HD // 4
    freqs = ROPE_BASE ** (-jnp.arange(n_freq, dtype=F32) / n_freq)
    ang = jnp.concatenate([row[:, None] * freqs, col[:, None] * freqs], axis=-1)
    cos = jnp.repeat(jnp.cos(ang), 2, axis=-1)
    sign = jnp.tile(jnp.array([-1.0, 1.0], F32), HD // 2)
    sin = jnp.repeat(jnp.sin(ang), 2, axis=-1) * sign
    return jnp.tile(cos, (1, N_KV)), jnp.tile(sin, (1, N_KV))


SEQ_PER_STEP_CTX = 4


def _four_kernel(u_ref, cs_ref, c64_ref, s64_ref, wf_ref, o_ref, *, n, n_sub):
    ub = u_ref[...]
    uc = _dot(ub, c64_ref[...]).astype(BF16)
    us = _dot(ub, s64_ref[...]).astype(BF16)
    f = jnp.concatenate([_dot(cs_ref[...], jnp.concatenate([uc[j * n:(j + 1) * n], us[j * n:(j + 1) * n]], axis=0))
                         for j in range(n_sub)], axis=0)
    o_ref[...] = _dot(f.astype(BF16), wf_ref[0].astype(BF16)).astype(BF16)


def _dft_tables(n):
    j = np.arange(n)
    ang = 2.0 * np.pi * ((j[:, None] * j[None, :]) % n) / n
    cs = np.concatenate([np.cos(ang), -np.sin(ang)], axis=1) / np.sqrt(n)
    return jnp.asarray(cs.astype(np.float32)).astype(BF16)


def _dft_channel_tables():
    j = np.arange(HD)
    ang = 2.0 * np.pi * ((j[:, None] * j[None, :]) % HD) / HD
    eye = np.eye(FOUR_H)
    c = np.kron(eye, np.cos(ang)) / np.sqrt(HD)
    s = np.kron(eye, np.sin(ang)) / np.sqrt(HD)
    return jnp.asarray(c.astype(np.float32)).astype(BF16), jnp.asarray(s.astype(np.float32)).astype(BF16)


def _fourier(uf, cs, c64, s64, w_fourier, l, n, n_seq, n_sub, blk_off):
    rows = n_sub * n
    return pl.pallas_call(
        functools.partial(_four_kernel, n=n, n_sub=n_sub),
        grid=(n_seq // n_sub,),
        in_specs=[pl.BlockSpec((rows, FOUR_W), lambda s: (blk_off + s, 0)),
                  pl.BlockSpec((n, 2 * n), lambda s: (0, 0)),
                  pl.BlockSpec((FOUR_W, FOUR_W), lambda s: (0, 0)),
                  pl.BlockSpec((FOUR_W, FOUR_W), lambda s: (0, 0)),
                  pl.BlockSpec((1, FOUR_W, FOUR_W), lambda s: (l, 0, 0))],
        out_specs=pl.BlockSpec((rows, FOUR_W), lambda s: (blk_off + s, 0)),
        out_shape=jax.ShapeDtypeStruct((ROWS, FOUR_W), BF16),
        input_output_aliases={0: 0},
        compiler_params=_cparams(("arbitrary",), 40),
        name=f"fourier_{n}",
    )(uf, cs, c64, s64, w_fourier)


S5_TW = S5_T * S5_H
S5_R = 128
S5_TOK = S5_R * S5_T
S5_STEPS_CTX = ROWS_CTX // S5_TOK
S5_STEPS_LAT = (ROWS - ROWS_CTX) // S5_TOK
S5_STEPS = S5_STEPS_CTX + S5_STEPS_LAT
S5_SEQ = N_CTX // S5_STEPS_CTX


def _s5_tables(a_re, a_im, log_dt, b_re, b_im, c_re, c_im):
    f = lambda a: a.astype(F32)
    a_re, a_im, b_re, b_im, c_re, c_im = map(f, (a_re, a_im, b_re, b_im, c_re, c_im))
    dt = jnp.exp(f(log_dt))[..., None]
    x, y = a_re * dt, a_im * dt
    kk = jnp.arange(S5_T + 1, dtype=F32)[None, None, :, None, None]
    mag = jnp.exp(kk * x[:, :, None])
    pr, pi = mag * jnp.cos(kk * y[:, :, None]), mag * jnp.sin(kk * y[:, :, None])
    nr, ni = pr[:, :, 1] - 1.0, pi[:, :, 1]
    den = a_re * a_re + a_im * a_im
    qr, qi = (nr * a_re + ni * a_im) / den, (ni * a_re - nr * a_im) / den
    bb_r = qr[..., None] * b_re - qi[..., None] * b_im
    bb_i = qr[..., None] * b_im + qi[..., None] * b_re
    cat = lambda u, v: jnp.concatenate([u, v], axis=-1)
    prg, pig = pr.transpose(0, 1, 3, 2, 4), pi.transpose(0, 1, 3, 2, 4)
    def lane_pairs(re, im, patterns):
        coef = lambda k: jnp.asarray(np.repeat(np.array([[p[0][k], p[1][k]] for p in patterns], np.float32),
                                               re.shape[-1], axis=-1))[:, None, :]
        return cat(re, re)[..., None, :, :] * coef(0) + cat(im, im)[..., None, :, :] * coef(1)

    keep_re, keep_im, neg_re, neg_im = (1, 0), (0, 1), (-1, 0), (0, -1)
    pows = lane_pairs(prg, pig, [(keep_re, keep_re), (neg_im, keep_im), (keep_re, neg_im), (neg_im, neg_re),
                                 (keep_re, keep_im), (neg_im, keep_re)])
    bt_r, bt_i = bb_r.transpose(0, 1, 2, 4, 3), bb_i.transpose(0, 1, 2, 4, 3)
    wrows = jnp.concatenate([lane_pairs(bt_r, bt_i, [(keep_re, keep_im), (keep_im, keep_re), (keep_re, neg_im)]),
                             lane_pairs(c_re, c_im, [(keep_re, keep_re), (keep_im, keep_im)])],
                            axis=3)
    ar, ai = pr[:, :, S5_T], pi[:, :, S5_T]
    decay = jnp.stack([cat(ar, ar), cat(-ai, ai), cat(ai, -ai), jnp.zeros_like(cat(ar, ar))], axis=3)
    return pows, wrows, decay


def _block_transpose(arrs):
    arrs = list(arrs)
    width = arrs[0].shape[1]
    blk = lax.broadcasted_iota(jnp.int32, arrs[0].shape, 1) >> 4
    s = len(arrs) // 2
    while s >= 1:
        keep = (blk & s) == 0
        for i in range(len(arrs)):
            if i & s:
                continue
            lo, hi = arrs[i], arrs[i + s]
            arrs[i] = jnp.where(keep, lo, pltpu.roll(hi, s * S5_H, 1))
            arrs[i + s] = jnp.where(keep, pltpu.roll(lo, width - s * S5_H, 1), hi)
        s //= 2
    return arrs


def _s5_build_operators(pw_ref, w_ref, e_scr, esw_scr, ft_scr, tp_scr):
    lane = lax.broadcasted_iota(jnp.int32, (S5_H, S5_TW), 1)
    for d in range(2):
        for g in range(S5_G):
            pw = lambda v, k: pw_ref[0, d, g, v, k:k + 1, :]
            b_ri, b_ir, b_conj, c_rr, c_ii = (w_ref[0, d, g, v] for v in range(5))
            lag_rows = []
            for t in range(S5_T):
                blk = slice(t * S5_H, (t + 1) * S5_H)
                ke = S5_T - 1 - t if d == 0 else t
                e_scr[d, g, blk, :] = (b_ri * pw(0, ke) + b_ir * pw(1, ke)).astype(BF16)
                esw_scr[d, g, blk, :] = (b_ir * pw(0, ke) - b_ri * pw(1, ke)).astype(BF16)
                kf = t + 1 if d == 0 else S5_T - t
                ft_scr[d, g, blk, :] = (c_rr * pw(2, kf) + c_ii * pw(3, kf)).astype(BF16)
                kl = t if d == 0 else S5_T - 1 - t
                lag_rows.append(c_rr * pw(4, kl) + c_ii * pw(5, kl))
            mh, ml = _split_bf16(jnp.concatenate(lag_rows, axis=0))
            bh, bl = _split_bf16(b_conj)
            kt = _dot_nt(bh, mh) + _dot_nt(bh, ml) + _dot_nt(bl, mh)
            for ti in range(S5_T):
                if d == 0:
                    sh = S5_H * ti
                    blk_rows = jnp.where(lane >= sh, pltpu.roll(kt, sh, 1) if sh else kt, 0.0)
                else:
                    sh = S5_H * (S5_T - 1 - ti)
                    blk_rows = jnp.where(lane < S5_TW - sh, pltpu.roll(kt, S5_TW - sh, 1) if sh else kt, 0.0)
                tp_scr[d, g, ti * S5_H:(ti + 1) * S5_H, :] = blk_rows.astype(BF16)


def _s5_kernel(us_ref, pw_ref, w_ref, a_ref, s0_ref, y_ref, fin_ref,
               e_scr, esw_scr, ft_scr, tp_scr, x_scr, y_scr, loc_scr, lsw_scr, prev_scr):
    step = pl.program_id(0)

    @pl.when(step == 0)
    def _():
        _s5_build_operators(pw_ref, w_ref, e_scr, esw_scr, ft_scr, tp_scr)

    rows = pl.ds
    x_by_step = [jnp.concatenate([us_ref[0, rows(tl, S5_R, stride=S5_T), :],
                                  us_ref[1, rows(tl, S5_R, stride=S5_T), :]], axis=1).astype(BF16)
                 for tl in range(S5_T)]
    for g, xg in enumerate(_block_transpose(x_by_step)):
        x_scr[g] = xg
    fin_ref[...] = jnp.zeros(fin_ref.shape, F32)

    def scan(d, nseq, nc):
        batch = S5_G
        for g0 in range(0, S5_G, batch):
            init = tuple((s0_ref[0, 0, d, g, 0:nseq, :2 * S5_P], s0_ref[0, 0, d, g, 0:nseq, 2 * S5_P:])
                         for g in range(g0, g0 + batch))

            def body(j, carry):
                ci = j if d == 0 else nc - 1 - j
                out = []
                for idx in range(batch):
                    g = g0 + idx
                    s, t = carry[idx]
                    ca, cb, cc = (a_ref[0, d, g, r:r + 1, :] for r in range(3))
                    prev_scr[g, rows(ci, nseq, stride=nc), :] = s
                    out.append((s * ca + t * cb + loc_scr[g, rows(ci, nseq, stride=nc), :],
                                t * ca + s * cc + lsw_scr[g, rows(ci, nseq, stride=nc), :]))
                return tuple(out)

            fin = lax.fori_loop(0, nc, body, init)
            for idx in range(batch):
                fin_ref[0, d, g0 + idx, 0:nseq, :] = fin[idx][0]

    for d in range(2):
        for g in range(S5_G):
            xg = x_scr[g]
            loc_scr[g] = _dot(xg, e_scr[d, g])
            lsw_scr[g] = _dot(xg, esw_scr[d, g])

        @pl.when(step < S5_STEPS_CTX)
        def _():
            scan(d, N_CTX // S5_STEPS_CTX, T_CTX // S5_T)

        @pl.when(step >= S5_STEPS_CTX)
        def _():
            scan(d, N_LAT // S5_STEPS_LAT, T_LAT // S5_T)

        for g in range(S5_G):
            yg = _dot(x_scr[g], tp_scr[d, g]) + _dot_nt(prev_scr[g].astype(BF16), ft_scr[d, g])
            if d == 0:
                y_scr[g] = yg
            else:
                y_scr[g] += yg
    for to, y_to in enumerate(_block_transpose([y_scr[g] for g in range(S5_G)])):
        y_ref[0, rows(to, S5_R, stride=S5_T), :] = y_to[:, :LANES]
        y_ref[1, rows(to, S5_R, stride=S5_T), :] = y_to[:, LANES:]


def _s5(us, tabs, s0, l):
    pows, wrows, decay = tabs
    lay = lambda *shape: pl.BlockSpec((1,) + shape, lambda h: (l,) + (0,) * len(shape))
    op_buf = pltpu.VMEM((2, S5_G, S5_TW, 2 * S5_P), BF16)
    scan_buf = pltpu.VMEM((S5_G, S5_R, 2 * S5_P), F32)
    return pl.pallas_call(
        _s5_kernel,
        grid=(S5_STEPS,),
        in_specs=[_halves_spec(S5_TOK, lambda h: h),
                  lay(2, S5_G, 6, S5_T + 1, 2 * S5_P), lay(2, S5_G, 5, S5_H, 2 * S5_P), lay(2, S5_G, 4, 2 * S5_P),
                  pl.BlockSpec((1, 1, 2, S5_G, S5_SEQ, 4 * S5_P), lambda h: (l, h, 0, 0, 0, 0))],
        out_specs=[_halves_spec(S5_TOK, lambda h: h),
                   pl.BlockSpec((1, 2, S5_G, S5_SEQ, 2 * S5_P), lambda h: (h, 0, 0, 0, 0))],
        out_shape=[_HALVES_SHAPE, jax.ShapeDtypeStruct((S5_STEPS, 2, S5_G, S5_SEQ, 2 * S5_P), F32)],
        scratch_shapes=[op_buf, op_buf, op_buf, pltpu.VMEM((2, S5_G, S5_TW, S5_TW), BF16),
                        pltpu.VMEM((S5_G, S5_R, S5_TW), BF16), pltpu.VMEM((S5_G, S5_R, S5_TW), F32),
                        scan_buf, scan_buf, scan_buf],
        compiler_params=_cparams(("arbitrary",), 56),
        name="s5",
    )(us, pows, wrows, decay, s0)


def _split_bf16(a):
    hi = a.astype(BF16)
    return hi, (a - hi.astype(F32)).astype(BF16)


def _outproj_kernel(*refs, n_x):
    x_refs = refs[:n_x]
    (at_ref, fo_ref, ys_ref, us_ref, d_ref, wg_ref, wo_ref, mod_ref, g_ref, b_ref, wr_ref,
     x1_ref, h2_ref, lg_ref, wob_ref) = refs[n_x:]

    @pl.when(pl.program_id(0) == 0)
    def _():
        wob_ref[...] = wo_ref[0].astype(BF16)

    ys = jnp.concatenate([ys_ref[0], ys_ref[1]], axis=1)
    us = jnp.concatenate([us_ref[0], us_ref[1]], axis=1)
    g = jax.nn.gelu(ys + us * d_ref[0])
    s5 = g * jax.nn.sigmoid(_dot(g.astype(BF16), wg_ref[0].astype(BF16)))
    mix = (_dot(at_ref[...], wob_ref[0:ATTN_W, :])
           + _dot(fo_ref[...], wob_ref[ATTN_W:ATTN_W + FOUR_W, :])
           + _dot(s5.astype(BF16), wob_ref[ATTN_W + FOUR_W:, :]))
    m = mod_ref[0, 0]
    r = ALPHA * _stream_block(x_refs, RB_OUT) + m[2:3] * mix
    rc = r - jnp.mean(r, axis=-1, keepdims=True)
    t = rc * lax.rsqrt(jnp.mean(rc * rc, axis=-1, keepdims=True) + LN_EPS)
    gain, bias = g_ref[0], b_ref[0]
    x1_ref[...] = t * gain + bias
    up = 1.0 + m[4:5]
    hb = (t * (gain * up) + (bias * up + m[3:4])).astype(BF16)
    h2_ref[...] = hb
    lg_ref[...] = _dot_nt(wr_ref[0].astype(BF16), hb)


def _outproj(attn, four, ys5, us, s5_d, w_glu, w_out, x, mod, ln_g, ln_b, w_router_t, l):
    row = lambda w: pl.BlockSpec((RB_OUT, w), lambda i: (i, 0))
    halves = _halves_spec(RB_OUT, lambda i: i)
    lay = lambda *shape: pl.BlockSpec((1,) + shape, lambda i: (l,) + (0,) * len(shape))
    x_specs = _stream_specs(x, RB_OUT)
    return pl.pallas_call(
        functools.partial(_outproj_kernel, n_x=len(x_specs)),
        grid=(ROWS // RB_OUT,),
        in_specs=x_specs + [row(ATTN_W), row(FOUR_W), halves, halves, lay(1, S5_W), lay(S5_W, S5_W), lay(D, D),
                            pl.BlockSpec((1, 1, 6, D), lambda i: (l, _mod_row(i, RB_OUT), 0, 0)),
                            lay(1, D), lay(1, D), lay(N_EXP, D)],
        out_specs=[row(D), row(D), pl.BlockSpec((N_EXP, RB_OUT), lambda i: (0, i))],
        out_shape=[jax.ShapeDtypeStruct((ROWS, D), F32), jax.ShapeDtypeStruct((ROWS, D), BF16),
                   jax.ShapeDtypeStruct((N_EXP, ROWS), F32)],
        scratch_shapes=[pltpu.VMEM((D, D), BF16)],
        compiler_params=_cparams(("arbitrary",), 40),
        name="outproj",
    )(*_stream_args(x), attn, four, ys5, us, s5_d, w_glu, w_out, mod, ln_g, ln_b, w_router_t)


def _router_kernel(lg_ref, crow_ref, grow_ref, *, n, cap, n_seq):
    rows = n_seq * N_EXP
    lg = jnp.concatenate([lg_ref[:, s * n:(s + 1) * n] for s in range(n_seq)], axis=0).reshape(n_seq, N_EXP, n)
    e = jnp.exp(lg - jnp.max(lg, axis=1, keepdims=True))
    aff = (e / jnp.sum(e, axis=1, keepdims=True)).reshape(rows, n)
    thr_bits = jnp.zeros((rows, 1), jnp.int32)
    for bit in range(30, -1, -1):
        cand = thr_bits | (1 << bit)
        cnt = jnp.sum(jnp.where(aff >= lax.bitcast_convert_type(cand, F32), 1.0, 0.0), axis=1, keepdims=True)
        thr_bits = jnp.where(cnt >= cap, cand, thr_bits)
    thr = lax.bitcast_convert_type(thr_bits, F32)
    above = aff > thr
    tied = aff == thr
    need = cap - jnp.sum(jnp.where(above, 1.0, 0.0), axis=1, keepdims=True)
    r0 = lax.broadcasted_iota(jnp.int32, (n, n), 0)
    r1 = lax.broadcasted_iota(jnp.int32, (n, n), 1)
    before = jnp.where(r0 < r1, 1.0, 0.0).astype(BF16)
    tied_rank = _dot(jnp.where(tied, 1.0, 0.0).astype(BF16), before)
    sel = above | (tied & (tied_rank < need))
    slot = _dot(jnp.where(sel, 1.0, 0.0).astype(BF16), before)
    code = jnp.where(sel, slot + 1.0, 0.0)
    gate = jnp.where(sel, aff, 0.0)
    for s in range(n_seq):
        crow_ref[:, s * n:(s + 1) * n] = code[s * N_EXP:(s + 1) * N_EXP]
        grow_ref[:, s * n:(s + 1) * n] = gate[s * N_EXP:(s + 1) * N_EXP]


def _router(lg, n, n_seq, blk, cap):
    table = pl.BlockSpec((N_EXP, n_seq * n), lambda i: (0, 0))
    return pl.pallas_call(
        functools.partial(_router_kernel, n=n, cap=cap, n_seq=n_seq),
        grid=(1,),
        in_specs=[pl.BlockSpec((N_EXP, n_seq * n), lambda i: (0, blk))],
        out_specs=[table, table],
        out_shape=[jax.ShapeDtypeStruct((N_EXP, n_seq * n), F32)] * 2,
        compiler_params=_cparams(("arbitrary",), 48),
        name=f"router_{n}",
    )(lg)


def _gather_kernel(crow_ref, h_ref, o_ref, *, n, cap, n_sub):
    slot1 = (lax.broadcasted_iota(jnp.int32, (cap, n), 0) + 1).astype(F32)
    for j in range(n_sub):
        code = crow_ref[:, j * n:(j + 1) * n]
        onehot = jnp.concatenate(
            [jnp.where(code[e:e + 1, :] == slot1, 1.0, 0.0).astype(BF16) for e in range(N_EXP)], axis=0)
        xs = _dot(onehot, h_ref[j * n:(j + 1) * n, :])
        o_ref[:, j * cap:(j + 1) * cap, :] = xs.reshape(N_EXP, cap, D).astype(BF16)


def _gather(crow, h2, n, n_seq, n_sub, blk_off, cap):
    return pl.pallas_call(
        functools.partial(_gather_kernel, n=n, cap=cap, n_sub=n_sub),
        grid=(n_seq // n_sub,),
        in_specs=[pl.BlockSpec((N_EXP, n_sub * n), lambda s: (0, s)),
                  pl.BlockSpec((n_sub * n, D), lambda s: (blk_off + s, 0))],
        out_specs=pl.BlockSpec((N_EXP, n_sub * cap, D), lambda s: (0, s, 0)),
        out_shape=jax.ShapeDtypeStruct((N_EXP, n_seq * cap, D), BF16),
        compiler_params=_cparams(("arbitrary",), 48),
        name=f"gather_{n}",
    )(crow, h2)


SLOTS = N_CTX * CAP_CTX


def _ffn_kernel(xc_ref, xq_ref, wg_ref, wu0_ref, wu1_ref, wd_ref, yc_ref, yq_ref, wgb_ref, wub_ref, wdb_ref):
    stream = pl.program_id(1)

    @pl.when(stream == 0)
    def _():
        wgb_ref[...] = wg_ref[0, 0].astype(BF16)
        wub_ref[:, :FF // 2] = wu0_ref[0, 0].astype(BF16)
        wub_ref[:, FF // 2:] = wu1_ref[0, 0].astype(BF16)
        wdb_ref[...] = wd_ref[0, 0].astype(BF16)

    def swiglu(x_ref, y_ref):
        x = x_ref[0]
        g = _dot(x, wgb_ref[...])
        u = _dot(x, wub_ref[...])
        hid = (g * jax.nn.sigmoid(g) * u).astype(BF16)
        y_ref[0] = _dot(hid, wdb_ref[...]).astype(BF16)

    @pl.when(stream == 0)
    def _():
        swiglu(xc_ref, yc_ref)

    @pl.when(stream == 1)
    def _():
        swiglu(xq_ref, yq_ref)


def _ffn(xs_c, xs_q, w_gate, w_up, w_down, l):
    xspec = pl.BlockSpec((1, SLOTS, D), lambda e, s: (e, 0, 0))
    ahead = lambda e, s: jnp.minimum(e + s, N_EXP - 1)
    return pl.pallas_call(
        _ffn_kernel,
        grid=(N_EXP, 2),
        in_specs=[xspec, xspec,
                  pl.BlockSpec((1, 1, D, FF), lambda e, s: (l, ahead(e, s), 0, 0)),
                  pl.BlockSpec((1, 1, D, FF // 2), lambda e, s: (l, ahead(e, s), 0, 0)),
                  pl.BlockSpec((1, 1, D, FF // 2), lambda e, s: (l, e, 0, 1)),
                  pl.BlockSpec((1, 1, FF, D), lambda e, s: (l, e, 0, 0))],
        out_specs=[xspec, xspec],
        out_shape=[jax.ShapeDtypeStruct((N_EXP, SLOTS, D), BF16)] * 2,
        scratch_shapes=[pltpu.VMEM((D, FF), BF16), pltpu.VMEM((D, FF), BF16), pltpu.VMEM((FF, D), BF16)],
        compiler_params=_cparams(("arbitrary", "arbitrary"), 56),
        name="ffn",
    )(xs_c, xs_q, w_gate, w_up, w_up, w_down)


def _combine_kernel(crow_ref, grow_ref, ys_ref, x_ref, mod_ref, g_ref, b_ref, o_ref, *, cap, rb, n_sub):
    width = N_EXP * cap
    slot1 = (lax.broadcasted_iota(jnp.int32, (cap, rb), 0) + 1).astype(F32)
    m = mod_ref[0, 0]
    for j in range(n_sub):
        rows = slice(j * rb, (j + 1) * rb)
        code, gate = crow_ref[:, rows], grow_ref[:, rows]
        weights_t = jnp.concatenate(
            [jnp.where(code[e:e + 1, :] == slot1, gate[e:e + 1, :], 0.0).astype(BF16) for e in range(N_EXP)], axis=0)
        moe = _dot_tn(weights_t, ys_ref[:, j * cap:(j + 1) * cap, :].reshape(width, D))
        o_ref[rows, :] = _layer_norm(ALPHA * x_ref[rows, :] + m[5:6] * moe, g_ref[0], b_ref[0])


def _combine(crow, grow, ys, x1, mod, ln_g, ln_b, l, cap, n_seq, n_sub, rb, rb_per_seq, rb_off, in_place):
    assert n_sub == 1 or rb_per_seq == 1
    rows = n_sub * rb
    lay = lambda *shape: pl.BlockSpec((1,) + shape, lambda s, r: (l,) + (0,) * len(shape))
    table = pl.BlockSpec((N_EXP, rows), lambda s, r: (0, s * rb_per_seq + r))
    block = lambda s, r: rb_off + s * rb_per_seq + r
    out_off = rb_off if in_place else 0
    return pl.pallas_call(
        functools.partial(_combine_kernel, cap=cap, rb=rb, n_sub=n_sub),
        grid=(n_seq // n_sub, rb_per_seq),
        in_specs=[table, table,
                  pl.BlockSpec((N_EXP, n_sub * cap, D), lambda s, r: (0, s, 0)),
                  pl.BlockSpec((rows, D), lambda s, r: (block(s, r), 0)),
                  pl.BlockSpec((1, 1, 6, D), lambda s, r: (l, _mod_row(block(s, r), rows), 0, 0)),
                  lay(1, D), lay(1, D)],
        out_specs=pl.BlockSpec((rows, D), lambda s, r: (block(s, r) - rb_off + out_off, 0)),
        out_shape=jax.ShapeDtypeStruct((ROWS if in_place else n_seq * rb_per_seq * rb, D), F32),
        input_output_aliases={3: 0} if in_place else {},
        compiler_params=_cparams(("arbitrary", "arbitrary"), 48),
        name=f"combine_{cap}",
    )(crow, grow, ys, x1, mod, ln_g, ln_b)


def kernel(x_prompt, x_sample, cache_k, cache_v, state_s5_re, state_s5_im, c, c_ctx, ln_in_g, ln_in_b, w_ada, b_ada,
           w_in, w_fourier, attn_sink, s5_a_re, s5_a_im, s5_log_dt, s5_b_re, s5_b_im, s5_c_re, s5_c_im, s5_d,
           s5_w_glu, w_out, ln1_g, ln1_b, w_router, w_gate, w_up, w_down, ln2_g, ln2_b):
    x = (x_prompt.reshape(ROWS_CTX, D), x_sample.reshape(ROWS - ROWS_CTX, D), ln_in_g.reshape(1, D), ln_in_b.reshape(1, D))
    cond8 = jnp.concatenate([c_ctx[None], c, jnp.zeros((8 - 1 - N_LAT, D), F32)], axis=0)
    mod = _ada(cond8, w_ada, b_ada).reshape(DEPTH, 8, 6, D)

    rope_tabs = _rope_tables()
    cs_ctx, cs_lat = _dft_tables(T_CTX), _dft_tables(T_LAT)
    c64, s64 = _dft_channel_tables()
    cache_k = cache_k.reshape(N_LAT, DEPTH, PAST, KV_W)
    cache_v = cache_v.reshape(N_LAT, DEPTH, PAST, KV_W)
    w_router_t = jnp.swapaxes(w_router, 1, 2)
    sink_b = jnp.broadcast_to(attn_sink[:, :, None], (DEPTH, N_HEADS, LANES))
    s5_tabs = _s5_tables(s5_a_re, s5_a_im, s5_log_dt, s5_b_re, s5_b_im, s5_c_re, s5_c_im)
    lat_per_step = N_LAT // S5_STEPS_LAT
    s0 = jnp.concatenate([state_s5_re, state_s5_im, state_s5_im, state_s5_re], axis=-1)
    s0 = s0.reshape(S5_STEPS_LAT, lat_per_step, DEPTH, 2, S5_G, 4 * S5_P).transpose(2, 0, 3, 4, 1, 5)
    s0 = jnp.pad(s0, ((0, 0), (S5_STEPS_CTX, 0), (0, 0), (0, 0), (0, S5_SEQ - lat_per_step), (0, 0)))
    ln1_g, ln1_b, ln2_g, ln2_b, s5_d = (a.reshape(DEPTH, 1, -1) for a in (ln1_g, ln1_b, ln2_g, ln2_b, s5_d))

    new_s = []
    caches = (jnp.zeros(CACHE_SHAPE.shape, F32), jnp.zeros(CACHE_SHAPE.shape, F32))
    for l in range(DEPTH):
        q, k, v, uf, us, *caches = _inproj(x, mod, w_in, rope_tabs, caches, l)
        attn = _attn_lat(_attn_ctx(q, k, v, sink_b[l]), k, v, cache_k, cache_v, sink_b[l], l)
        four = _fourier(uf, cs_ctx, c64, s64, w_fourier, l, T_CTX, N_CTX, SEQ_PER_STEP_CTX, 0)
        four = _fourier(four, cs_lat, c64, s64, w_fourier, l, T_LAT, N_LAT, 1, ROWS_CTX // T_LAT)
        y5, fin = _s5(us, s5_tabs, s0, l)
        new_s.append(fin[:S5_STEPS_CTX].transpose(0, 3, 1, 2, 4).reshape(N_CTX, 2, S5_G, 2 * S5_P))
        x1, h2, lg = _outproj(attn, four, y5, us, s5_d, s5_w_glu, w_out, x, mod, ln1_g, ln1_b, w_router_t, l)
        crow_c, grow_c = _router(lg, T_CTX, N_CTX, 0, CAP_CTX)
        crow_q, grow_q = _router(lg, T_LAT, N_LAT, 1, CAP_LAT)
        xs_c = _gather(crow_c, h2, T_CTX, N_CTX, SEQ_PER_STEP_CTX, 0, CAP_CTX)
        xs_q = _gather(crow_q, h2, T_LAT, N_LAT, 1, ROWS_CTX // T_LAT, CAP_LAT)
        ys_c, ys_q = _ffn(xs_c, xs_q, w_gate, w_up, w_down, l)
        args_c = (crow_c, grow_c, ys_c), (mod, ln2_g, ln2_b, l, CAP_CTX, N_CTX, SEQ_PER_STEP_CTX, T_CTX, 1, 0)
        args_q = (crow_q, grow_q, ys_q), (mod, ln2_g, ln2_b, l, CAP_LAT, N_LAT, 1, RB_COMBINE_LAT,
                                          T_LAT // RB_COMBINE_LAT, ROWS_CTX // RB_COMBINE_LAT)
        if l < DEPTH - 1:
            x = _combine(*args_c[0], x1, *args_c[1], in_place=True)
            x = _combine(*args_q[0], x, *args_q[1], in_place=True)
        else:
            y_ctx = _combine(*args_c[0], x1, *args_c[1], in_place=False)
            y_lat = _combine(*args_q[0], x1, *args_q[1], in_place=False)

    new_s = jnp.stack(new_s, axis=1)

    def cache(feat_major):
        return feat_major.reshape(N_CTX, DEPTH, N_KV, HD, T_CTX).transpose(0, 1, 4, 2, 3)

    return (y_ctx.reshape(N_CTX, T_CTX, D), y_lat.reshape(N_LAT, T_LAT, D),
            cache(caches[0]), cache(caches[1]), new_s[..., :S5_P], new_s[..., S5_P:])
```

```python
import functools

import jax
import jax.numpy as jnp
import numpy as np
from jax import lax
from jax.experimental import pallas as pl
from jax.experimental.pallas import tpu as pltpu

F32 = jnp.float32
BF16 = jnp.bfloat16

D = 1024
N_CTX, T_CTX = 16, 256
N_LAT, T_LAT = 4, 1024
ROWS_CTX = N_CTX * T_CTX
ROWS = ROWS_CTX + N_LAT * T_LAT
DEPTH = 4
PAST = 512
GRID_W = 64
N_HEADS, N_KV, HD = 8, 2, 64
GQA = N_HEADS // N_KV
ATTN_W, KV_W = N_HEADS * HD, N_KV * HD
LOG2E = 1.4426950408889634
Q_SCALE = HD ** -0.5 * LOG2E
WINDOW = 128
FOUR_H, FOUR_W = 4, 256
S5_G, S5_H, S5_P, S5_W = 16, 16, 64, 256
S5_T = 16
IN_W = ATTN_W + 2 * KV_W + FOUR_W + S5_W
N_EXP, FF = 16, 1024
CAP_CTX, CAP_LAT = 2 * T_CTX // N_EXP, 2 * T_LAT // N_EXP
LN_EPS = 1e-5
NEG_INF = -1e30
ALPHA = (2 * DEPTH) ** 0.25
ROPE_BASE = 10000.0
RB_IN = 1024
RB_OUT = 512
RB_COMBINE_LAT = 512
LANES = 128
MIB = 2 ** 20


def _cparams(sem, vmem_mib):
    return pltpu.CompilerParams(dimension_semantics=sem, vmem_limit_bytes=vmem_mib * MIB)


def _dot(a, b):
    return jnp.dot(a, b, preferred_element_type=F32)


def _dot_nt(a, b):
    return lax.dot_general(a, b, (((1,), (1,)), ((), ())), preferred_element_type=F32)


def _dot_tn(a, b):
    return lax.dot_general(a, b, (((0,), (0,)), ((), ())), preferred_element_type=F32)


def _layer_norm(x, g, b):
    mu = jnp.mean(x, axis=-1, keepdims=True)
    xc = x - mu
    var = jnp.mean(xc * xc, axis=-1, keepdims=True)
    return xc * lax.rsqrt(var + LN_EPS) * g + b


def _mod_row(block, block_rows):
    first = block * block_rows
    return jnp.where(first < ROWS_CTX, 0, 1 + (first - ROWS_CTX) // T_LAT)


_HALVES_SHAPE = jax.ShapeDtypeStruct((2, ROWS, LANES), F32)


def _halves_spec(rows, row_block):
    return pl.BlockSpec((2, rows, LANES), lambda *ids: (0, row_block(*ids), 0))


def _stream_specs(x, rows):
    if not isinstance(x, tuple):
        return [pl.BlockSpec((rows, D), lambda i: (i, 0))]
    n_ctx = ROWS_CTX // rows
    vec = pl.BlockSpec((1, D), lambda i: (0, 0))
    return [pl.BlockSpec((rows, D), lambda i: (jnp.minimum(i, n_ctx - 1), 0)),
            pl.BlockSpec((rows, D), lambda i: (jnp.maximum(i - n_ctx, 0), 0)), vec, vec]


def _stream_args(x):
    return x if isinstance(x, tuple) else (x,)


def _stream_block(refs, rows):
    if len(refs) == 1:
        return refs[0][...]
    xc_ref, xq_ref, g_ref, b_ref = refs
    raw = jnp.where(pl.program_id(0) < ROWS_CTX // rows, xc_ref[...], xq_ref[...])
    return _layer_norm(raw, g_ref[...], b_ref[...])


def _ada_kernel(c_ref, w_ref, b_ref, o_ref):
    c = c_ref[...]
    s = (c * jax.nn.sigmoid(c)).astype(BF16)
    o_ref[0] = _dot(s, w_ref[0].astype(BF16)) + b_ref[0]


def _ada(cond8, w_ada, b_ada):
    tn = 1536
    return pl.pallas_call(
        _ada_kernel,
        grid=(DEPTH, 6 * D // tn),
        in_specs=[pl.BlockSpec((8, D), lambda l, j: (0, 0)),
                  pl.BlockSpec((1, D, tn), lambda l, j: (l, 0, j)),
                  pl.BlockSpec((1, 1, tn), lambda l, j: (l, 0, j))],
        out_specs=pl.BlockSpec((1, 8, tn), lambda l, j: (l, 0, j)),
        out_shape=jax.ShapeDtypeStruct((DEPTH, 8, 6 * D), F32),
        compiler_params=_cparams(("arbitrary", "arbitrary"), 40),
        name="adaln",
    )(cond8, w_ada, b_ada.reshape(DEPTH, 1, 6 * D))


def _inproj_kernel(*refs, n_x):
    x_refs = refs[:n_x]
    (mod_ref, w_ref, cos_ref, sin_ref, _, _, q_ref, k_ref, v_ref, uf_ref, us_ref, kc_ref, vc_ref,
     wb_ref, wswb_ref) = refs[n_x:]
    step = pl.program_id(0)
    n_ctx_steps = ROWS_CTX // RB_IN

    @pl.when(step == 0)
    def _():
        wb = w_ref[0].astype(BF16)
        wb_ref[...] = wb
        n_qk = ATTN_W + KV_W
        src = lax.broadcasted_iota(jnp.int32, (n_qk, n_qk), 0)
        dst = lax.broadcasted_iota(jnp.int32, (n_qk, n_qk), 1)
        swap = jnp.where(src == (dst ^ 1), 1.0, 0.0).astype(BF16)
        wswb_ref[...] = _dot(wb[:, :n_qk], swap).astype(BF16)

    m = mod_ref[0, 0]
    hb = (_stream_block(x_refs, RB_IN) * (1.0 + m[1:2]) + m[0:1]).astype(BF16)
    p = _dot(hb, wb_ref[...])
    q = p[:, :ATTN_W]
    k = p[:, ATTN_W:ATTN_W + KV_W]
    v = p[:, ATTN_W + KV_W:ATTN_W + 2 * KV_W]

    psw = _dot(hb, wswb_ref[...])
    cos, sin = cos_ref[...], sin_ref[...]
    wide = lambda t: jnp.concatenate([t] * (ATTN_W // KV_W), axis=1)
    latent = step >= n_ctx_steps
    q = jnp.where(latent, q * wide(cos) + psw[:, :ATTN_W] * wide(sin), q)
    k = jnp.where(latent, k * cos + psw[:, ATTN_W:] * sin, k)
    q_ref[...] = (q * Q_SCALE).astype(BF16)
    k_ref[...] = k.astype(BF16)
    v_ref[...] = v.astype(BF16)

    @pl.when(step < n_ctx_steps)
    def _():
        for j in range(RB_IN // T_CTX):
            rows = slice(j * T_CTX, (j + 1) * T_CTX)
            kc_ref[j, 0] = k[rows].T
            vc_ref[j, 0] = v[rows].T

    uf_ref[...] = p[:, ATTN_W + 2 * KV_W:ATTN_W + 2 * KV_W + FOUR_W].astype(BF16)
    for half in range(S5_W // LANES):
        lo = ATTN_W + 2 * KV_W + FOUR_W + half * LANES
        us_ref[half] = p[:, lo:lo + LANES]


CACHE_SHAPE = jax.ShapeDtypeStruct((N_CTX, DEPTH, KV_W, T_CTX), F32)


def _inproj(x, mod, w_in, rope_tabs, caches, l):
    assert RB_IN == T_LAT
    outs = ((ATTN_W, BF16), (KV_W, BF16), (KV_W, BF16), (FOUR_W, BF16))
    table = pl.BlockSpec((T_LAT, KV_W), lambda i: (0, 0))
    any_space = pl.BlockSpec(memory_space=pl.ANY)
    slab = pl.BlockSpec((RB_IN // T_CTX, 1, KV_W, T_CTX), lambda i: (jnp.minimum(i, ROWS_CTX // RB_IN - 1), l, 0, 0))
    x_specs = _stream_specs(x, RB_IN)
    n_in = len(x_specs) + 4
    return pl.pallas_call(
        functools.partial(_inproj_kernel, n_x=len(x_specs)),
        grid=(ROWS // RB_IN,),
        in_specs=x_specs + [pl.BlockSpec((1, 1, 6, D), lambda i: (l, _mod_row(i, RB_IN), 0, 0)),
                            pl.BlockSpec((1, D, IN_W), lambda i: (l, 0, 0)),
                            table, table, any_space, any_space],
        out_specs=([pl.BlockSpec((RB_IN, w), lambda i: (i, 0)) for w, _ in outs] + [_halves_spec(RB_IN, lambda i: i)]
                   + [slab, slab]),
        out_shape=([jax.ShapeDtypeStruct((ROWS, w), dt) for w, dt in outs] + [_HALVES_SHAPE] + [CACHE_SHAPE] * 2),
        input_output_aliases={n_in: 5, n_in + 1: 6},
        scratch_shapes=[pltpu.VMEM((D, IN_W), BF16), pltpu.VMEM((D, ATTN_W + KV_W), BF16)],
        compiler_params=_cparams(("arbitrary",), 56),
        name="inproj",
    )(*_stream_args(x), mod, w_in, *rope_tabs, *caches)


SINK_ROWS = 16


def _softmax_av(score_blocks, sink, value_blocks):
    m = sink
    for s in score_blocks:
        m = jnp.maximum(m, jnp.max(s, axis=0, keepdims=True))
    first = lax.broadcasted_iota(jnp.int32, (SINK_ROWS, 1), 0) == 0
    probs = [jnp.exp2(s - m).astype(BF16) for s in score_blocks]
    probs.append(jnp.where(first, jnp.exp2(sink - m), 0.0).astype(BF16))
    ext = []
    for v in value_blocks:
        ones = jnp.ones(v.shape, BF16)
        ext.append(jnp.concatenate([v, v, ones, ones], axis=1))
    row = lax.broadcasted_iota(jnp.int32, (SINK_ROWS, 4 * HD), 0)
    lane = lax.broadcasted_iota(jnp.int32, (SINK_ROWS, 4 * HD), 1)
    ext.append(jnp.where((row == 0) & (lane >= 2 * HD), 1.0, 0.0).astype(BF16))
    acc = _dot_tn(jnp.concatenate(probs, axis=0), jnp.concatenate(ext, axis=0))
    return acc[:, :2 * HD] / acc[:, 2 * HD:]


def _stack_group(q, kv):
    return jnp.concatenate([q[:, (kv * GQA + g) * HD:(kv * GQA + g + 1) * HD] for g in range(GQA)], axis=0)


def _sink_row(sink_ref, kv, rows):
    return LOG2E * jnp.concatenate(
        [jnp.broadcast_to(sink_ref[kv * GQA + g:kv * GQA + g + 1, 0:1], (1, rows)) for g in range(GQA)], axis=1)


def _unstack_group(o, rows):
    lane = lax.broadcasted_iota(jnp.int32, (rows, 2 * HD), 1)
    return [jnp.where(lane < HD, o[2 * j * rows:(2 * j + 1) * rows], o[(2 * j + 1) * rows:(2 * j + 2) * rows])
            for j in range(GQA // 2)]


def _attn_ctx_kernel(q_ref, k_ref, v_ref, sink_ref, o_ref):
    q = q_ref[...]
    k = k_ref[...]
    v = v_ref[...]
    outs = []
    for kv in range(N_KV):
        head = slice(kv * HD, (kv + 1) * HD)
        s = _dot_nt(k[:, head], _stack_group(q, kv))
        o = _softmax_av([s], _sink_row(sink_ref, kv, T_CTX), [v[:, head]])
        outs += _unstack_group(o, T_CTX)
    o_ref[...] = jnp.concatenate(outs, axis=1).astype(BF16)


def _attn_ctx(q, k, v, sink_b):
    return pl.pallas_call(
        _attn_ctx_kernel,
        grid=(N_CTX,),
        in_specs=[pl.BlockSpec((T_CTX, ATTN_W), lambda s: (s, 0)),
                  pl.BlockSpec((T_CTX, KV_W), lambda s: (s, 0)),
                  pl.BlockSpec((T_CTX, KV_W), lambda s: (s, 0)),
                  pl.BlockSpec((N_HEADS, LANES), lambda s: (0, 0))],
        out_specs=pl.BlockSpec((T_CTX, ATTN_W), lambda s: (s, 0)),
        out_shape=jax.ShapeDtypeStruct((ROWS, ATTN_W), BF16),
        input_output_aliases={0: 0},
        compiler_params=_cparams(("arbitrary",), 32),
        name="attn_ctx",
    )(q, k, v, sink_b)


def _attn_lat_kernel(q_ref, k_ref, v_ref, ck_ref, cv_ref, sink_ref, o_ref, kc_scr, vc_scr):
    i = pl.program_id(1)
    n_blk = T_LAT // WINDOW
    q = q_ref[...]

    @pl.when(i == 0)
    def _():
        kc_scr[...] = ck_ref[0, 0].T.astype(BF16)
        vc_scr[...] = cv_ref[0, 0].T.astype(BF16)

    def kv_block(j):
        st = pl.multiple_of(j * WINDOW, WINDOW)
        return k_ref[pl.ds(st, WINDOW), :], v_ref[pl.ds(st, WINDOW), :]

    k0, v0 = kv_block(jnp.maximum(i - 1, 0))
    k1, v1 = kv_block(i)
    k2, v2 = kv_block(jnp.minimum(i + 1, n_blk - 1))
    k_win = jnp.concatenate([k0, k1, k2], axis=0)
    k_ctx = kc_scr[...]
    v_ctx = vc_scr[...]
    rows = GQA * WINDOW
    c = lax.broadcasted_iota(jnp.int32, (WINDOW, rows), 0)
    r = lax.broadcasted_iota(jnp.int32, (WINDOW, rows), 1) & (WINDOW - 1)
    far = 2 * WINDOW
    prev_ok = c >= r + jnp.where(i > 0, 0, far)
    next_ok = c + jnp.where(i < n_blk - 1, 0, far) <= r
    outs = []
    for kv in range(N_KV):
        head = slice(kv * HD, (kv + 1) * HD)
        qs = _stack_group(q, kv)
        s_win = _dot_nt(k_win[:, head], qs)
        scores = [jnp.where(prev_ok, s_win[:WINDOW], NEG_INF), s_win[WINDOW:2 * WINDOW],
                  jnp.where(next_ok, s_win[2 * WINDOW:], NEG_INF), _dot_nt(k_ctx[:, head], qs)]
        values = [v0[:, head], v1[:, head], v2[:, head], v_ctx[:, head]]
        outs += _unstack_group(_softmax_av(scores, _sink_row(sink_ref, kv, WINDOW), values), WINDOW)
    o_ref[...] = jnp.concatenate(outs, axis=1).astype(BF16)


def _attn_lat(q, k, v, cache_k, cache_v, sink_b, l):
    n_blk = T_LAT // WINDOW
    off = ROWS_CTX // WINDOW
    return pl.pallas_call(
        _attn_lat_kernel,
        grid=(N_LAT, n_blk),
        in_specs=[pl.BlockSpec((WINDOW, ATTN_W), lambda b, i: (off + b * n_blk + i, 0)),
                  pl.BlockSpec((T_LAT, KV_W), lambda b, i: (ROWS_CTX // T_LAT + b, 0)),
                  pl.BlockSpec((T_LAT, KV_W), lambda b, i: (ROWS_CTX // T_LAT + b, 0)),
                  pl.BlockSpec((1, 1, KV_W, PAST), lambda b, i: (b, l, 0, 0)),
                  pl.BlockSpec((1, 1, KV_W, PAST), lambda b, i: (b, l, 0, 0)),
                  pl.BlockSpec((N_HEADS, LANES), lambda b, i: (0, 0))],
        out_specs=pl.BlockSpec((WINDOW, ATTN_W), lambda b, i: (off + b * n_blk + i, 0)),
        out_shape=jax.ShapeDtypeStruct((ROWS, ATTN_W), BF16),
        scratch_shapes=[pltpu.VMEM((PAST, KV_W), BF16), pltpu.VMEM((PAST, KV_W), BF16)],
        input_output_aliases={0: 0},
        compiler_params=_cparams(("arbitrary", "arbitrary"), 40),
        name="attn_lat",
    )(q, k, v, cache_k, cache_v, sink_b)


def _rope_tables():
    rows = T_LAT // GRID_W
    row = jnp.repeat(jnp.arange(rows, dtype=F32), GRID_W)
    col = jnp.tile(jnp.arange(GRID_W, dtype=F32), rows)
    n_freq = HD // 4
    freqs = ROPE_BASE ** (-jnp.arange(n_freq, dtype=F32) / n_freq)
    ang = jnp.concatenate([row[:, None] * freqs, col[:, None] * freqs], axis=-1)
    cos = jnp.repeat(jnp.cos(ang), 2, axis=-1)
    sign = jnp.tile(jnp.array([-1.0, 1.0], F32), HD // 2)
    sin = jnp.repeat(jnp.sin(ang), 2, axis=-1) * sign
    return jnp.tile(cos, (1, N_KV)), jnp.tile(sin, (1, N_KV))


SEQ_PER_STEP_CTX = 4


def _four_kernel(u_ref, cs_ref, c64_ref, s64_ref, wf_ref, o_ref, *, n, n_sub):
    ub = u_ref[...]
    uc = _dot(ub, c64_ref[...]).astype(BF16)
    us = _dot(ub, s64_ref[...]).astype(BF16)
    f = jnp.concatenate([_dot(cs_ref[...], jnp.concatenate([uc[j * n:(j + 1) * n], us[j * n:(j + 1) * n]], axis=0))
                         for j in range(n_sub)], axis=0)
    o_ref[...] = _dot(f.astype(BF16), wf_ref[0].astype(BF16)).astype(BF16)


def _dft_tables(n):
    j = np.arange(n)
    ang = 2.0 * np.pi * ((j[:, None] * j[None, :]) % n) / n
    cs = np.concatenate([np.cos(ang), -np.sin(ang)], axis=1) / np.sqrt(n)
    return jnp.asarray(cs.astype(np.float32)).astype(BF16)


def _dft_channel_tables():
    j = np.arange(HD)
    ang = 2.0 * np.pi * ((j[:, None] * j[None, :]) % HD) / HD
    eye = np.eye(FOUR_H)
    c = np.kron(eye, np.cos(ang)) / np.sqrt(HD)
    s = np.kron(eye, np.sin(ang)) / np.sqrt(HD)
    return jnp.asarray(c.astype(np.float32)).astype(BF16), jnp.asarray(s.astype(np.float32)).astype(BF16)


def _fourier(uf, cs, c64, s64, w_fourier, l, n, n_seq, n_sub, blk_off):
    rows = n_sub * n
    return pl.pallas_call(
        functools.partial(_four_kernel, n=n, n_sub=n_sub),
        grid=(n_seq // n_sub,),
        in_specs=[pl.BlockSpec((rows, FOUR_W), lambda s: (blk_off + s, 0)),
                  pl.BlockSpec((n, 2 * n), lambda s: (0, 0)),
                  pl.BlockSpec((FOUR_W, FOUR_W), lambda s: (0, 0)),
                  pl.BlockSpec((FOUR_W, FOUR_W), lambda s: (0, 0)),
                  pl.BlockSpec((1, FOUR_W, FOUR_W), lambda s: (l, 0, 0))],
        out_specs=pl.BlockSpec((rows, FOUR_W), lambda s: (blk_off + s, 0)),
        out_shape=jax.ShapeDtypeStruct((ROWS, FOUR_W), BF16),
        input_output_aliases={0: 0},
        compiler_params=_cparams(("arbitrary",), 40),
        name=f"fourier_{n}",
    )(uf, cs, c64, s64, w_fourier)


S5_TW = S5_T * S5_H
S5_R = 128
S5_TOK = S5_R * S5_T
S5_STEPS_CTX = ROWS_CTX // S5_TOK
S5_STEPS_LAT = (ROWS - ROWS_CTX) // S5_TOK
S5_STEPS = S5_STEPS_CTX + S5_STEPS_LAT
S5_SEQ = N_CTX // S5_STEPS_CTX


def _s5_tables(a_re, a_im, log_dt, b_re, b_im, c_re, c_im):
    f = lambda a: a.astype(F32)
    a_re, a_im, b_re, b_im, c_re, c_im = map(f, (a_re, a_im, b_re, b_im, c_re, c_im))
    dt = jnp.exp(f(log_dt))[..., None]
    x, y = a_re * dt, a_im * dt
    kk = jnp.arange(S5_T + 1, dtype=F32)[None, None, :, None, None]
    mag = jnp.exp(kk * x[:, :, None])
    pr, pi = mag * jnp.cos(kk * y[:, :, None]), mag * jnp.sin(kk * y[:, :, None])
    nr, ni = pr[:, :, 1] - 1.0, pi[:, :, 1]
    den = a_re * a_re + a_im * a_im
    qr, qi = (nr * a_re + ni * a_im) / den, (ni * a_re - nr * a_im) / den
    bb_r = qr[..., None] * b_re - qi[..., None] * b_im
    bb_i = qr[..., None] * b_im + qi[..., None] * b_re
    cat = lambda u, v: jnp.concatenate([u, v], axis=-1)
    prg, pig = pr.transpose(0, 1, 3, 2, 4), pi.transpose(0, 1, 3, 2, 4)
    def lane_pairs(re, im, patterns):
        coef = lambda k: jnp.asarray(np.repeat(np.array([[p[0][k], p[1][k]] for p in patterns], np.float32),
                                               re.shape[-1], axis=-1))[:, None, :]
        return cat(re, re)[..., None, :, :] * coef(0) + cat(im, im)[..., None, :, :] * coef(1)

    keep_re, keep_im, neg_re, neg_im = (1, 0), (0, 1), (-1, 0), (0, -1)
    pows = lane_pairs(prg, pig, [(keep_re, keep_re), (neg_im, keep_im), (keep_re, neg_im), (neg_im, neg_re),
                                 (keep_re, keep_im), (neg_im, keep_re)])
    bt_r, bt_i = bb_r.transpose(0, 1, 2, 4, 3), bb_i.transpose(0, 1, 2, 4, 3)
    wrows = jnp.concatenate([lane_pairs(bt_r, bt_i, [(keep_re, keep_im), (keep_im, keep_re), (keep_re, neg_im)]),
                             lane_pairs(c_re, c_im, [(keep_re, keep_re), (keep_im, keep_im)])],
                            axis=3)
    ar, ai = pr[:, :, S5_T], pi[:, :, S5_T]
    decay = jnp.stack([cat(ar, ar), cat(-ai, ai), cat(ai, -ai), jnp.zeros_like(cat(ar, ar))], axis=3)
    return pows, wrows, decay


def _block_transpose(arrs):
    arrs = list(arrs)
    width = arrs[0].shape[1]
    blk = lax.broadcasted_iota(jnp.int32, arrs[0].shape, 1) >> 4
    s = len(arrs) // 2
    while s >= 1:
        keep = (blk & s) == 0
        for i in range(len(arrs)):
            if i & s:
                continue
            lo, hi = arrs[i], arrs[i + s]
            arrs[i] = jnp.where(keep, lo, pltpu.roll(hi, s * S5_H, 1))
            arrs[i + s] = jnp.where(keep, pltpu.roll(lo, width - s * S5_H, 1), hi)
        s //= 2
    return arrs


def _s5_build_operators(pw_ref, w_ref, e_scr, esw_scr, ft_scr, tp_scr):
    lane = lax.broadcasted_iota(jnp.int32, (S5_H, S5_TW), 1)
    for d in range(2):
        for g in range(S5_G):
            pw = lambda v, k: pw_ref[0, d, g, v, k:k + 1, :]
            b_ri, b_ir, b_conj, c_rr, c_ii = (w_ref[0, d, g, v] for v in range(5))
            lag_rows = []
            for t in range(S5_T):
                blk = slice(t * S5_H, (t + 1) * S5_H)
                ke = S5_T - 1 - t if d == 0 else t
                e_scr[d, g, blk, :] = (b_ri * pw(0, ke) + b_ir * pw(1, ke)).astype(BF16)
                esw_scr[d, g, blk, :] = (b_ir * pw(0, ke) - b_ri * pw(1, ke)).astype(BF16)
                kf = t + 1 if d == 0 else S5_T - t
                ft_scr[d, g, blk, :] = (c_rr * pw(2, kf) + c_ii * pw(3, kf)).astype(BF16)
                kl = t if d == 0 else S5_T - 1 - t
                lag_rows.append(c_rr * pw(4, kl) + c_ii * pw(5, kl))
            mh, ml = _split_bf16(jnp.concatenate(lag_rows, axis=0))
            bh, bl = _split_bf16(b_conj)
            kt = _dot_nt(bh, mh) + _dot_nt(bh, ml) + _dot_nt(bl, mh)
            for ti in range(S5_T):
                if d == 0:
                    sh = S5_H * ti
                    blk_rows = jnp.where(lane >= sh, pltpu.roll(kt, sh, 1) if sh else kt, 0.0)
                else:
                    sh = S5_H * (S5_T - 1 - ti)
                    blk_rows = jnp.where(lane < S5_TW - sh, pltpu.roll(kt, S5_TW - sh, 1) if sh else kt, 0.0)
                tp_scr[d, g, ti * S5_H:(ti + 1) * S5_H, :] = blk_rows.astype(BF16)


def _s5_kernel(us_ref, pw_ref, w_ref, a_ref, s0_ref, y_ref, fin_ref,
               e_scr, esw_scr, ft_scr, tp_scr, x_scr, y_scr, loc_scr, lsw_scr, prev_scr):
    step = pl.program_id(0)

    @pl.when(step == 0)
    def _():
        _s5_build_operators(pw_ref, w_ref, e_scr, esw_scr, ft_scr, tp_scr)

    rows = pl.ds
    x_by_step = [jnp.concatenate([us_ref[0, rows(tl, S5_R, stride=S5_T), :],
                                  us_ref[1, rows(tl, S5_R, stride=S5_T), :]], axis=1).astype(BF16)
                 for tl in range(S5_T)]
    for g, xg in enumerate(_block_transpose(x_by_step)):
        x_scr[g] = xg
    fin_ref[...] = jnp.zeros(fin_ref.shape, F32)

    def scan(d, nseq, nc):
        batch = S5_G
        for g0 in range(0, S5_G, batch):
            init = tuple((s0_ref[0, 0, d, g, 0:nseq, :2 * S5_P], s0_ref[0, 0, d, g, 0:nseq, 2 * S5_P:])
                         for g in range(g0, g0 + batch))

            def body(j, carry):
                ci = j if d == 0 else nc - 1 - j
                out = []
                for idx in range(batch):
                    g = g0 + idx
                    s, t = carry[idx]
                    ca, cb, cc = (a_ref[0, d, g, r:r + 1, :] for r in range(3))
                    prev_scr[g, rows(ci, nseq, stride=nc), :] = s
                    out.append((s * ca + t * cb + loc_scr[g, rows(ci, nseq, stride=nc), :],
                                t * ca + s * cc + lsw_scr[g, rows(ci, nseq, stride=nc), :]))
                return tuple(out)

            fin = lax.fori_loop(0, nc, body, init)
            for idx in range(batch):
                fin_ref[0, d, g0 + idx, 0:nseq, :] = fin[idx][0]

    for d in range(2):
        for g in range(S5_G):
            xg = x_scr[g]
            loc_scr[g] = _dot(xg, e_scr[d, g])
            lsw_scr[g] = _dot(xg, esw_scr[d, g])

        @pl.when(step < S5_STEPS_CTX)
        def _():
            scan(d, N_CTX // S5_STEPS_CTX, T_CTX // S5_T)

        @pl.when(step >= S5_STEPS_CTX)
        def _():
            scan(d, N_LAT // S5_STEPS_LAT, T_LAT // S5_T)

        for g in range(S5_G):
            yg = _dot(x_scr[g], tp_scr[d, g]) + _dot_nt(prev_scr[g].astype(BF16), ft_scr[d, g])
            if d == 0:
                y_scr[g] = yg
            else:
                y_scr[g] += yg
    for to, y_to in enumerate(_block_transpose([y_scr[g] for g in range(S5_G)])):
        y_ref[0, rows(to, S5_R, stride=S5_T), :] = y_to[:, :LANES]
        y_ref[1, rows(to, S5_R, stride=S5_T), :] = y_to[:, LANES:]


def _s5(us, tabs, s0, l):
    pows, wrows, decay = tabs
    lay = lambda *shape: pl.BlockSpec((1,) + shape, lambda h: (l,) + (0,) * len(shape))
    op_buf = pltpu.VMEM((2, S5_G, S5_TW, 2 * S5_P), BF16)
    scan_buf = pltpu.VMEM((S5_G, S5_R, 2 * S5_P), F32)
    return pl.pallas_call(
        _s5_kernel,
        grid=(S5_STEPS,),
        in_specs=[_halves_spec(S5_TOK, lambda h: h),
                  lay(2, S5_G, 6, S5_T + 1, 2 * S5_P), lay(2, S5_G, 5, S5_H, 2 * S5_P), lay(2, S5_G, 4, 2 * S5_P),
                  pl.BlockSpec((1, 1, 2, S5_G, S5_SEQ, 4 * S5_P), lambda h: (l, h, 0, 0, 0, 0))],
        out_specs=[_halves_spec(S5_TOK, lambda h: h),
                   pl.BlockSpec((1, 2, S5_G, S5_SEQ, 2 * S5_P), lambda h: (h, 0, 0, 0, 0))],
        out_shape=[_HALVES_SHAPE, jax.ShapeDtypeStruct((S5_STEPS, 2, S5_G, S5_SEQ, 2 * S5_P), F32)],
        scratch_shapes=[op_buf, op_buf, op_buf, pltpu.VMEM((2, S5_G, S5_TW, S5_TW), BF16),
                        pltpu.VMEM((S5_G, S5_R, S5_TW), BF16), pltpu.VMEM((S5_G, S5_R, S5_TW), F32),
                        scan_buf, scan_buf, scan_buf],
        compiler_params=_cparams(("arbitrary",), 56),
        name="s5",
    )(us, pows, wrows, decay, s0)


def _split_bf16(a):
    hi = a.astype(BF16)
    return hi, (a - hi.astype(F32)).astype(BF16)


def _outproj_kernel(*refs, n_x):
    x_refs = refs[:n_x]
    (at_ref, fo_ref, ys_ref, us_ref, d_ref, wg_ref, wo_ref, mod_ref, g_ref, b_ref, wr_ref,
     x1_ref, h2_ref, lg_ref, wob_ref) = refs[n_x:]

    @pl.when(pl.program_id(0) == 0)
    def _():
        wob_ref[...] = wo_ref[0].astype(BF16)

    ys = jnp.concatenate([ys_ref[0], ys_ref[1]], axis=1)
    us = jnp.concatenate([us_ref[0], us_ref[1]], axis=1)
    g = jax.nn.gelu(ys + us * d_ref[0])
    s5 = g * jax.nn.sigmoid(_dot(g.astype(BF16), wg_ref[0].astype(BF16)))
    mix = (_dot(at_ref[...], wob_ref[0:ATTN_W, :])
           + _dot(fo_ref[...], wob_ref[ATTN_W:ATTN_W + FOUR_W, :])
           + _dot(s5.astype(BF16), wob_ref[ATTN_W + FOUR_W:, :]))
    m = mod_ref[0, 0]
    r = ALPHA * _stream_block(x_refs, RB_OUT) + m[2:3] * mix
    rc = r - jnp.mean(r, axis=-1, keepdims=True)
    t = rc * lax.rsqrt(jnp.mean(rc * rc, axis=-1, keepdims=True) + LN_EPS)
    gain, bias = g_ref[0], b_ref[0]
    x1_ref[...] = t * gain + bias
    up = 1.0 + m[4:5]
    hb = (t * (gain * up) + (bias * up + m[3:4])).astype(BF16)
    h2_ref[...] = hb
    lg_ref[...] = _dot_nt(wr_ref[0].astype(BF16), hb)


def _outproj(attn, four, ys5, us, s5_d, w_glu, w_out, x, mod, ln_g, ln_b, w_router_t, l):
    row = lambda w: pl.BlockSpec((RB_OUT, w), lambda i: (i, 0))
    halves = _halves_spec(RB_OUT, lambda i: i)
    lay = lambda *shape: pl.BlockSpec((1,) + shape, lambda i: (l,) + (0,) * len(shape))
    x_specs = _stream_specs(x, RB_OUT)
    return pl.pallas_call(
        functools.partial(_outproj_kernel, n_x=len(x_specs)),
        grid=(ROWS // RB_OUT,),
        in_specs=x_specs + [row(ATTN_W), row(FOUR_W), halves, halves, lay(1, S5_W), lay(S5_W, S5_W), lay(D, D),
                            pl.BlockSpec((1, 1, 6, D), lambda i: (l, _mod_row(i, RB_OUT), 0, 0)),
                            lay(1, D), lay(1, D), lay(N_EXP, D)],
        out_specs=[row(D), row(D), pl.BlockSpec((N_EXP, RB_OUT), lambda i: (0, i))],
        out_shape=[jax.ShapeDtypeStruct((ROWS, D), F32), jax.ShapeDtypeStruct((ROWS, D), BF16),
                   jax.ShapeDtypeStruct((N_EXP, ROWS), F32)],
        scratch_shapes=[pltpu.VMEM((D, D), BF16)],
        compiler_params=_cparams(("arbitrary",), 40),
        name="outproj",
    )(*_stream_args(x), attn, four, ys5, us, s5_d, w_glu, w_out, mod, ln_g, ln_b, w_router_t)


def _router_kernel(lg_ref, crow_ref, grow_ref, *, n, cap, n_seq):
    rows = n_seq * N_EXP
    lg = jnp.concatenate([lg_ref[:, s * n:(s + 1) * n] for s in range(n_seq)], axis=0).reshape(n_seq, N_EXP, n)
    e = jnp.exp(lg - jnp.max(lg, axis=1, keepdims=True))
    aff = (e / jnp.sum(e, axis=1, keepdims=True)).reshape(rows, n)
    thr_bits = jnp.zeros((rows, 1), jnp.int32)
    for bit in range(30, -1, -1):
        cand = thr_bits | (1 << bit)
        cnt = jnp.sum(jnp.where(aff >= lax.bitcast_convert_type(cand, F32), 1.0, 0.0), axis=1, keepdims=True)
        thr_bits = jnp.where(cnt >= cap, cand, thr_bits)
    thr = lax.bitcast_convert_type(thr_bits, F32)
    above = aff > thr
    tied = aff == thr
    need = cap - jnp.sum(jnp.where(above, 1.0, 0.0), axis=1, keepdims=True)
    r0 = lax.broadcasted_iota(jnp.int32, (n, n), 0)
    r1 = lax.broadcasted_iota(jnp.int32, (n, n), 1)
    before = jnp.where(r0 < r1, 1.0, 0.0).astype(BF16)
    tied_rank = _dot(jnp.where(tied, 1.0, 0.0).astype(BF16), before)
    sel = above | (tied & (tied_rank < need))
    slot = _dot(jnp.where(sel, 1.0, 0.0).astype(BF16), before)
    code = jnp.where(sel, slot + 1.0, 0.0)
    gate = jnp.where(sel, aff, 0.0)
    for s in range(n_seq):
        crow_ref[:, s * n:(s + 1) * n] = code[s * N_EXP:(s + 1) * N_EXP]
        grow_ref[:, s * n:(s + 1) * n] = gate[s * N_EXP:(s + 1) * N_EXP]


def _router(lg, n, n_seq, blk, cap):
    table = pl.BlockSpec((N_EXP, n_seq * n), lambda i: (0, 0))
    return pl.pallas_call(
        functools.partial(_router_kernel, n=n, cap=cap, n_seq=n_seq),
        grid=(1,),
        in_specs=[pl.BlockSpec((N_EXP, n_seq * n), lambda i: (0, blk))],
        out_specs=[table, table],
        out_shape=[jax.ShapeDtypeStruct((N_EXP, n_seq * n), F32)] * 2,
        compiler_params=_cparams(("arbitrary",), 48),
        name=f"router_{n}",
    )(lg)


def _gather_kernel(crow_ref, h_ref, o_ref, *, n, cap, n_sub):
    slot1 = (lax.broadcasted_iota(jnp.int32, (cap, n), 0) + 1).astype(F32)
    for j in range(n_sub):
        code = crow_ref[:, j * n:(j + 1) * n]
        onehot = jnp.concatenate(
            [jnp.where(code[e:e + 1, :] == slot1, 1.0, 0.0).astype(BF16) for e in range(N_EXP)], axis=0)
        xs = _dot(onehot, h_ref[j * n:(j + 1) * n, :])
        o_ref[:, j * cap:(j + 1) * cap, :] = xs.reshape(N_EXP, cap, D).astype(BF16)


def _gather(crow, h2, n, n_seq, n_sub, blk_off, cap):
    return pl.pallas_call(
        functools.partial(_gather_kernel, n=n, cap=cap, n_sub=n_sub),
        grid=(n_seq // n_sub,),
        in_specs=[pl.BlockSpec((N_EXP, n_sub * n), lambda s: (0, s)),
                  pl.BlockSpec((n_sub * n, D), lambda s: (blk_off + s, 0))],
        out_specs=pl.BlockSpec((N_EXP, n_sub * cap, D), lambda s: (0, s, 0)),
        out_shape=jax.ShapeDtypeStruct((N_EXP, n_seq * cap, D), BF16),
        compiler_params=_cparams(("arbitrary",), 48),
        name=f"gather_{n}",
    )(crow, h2)


SLOTS = N_CTX * CAP_CTX


def _ffn_kernel(xc_ref, xq_ref, wg_ref, wu0_ref, wu1_ref, wd_ref, yc_ref, yq_ref, wgb_ref, wub_ref, wdb_ref):
    stream = pl.program_id(1)

    @pl.when(stream == 0)
    def _():
        wgb_ref[...] = wg_ref[0, 0].astype(BF16)
        wub_ref[:, :FF // 2] = wu0_ref[0, 0].astype(BF16)
        wub_ref[:, FF // 2:] = wu1_ref[0, 0].astype(BF16)
        wdb_ref[...] = wd_ref[0, 0].astype(BF16)

    def swiglu(x_ref, y_ref):
        x = x_ref[0]
        g = _dot(x, wgb_ref[...])
        u = _dot(x, wub_ref[...])
        hid = (g * jax.nn.sigmoid(g) * u).astype(BF16)
        y_ref[0] = _dot(hid, wdb_ref[...]).astype(BF16)

    @pl.when(stream == 0)
    def _():
        swiglu(xc_ref, yc_ref)

    @pl.when(stream == 1)
    def _():
        swiglu(xq_ref, yq_ref)


def _ffn(xs_c, xs_q, w_gate, w_up, w_down, l):
    xspec = pl.BlockSpec((1, SLOTS, D), lambda e, s: (e, 0, 0))
    ahead = lambda e, s: jnp.minimum(e + s, N_EXP - 1)
    return pl.pallas_call(
        _ffn_kernel,
        grid=(N_EXP, 2),
        in_specs=[xspec, xspec,
                  pl.BlockSpec((1, 1, D, FF), lambda e, s: (l, ahead(e, s), 0, 0)),
                  pl.BlockSpec((1, 1, D, FF // 2), lambda e, s: (l, ahead(e, s), 0, 0)),
                  pl.BlockSpec((1, 1, D, FF // 2), lambda e, s: (l, e, 0, 1)),
                  pl.BlockSpec((1, 1, FF, D), lambda e, s: (l, e, 0, 0))],
        out_specs=[xspec, xspec],
        out_shape=[jax.ShapeDtypeStruct((N_EXP, SLOTS, D), BF16)] * 2,
        scratch_shapes=[pltpu.VMEM((D, FF), BF16), pltpu.VMEM((D, FF), BF16), pltpu.VMEM((FF, D), BF16)],
        compiler_params=_cparams(("arbitrary", "arbitrary"), 56),
        name="ffn",
    )(xs_c, xs_q, w_gate, w_up, w_up, w_down)


def _combine_kernel(crow_ref, grow_ref, ys_ref, x_ref, mod_ref, g_ref, b_ref, o_ref, *, cap, rb, n_sub):
    width = N_EXP * cap
    slot1 = (lax.broadcasted_iota(jnp.int32, (cap, rb), 0) + 1).astype(F32)
    m = mod_ref[0, 0]
    for j in range(n_sub):
        rows = slice(j * rb, (j + 1) * rb)
        code, gate = crow_ref[:, rows], grow_ref[:, rows]
        weights_t = jnp.concatenate(
            [jnp.where(code[e:e + 1, :] == slot1, gate[e:e + 1, :], 0.0).astype(BF16) for e in range(N_EXP)], axis=0)
        moe = _dot_tn(weights_t, ys_ref[:, j * cap:(j + 1) * cap, :].reshape(width, D))
        o_ref[rows, :] = _layer_norm(ALPHA * x_ref[rows, :] + m[5:6] * moe, g_ref[0], b_ref[0])


def _combine(crow, grow, ys, x1, mod, ln_g, ln_b, l, cap, n_seq, n_sub, rb, rb_per_seq, rb_off, in_place):
    assert n_sub == 1 or rb_per_seq == 1
    rows = n_sub * rb
    lay = lambda *shape: pl.BlockSpec((1,) + shape, lambda s, r: (l,) + (0,) * len(shape))
    table = pl.BlockSpec((N_EXP, rows), lambda s, r: (0, s * rb_per_seq + r))
    block = lambda s, r: rb_off + s * rb_per_seq + r
    out_off = rb_off if in_place else 0
    return pl.pallas_call(
        functools.partial(_combine_kernel, cap=cap, rb=rb, n_sub=n_sub),
        grid=(n_seq // n_sub, rb_per_seq),
        in_specs=[table, table,
                  pl.BlockSpec((N_EXP, n_sub * cap, D), lambda s, r: (0, s, 0)),
                  pl.BlockSpec((rows, D), lambda s, r: (block(s, r), 0)),
                  pl.BlockSpec((1, 1, 6, D), lambda s, r: (l, _mod_row(block(s, r), rows), 0, 0)),
                  lay(1, D), lay(1, D)],
        out_specs=pl.BlockSpec((rows, D), lambda s, r: (block(s, r) - rb_off + out_off, 0)),
        out_shape=jax.ShapeDtypeStruct((ROWS if in_place else n_seq * rb_per_seq * rb, D), F32),
        input_output_aliases={3: 0} if in_place else {},
        compiler_params=_cparams(("arbitrary", "arbitrary"), 48),
        name=f"combine_{cap}",
    )(crow, grow, ys, x1, mod, ln_g, ln_b)


def kernel(x_prompt, x_sample, cache_k, cache_v, state_s5_re, state_s5_im, c, c_ctx, ln_in_g, ln_in_b, w_ada, b_ada,
           w_in, w_fourier, attn_sink, s5_a_re, s5_a_im, s5_log_dt, s5_b_re, s5_b_im, s5_c_re, s5_c_im, s5_d,
           s5_w_glu, w_out, ln1_g, ln1_b, w_router, w_gate, w_up, w_down, ln2_g, ln2_b):
    x = (x_prompt.reshape(ROWS_CTX, D), x_sample.reshape(ROWS - ROWS_CTX, D), ln_in_g.reshape(1, D), ln_in_b.reshape(1, D))
    cond8 = jnp.concatenate([c_ctx[None], c, jnp.zeros((8 - 1 - N_LAT, D), F32)], axis=0)
    mod = _ada(cond8, w_ada, b_ada).reshape(DEPTH, 8, 6, D)

    rope_tabs = _rope_tables()
    cs_ctx, cs_lat = _dft_tables(T_CTX), _dft_tables(T_LAT)
    c64, s64 = _dft_channel_tables()
    cache_k = cache_k.transpose(0, 1, 3, 4, 2).reshape(N_LAT, DEPTH, KV_W, PAST)
    cache_v = cache_v.transpose(0, 1, 3, 4, 2).reshape(N_LAT, DEPTH, KV_W, PAST)
    w_router_t = jnp.swapaxes(w_router, 1, 2)
    sink_b = jnp.broadcast_to(attn_sink[:, :, None], (DEPTH, N_HEADS, LANES))
    s5_tabs = _s5_tables(s5_a_re, s5_a_im, s5_log_dt, s5_b_re, s5_b_im, s5_c_re, s5_c_im)
    lat_per_step = N_LAT // S5_STEPS_LAT
    s0 = jnp.concatenate([state_s5_re, state_s5_im, state_s5_im, state_s5_re], axis=-1)
    s0 = s0.reshape(S5_STEPS_LAT, lat_per_step, DEPTH, 2, S5_G, 4 * S5_P).transpose(2, 0, 3, 4, 1, 5)
    s0 = jnp.pad(s0, ((0, 0), (S5_STEPS_CTX, 0), (0, 0), (0, 0), (0, S5_SEQ - lat_per_step), (0, 0)))
    ln1_g, ln1_b, ln2_g, ln2_b, s5_d = (a.reshape(DEPTH, 1, -1) for a in (ln1_g, ln1_b, ln2_g, ln2_b, s5_d))

    new_s = []
    caches = (jnp.zeros(CACHE_SHAPE.shape, F32), jnp.zeros(CACHE_SHAPE.shape, F32))
    for l in range(DEPTH):
        q, k, v, uf, us, *caches = _inproj(x, mod, w_in, rope_tabs, caches, l)
        attn = _attn_lat(_attn_ctx(q, k, v, sink_b[l]), k, v, cache_k, cache_v, sink_b[l], l)
        four = _fourier(uf, cs_ctx, c64, s64, w_fourier, l, T_CTX, N_CTX, SEQ_PER_STEP_CTX, 0)
        four = _fourier(four, cs_lat, c64, s64, w_fourier, l, T_LAT, N_LAT, 1, ROWS_CTX // T_LAT)
        y5, fin = _s5(us, s5_tabs, s0, l)
        new_s.append(fin[:S5_STEPS_CTX].transpose(0, 3, 1, 2, 4).reshape(N_CTX, 2, S5_G, 2 * S5_P))
        x1, h2, lg = _outproj(attn, four, y5, us, s5_d, s5_w_glu, w_out, x, mod, ln1_g, ln1_b, w_router_t, l)
        crow_c, grow_c = _router(lg, T_CTX, N_CTX, 0, CAP_CTX)
        crow_q, grow_q = _router(lg, T_LAT, N_LAT, 1, CAP_LAT)
        xs_c = _gather(crow_c, h2, T_CTX, N_CTX, SEQ_PER_STEP_CTX, 0, CAP_CTX)
        xs_q = _gather(crow_q, h2, T_LAT, N_LAT, 1, ROWS_CTX // T_LAT, CAP_LAT)
        ys_c, ys_q = _ffn(xs_c, xs_q, w_gate, w_up, w_down, l)
        args_c = (crow_c, grow_c, ys_c), (mod, ln2_g, ln2_b, l, CAP_CTX, N_CTX, SEQ_PER_STEP_CTX, T_CTX, 1, 0)
        args_q = (crow_q, grow_q, ys_q), (mod, ln2_g, ln2_b, l, CAP_LAT, N_LAT, 1, RB_COMBINE_LAT,
                                          T_LAT // RB_COMBINE_LAT, ROWS_CTX // RB_COMBINE_LAT)
        if l < DEPTH - 1:
            x = _combine(*args_c[0], x1, *args_c[1], in_place=True)
            x = _combine(*args_q[0], x, *args_q[1], in_place=True)
        else:
            y_ctx = _combine(*args_c[0], x1, *args_c[1], in_place=False)
            y_lat = _combine(*args_q[0], x1, *args_q[1], in_place=False)

    new_s = jnp.stack(new_s, axis=1)

    def cache(feat_major):
        return feat_major.reshape(N_CTX, DEPTH, N_KV, HD, T_CTX).transpose(0, 1, 4, 2, 3)

    return (y_ctx.reshape(N_CTX, T_CTX, D), y_lat.reshape(N_LAT, T_LAT, D),
            cache(caches[0]), cache(caches[1]), new_s[..., :S5_P], new_s[..., S5_P:])
```

```python
import functools

import jax
import jax.numpy as jnp
import numpy as np
from jax import lax
from jax.experimental import pallas as pl
from jax.experimental.pallas import tpu as pltpu

F32 = jnp.float32
BF16 = jnp.bfloat16

D = 1024
N_CTX, T_CTX = 16, 256
N_LAT, T_LAT = 4, 1024
ROWS_CTX = N_CTX * T_CTX
ROWS = ROWS_CTX + N_LAT * T_LAT
DEPTH = 4
PAST = 512
GRID_W = 64
N_HEADS, N_KV, HD = 8, 2, 64
GQA = N_HEADS // N_KV
ATTN_W, KV_W = N_HEADS * HD, N_KV * HD
LOG2E = 1.4426950408889634
Q_SCALE = HD ** -0.5 * LOG2E
WINDOW = 128
FOUR_H, FOUR_W = 4, 256
S5_G, S5_H, S5_P, S5_W = 16, 16, 64, 256
S5_T = 16
IN_W = ATTN_W + 2 * KV_W + FOUR_W + S5_W
N_EXP, FF = 16, 1024
CAP_CTX, CAP_LAT = 2 * T_CTX // N_EXP, 2 * T_LAT // N_EXP
LN_EPS = 1e-5
NEG_INF = -1e30
ALPHA = (2 * DEPTH) ** 0.25
ROPE_BASE = 10000.0
RB_IN = 1024
RB_OUT = 512
RB_COMBINE_LAT = 512
LANES = 128
MIB = 2 ** 20


def _cparams(sem, vmem_mib):
    return pltpu.CompilerParams(dimension_semantics=sem, vmem_limit_bytes=vmem_mib * MIB)


def _dot(a, b):
    return jnp.dot(a, b, preferred_element_type=F32)


def _dot_nt(a, b):
    return lax.dot_general(a, b, (((1,), (1,)), ((), ())), preferred_element_type=F32)


def _dot_tn(a, b):
    return lax.dot_general(a, b, (((0,), (0,)), ((), ())), preferred_element_type=F32)


def _layer_norm(x, g, b):
    mu = jnp.mean(x, axis=-1, keepdims=True)
    xc = x - mu
    var = jnp.mean(xc * xc, axis=-1, keepdims=True)
    return xc * lax.rsqrt(var + LN_EPS) * g + b


def _mod_row(block, block_rows):
    first = block * block_rows
    return jnp.where(first < ROWS_CTX, 0, 1 + (first - ROWS_CTX) // T_LAT)


_HALVES_SHAPE = jax.ShapeDtypeStruct((2, ROWS, LANES), F32)


def _halves_spec(rows, row_block):
    return pl.BlockSpec((2, rows, LANES), lambda *ids: (0, row_block(*ids), 0))


def _stream_specs(x, rows):
    if not isinstance(x, tuple):
        return [pl.BlockSpec((rows, D), lambda i: (i, 0))]
    n_ctx = ROWS_CTX // rows
    vec = pl.BlockSpec((1, D), lambda i: (0, 0))
    return [pl.BlockSpec((rows, D), lambda i: (jnp.minimum(i, n_ctx - 1), 0)),
            pl.BlockSpec((rows, D), lambda i: (jnp.maximum(i - n_ctx, 0), 0)), vec, vec]


def _stream_args(x):
    return x if isinstance(x, tuple) else (x,)


def _stream_block(refs, rows):
    if len(refs) == 1:
        return refs[0][...]
    xc_ref, xq_ref, g_ref, b_ref = refs
    raw = jnp.where(pl.program_id(0) < ROWS_CTX // rows, xc_ref[...], xq_ref[...])
    return _layer_norm(raw, g_ref[...], b_ref[...])


def _ada_kernel(c_ref, w_ref, b_ref, o_ref):
    c = c_ref[...]
    s = (c * jax.nn.sigmoid(c)).astype(BF16)
    o_ref[0] = _dot(s, w_ref[0].astype(BF16)) + b_ref[0]


def _ada(cond8, w_ada, b_ada):
    tn = 1536
    return pl.pallas_call(
        _ada_kernel,
        grid=(DEPTH, 6 * D // tn),
        in_specs=[pl.BlockSpec((8, D), lambda l, j: (0, 0)),
                  pl.BlockSpec((1, D, tn), lambda l, j: (l, 0, j)),
                  pl.BlockSpec((1, 1, tn), lambda l, j: (l, 0, j))],
        out_specs=pl.BlockSpec((1, 8, tn), lambda l, j: (l, 0, j)),
        out_shape=jax.ShapeDtypeStruct((DEPTH, 8, 6 * D), F32),
        compiler_params=_cparams(("arbitrary", "arbitrary"), 40),
        name="adaln",
    )(cond8, w_ada, b_ada.reshape(DEPTH, 1, 6 * D))


def _inproj_kernel(*refs, n_x):
    x_refs = refs[:n_x]
    (mod_ref, w_ref, cos_ref, sin_ref, _, _, q_ref, k_ref, v_ref, uf_ref, us_ref, kc_ref, vc_ref,
     wb_ref, wswb_ref) = refs[n_x:]
    step = pl.program_id(0)
    n_ctx_steps = ROWS_CTX // RB_IN

    @pl.when(step == 0)
    def _():
        wb = w_ref[0].astype(BF16)
        wb_ref[...] = wb
        n_qk = ATTN_W + KV_W
        src = lax.broadcasted_iota(jnp.int32, (n_qk, n_qk), 0)
        dst = lax.broadcasted_iota(jnp.int32, (n_qk, n_qk), 1)
        swap = jnp.where(src == (dst ^ 1), 1.0, 0.0).astype(BF16)
        wswb_ref[...] = _dot(wb[:, :n_qk], swap).astype(BF16)

    m = mod_ref[0, 0]
    hb = (_stream_block(x_refs, RB_IN) * (1.0 + m[1:2]) + m[0:1]).astype(BF16)
    p = _dot(hb, wb_ref[...])
    q = p[:, :ATTN_W]
    k = p[:, ATTN_W:ATTN_W + KV_W]
    v = p[:, ATTN_W + KV_W:ATTN_W + 2 * KV_W]

    psw = _dot(hb, wswb_ref[...])
    cos, sin = cos_ref[...], sin_ref[...]
    wide = lambda t: jnp.concatenate([t] * (ATTN_W // KV_W), axis=1)
    latent = step >= n_ctx_steps
    q = jnp.where(latent, q * wide(cos) + psw[:, :ATTN_W] * wide(sin), q)
    k = jnp.where(latent, k * cos + psw[:, ATTN_W:] * sin, k)
    q_ref[...] = (q * Q_SCALE).astype(BF16)
    k_ref[...] = k.astype(BF16)
    v_ref[...] = v.astype(BF16)

    @pl.when(step < n_ctx_steps)
    def _():
        for j in range(RB_IN // T_CTX):
            rows = slice(j * T_CTX, (j + 1) * T_CTX)
            kc_ref[j, 0] = k[rows].T
            vc_ref[j, 0] = v[rows].T

    uf_ref[...] = p[:, ATTN_W + 2 * KV_W:ATTN_W + 2 * KV_W + FOUR_W].astype(BF16)
    for half in range(S5_W // LANES):
        lo = ATTN_W + 2 * KV_W + FOUR_W + half * LANES
        us_ref[half] = p[:, lo:lo + LANES]


CACHE_SHAPE = jax.ShapeDtypeStruct((N_CTX, DEPTH, KV_W, T_CTX), F32)


def _inproj(x, mod, w_in, rope_tabs, caches, l):
    assert RB_IN == T_LAT
    outs = ((ATTN_W, BF16), (KV_W, BF16), (KV_W, BF16), (FOUR_W, BF16))
    table = pl.BlockSpec((T_LAT, KV_W), lambda i: (0, 0))
    any_space = pl.BlockSpec(memory_space=pl.ANY)
    slab = pl.BlockSpec((RB_IN // T_CTX, 1, KV_W, T_CTX), lambda i: (jnp.minimum(i, ROWS_CTX // RB_IN - 1), l, 0, 0))
    x_specs = _stream_specs(x, RB_IN)
    n_in = len(x_specs) + 4
    return pl.pallas_call(
        functools.partial(_inproj_kernel, n_x=len(x_specs)),
        grid=(ROWS // RB_IN,),
        in_specs=x_specs + [pl.BlockSpec((1, 1, 6, D), lambda i: (l, _mod_row(i, RB_IN), 0, 0)),
                            pl.BlockSpec((1, D, IN_W), lambda i: (l, 0, 0)),
                            table, table, any_space, any_space],
        out_specs=([pl.BlockSpec((RB_IN, w), lambda i: (i, 0)) for w, _ in outs] + [_halves_spec(RB_IN, lambda i: i)]
                   + [slab, slab]),
        out_shape=([jax.ShapeDtypeStruct((ROWS, w), dt) for w, dt in outs] + [_HALVES_SHAPE] + [CACHE_SHAPE] * 2),
        input_output_aliases={n_in: 5, n_in + 1: 6},
        scratch_shapes=[pltpu.VMEM((D, IN_W), BF16), pltpu.VMEM((D, ATTN_W + KV_W), BF16)],
        compiler_params=_cparams(("arbitrary",), 56),
        name="inproj",
    )(*_stream_args(x), mod, w_in, *rope_tabs, *caches)


SINK_ROWS = 16


def _softmax_av(score_blocks, sink, value_blocks):
    m = sink
    for s in score_blocks:
        m = jnp.maximum(m, jnp.max(s, axis=0, keepdims=True))
    first = lax.broadcasted_iota(jnp.int32, (SINK_ROWS, 1), 0) == 0
    probs = [jnp.exp2(s - m).astype(BF16) for s in score_blocks]
    probs.append(jnp.where(first, jnp.exp2(sink - m), 0.0).astype(BF16))
    ext = []
    for v in value_blocks:
        ones = jnp.ones(v.shape, BF16)
        ext.append(jnp.concatenate([v, v, ones, ones], axis=1))
    row = lax.broadcasted_iota(jnp.int32, (SINK_ROWS, 4 * HD), 0)
    lane = lax.broadcasted_iota(jnp.int32, (SINK_ROWS, 4 * HD), 1)
    ext.append(jnp.where((row == 0) & (lane >= 2 * HD), 1.0, 0.0).astype(BF16))
    acc = _dot_tn(jnp.concatenate(probs, axis=0), jnp.concatenate(ext, axis=0))
    return acc[:, :2 * HD] / acc[:, 2 * HD:]


def _stack_group(q, kv):
    return jnp.concatenate([q[:, (kv * GQA + g) * HD:(kv * GQA + g + 1) * HD] for g in range(GQA)], axis=0)


def _sink_row(sink_ref, kv, rows):
    return LOG2E * jnp.concatenate(
        [jnp.broadcast_to(sink_ref[kv * GQA + g:kv * GQA + g + 1, 0:1], (1, rows)) for g in range(GQA)], axis=1)


def _unstack_group(o, rows):
    lane = lax.broadcasted_iota(jnp.int32, (rows, 2 * HD), 1)
    return [jnp.where(lane < HD, o[2 * j * rows:(2 * j + 1) * rows], o[(2 * j + 1) * rows:(2 * j + 2) * rows])
            for j in range(GQA // 2)]


def _attn_ctx_kernel(q_ref, k_ref, v_ref, sink_ref, o_ref):
    q = q_ref[...]
    k = k_ref[...]
    v = v_ref[...]
    outs = []
    for kv in range(N_KV):
        head = slice(kv * HD, (kv + 1) * HD)
        s = _dot_nt(k[:, head], _stack_group(q, kv))
        o = _softmax_av([s], _sink_row(sink_ref, kv, T_CTX), [v[:, head]])
        outs += _unstack_group(o, T_CTX)
    o_ref[...] = jnp.concatenate(outs, axis=1).astype(BF16)


def _attn_ctx(q, k, v, sink_b):
    return pl.pallas_call(
        _attn_ctx_kernel,
        grid=(N_CTX,),
        in_specs=[pl.BlockSpec((T_CTX, ATTN_W), lambda s: (s, 0)),
                  pl.BlockSpec((T_CTX, KV_W), lambda s: (s, 0)),
                  pl.BlockSpec((T_CTX, KV_W), lambda s: (s, 0)),
                  pl.BlockSpec((N_HEADS, LANES), lambda s: (0, 0))],
        out_specs=pl.BlockSpec((T_CTX, ATTN_W), lambda s: (s, 0)),
        out_shape=jax.ShapeDtypeStruct((ROWS, ATTN_W), BF16),
        input_output_aliases={0: 0},
        compiler_params=_cparams(("arbitrary",), 32),
        name="attn_ctx",
    )(q, k, v, sink_b)


def _attn_lat_kernel(q_ref, k_ref, v_ref, ck_ref, cv_ref, sink_ref, o_ref, kc_scr, vc_scr):
    i = pl.program_id(1)
    n_blk = T_LAT // WINDOW
    q = q_ref[...]

    @pl.when(i == 0)
    def _():
        kc_scr[...] = ck_ref[0, 0].T.astype(BF16)
        vc_scr[...] = cv_ref[0, 0].T.astype(BF16)

    def kv_block(j):
        st = pl.multiple_of(j * WINDOW, WINDOW)
        return k_ref[pl.ds(st, WINDOW), :], v_ref[pl.ds(st, WINDOW), :]

    k0, v0 = kv_block(jnp.maximum(i - 1, 0))
    k1, v1 = kv_block(i)
    k2, v2 = kv_block(jnp.minimum(i + 1, n_blk - 1))
    k_win = jnp.concatenate([k0, k1, k2], axis=0)
    k_ctx = kc_scr[...]
    v_ctx = vc_scr[...]
    rows = GQA * WINDOW
    c = lax.broadcasted_iota(jnp.int32, (WINDOW, rows), 0)
    r = lax.broadcasted_iota(jnp.int32, (WINDOW, rows), 1) & (WINDOW - 1)
    far = 2 * WINDOW
    prev_ok = c >= r + jnp.where(i > 0, 0, far)
    next_ok = c + jnp.where(i < n_blk - 1, 0, far) <= r
    outs = []
    for kv in range(N_KV):
        head = slice(kv * HD, (kv + 1) * HD)
        qs = _stack_group(q, kv)
        s_win = _dot_nt(k_win[:, head], qs)
        scores = [jnp.where(prev_ok, s_win[:WINDOW], NEG_INF), s_win[WINDOW:2 * WINDOW],
                  jnp.where(next_ok, s_win[2 * WINDOW:], NEG_INF), _dot_nt(k_ctx[:, head], qs)]
        values = [v0[:, head], v1[:, head], v2[:, head], v_ctx[:, head]]
        outs += _unstack_group(_softmax_av(scores, _sink_row(sink_ref, kv, WINDOW), values), WINDOW)
    o_ref[...] = jnp.concatenate(outs, axis=1).astype(BF16)


def _attn_lat(q, k, v, cache_k, cache_v, sink_b, l):
    n_blk = T_LAT // WINDOW
    off = ROWS_CTX // WINDOW
    return pl.pallas_call(
        _attn_lat_kernel,
        grid=(N_LAT, n_blk),
        in_specs=[pl.BlockSpec((WINDOW, ATTN_W), lambda b, i: (off + b * n_blk + i, 0)),
                  pl.BlockSpec((T_LAT, KV_W), lambda b, i: (ROWS_CTX // T_LAT + b, 0)),
                  pl.BlockSpec((T_LAT, KV_W), lambda b, i: (ROWS_CTX // T_LAT + b, 0)),
                  pl.BlockSpec((1, 1, KV_W, PAST), lambda b, i: (b, l, 0, 0)),
                  pl.BlockSpec((1, 1, KV_W, PAST), lambda b, i: (b, l, 0, 0)),
                  pl.BlockSpec((N_HEADS, LANES), lambda b, i: (0, 0))],
        out_specs=pl.BlockSpec((WINDOW, ATTN_W), lambda b, i: (off + b * n_blk + i, 0)),
        out_shape=jax.ShapeDtypeStruct((ROWS, ATTN_W), BF16),
        scratch_shapes=[pltpu.VMEM((PAST, KV_W), BF16), pltpu.VMEM((PAST, KV_W), BF16)],
        input_output_aliases={0: 0},
        compiler_params=_cparams(("arbitrary", "arbitrary"), 40),
        name="attn_lat",
    )(q, k, v, cache_k, cache_v, sink_b)


def _rope_tables():
    rows = T_LAT // GRID_W
    row = jnp.repeat(jnp.arange(rows, dtype=F32), GRID_W)
    col = jnp.tile(jnp.arange(GRID_W, dtype=F32), rows)
    n_freq = HD // 4
    freqs = ROPE_BASE ** (-jnp.arange(n_freq, dtype=F32) / n_freq)
    ang = jnp.concatenate([row[:, None] * freqs, col[:, None] * freqs], axis=-1)
    cos = jnp.repeat(jnp.cos(ang), 2, axis=-1)
    sign = jnp.tile(jnp.array([-1.0, 1.0], F32), HD // 2)
    sin = jnp.repeat(jnp.sin(ang), 2, axis=-1) * sign
    return jnp.tile(cos, (1, N_KV)), jnp.tile(sin, (1, N_KV))


SEQ_PER_STEP_CTX = 4


def _four_kernel(u_ref, cs_ref, c64_ref, s64_ref, wf_ref, o_ref, *, n, n_sub):
    ub = u_ref[...]
    uc = _dot(ub, c64_ref[...]).astype(BF16)
    us = _dot(ub, s64_ref[...]).astype(BF16)
    f = jnp.concatenate([_dot(cs_ref[...], jnp.concatenate([uc[j * n:(j + 1) * n], us[j * n:(j + 1) * n]], axis=0))
                         for j in range(n_sub)], axis=0)
    o_ref[...] = _dot(f.astype(BF16), wf_ref[0].astype(BF16)).astype(BF16)


def _dft_tables(n):
    j = np.arange(n)
    ang = 2.0 * np.pi * ((j[:, None] * j[None, :]) % n) / n
    cs = np.concatenate([np.cos(ang), -np.sin(ang)], axis=1) / np.sqrt(n)
    return jnp.asarray(cs.astype(np.float32)).astype(BF16)


def _dft_channel_tables():
    j = np.arange(HD)
    ang = 2.0 * np.pi * ((j[:, None] * j[None, :]) % HD) / HD
    eye = np.eye(FOUR_H)
    c = np.kron(eye, np.cos(ang)) / np.sqrt(HD)
    s = np.kron(eye, np.sin(ang)) / np.sqrt(HD)
    return jnp.asarray(c.astype(np.float32)).astype(BF16), jnp.asarray(s.astype(np.float32)).astype(BF16)


def _fourier(uf, cs, c64, s64, w_fourier, l, n, n_seq, n_sub, blk_off):
    rows = n_sub * n
    return pl.pallas_call(
        functools.partial(_four_kernel, n=n, n_sub=n_sub),
        grid=(n_seq // n_sub,),
        in_specs=[pl.BlockSpec((rows, FOUR_W), lambda s: (blk_off + s, 0)),
                  pl.BlockSpec((n, 2 * n), lambda s: (0, 0)),
                  pl.BlockSpec((FOUR_W, FOUR_W), lambda s: (0, 0)),
                  pl.BlockSpec((FOUR_W, FOUR_W), lambda s: (0, 0)),
                  pl.BlockSpec((1, FOUR_W, FOUR_W), lambda s: (l, 0, 0))],
        out_specs=pl.BlockSpec((rows, FOUR_W), lambda s: (blk_off + s, 0)),
        out_shape=jax.ShapeDtypeStruct((ROWS, FOUR_W), BF16),
        input_output_aliases={0: 0},
        compiler_params=_cparams(("arbitrary",), 40),
        name=f"fourier_{n}",
    )(uf, cs, c64, s64, w_fourier)


S5_TW = S5_T * S5_H
S5_R = 128
S5_TOK = S5_R * S5_T
S5_STEPS_CTX = ROWS_CTX // S5_TOK
S5_STEPS_LAT = (ROWS - ROWS_CTX) // S5_TOK
S5_STEPS = S5_STEPS_CTX + S5_STEPS_LAT
S5_SEQ = N_CTX // S5_STEPS_CTX


def _s5_tables(a_re, a_im, log_dt, b_re, b_im, c_re, c_im):
    f = lambda a: a.astype(F32)
    a_re, a_im, b_re, b_im, c_re, c_im = map(f, (a_re, a_im, b_re, b_im, c_re, c_im))
    dt = jnp.exp(f(log_dt))[..., None]
    x, y = a_re * dt, a_im * dt
    kk = jnp.arange(S5_T + 1, dtype=F32)[None, None, :, None, None]
    mag = jnp.exp(kk * x[:, :, None])
    pr, pi = mag * jnp.cos(kk * y[:, :, None]), mag * jnp.sin(kk * y[:, :, None])
    nr, ni = pr[:, :, 1] - 1.0, pi[:, :, 1]
    den = a_re * a_re + a_im * a_im
    qr, qi = (nr * a_re + ni * a_im) / den, (ni * a_re - nr * a_im) / den
    bb_r = qr[..., None] * b_re - qi[..., None] * b_im
    bb_i = qr[..., None] * b_im + qi[..., None] * b_re
    cat = lambda u, v: jnp.concatenate([u, v], axis=-1)
    prg, pig = pr.transpose(0, 1, 3, 2, 4), pi.transpose(0, 1, 3, 2, 4)
    def lane_pairs(re, im, patterns):
        coef = lambda k: jnp.asarray(np.repeat(np.array([[p[0][k], p[1][k]] for p in patterns], np.float32),
                                               re.shape[-1], axis=-1))[:, None, :]
        return cat(re, re)[..., None, :, :] * coef(0) + cat(im, im)[..., None, :, :] * coef(1)

    keep_re, keep_im, neg_re, neg_im = (1, 0), (0, 1), (-1, 0), (0, -1)
    pows = lane_pairs(prg, pig, [(keep_re, keep_re), (neg_im, keep_im), (keep_re, neg_im), (neg_im, neg_re),
                                 (keep_re, keep_im), (neg_im, keep_re)])
    bt_r, bt_i = bb_r.transpose(0, 1, 2, 4, 3), bb_i.transpose(0, 1, 2, 4, 3)
    wrows = jnp.concatenate([lane_pairs(bt_r, bt_i, [(keep_re, keep_im), (keep_im, keep_re), (keep_re, neg_im)]),
                             lane_pairs(c_re, c_im, [(keep_re, keep_re), (keep_im, keep_im)])],
                            axis=3)
    ar, ai = pr[:, :, S5_T], pi[:, :, S5_T]
    decay = jnp.stack([cat(ar, ar), cat(-ai, ai), cat(ai, -ai), jnp.zeros_like(cat(ar, ar))], axis=3)
    return pows, wrows, decay


def _block_transpose(arrs):
    arrs = list(arrs)
    width = arrs[0].shape[1]
    blk = lax.broadcasted_iota(jnp.int32, arrs[0].shape, 1) >> 4
    s = len(arrs) // 2
    while s >= 1:
        keep = (blk & s) == 0
        for i in range(len(arrs)):
            if i & s:
                continue
            lo, hi = arrs[i], arrs[i + s]
            arrs[i] = jnp.where(keep, lo, pltpu.roll(hi, s * S5_H, 1))
            arrs[i + s] = jnp.where(keep, pltpu.roll(lo, width - s * S5_H, 1), hi)
        s //= 2
    return arrs


def _s5_build_operators(pw_ref, w_ref, e_scr, esw_scr, ft_scr, tp_scr):
    lane = lax.broadcasted_iota(jnp.int32, (S5_H, S5_TW), 1)
    for d in range(2):
        for g in range(S5_G):
            pw = lambda v, k: pw_ref[0, d, g, v, k:k + 1, :]
            b_ri, b_ir, b_conj, c_rr, c_ii = (w_ref[0, d, g, v] for v in range(5))
            lag_rows = []
            for t in range(S5_T):
                blk = slice(t * S5_H, (t + 1) * S5_H)
                ke = S5_T - 1 - t if d == 0 else t
                e_scr[d, g, blk, :] = (b_ri * pw(0, ke) + b_ir * pw(1, ke)).astype(BF16)
                esw_scr[d, g, blk, :] = (b_ir * pw(0, ke) - b_ri * pw(1, ke)).astype(BF16)
                kf = t + 1 if d == 0 else S5_T - t
                ft_scr[d, g, blk, :] = (c_rr * pw(2, kf) + c_ii * pw(3, kf)).astype(BF16)
                kl = t if d == 0 else S5_T - 1 - t
                lag_rows.append(c_rr * pw(4, kl) + c_ii * pw(5, kl))
            mh, ml = _split_bf16(jnp.concatenate(lag_rows, axis=0))
            bh, bl = _split_bf16(b_conj)
            kt = _dot_nt(bh, mh) + _dot_nt(bh, ml) + _dot_nt(bl, mh)
            for ti in range(S5_T):
                if d == 0:
                    sh = S5_H * ti
                    blk_rows = jnp.where(lane >= sh, pltpu.roll(kt, sh, 1) if sh else kt, 0.0)
                else:
                    sh = S5_H * (S5_T - 1 - ti)
                    blk_rows = jnp.where(lane < S5_TW - sh, pltpu.roll(kt, S5_TW - sh, 1) if sh else kt, 0.0)
                tp_scr[d, g, ti * S5_H:(ti + 1) * S5_H, :] = blk_rows.astype(BF16)


def _s5_kernel(us_ref, pw_ref, w_ref, a_ref, s0_ref, y_ref, fin_ref,
               e_scr, esw_scr, ft_scr, tp_scr, x_scr, y_scr, loc_scr, lsw_scr, prev_scr):
    step = pl.program_id(0)

    @pl.when(step == 0)
    def _():
        _s5_build_operators(pw_ref, w_ref, e_scr, esw_scr, ft_scr, tp_scr)

    rows = pl.ds
    x_by_step = [jnp.concatenate([us_ref[0, rows(tl, S5_R, stride=S5_T), :],
                                  us_ref[1, rows(tl, S5_R, stride=S5_T), :]], axis=1).astype(BF16)
                 for tl in range(S5_T)]
    for g, xg in enumerate(_block_transpose(x_by_step)):
        x_scr[g] = xg
    fin_ref[...] = jnp.zeros(fin_ref.shape, F32)

    def scan(d, nseq, nc):
        batch = S5_G
        for g0 in range(0, S5_G, batch):
            init = tuple((s0_ref[0, 0, d, g, 0:nseq, :2 * S5_P], s0_ref[0, 0, d, g, 0:nseq, 2 * S5_P:])
                         for g in range(g0, g0 + batch))

            def body(j, carry):
                ci = j if d == 0 else nc - 1 - j
                out = []
                for idx in range(batch):
                    g = g0 + idx
                    s, t = carry[idx]
                    ca, cb, cc = (a_ref[0, d, g, r:r + 1, :] for r in range(3))
                    prev_scr[g, rows(ci, nseq, stride=nc), :] = s
                    out.append((s * ca + t * cb + loc_scr[g, rows(ci, nseq, stride=nc), :],
                                t * ca + s * cc + lsw_scr[g, rows(ci, nseq, stride=nc), :]))
                return tuple(out)

            fin = lax.fori_loop(0, nc, body, init)
            for idx in range(batch):
                fin_ref[0, d, g0 + idx, 0:nseq, :] = fin[idx][0]

    for d in range(2):
        for g in range(S5_G):
            xg = x_scr[g]
            loc_scr[g] = _dot(xg, e_scr[d, g])
            lsw_scr[g] = _dot(xg, esw_scr[d, g])

        @pl.when(step < S5_STEPS_CTX)
        def _():
            scan(d, N_CTX // S5_STEPS_CTX, T_CTX // S5_T)

        @pl.when(step >= S5_STEPS_CTX)
        def _():
            scan(d, N_LAT // S5_STEPS_LAT, T_LAT // S5_T)

        for g in range(S5_G):
            yg = _dot(x_scr[g], tp_scr[d, g]) + _dot_nt(prev_scr[g].astype(BF16), ft_scr[d, g])
            if d == 0:
                y_scr[g] = yg
            else:
                y_scr[g] += yg
    for to, y_to in enumerate(_block_transpose([y_scr[g] for g in range(S5_G)])):
        y_ref[0, rows(to, S5_R, stride=S5_T), :] = y_to[:, :LANES]
        y_ref[1, rows(to, S5_R, stride=S5_T), :] = y_to[:, LANES:]


def _s5(us, tabs, s0, l):
    pows, wrows, decay = tabs
    lay = lambda *shape: pl.BlockSpec((1,) + shape, lambda h: (l,) + (0,) * len(shape))
    op_buf = pltpu.VMEM((2, S5_G, S5_TW, 2 * S5_P), BF16)
    scan_buf = pltpu.VMEM((S5_G, S5_R, 2 * S5_P), F32)
    return pl.pallas_call(
        _s5_kernel,
        grid=(S5_STEPS,),
        in_specs=[_halves_spec(S5_TOK, lambda h: h),
                  lay(2, S5_G, 6, S5_T + 1, 2 * S5_P), lay(2, S5_G, 5, S5_H, 2 * S5_P), lay(2, S5_G, 4, 2 * S5_P),
                  pl.BlockSpec((1, 1, 2, S5_G, S5_SEQ, 4 * S5_P), lambda h: (l, h, 0, 0, 0, 0))],
        out_specs=[_halves_spec(S5_TOK, lambda h: h),
                   pl.BlockSpec((1, 2, S5_G, S5_SEQ, 2 * S5_P), lambda h: (h, 0, 0, 0, 0))],
        out_shape=[_HALVES_SHAPE, jax.ShapeDtypeStruct((S5_STEPS, 2, S5_G, S5_SEQ, 2 * S5_P), F32)],
        scratch_shapes=[op_buf, op_buf, op_buf, pltpu.VMEM((2, S5_G, S5_TW, S5_TW), BF16),
                        pltpu.VMEM((S5_G, S5_R, S5_TW), BF16), pltpu.VMEM((S5_G, S5_R, S5_TW), F32),
                        scan_buf, scan_buf, scan_buf],
        compiler_params=_cparams(("arbitrary",), 56),
        name="s5",
    )(us, pows, wrows, decay, s0)


def _split_bf16(a):
    hi = a.astype(BF16)
    return hi, (a - hi.astype(F32)).astype(BF16)


def _outproj_kernel(*refs, n_x):
    x_refs = refs[:n_x]
    (at_ref, fo_ref, ys_ref, us_ref, d_ref, wg_ref, wo_ref, mod_ref, g_ref, b_ref, wr_ref,
     x1_ref, h2_ref, lg_ref, wob_ref) = refs[n_x:]

    @pl.when(pl.program_id(0) == 0)
    def _():
        wob_ref[...] = wo_ref[0].astype(BF16)

    ys = jnp.concatenate([ys_ref[0], ys_ref[1]], axis=1)
    us = jnp.concatenate([us_ref[0], us_ref[1]], axis=1)
    g = jax.nn.gelu(ys + us * d_ref[0])
    s5 = g * jax.nn.sigmoid(_dot(g.astype(BF16), wg_ref[0].astype(BF16)))
    mix = (_dot(at_ref[...], wob_ref[0:ATTN_W, :])
           + _dot(fo_ref[...], wob_ref[ATTN_W:ATTN_W + FOUR_W, :])
           + _dot(s5.astype(BF16), wob_ref[ATTN_W + FOUR_W:, :]))
    m = mod_ref[0, 0]
    r = ALPHA * _stream_block(x_refs, RB_OUT) + m[2:3] * mix
    rc = r - jnp.mean(r, axis=-1, keepdims=True)
    t = rc * lax.rsqrt(jnp.mean(rc * rc, axis=-1, keepdims=True) + LN_EPS)
    gain, bias = g_ref[0], b_ref[0]
    x1_ref[...] = t * gain + bias
    up = 1.0 + m[4:5]
    hb = (t * (gain * up) + (bias * up + m[3:4])).astype(BF16)
    h2_ref[...] = hb
    lg_ref[...] = _dot_nt(wr_ref[0].astype(BF16), hb)


def _outproj(attn, four, ys5, us, s5_d, w_glu, w_out, x, mod, ln_g, ln_b, w_router_t, l):
    row = lambda w: pl.BlockSpec((RB_OUT, w), lambda i: (i, 0))
    halves = _halves_spec(RB_OUT, lambda i: i)
    lay = lambda *shape: pl.BlockSpec((1,) + shape, lambda i: (l,) + (0,) * len(shape))
    x_specs = _stream_specs(x, RB_OUT)
    return pl.pallas_call(
        functools.partial(_outproj_kernel, n_x=len(x_specs)),
        grid=(ROWS // RB_OUT,),
        in_specs=x_specs + [row(ATTN_W), row(FOUR_W), halves, halves, lay(1, S5_W), lay(S5_W, S5_W), lay(D, D),
                            pl.BlockSpec((1, 1, 6, D), lambda i: (l, _mod_row(i, RB_OUT), 0, 0)),
                            lay(1, D), lay(1, D), lay(N_EXP, D)],
        out_specs=[row(D), row(D), pl.BlockSpec((N_EXP, RB_OUT), lambda i: (0, i))],
        out_shape=[jax.ShapeDtypeStruct((ROWS, D), F32), jax.ShapeDtypeStruct((ROWS, D), BF16),
                   jax.ShapeDtypeStruct((N_EXP, ROWS), F32)],
        scratch_shapes=[pltpu.VMEM((D, D), BF16)],
        compiler_params=_cparams(("arbitrary",), 40),
        name="outproj",
    )(*_stream_args(x), attn, four, ys5, us, s5_d, w_glu, w_out, mod, ln_g, ln_b, w_router_t)


def _router_kernel(lg_ref, crow_ref, grow_ref, *, n, cap, n_seq):
    rows = n_seq * N_EXP
    lg = jnp.concatenate([lg_ref[:, s * n:(s + 1) * n] for s in range(n_seq)], axis=0).reshape(n_seq, N_EXP, n)
    e = jnp.exp(lg - jnp.max(lg, axis=1, keepdims=True))
    aff = (e / jnp.sum(e, axis=1, keepdims=True)).reshape(rows, n)
    thr_bits = jnp.zeros((rows, 1), jnp.int32)
    for bit in range(30, -1, -1):
        cand = thr_bits | (1 << bit)
        cnt = jnp.sum(jnp.where(aff >= lax.bitcast_convert_type(cand, F32), 1.0, 0.0), axis=1, keepdims=True)
        thr_bits = jnp.where(cnt >= cap, cand, thr_bits)
    thr = lax.bitcast_convert_type(thr_bits, F32)
    above = aff > thr
    tied = aff == thr
    need = cap - jnp.sum(jnp.where(above, 1.0, 0.0), axis=1, keepdims=True)
    r0 = lax.broadcasted_iota(jnp.int32, (n, n), 0)
    r1 = lax.broadcasted_iota(jnp.int32, (n, n), 1)
    before = jnp.where(r0 < r1, 1.0, 0.0).astype(BF16)
    tied_rank = _dot(jnp.where(tied, 1.0, 0.0).astype(BF16), before)
    sel = above | (tied & (tied_rank < need))
    slot = _dot(jnp.where(sel, 1.0, 0.0).astype(BF16), before)
    code = jnp.where(sel, slot + 1.0, 0.0)
    gate = jnp.where(sel, aff, 0.0)
    for s in range(n_seq):
        crow_ref[:, s * n:(s + 1) * n] = code[s * N_EXP:(s + 1) * N_EXP]
        grow_ref[:, s * n:(s + 1) * n] = gate[s * N_EXP:(s + 1) * N_EXP]


def _router(lg, n, n_seq, blk, cap):
    table = pl.BlockSpec((N_EXP, n_seq * n), lambda i: (0, 0))
    return pl.pallas_call(
        functools.partial(_router_kernel, n=n, cap=cap, n_seq=n_seq),
        grid=(1,),
        in_specs=[pl.BlockSpec((N_EXP, n_seq * n), lambda i: (0, blk))],
        out_specs=[table, table],
        out_shape=[jax.ShapeDtypeStruct((N_EXP, n_seq * n), F32)] * 2,
        compiler_params=_cparams(("arbitrary",), 48),
        name=f"router_{n}",
    )(lg)


def _gather_kernel(crow_ref, h_ref, o_ref, *, n, cap, n_sub):
    slot1 = (lax.broadcasted_iota(jnp.int32, (cap, n), 0) + 1).astype(F32)
    for j in range(n_sub):
        code = crow_ref[:, j * n:(j + 1) * n]
        onehot = jnp.concatenate(
            [jnp.where(code[e:e + 1, :] == slot1, 1.0, 0.0).astype(BF16) for e in range(N_EXP)], axis=0)
        xs = _dot(onehot, h_ref[j * n:(j + 1) * n, :])
        o_ref[:, j * cap:(j + 1) * cap, :] = xs.reshape(N_EXP, cap, D).astype(BF16)


def _gather(crow, h2, n, n_seq, n_sub, blk_off, cap):
    return pl.pallas_call(
        functools.partial(_gather_kernel, n=n, cap=cap, n_sub=n_sub),
        grid=(n_seq // n_sub,),
        in_specs=[pl.BlockSpec((N_EXP, n_sub * n), lambda s: (0, s)),
                  pl.BlockSpec((n_sub * n, D), lambda s: (blk_off + s, 0))],
        out_specs=pl.BlockSpec((N_EXP, n_sub * cap, D), lambda s: (0, s, 0)),
        out_shape=jax.ShapeDtypeStruct((N_EXP, n_seq * cap, D), BF16),
        compiler_params=_cparams(("arbitrary",), 48),
        name=f"gather_{n}",
    )(crow, h2)


SLOTS = N_CTX * CAP_CTX


def _ffn_kernel(xc_ref, xq_ref, wg_ref, wu_ref, wd_ref, yc_ref, yq_ref, wgb_ref, wub_ref, wdb_ref):
    wgb_ref[...] = wg_ref[0, 0].astype(BF16)
    wub_ref[...] = wu_ref[0, 0].astype(BF16)
    wdb_ref[...] = wd_ref[0, 0].astype(BF16)
    for x_ref, y_ref in ((xc_ref, yc_ref), (xq_ref, yq_ref)):
        x = x_ref[0]
        g = _dot(x, wgb_ref[...])
        u = _dot(x, wub_ref[...])
        hid = (g * jax.nn.sigmoid(g) * u).astype(BF16)
        y_ref[0] = _dot(hid, wdb_ref[...]).astype(BF16)


def _ffn(xs_c, xs_q, w_gate, w_up, w_down, l):
    xspec = pl.BlockSpec((1, SLOTS, D), lambda e: (e, 0, 0))
    wspec = lambda rows, cols: pl.BlockSpec((1, 1, rows, cols), lambda e: (l, e, 0, 0))
    return pl.pallas_call(
        _ffn_kernel,
        grid=(N_EXP,),
        in_specs=[xspec, xspec, wspec(D, FF), wspec(D, FF), wspec(FF, D)],
        out_specs=[xspec, xspec],
        out_shape=[jax.ShapeDtypeStruct((N_EXP, SLOTS, D), BF16)] * 2,
        scratch_shapes=[pltpu.VMEM((D, FF), BF16), pltpu.VMEM((D, FF), BF16), pltpu.VMEM((FF, D), BF16)],
        compiler_params=_cparams(("arbitrary",), 56),
        name="ffn",
    )(xs_c, xs_q, w_gate, w_up, w_down)


def _combine_kernel(crow_ref, grow_ref, ys_ref, x_ref, mod_ref, g_ref, b_ref, o_ref, *, cap, rb, n_sub):
    width = N_EXP * cap
    slot1 = (lax.broadcasted_iota(jnp.int32, (cap, rb), 0) + 1).astype(F32)
    m = mod_ref[0, 0]
    for j in range(n_sub):
        rows = slice(j * rb, (j + 1) * rb)
        code, gate = crow_ref[:, rows], grow_ref[:, rows]
        weights_t = jnp.concatenate(
            [jnp.where(code[e:e + 1, :] == slot1, gate[e:e + 1, :], 0.0).astype(BF16) for e in range(N_EXP)], axis=0)
        moe = _dot_tn(weights_t, ys_ref[:, j * cap:(j + 1) * cap, :].reshape(width, D))
        o_ref[rows, :] = _layer_norm(ALPHA * x_ref[rows, :] + m[5:6] * moe, g_ref[0], b_ref[0])


def _combine(crow, grow, ys, x1, mod, ln_g, ln_b, l, cap, n_seq, n_sub, rb, rb_per_seq, rb_off, in_place):
    assert n_sub == 1 or rb_per_seq == 1
    rows = n_sub * rb
    lay = lambda *shape: pl.BlockSpec((1,) + shape, lambda s, r: (l,) + (0,) * len(shape))
    table = pl.BlockSpec((N_EXP, rows), lambda s, r: (0, s * rb_per_seq + r))
    block = lambda s, r: rb_off + s * rb_per_seq + r
    out_off = rb_off if in_place else 0
    return pl.pallas_call(
        functools.partial(_combine_kernel, cap=cap, rb=rb, n_sub=n_sub),
        grid=(n_seq // n_sub, rb_per_seq),
        in_specs=[table, table,
                  pl.BlockSpec((N_EXP, n_sub * cap, D), lambda s, r: (0, s, 0)),
                  pl.BlockSpec((rows, D), lambda s, r: (block(s, r), 0)),
                  pl.BlockSpec((1, 1, 6, D), lambda s, r: (l, _mod_row(block(s, r), rows), 0, 0)),
                  lay(1, D), lay(1, D)],
        out_specs=pl.BlockSpec((rows, D), lambda s, r: (block(s, r) - rb_off + out_off, 0)),
        out_shape=jax.ShapeDtypeStruct((ROWS if in_place else n_seq * rb_per_seq * rb, D), F32),
        input_output_aliases={3: 0} if in_place else {},
        compiler_params=_cparams(("arbitrary", "arbitrary"), 48),
        name=f"combine_{cap}",
    )(crow, grow, ys, x1, mod, ln_g, ln_b)


def kernel(x_prompt, x_sample, cache_k, cache_v, state_s5_re, state_s5_im, c, c_ctx, ln_in_g, ln_in_b, w_ada, b_ada,
           w_in, w_fourier, attn_sink, s5_a_re, s5_a_im, s5_log_dt, s5_b_re, s5_b_im, s5_c_re, s5_c_im, s5_d,
           s5_w_glu, w_out, ln1_g, ln1_b, w_router, w_gate, w_up, w_down, ln2_g, ln2_b):
    x = (x_prompt.reshape(ROWS_CTX, D), x_sample.reshape(ROWS - ROWS_CTX, D), ln_in_g.reshape(1, D), ln_in_b.reshape(1, D))
    cond8 = jnp.concatenate([c_ctx[None], c, jnp.zeros((8 - 1 - N_LAT, D), F32)], axis=0)
    mod = _ada(cond8, w_ada, b_ada).reshape(DEPTH, 8, 6, D)

    rope_tabs = _rope_tables()
    cs_ctx, cs_lat = _dft_tables(T_CTX), _dft_tables(T_LAT)
    c64, s64 = _dft_channel_tables()
    cache_k = cache_k.transpose(0, 1, 3, 4, 2).reshape(N_LAT, DEPTH, KV_W, PAST)
    cache_v = cache_v.transpose(0, 1, 3, 4, 2).reshape(N_LAT, DEPTH, KV_W, PAST)
    w_router_t = jnp.swapaxes(w_router, 1, 2)
    sink_b = jnp.broadcast_to(attn_sink[:, :, None], (DEPTH, N_HEADS, LANES))
    s5_tabs = _s5_tables(s5_a_re, s5_a_im, s5_log_dt, s5_b_re, s5_b_im, s5_c_re, s5_c_im)
    lat_per_step = N_LAT // S5_STEPS_LAT
    s0 = jnp.concatenate([state_s5_re, state_s5_im, state_s5_im, state_s5_re], axis=-1)
    s0 = s0.reshape(S5_STEPS_LAT, lat_per_step, DEPTH, 2, S5_G, 4 * S5_P).transpose(2, 0, 3, 4, 1, 5)
    s0 = jnp.pad(s0, ((0, 0), (S5_STEPS_CTX, 0), (0, 0), (0, 0), (0, S5_SEQ - lat_per_step), (0, 0)))
    ln1_g, ln1_b, ln2_g, ln2_b, s5_d = (a.reshape(DEPTH, 1, -1) for a in (ln1_g, ln1_b, ln2_g, ln2_b, s5_d))

    new_s = []
    caches = (jnp.zeros(CACHE_SHAPE.shape, F32), jnp.zeros(CACHE_SHAPE.shape, F32))
    for l in range(DEPTH):
        q, k, v, uf, us, *caches = _inproj(x, mod, w_in, rope_tabs, caches, l)
        attn = _attn_lat(_attn_ctx(q, k, v, sink_b[l]), k, v, cache_k, cache_v, sink_b[l], l)
        four = _fourier(uf, cs_ctx, c64, s64, w_fourier, l, T_CTX, N_CTX, SEQ_PER_STEP_CTX, 0)
        four = _fourier(four, cs_lat, c64, s64, w_fourier, l, T_LAT, N_LAT, 1, ROWS_CTX // T_LAT)
        y5, fin = _s5(us, s5_tabs, s0, l)
        new_s.append(fin[:S5_STEPS_CTX].transpose(0, 3, 1, 2, 4).reshape(N_CTX, 2, S5_G, 2 * S5_P))
        x1, h2, lg = _outproj(attn, four, y5, us, s5_d, s5_w_glu, w_out, x, mod, ln1_g, ln1_b, w_router_t, l)
        crow_c, grow_c = _router(lg, T_CTX, N_CTX, 0, CAP_CTX)
        crow_q, grow_q = _router(lg, T_LAT, N_LAT, 1, CAP_LAT)
        xs_c = _gather(crow_c, h2, T_CTX, N_CTX, SEQ_PER_STEP_CTX, 0, CAP_CTX)
        xs_q = _gather(crow_q, h2, T_LAT, N_LAT, 1, ROWS_CTX // T_LAT, CAP_LAT)
        ys_c, ys_q = _ffn(xs_c, xs_q, w_gate, w_up, w_down, l)
        args_c = (crow_c, grow_c, ys_c), (mod, ln2_g, ln2_b, l, CAP_CTX, N_CTX, SEQ_PER_STEP_CTX, T_CTX, 1, 0)
        args_q = (crow_q, grow_q, ys_q), (mod, ln2_g, ln2_b, l, CAP_LAT, N_LAT, 1, RB_COMBINE_LAT,
                                          T_LAT // RB_COMBINE_LAT, ROWS_CTX // RB_COMBINE_LAT)
        if l < DEPTH - 1:
            x = _combine(*args_c[0], x1, *args_c[1], in_place=True)
            x = _combine(*args_q[0], x, *args_q[1], in_place=True)
        else:
            y_ctx = _combine(*args_c[0], x1, *args_c[1], in_place=False)
            y_lat = _combine(*args_q[0], x1, *args_q[1], in_place=False)

    new_s = jnp.stack(new_s, axis=1)

    def cache(feat_major):
        return feat_major.reshape(N_CTX, DEPTH, N_KV, HD, T_CTX).transpose(0, 1, 4, 2, 3)

    return (y_ctx.reshape(N_CTX, T_CTX, D), y_lat.reshape(N_LAT, T_LAT, D),
            cache(caches[0]), cache(caches[1]), new_s[..., :S5_P], new_s[..., S5_P:])
```

```python
import functools

import jax
import jax.numpy as jnp
import numpy as np
from jax import lax
from jax.experimental import pallas as pl
from jax.experimental.pallas import tpu as pltpu

F32 = jnp.float32
BF16 = jnp.bfloat16

D = 1024
N_CTX, T_CTX = 16, 256
N_LAT, T_LAT = 4, 1024
ROWS_CTX = N_CTX * T_CTX
ROWS = ROWS_CTX + N_LAT * T_LAT
DEPTH = 4
PAST = 512
GRID_W = 64
N_HEADS, N_KV, HD = 8, 2, 64
GQA = N_HEADS // N_KV
ATTN_W, KV_W = N_HEADS * HD, N_KV * HD
LOG2E = 1.4426950408889634
Q_SCALE = HD ** -0.5 * LOG2E
WINDOW = 128
FOUR_H, FOUR_W = 4, 256
S5_G, S5_H, S5_P, S5_W = 16, 16, 64, 256
S5_T = 16
IN_W = ATTN_W + 2 * KV_W + FOUR_W + S5_W
N_EXP, FF = 16, 1024
CAP_CTX, CAP_LAT = 2 * T_CTX // N_EXP, 2 * T_LAT // N_EXP
LN_EPS = 1e-5
NEG_INF = -1e30
ALPHA = (2 * DEPTH) ** 0.25
ROPE_BASE = 10000.0
RB_IN = 1024
RB_OUT = 512
RB_COMBINE_LAT = 1024
LANES = 128
MIB = 2 ** 20


def _cparams(sem, vmem_mib):
    return pltpu.CompilerParams(dimension_semantics=sem, vmem_limit_bytes=vmem_mib * MIB)


def _dot(a, b):
    return jnp.dot(a, b, preferred_element_type=F32)


def _dot_nt(a, b):
    return lax.dot_general(a, b, (((1,), (1,)), ((), ())), preferred_element_type=F32)


def _dot_tn(a, b):
    return lax.dot_general(a, b, (((0,), (0,)), ((), ())), preferred_element_type=F32)


def _layer_norm(x, g, b):
    mu = jnp.mean(x, axis=-1, keepdims=True)
    xc = x - mu
    var = jnp.mean(xc * xc, axis=-1, keepdims=True)
    return xc * lax.rsqrt(var + LN_EPS) * g + b


def _mod_row(block, block_rows):
    first = block * block_rows
    return jnp.where(first < ROWS_CTX, 0, 1 + (first - ROWS_CTX) // T_LAT)


_HALVES_SHAPE = jax.ShapeDtypeStruct((2, ROWS, LANES), F32)


def _halves_spec(rows, row_block):
    return pl.BlockSpec((2, rows, LANES), lambda *ids: (0, row_block(*ids), 0))


def _stream_specs(x, rows):
    if not isinstance(x, tuple):
        return [pl.BlockSpec((rows, D), lambda i: (i, 0))]
    n_ctx = ROWS_CTX // rows
    vec = pl.BlockSpec((1, D), lambda i: (0, 0))
    return [pl.BlockSpec((rows, D), lambda i: (jnp.minimum(i, n_ctx - 1), 0)),
            pl.BlockSpec((rows, D), lambda i: (jnp.maximum(i - n_ctx, 0), 0)), vec, vec]


def _stream_args(x):
    return x if isinstance(x, tuple) else (x,)


def _stream_block(refs, rows):
    if len(refs) == 1:
        return refs[0][...]
    xc_ref, xq_ref, g_ref, b_ref = refs
    raw = jnp.where(pl.program_id(0) < ROWS_CTX // rows, xc_ref[...], xq_ref[...])
    return _layer_norm(raw, g_ref[...], b_ref[...])


def _ada_kernel(c_ref, w_ref, b_ref, o_ref):
    c = c_ref[...]
    s = (c * jax.nn.sigmoid(c)).astype(BF16)
    o_ref[0] = _dot(s, w_ref[0].astype(BF16)) + b_ref[0]


def _ada(cond8, w_ada, b_ada):
    tn = 1536
    return pl.pallas_call(
        _ada_kernel,
        grid=(DEPTH, 6 * D // tn),
        in_specs=[pl.BlockSpec((8, D), lambda l, j: (0, 0)),
                  pl.BlockSpec((1, D, tn), lambda l, j: (l, 0, j)),
                  pl.BlockSpec((1, 1, tn), lambda l, j: (l, 0, j))],
        out_specs=pl.BlockSpec((1, 8, tn), lambda l, j: (l, 0, j)),
        out_shape=jax.ShapeDtypeStruct((DEPTH, 8, 6 * D), F32),
        compiler_params=_cparams(("arbitrary", "arbitrary"), 40),
        name="adaln",
    )(cond8, w_ada, b_ada.reshape(DEPTH, 1, 6 * D))


def _inproj_kernel(*refs, n_x):
    x_refs = refs[:n_x]
    (mod_ref, w_ref, cos_ref, sin_ref, _, _, q_ref, k_ref, v_ref, uf_ref, us_ref, kc_ref, vc_ref,
     wb_ref, wswb_ref) = refs[n_x:]
    step = pl.program_id(0)
    n_ctx_steps = ROWS_CTX // RB_IN

    @pl.when(step == 0)
    def _():
        wb = w_ref[0].astype(BF16)
        wb_ref[...] = wb
        n_qk = ATTN_W + KV_W
        src = lax.broadcasted_iota(jnp.int32, (n_qk, n_qk), 0)
        dst = lax.broadcasted_iota(jnp.int32, (n_qk, n_qk), 1)
        swap = jnp.where(src == (dst ^ 1), 1.0, 0.0).astype(BF16)
        wswb_ref[...] = _dot(wb[:, :n_qk], swap).astype(BF16)

    m = mod_ref[0, 0]
    hb = (_stream_block(x_refs, RB_IN) * (1.0 + m[1:2]) + m[0:1]).astype(BF16)
    p = _dot(hb, wb_ref[...])
    q = p[:, :ATTN_W]
    k = p[:, ATTN_W:ATTN_W + KV_W]
    v = p[:, ATTN_W + KV_W:ATTN_W + 2 * KV_W]

    psw = _dot(hb, wswb_ref[...])
    cos, sin = cos_ref[...], sin_ref[...]
    wide = lambda t: jnp.concatenate([t] * (ATTN_W // KV_W), axis=1)
    latent = step >= n_ctx_steps
    q = jnp.where(latent, q * wide(cos) + psw[:, :ATTN_W] * wide(sin), q)
    k = jnp.where(latent, k * cos + psw[:, ATTN_W:] * sin, k)
    q_ref[...] = (q * Q_SCALE).astype(BF16)
    k_ref[...] = k.astype(BF16)
    v_ref[...] = v.astype(BF16)

    @pl.when(step < n_ctx_steps)
    def _():
        for j in range(RB_IN // T_CTX):
            rows = slice(j * T_CTX, (j + 1) * T_CTX)
            kc_ref[j, 0] = k[rows].T
            vc_ref[j, 0] = v[rows].T

    uf_ref[...] = p[:, ATTN_W + 2 * KV_W:ATTN_W + 2 * KV_W + FOUR_W].astype(BF16)
    for half in range(S5_W // LANES):
        lo = ATTN_W + 2 * KV_W + FOUR_W + half * LANES
        us_ref[half] = p[:, lo:lo + LANES]


CACHE_SHAPE = jax.ShapeDtypeStruct((N_CTX, DEPTH, KV_W, T_CTX), F32)


def _inproj(x, mod, w_in, rope_tabs, caches, l):
    assert RB_IN == T_LAT
    outs = ((ATTN_W, BF16), (KV_W, BF16), (KV_W, BF16), (FOUR_W, BF16))
    table = pl.BlockSpec((T_LAT, KV_W), lambda i: (0, 0))
    any_space = pl.BlockSpec(memory_space=pl.ANY)
    slab = pl.BlockSpec((RB_IN // T_CTX, 1, KV_W, T_CTX), lambda i: (jnp.minimum(i, ROWS_CTX // RB_IN - 1), l, 0, 0))
    x_specs = _stream_specs(x, RB_IN)
    n_in = len(x_specs) + 4
    return pl.pallas_call(
        functools.partial(_inproj_kernel, n_x=len(x_specs)),
        grid=(ROWS // RB_IN,),
        in_specs=x_specs + [pl.BlockSpec((1, 1, 6, D), lambda i: (l, _mod_row(i, RB_IN), 0, 0)),
                            pl.BlockSpec((1, D, IN_W), lambda i: (l, 0, 0)),
                            table, table, any_space, any_space],
        out_specs=([pl.BlockSpec((RB_IN, w), lambda i: (i, 0)) for w, _ in outs] + [_halves_spec(RB_IN, lambda i: i)]
                   + [slab, slab]),
        out_shape=([jax.ShapeDtypeStruct((ROWS, w), dt) for w, dt in outs] + [_HALVES_SHAPE] + [CACHE_SHAPE] * 2),
        input_output_aliases={n_in: 5, n_in + 1: 6},
        scratch_shapes=[pltpu.VMEM((D, IN_W), BF16), pltpu.VMEM((D, ATTN_W + KV_W), BF16)],
        compiler_params=_cparams(("arbitrary",), 56),
        name="inproj",
    )(*_stream_args(x), mod, w_in, *rope_tabs, *caches)


SINK_ROWS = 16


def _softmax_av(score_blocks, sink, value_blocks):
    m = sink
    for s in score_blocks:
        m = jnp.maximum(m, jnp.max(s, axis=0, keepdims=True))
    first = lax.broadcasted_iota(jnp.int32, (SINK_ROWS, 1), 0) == 0
    probs = [jnp.exp2(s - m).astype(BF16) for s in score_blocks]
    probs.append(jnp.where(first, jnp.exp2(sink - m), 0.0).astype(BF16))
    ext = []
    for v in value_blocks:
        ones = jnp.ones(v.shape, BF16)
        ext.append(jnp.concatenate([v, v, ones, ones], axis=1))
    row = lax.broadcasted_iota(jnp.int32, (SINK_ROWS, 4 * HD), 0)
    lane = lax.broadcasted_iota(jnp.int32, (SINK_ROWS, 4 * HD), 1)
    ext.append(jnp.where((row == 0) & (lane >= 2 * HD), 1.0, 0.0).astype(BF16))
    acc = _dot_tn(jnp.concatenate(probs, axis=0), jnp.concatenate(ext, axis=0))
    return acc[:, :2 * HD] / acc[:, 2 * HD:]


def _stack_group(q, kv):
    return jnp.concatenate([q[:, (kv * GQA + g) * HD:(kv * GQA + g + 1) * HD] for g in range(GQA)], axis=0)


def _sink_row(sink_ref, kv, rows):
    return LOG2E * jnp.concatenate(
        [jnp.broadcast_to(sink_ref[kv * GQA + g:kv * GQA + g + 1, 0:1], (1, rows)) for g in range(GQA)], axis=1)


def _unstack_group(o, rows):
    lane = lax.broadcasted_iota(jnp.int32, (rows, 2 * HD), 1)
    return [jnp.where(lane < HD, o[2 * j * rows:(2 * j + 1) * rows], o[(2 * j + 1) * rows:(2 * j + 2) * rows])
            for j in range(GQA // 2)]


def _attn_ctx_kernel(q_ref, k_ref, v_ref, sink_ref, o_ref):
    q = q_ref[...]
    k = k_ref[...]
    v = v_ref[...]
    outs = []
    for kv in range(N_KV):
        head = slice(kv * HD, (kv + 1) * HD)
        s = _dot_nt(k[:, head], _stack_group(q, kv))
        o = _softmax_av([s], _sink_row(sink_ref, kv, T_CTX), [v[:, head]])
        outs += _unstack_group(o, T_CTX)
    o_ref[...] = jnp.concatenate(outs, axis=1).astype(BF16)


def _attn_ctx(q, k, v, sink_b):
    return pl.pallas_call(
        _attn_ctx_kernel,
        grid=(N_CTX,),
        in_specs=[pl.BlockSpec((T_CTX, ATTN_W), lambda s: (s, 0)),
                  pl.BlockSpec((T_CTX, KV_W), lambda s: (s, 0)),
                  pl.BlockSpec((T_CTX, KV_W), lambda s: (s, 0)),
                  pl.BlockSpec((N_HEADS, LANES), lambda s: (0, 0))],
        out_specs=pl.BlockSpec((T_CTX, ATTN_W), lambda s: (s, 0)),
        out_shape=jax.ShapeDtypeStruct((ROWS, ATTN_W), BF16),
        input_output_aliases={0: 0},
        compiler_params=_cparams(("arbitrary",), 32),
        name="attn_ctx",
    )(q, k, v, sink_b)


def _attn_lat_kernel(q_ref, k_ref, v_ref, ck_ref, cv_ref, sink_ref, o_ref, kc_scr, vc_scr):
    i = pl.program_id(1)
    n_blk = T_LAT // WINDOW
    q = q_ref[...]

    @pl.when(i == 0)
    def _():
        kc_scr[...] = ck_ref[0, 0].T.astype(BF16)
        vc_scr[...] = cv_ref[0, 0].T.astype(BF16)

    def kv_block(j):
        st = pl.multiple_of(j * WINDOW, WINDOW)
        return k_ref[pl.ds(st, WINDOW), :], v_ref[pl.ds(st, WINDOW), :]

    k0, v0 = kv_block(jnp.maximum(i - 1, 0))
    k1, v1 = kv_block(i)
    k2, v2 = kv_block(jnp.minimum(i + 1, n_blk - 1))
    k_win = jnp.concatenate([k0, k1, k2], axis=0)
    k_ctx = kc_scr[...]
    v_ctx = vc_scr[...]
    rows = GQA * WINDOW
    c = lax.broadcasted_iota(jnp.int32, (WINDOW, rows), 0)
    r = lax.broadcasted_iota(jnp.int32, (WINDOW, rows), 1) & (WINDOW - 1)
    far = 2 * WINDOW
    prev_ok = c >= r + jnp.where(i > 0, 0, far)
    next_ok = c + jnp.where(i < n_blk - 1, 0, far) <= r
    outs = []
    for kv in range(N_KV):
        head = slice(kv * HD, (kv + 1) * HD)
        qs = _stack_group(q, kv)
        s_win = _dot_nt(k_win[:, head], qs)
        scores = [jnp.where(prev_ok, s_win[:WINDOW], NEG_INF), s_win[WINDOW:2 * WINDOW],
                  jnp.where(next_ok, s_win[2 * WINDOW:], NEG_INF), _dot_nt(k_ctx[:, head], qs)]
        values = [v0[:, head], v1[:, head], v2[:, head], v_ctx[:, head]]
        outs += _unstack_group(_softmax_av(scores, _sink_row(sink_ref, kv, WINDOW), values), WINDOW)
    o_ref[...] = jnp.concatenate(outs, axis=1).astype(BF16)


def _attn_lat(q, k, v, cache_k, cache_v, sink_b, l):
    n_blk = T_LAT // WINDOW
    off = ROWS_CTX // WINDOW
    return pl.pallas_call(
        _attn_lat_kernel,
        grid=(N_LAT, n_blk),
        in_specs=[pl.BlockSpec((WINDOW, ATTN_W), lambda b, i: (off + b * n_blk + i, 0)),
                  pl.BlockSpec((T_LAT, KV_W), lambda b, i: (ROWS_CTX // T_LAT + b, 0)),
                  pl.BlockSpec((T_LAT, KV_W), lambda b, i: (ROWS_CTX // T_LAT + b, 0)),
                  pl.BlockSpec((1, 1, KV_W, PAST), lambda b, i: (b, l, 0, 0)),
                  pl.BlockSpec((1, 1, KV_W, PAST), lambda b, i: (b, l, 0, 0)),
                  pl.BlockSpec((N_HEADS, LANES), lambda b, i: (0, 0))],
        out_specs=pl.BlockSpec((WINDOW, ATTN_W), lambda b, i: (off + b * n_blk + i, 0)),
        out_shape=jax.ShapeDtypeStruct((ROWS, ATTN_W), BF16),
        scratch_shapes=[pltpu.VMEM((PAST, KV_W), BF16), pltpu.VMEM((PAST, KV_W), BF16)],
        input_output_aliases={0: 0},
        compiler_params=_cparams(("arbitrary", "arbitrary"), 40),
        name="attn_lat",
    )(q, k, v, cache_k, cache_v, sink_b)


def _rope_tables():
    rows = T_LAT // GRID_W
    row = jnp.repeat(jnp.arange(rows, dtype=F32), GRID_W)
    col = jnp.tile(jnp.arange(GRID_W, dtype=F32), rows)
    n_freq = HD // 4
    freqs = ROPE_BASE ** (-jnp.arange(n_freq, dtype=F32) / n_freq)
    ang = jnp.concatenate([row[:, None] * freqs, col[:, None] * freqs], axis=-1)
    cos = jnp.repeat(jnp.cos(ang), 2, axis=-1)
    sign = jnp.tile(jnp.array([-1.0, 1.0], F32), HD // 2)
    sin = jnp.repeat(jnp.sin(ang), 2, axis=-1) * sign
    return jnp.tile(cos, (1, N_KV)), jnp.tile(sin, (1, N_KV))


SEQ_PER_STEP_CTX = 4


def _four_kernel(u_ref, cs_ref, c64_ref, s64_ref, wf_ref, o_ref, *, n, n_sub):
    ub = u_ref[...]
    uc = _dot(ub, c64_ref[...]).astype(BF16)
    us = _dot(ub, s64_ref[...]).astype(BF16)
    f = jnp.concatenate([_dot(cs_ref[...], jnp.concatenate([uc[j * n:(j + 1) * n], us[j * n:(j + 1) * n]], axis=0))
                         for j in range(n_sub)], axis=0)
    o_ref[...] = _dot(f.astype(BF16), wf_ref[0].astype(BF16)).astype(BF16)


def _dft_tables(n):
    j = np.arange(n)
    ang = 2.0 * np.pi * ((j[:, None] * j[None, :]) % n) / n
    cs = np.concatenate([np.cos(ang), -np.sin(ang)], axis=1) / np.sqrt(n)
    return jnp.asarray(cs.astype(np.float32)).astype(BF16)


def _dft_channel_tables():
    j = np.arange(HD)
    ang = 2.0 * np.pi * ((j[:, None] * j[None, :]) % HD) / HD
    eye = np.eye(FOUR_H)
    c = np.kron(eye, np.cos(ang)) / np.sqrt(HD)
    s = np.kron(eye, np.sin(ang)) / np.sqrt(HD)
    return jnp.asarray(c.astype(np.float32)).astype(BF16), jnp.asarray(s.astype(np.float32)).astype(BF16)


def _fourier(uf, cs, c64, s64, w_fourier, l, n, n_seq, n_sub, blk_off):
    rows = n_sub * n
    return pl.pallas_call(
        functools.partial(_four_kernel, n=n, n_sub=n_sub),
        grid=(n_seq // n_sub,),
        in_specs=[pl.BlockSpec((rows, FOUR_W), lambda s: (blk_off + s, 0)),
                  pl.BlockSpec((n, 2 * n), lambda s: (0, 0)),
                  pl.BlockSpec((FOUR_W, FOUR_W), lambda s: (0, 0)),
                  pl.BlockSpec((FOUR_W, FOUR_W), lambda s: (0, 0)),
                  pl.BlockSpec((1, FOUR_W, FOUR_W), lambda s: (l, 0, 0))],
        out_specs=pl.BlockSpec((rows, FOUR_W), lambda s: (blk_off + s, 0)),
        out_shape=jax.ShapeDtypeStruct((ROWS, FOUR_W), BF16),
        input_output_aliases={0: 0},
        compiler_params=_cparams(("arbitrary",), 40),
        name=f"fourier_{n}",
    )(uf, cs, c64, s64, w_fourier)


S5_TW = S5_T * S5_H
S5_R = 128
S5_TOK = S5_R * S5_T
S5_STEPS_CTX = ROWS_CTX // S5_TOK
S5_STEPS_LAT = (ROWS - ROWS_CTX) // S5_TOK
S5_STEPS = S5_STEPS_CTX + S5_STEPS_LAT
S5_SEQ = N_CTX // S5_STEPS_CTX


def _s5_tables(a_re, a_im, log_dt, b_re, b_im, c_re, c_im):
    f = lambda a: a.astype(F32)
    a_re, a_im, b_re, b_im, c_re, c_im = map(f, (a_re, a_im, b_re, b_im, c_re, c_im))
    dt = jnp.exp(f(log_dt))[..., None]
    x, y = a_re * dt, a_im * dt
    kk = jnp.arange(S5_T + 1, dtype=F32)[None, None, :, None, None]
    mag = jnp.exp(kk * x[:, :, None])
    pr, pi = mag * jnp.cos(kk * y[:, :, None]), mag * jnp.sin(kk * y[:, :, None])
    nr, ni = pr[:, :, 1] - 1.0, pi[:, :, 1]
    den = a_re * a_re + a_im * a_im
    qr, qi = (nr * a_re + ni * a_im) / den, (ni * a_re - nr * a_im) / den
    bb_r = qr[..., None] * b_re - qi[..., None] * b_im
    bb_i = qr[..., None] * b_im + qi[..., None] * b_re
    cat = lambda u, v: jnp.concatenate([u, v], axis=-1)
    prg, pig = pr.transpose(0, 1, 3, 2, 4), pi.transpose(0, 1, 3, 2, 4)
    def lane_pairs(re, im, patterns):
        coef = lambda k: jnp.asarray(np.repeat(np.array([[p[0][k], p[1][k]] for p in patterns], np.float32),
                                               re.shape[-1], axis=-1))[:, None, :]
        return cat(re, re)[..., None, :, :] * coef(0) + cat(im, im)[..., None, :, :] * coef(1)

    keep_re, keep_im, neg_re, neg_im = (1, 0), (0, 1), (-1, 0), (0, -1)
    pows = lane_pairs(prg, pig, [(keep_re, keep_re), (neg_im, keep_im), (keep_re, neg_im), (neg_im, neg_re),
                                 (keep_re, keep_im), (neg_im, keep_re)])
    bt_r, bt_i = bb_r.transpose(0, 1, 2, 4, 3), bb_i.transpose(0, 1, 2, 4, 3)
    wrows = jnp.concatenate([lane_pairs(bt_r, bt_i, [(keep_re, keep_im), (keep_im, keep_re), (keep_re, neg_im)]),
                             lane_pairs(c_re, c_im, [(keep_re, keep_re), (keep_im, keep_im)])],
                            axis=3)
    ar, ai = pr[:, :, S5_T], pi[:, :, S5_T]
    decay = jnp.stack([cat(ar, ar), cat(-ai, ai), cat(ai, -ai), jnp.zeros_like(cat(ar, ar))], axis=3)
    return pows, wrows, decay


def _block_transpose(arrs):
    arrs = list(arrs)
    width = arrs[0].shape[1]
    blk = lax.broadcasted_iota(jnp.int32, arrs[0].shape, 1) >> 4
    s = len(arrs) // 2
    while s >= 1:
        keep = (blk & s) == 0
        for i in range(len(arrs)):
            if i & s:
                continue
            lo, hi = arrs[i], arrs[i + s]
            arrs[i] = jnp.where(keep, lo, pltpu.roll(hi, s * S5_H, 1))
            arrs[i + s] = jnp.where(keep, pltpu.roll(lo, width - s * S5_H, 1), hi)
        s //= 2
    return arrs


def _s5_build_operators(pw_ref, w_ref, e_scr, esw_scr, ft_scr, tp_scr):
    lane = lax.broadcasted_iota(jnp.int32, (S5_H, S5_TW), 1)
    for d in range(2):
        for g in range(S5_G):
            pw = lambda v, k: pw_ref[0, d, g, v, k:k + 1, :]
            b_ri, b_ir, b_conj, c_rr, c_ii = (w_ref[0, d, g, v] for v in range(5))
            lag_rows = []
            for t in range(S5_T):
                blk = slice(t * S5_H, (t + 1) * S5_H)
                ke = S5_T - 1 - t if d == 0 else t
                e_scr[d, g, blk, :] = (b_ri * pw(0, ke) + b_ir * pw(1, ke)).astype(BF16)
                esw_scr[d, g, blk, :] = (b_ir * pw(0, ke) - b_ri * pw(1, ke)).astype(BF16)
                kf = t + 1 if d == 0 else S5_T - t
                ft_scr[d, g, blk, :] = (c_rr * pw(2, kf) + c_ii * pw(3, kf)).astype(BF16)
                kl = t if d == 0 else S5_T - 1 - t
                lag_rows.append(c_rr * pw(4, kl) + c_ii * pw(5, kl))
            mh, ml = _split_bf16(jnp.concatenate(lag_rows, axis=0))
            bh, bl = _split_bf16(b_conj)
            kt = _dot_nt(bh, mh) + _dot_nt(bh, ml) + _dot_nt(bl, mh)
            for ti in range(S5_T):
                if d == 0:
                    sh = S5_H * ti
                    blk_rows = jnp.where(lane >= sh, pltpu.roll(kt, sh, 1) if sh else kt, 0.0)
                else:
                    sh = S5_H * (S5_T - 1 - ti)
                    blk_rows = jnp.where(lane < S5_TW - sh, pltpu.roll(kt, S5_TW - sh, 1) if sh else kt, 0.0)
                tp_scr[d, g, ti * S5_H:(ti + 1) * S5_H, :] = blk_rows.astype(BF16)


def _s5_kernel(us_ref, pw_ref, w_ref, a_ref, s0_ref, y_ref, fin_ref,
               e_scr, esw_scr, ft_scr, tp_scr, x_scr, y_scr, loc_scr, lsw_scr, prev_scr):
    step = pl.program_id(0)

    @pl.when(step == 0)
    def _():
        _s5_build_operators(pw_ref, w_ref, e_scr, esw_scr, ft_scr, tp_scr)

    rows = pl.ds
    x_by_step = [jnp.concatenate([us_ref[0, rows(tl, S5_R, stride=S5_T), :],
                                  us_ref[1, rows(tl, S5_R, stride=S5_T), :]], axis=1).astype(BF16)
                 for tl in range(S5_T)]
    for g, xg in enumerate(_block_transpose(x_by_step)):
        x_scr[g] = xg
    fin_ref[...] = jnp.zeros(fin_ref.shape, F32)

    def scan(d, nseq, nc):
        batch = S5_G
        for g0 in range(0, S5_G, batch):
            init = tuple((s0_ref[0, 0, d, g, 0:nseq, :2 * S5_P], s0_ref[0, 0, d, g, 0:nseq, 2 * S5_P:])
                         for g in range(g0, g0 + batch))

            def body(j, carry):
                ci = j if d == 0 else nc - 1 - j
                out = []
                for idx in range(batch):
                    g = g0 + idx
                    s, t = carry[idx]
                    ca, cb, cc = (a_ref[0, d, g, r:r + 1, :] for r in range(3))
                    prev_scr[g, rows(ci, nseq, stride=nc), :] = s
                    out.append((s * ca + t * cb + loc_scr[g, rows(ci, nseq, stride=nc), :],
                                t * ca + s * cc + lsw_scr[g, rows(ci, nseq, stride=nc), :]))
                return tuple(out)

            fin = lax.fori_loop(0, nc, body, init)
            for idx in range(batch):
                fin_ref[0, d, g0 + idx, 0:nseq, :] = fin[idx][0]

    for d in range(2):
        for g in range(S5_G):
            xg = x_scr[g]
            loc_scr[g] = _dot(xg, e_scr[d, g])
            lsw_scr[g] = _dot(xg, esw_scr[d, g])

        @pl.when(step < S5_STEPS_CTX)
        def _():
            scan(d, N_CTX // S5_STEPS_CTX, T_CTX // S5_T)

        @pl.when(step >= S5_STEPS_CTX)
        def _():
            scan(d, N_LAT // S5_STEPS_LAT, T_LAT // S5_T)

        for g in range(S5_G):
            yg = _dot(x_scr[g], tp_scr[d, g]) + _dot_nt(prev_scr[g].astype(BF16), ft_scr[d, g])
            if d == 0:
                y_scr[g] = yg
            else:
                y_scr[g] += yg
    for to, y_to in enumerate(_block_transpose([y_scr[g] for g in range(S5_G)])):
        y_ref[0, rows(to, S5_R, stride=S5_T), :] = y_to[:, :LANES]
        y_ref[1, rows(to, S5_R, stride=S5_T), :] = y_to[:, LANES:]


def _s5(us, tabs, s0, l):
    pows, wrows, decay = tabs
    lay = lambda *shape: pl.BlockSpec((1,) + shape, lambda h: (l,) + (0,) * len(shape))
    op_buf = pltpu.VMEM((2, S5_G, S5_TW, 2 * S5_P), BF16)
    scan_buf = pltpu.VMEM((S5_G, S5_R, 2 * S5_P), F32)
    return pl.pallas_call(
        _s5_kernel,
        grid=(S5_STEPS,),
        in_specs=[_halves_spec(S5_TOK, lambda h: h),
                  lay(2, S5_G, 6, S5_T + 1, 2 * S5_P), lay(2, S5_G, 5, S5_H, 2 * S5_P), lay(2, S5_G, 4, 2 * S5_P),
                  pl.BlockSpec((1, 1, 2, S5_G, S5_SEQ, 4 * S5_P), lambda h: (l, h, 0, 0, 0, 0))],
        out_specs=[_halves_spec(S5_TOK, lambda h: h),
                   pl.BlockSpec((1, 2, S5_G, S5_SEQ, 2 * S5_P), lambda h: (h, 0, 0, 0, 0))],
        out_shape=[_HALVES_SHAPE, jax.ShapeDtypeStruct((S5_STEPS, 2, S5_G, S5_SEQ, 2 * S5_P), F32)],
        scratch_shapes=[op_buf, op_buf, op_buf, pltpu.VMEM((2, S5_G, S5_TW, S5_TW), BF16),
                        pltpu.VMEM((S5_G, S5_R, S5_TW), BF16), pltpu.VMEM((S5_G, S5_R, S5_TW), F32),
                        scan_buf, scan_buf, scan_buf],
        compiler_params=_cparams(("arbitrary",), 56),
        name="s5",
    )(us, pows, wrows, decay, s0)


def _split_bf16(a):
    hi = a.astype(BF16)
    return hi, (a - hi.astype(F32)).astype(BF16)


def _outproj_kernel(*refs, n_x):
    x_refs = refs[:n_x]
    (at_ref, fo_ref, ys_ref, us_ref, d_ref, wg_ref, wo_ref, mod_ref, g_ref, b_ref, wr_ref,
     x1_ref, h2_ref, lg_ref, wob_ref) = refs[n_x:]

    @pl.when(pl.program_id(0) == 0)
    def _():
        wob_ref[...] = wo_ref[0].astype(BF16)

    ys = jnp.concatenate([ys_ref[0], ys_ref[1]], axis=1)
    us = jnp.concatenate([us_ref[0], us_ref[1]], axis=1)
    g = jax.nn.gelu(ys + us * d_ref[0])
    s5 = g * jax.nn.sigmoid(_dot(g.astype(BF16), wg_ref[0].astype(BF16)))
    mix = (_dot(at_ref[...], wob_ref[0:ATTN_W, :])
           + _dot(fo_ref[...], wob_ref[ATTN_W:ATTN_W + FOUR_W, :])
           + _dot(s5.astype(BF16), wob_ref[ATTN_W + FOUR_W:, :]))
    m = mod_ref[0, 0]
    r = ALPHA * _stream_block(x_refs, RB_OUT) + m[2:3] * mix
    rc = r - jnp.mean(r, axis=-1, keepdims=True)
    t = rc * lax.rsqrt(jnp.mean(rc * rc, axis=-1, keepdims=True) + LN_EPS)
    gain, bias = g_ref[0], b_ref[0]
    x1_ref[...] = t * gain + bias
    up = 1.0 + m[4:5]
    hb = (t * (gain * up) + (bias * up + m[3:4])).astype(BF16)
    h2_ref[...] = hb
    lg_ref[...] = _dot_nt(wr_ref[0].astype(BF16), hb)


def _outproj(attn, four, ys5, us, s5_d, w_glu, w_out, x, mod, ln_g, ln_b, w_router_t, l):
    row = lambda w: pl.BlockSpec((RB_OUT, w), lambda i: (i, 0))
    halves = _halves_spec(RB_OUT, lambda i: i)
    lay = lambda *shape: pl.BlockSpec((1,) + shape, lambda i: (l,) + (0,) * len(shape))
    x_specs = _stream_specs(x, RB_OUT)
    return pl.pallas_call(
        functools.partial(_outproj_kernel, n_x=len(x_specs)),
        grid=(ROWS // RB_OUT,),
        in_specs=x_specs + [row(ATTN_W), row(FOUR_W), halves, halves, lay(1, S5_W), lay(S5_W, S5_W), lay(D, D),
                            pl.BlockSpec((1, 1, 6, D), lambda i: (l, _mod_row(i, RB_OUT), 0, 0)),
                            lay(1, D), lay(1, D), lay(N_EXP, D)],
        out_specs=[row(D), row(D), pl.BlockSpec((N_EXP, RB_OUT), lambda i: (0, i))],
        out_shape=[jax.ShapeDtypeStruct((ROWS, D), F32), jax.ShapeDtypeStruct((ROWS, D), BF16),
                   jax.ShapeDtypeStruct((N_EXP, ROWS), F32)],
        scratch_shapes=[pltpu.VMEM((D, D), BF16)],
        compiler_params=_cparams(("arbitrary",), 40),
        name="outproj",
    )(*_stream_args(x), attn, four, ys5, us, s5_d, w_glu, w_out, mod, ln_g, ln_b, w_router_t)


def _router_kernel(lg_ref, crow_ref, grow_ref, *, n, cap, n_seq):
    rows = n_seq * N_EXP
    lg = jnp.concatenate([lg_ref[:, s * n:(s + 1) * n] for s in range(n_seq)], axis=0).reshape(n_seq, N_EXP, n)
    e = jnp.exp(lg - jnp.max(lg, axis=1, keepdims=True))
    aff = (e / jnp.sum(e, axis=1, keepdims=True)).reshape(rows, n)
    thr_bits = jnp.zeros((rows, 1), jnp.int32)
    for bit in range(30, -1, -1):
        cand = thr_bits | (1 << bit)
        cnt = jnp.sum(jnp.where(aff >= lax.bitcast_convert_type(cand, F32), 1.0, 0.0), axis=1, keepdims=True)
        thr_bits = jnp.where(cnt >= cap, cand, thr_bits)
    thr = lax.bitcast_convert_type(thr_bits, F32)
    above = aff > thr
    tied = aff == thr
    need = cap - jnp.sum(jnp.where(above, 1.0, 0.0), axis=1, keepdims=True)
    r0 = lax.broadcasted_iota(jnp.int32, (n, n), 0)
    r1 = lax.broadcasted_iota(jnp.int32, (n, n), 1)
    before = jnp.where(r0 < r1, 1.0, 0.0).astype(BF16)
    tied_rank = _dot(jnp.where(tied, 1.0, 0.0).astype(BF16), before)
    sel = above | (tied & (tied_rank < need))
    slot = _dot(jnp.where(sel, 1.0, 0.0).astype(BF16), before)
    code = jnp.where(sel, slot + 1.0, 0.0)
    gate = jnp.where(sel, aff, 0.0)
    for s in range(n_seq):
        crow_ref[:, s * n:(s + 1) * n] = code[s * N_EXP:(s + 1) * N_EXP]
        grow_ref[:, s * n:(s + 1) * n] = gate[s * N_EXP:(s + 1) * N_EXP]


def _router(lg, n, n_seq, blk, cap):
    table = pl.BlockSpec((N_EXP, n_seq * n), lambda i: (0, 0))
    return pl.pallas_call(
        functools.partial(_router_kernel, n=n, cap=cap, n_seq=n_seq),
        grid=(1,),
        in_specs=[pl.BlockSpec((N_EXP, n_seq * n), lambda i: (0, blk))],
        out_specs=[table, table],
        out_shape=[jax.ShapeDtypeStruct((N_EXP, n_seq * n), F32)] * 2,
        compiler_params=_cparams(("arbitrary",), 48),
        name=f"router_{n}",
    )(lg)


def _gather_kernel(crow_ref, h_ref, o_ref, *, n, cap, n_sub):
    slot1 = (lax.broadcasted_iota(jnp.int32, (cap, n), 0) + 1).astype(F32)
    for j in range(n_sub):
        code = crow_ref[:, j * n:(j + 1) * n]
        onehot = jnp.concatenate(
            [jnp.where(code[e:e + 1, :] == slot1, 1.0, 0.0).astype(BF16) for e in range(N_EXP)], axis=0)
        xs = _dot(onehot, h_ref[j * n:(j + 1) * n, :])
        o_ref[:, j * cap:(j + 1) * cap, :] = xs.reshape(N_EXP, cap, D).astype(BF16)


def _gather(crow, h2, n, n_seq, n_sub, blk_off, cap):
    return pl.pallas_call(
        functools.partial(_gather_kernel, n=n, cap=cap, n_sub=n_sub),
        grid=(n_seq // n_sub,),
        in_specs=[pl.BlockSpec((N_EXP, n_sub * n), lambda s: (0, s)),
                  pl.BlockSpec((n_sub * n, D), lambda s: (blk_off + s, 0))],
        out_specs=pl.BlockSpec((N_EXP, n_sub * cap, D), lambda s: (0, s, 0)),
        out_shape=jax.ShapeDtypeStruct((N_EXP, n_seq * cap, D), BF16),
        compiler_params=_cparams(("arbitrary",), 48),
        name=f"gather_{n}",
    )(crow, h2)


SLOTS = N_CTX * CAP_CTX


def _ffn_kernel(xc_ref, xq_ref, wg_ref, wu_ref, wd_ref, yc_ref, yq_ref, wgb_ref, wub_ref, wdb_ref):
    wgb_ref[...] = wg_ref[0, 0].astype(BF16)
    wub_ref[...] = wu_ref[0, 0].astype(BF16)
    wdb_ref[...] = wd_ref[0, 0].astype(BF16)
    for x_ref, y_ref in ((xc_ref, yc_ref), (xq_ref, yq_ref)):
        x = x_ref[0]
        g = _dot(x, wgb_ref[...])
        u = _dot(x, wub_ref[...])
        hid = (g * jax.nn.sigmoid(g) * u).astype(BF16)
        y_ref[0] = _dot(hid, wdb_ref[...]).astype(BF16)


def _ffn(xs_c, xs_q, w_gate, w_up, w_down, l):
    xspec = pl.BlockSpec((1, SLOTS, D), lambda e: (e, 0, 0))
    wspec = lambda rows, cols: pl.BlockSpec((1, 1, rows, cols), lambda e: (l, e, 0, 0))
    return pl.pallas_call(
        _ffn_kernel,
        grid=(N_EXP,),
        in_specs=[xspec, xspec, wspec(D, FF), wspec(D, FF), wspec(FF, D)],
        out_specs=[xspec, xspec],
        out_shape=[jax.ShapeDtypeStruct((N_EXP, SLOTS, D), BF16)] * 2,
        scratch_shapes=[pltpu.VMEM((D, FF), BF16), pltpu.VMEM((D, FF), BF16), pltpu.VMEM((FF, D), BF16)],
        compiler_params=_cparams(("arbitrary",), 56),
        name="ffn",
    )(xs_c, xs_q, w_gate, w_up, w_down)


def _combine_kernel(crow_ref, grow_ref, ys_ref, x_ref, mod_ref, g_ref, b_ref, o_ref, *, cap, rb, n_sub):
    width = N_EXP * cap
    slot1 = (lax.broadcasted_iota(jnp.int32, (cap, rb), 0) + 1).astype(F32)
    m = mod_ref[0, 0]
    for j in range(n_sub):
        rows = slice(j * rb, (j + 1) * rb)
        code, gate = crow_ref[:, rows], grow_ref[:, rows]
        weights_t = jnp.concatenate(
            [jnp.where(code[e:e + 1, :] == slot1, gate[e:e + 1, :], 0.0).astype(BF16) for e in range(N_EXP)], axis=0)
        moe = _dot_tn(weights_t, ys_ref[:, j * cap:(j + 1) * cap, :].reshape(width, D))
        o_ref[rows, :] = _layer_norm(ALPHA * x_ref[rows, :] + m[5:6] * moe, g_ref[0], b_ref[0])


def _combine(crow, grow, ys, x1, mod, ln_g, ln_b, l, cap, n_seq, n_sub, rb, rb_per_seq, rb_off, in_place):
    assert n_sub == 1 or rb_per_seq == 1
    rows = n_sub * rb
    lay = lambda *shape: pl.BlockSpec((1,) + shape, lambda s, r: (l,) + (0,) * len(shape))
    table = pl.BlockSpec((N_EXP, rows), lambda s, r: (0, s * rb_per_seq + r))
    block = lambda s, r: rb_off + s * rb_per_seq + r
    out_off = rb_off if in_place else 0
    return pl.pallas_call(
        functools.partial(_combine_kernel, cap=cap, rb=rb, n_sub=n_sub),
        grid=(n_seq // n_sub, rb_per_seq),
        in_specs=[table, table,
                  pl.BlockSpec((N_EXP, n_sub * cap, D), lambda s, r: (0, s, 0)),
                  pl.BlockSpec((rows, D), lambda s, r: (block(s, r), 0)),
                  pl.BlockSpec((1, 1, 6, D), lambda s, r: (l, _mod_row(block(s, r), rows), 0, 0)),
                  lay(1, D), lay(1, D)],
        out_specs=pl.BlockSpec((rows, D), lambda s, r: (block(s, r) - rb_off + out_off, 0)),
        out_shape=jax.ShapeDtypeStruct((ROWS if in_place else n_seq * rb_per_seq * rb, D), F32),
        input_output_aliases={3: 0} if in_place else {},
        compiler_params=_cparams(("arbitrary", "arbitrary"), 48),
        name=f"combine_{cap}",
    )(crow, grow, ys, x1, mod, ln_g, ln_b)


def kernel(x_prompt, x_sample, cache_k, cache_v, state_s5_re, state_s5_im, c, c_ctx, ln_in_g, ln_in_b, w_ada, b_ada,
           w_in, w_fourier, attn_sink, s5_a_re, s5_a_im, s5_log_dt, s5_b_re, s5_b_im, s5_c_re, s5_c_im, s5_d,
           s5_w_glu, w_out, ln1_g, ln1_b, w_router, w_gate, w_up, w_down, ln2_g, ln2_b):
    x = (x_prompt.reshape(ROWS_CTX, D), x_sample.reshape(ROWS - ROWS_CTX, D), ln_in_g.reshape(1, D), ln_in_b.reshape(1, D))
    cond8 = jnp.concatenate([c_ctx[None], c, jnp.zeros((8 - 1 - N_LAT, D), F32)], axis=0)
    mod = _ada(cond8, w_ada, b_ada).reshape(DEPTH, 8, 6, D)

    rope_tabs = _rope_tables()
    cs_ctx, cs_lat = _dft_tables(T_CTX), _dft_tables(T_LAT)
    c64, s64 = _dft_channel_tables()
    cache_k = cache_k.transpose(0, 1, 3, 4, 2).reshape(N_LAT, DEPTH, KV_W, PAST)
    cache_v = cache_v.transpose(0, 1, 3, 4, 2).reshape(N_LAT, DEPTH, KV_W, PAST)
    w_router_t = jnp.swapaxes(w_router, 1, 2)
    sink_b = jnp.broadcast_to(attn_sink[:, :, None], (DEPTH, N_HEADS, LANES))
    s5_tabs = _s5_tables(s5_a_re, s5_a_im, s5_log_dt, s5_b_re, s5_b_im, s5_c_re, s5_c_im)
    lat_per_step = N_LAT // S5_STEPS_LAT
    s0 = jnp.concatenate([state_s5_re, state_s5_im, state_s5_im, state_s5_re], axis=-1)
    s0 = s0.reshape(S5_STEPS_LAT, lat_per_step, DEPTH, 2, S5_G, 4 * S5_P).transpose(2, 0, 3, 4, 1, 5)
    s0 = jnp.pad(s0, ((0, 0), (S5_STEPS_CTX, 0), (0, 0), (0, 0), (0, S5_SEQ - lat_per_step), (0, 0)))
    ln1_g, ln1_b, ln2_g, ln2_b, s5_d = (a.reshape(DEPTH, 1, -1) for a in (ln1_g, ln1_b, ln2_g, ln2_b, s5_d))

    new_s = []
    caches = (jnp.zeros(CACHE_SHAPE.shape, F32), jnp.zeros(CACHE_SHAPE.shape, F32))
    for l in range(DEPTH):
        q, k, v, uf, us, *caches = _inproj(x, mod, w_in, rope_tabs, caches, l)
        attn = _attn_lat(_attn_ctx(q, k, v, sink_b[l]), k, v, cache_k, cache_v, sink_b[l], l)
        four = _fourier(uf, cs_ctx, c64, s64, w_fourier, l, T_CTX, N_CTX, SEQ_PER_STEP_CTX, 0)
        four = _fourier(four, cs_lat, c64, s64, w_fourier, l, T_LAT, N_LAT, 1, ROWS_CTX // T_LAT)
        y5, fin = _s5(us, s5_tabs, s0, l)
        new_s.append(fin[:S5_STEPS_CTX].transpose(0, 3, 1, 2, 4).reshape(N_CTX, 2, S5_G, 2 * S5_P))
        x1, h2, lg = _outproj(attn, four, y5, us, s5_d, s5_w_glu, w_out, x, mod, ln1_g, ln1_b, w_router_t, l)
        crow_c, grow_c = _router(lg, T_CTX, N_CTX, 0, CAP_CTX)
        crow_q, grow_q = _router(lg, T_LAT, N_LAT, 1, CAP_LAT)
        xs_c = _gather(crow_c, h2, T_CTX, N_CTX, SEQ_PER_STEP_CTX, 0, CAP_CTX)
        xs_q = _gather(crow_q, h2, T_LAT, N_LAT, 1, ROWS_CTX // T_LAT, CAP_LAT)
        ys_c, ys_q = _ffn(xs_c, xs_q, w_gate, w_up, w_down, l)
        args_c = (crow_c, grow_c, ys_c), (mod, ln2_g, ln2_b, l, CAP_CTX, N_CTX, SEQ_PER_STEP_CTX, T_CTX, 1, 0)
        args_q = (crow_q, grow_q, ys_q), (mod, ln2_g, ln2_b, l, CAP_LAT, N_LAT, 1, RB_COMBINE_LAT,
                                          T_LAT // RB_COMBINE_LAT, ROWS_CTX // RB_COMBINE_LAT)
        if l < DEPTH - 1:
            x = _combine(*args_c[0], x1, *args_c[1], in_place=True)
            x = _combine(*args_q[0], x, *args_q[1], in_place=True)
        else:
            y_ctx = _combine(*args_c[0], x1, *args_c[1], in_place=False)
            y_lat = _combine(*args_q[0], x1, *args_q[1], in_place=False)

    new_s = jnp.stack(new_s, axis=1)

    def cache(feat_major):
        return feat_major.reshape(N_CTX, DEPTH, N_KV, HD, T_CTX).transpose(0, 1, 4, 2, 3)

    return (y_ctx.reshape(N_CTX, T_CTX, D), y_lat.reshape(N_LAT, T_LAT, D),
            cache(caches[0]), cache(caches[1]), new_s[..., :S5_P], new_s[..., S5_P:])
```
